```python
import math
import jax, jax.numpy as jnp
from jax import lax
import numpy as np

D_MODEL = 1024
BATCH = 8
SEQ = 4096
DEPTH = 2

MEM_LEN = 256
GROUP_WIDTH = D_MODEL // 2
D_MIX = 3 * GROUP_WIDTH
LRU_HEADS = 8
LRU_BLOCK = GROUP_WIDTH // LRU_HEADS
CONV_WIDTH = 4
LRU_C = 8.0
DIFF_HEADS = 4
DIFF_HEAD_DIM = GROUP_WIDTH // DIFF_HEADS
DIFF_QK_DIM = DIFF_HEAD_DIM // 2
MEM_HEADS = 4
MEM_HEAD_DIM = GROUP_WIDTH // MEM_HEADS
Q_BLOCK = 128
N_GROUPS = 4
EXPERTS_PER_GROUP = 8
N_EXPERTS = N_GROUPS * EXPERTS_PER_GROUP
TOP_K = 2
D_EXPERT = 256
EPS = 1e-6

LRU_X_OFF = 0
LRU_G_OFF = LRU_X_OFF + GROUP_WIDTH
DIFF_Q_OFF = LRU_G_OFF + GROUP_WIDTH
DIFF_K_OFF = DIFF_Q_OFF + GROUP_WIDTH
DIFF_V_OFF = DIFF_K_OFF + GROUP_WIDTH
MEM_Q_OFF = DIFF_V_OFF + GROUP_WIDTH
D_IN_PROJ = MEM_Q_OFF + GROUP_WIDTH

kernel_name = 'hymba_rglru_diffattn_hiermoe'


def rms_norm(x, g):
    xf = x.astype(jnp.float32)
    xf = xf * lax.rsqrt(jnp.mean(xf * xf, axis=-1, keepdims=True) + EPS)
    return (xf * g.astype(jnp.float32)).astype(x.dtype)


def _lin_combine(c1, c2):
    a1, b1 = c1
    a2, b2 = c2
    return a1 * a2, a2 * b1 + b2


def rg_lru_group(xb, gb, conv_w, conv_b, wa, ba, wx, bx, lam):
    B, S, C = xb.shape
    xc = lax.conv_general_dilated(xb, conv_w[:, None, :], window_strides=(1,), padding=[(CONV_WIDTH - 1, 0)],
                                  dimension_numbers=('NWC', 'WIO', 'NWC'), feature_group_count=C) + conv_b
    xh = xc.reshape(B, S, LRU_HEADS, LRU_BLOCK)
    r = jax.nn.sigmoid(jnp.einsum('bshi,hij->bshj', xh, wa).reshape(B, S, C) + ba)
    i = jax.nn.sigmoid(jnp.einsum('bshi,hij->bshj', xh, wx).reshape(B, S, C) + bx)
    log_a = (-LRU_C * r.astype(jnp.float32)) * jax.nn.softplus(-lam.astype(jnp.float32))
    a = jnp.exp(log_a)
    b = jnp.sqrt(-jnp.expm1(2.0 * log_a)) * (i * xc).astype(jnp.float32)
    _, h = lax.associative_scan(_lin_combine, (a, b), axis=1)
    return h.astype(xb.dtype) * jax.nn.gelu(gb)


def diff_attention(q, k, v, lam):
    B, S, H, _, d = q.shape
    dv = v.shape[-1]
    nb = S // Q_BLOCK
    qb = jnp.moveaxis(q.reshape(B, nb, Q_BLOCK, H, 2, d), 1, 0)
    key_pos = jnp.arange(S)
    scale = d ** -0.5

    def one_block(args):
        q_blk, blk = args
        s = jnp.einsum('bqhcd,bkhcd->bhcqk', q_blk, k).astype(jnp.float32) * scale
        q_pos = blk * Q_BLOCK + jnp.arange(Q_BLOCK)
        mask = key_pos[None, :] <= q_pos[:, None]
        p = jax.nn.softmax(jnp.where(mask, s, -jnp.inf), axis=-1)
        p = p[:, :, 0] - lam * p[:, :, 1]
        return jnp.einsum('bhqk,bkhv->bqhv', p.astype(v.dtype), v)

    o = lax.map(one_block, (qb, jnp.arange(nb)))
    return jnp.moveaxis(o, 0, 1).reshape(B, S, H, dv)


def memory_attention(q, k, v):
    B, S, H, d = q.shape
    s = jnp.einsum('bshd,bmhd->bhsm', q, k).astype(jnp.float32) * (d ** -0.5)
    p = jax.nn.softmax(s, axis=-1)
    return jnp.einsum('bhsm,bmhd->bshd', p.astype(v.dtype), v).reshape(B, S, H * d)


def hier_moe(xn, w_rg, b_rg, w_re, b_re, w_g, w_u, w_d):
    B, S, D = xn.shape
    xt = xn.reshape(B * S, D)
    pg = jax.nn.softmax((xt @ w_rg + b_rg).astype(jnp.float32), axis=-1)
    g_idx = jnp.argmax(pg, axis=-1)
    g_gate = jnp.max(pg, axis=-1)
    le = (xt @ w_re + b_re).astype(jnp.float32).reshape(-1, N_GROUPS, EXPERTS_PER_GROUP)
    le_sel = jnp.take_along_axis(le, g_idx[:, None, None], axis=1)[:, 0]
    top_w, top_i = lax.top_k(jax.nn.softmax(le_sel, axis=-1), TOP_K)
    top_w = top_w / jnp.sum(top_w, axis=-1, keepdims=True)
    e_w = jnp.sum(jax.nn.one_hot(top_i, EXPERTS_PER_GROUP, dtype=jnp.float32) * top_w[..., None], axis=1)
    combine = (jax.nn.one_hot(g_idx, N_GROUPS, dtype=jnp.float32)[:, :, None]
               * (g_gate[:, None] * e_w)[:, None, :]).astype(xn.dtype)
    wg = w_g.reshape(N_GROUPS, EXPERTS_PER_GROUP, D, D_EXPERT)
    wu = w_u.reshape(N_GROUPS, EXPERTS_PER_GROUP, D, D_EXPERT)
    wd = w_d.reshape(N_GROUPS, EXPERTS_PER_GROUP, D_EXPERT, D)
    out = jnp.zeros_like(xt)
    for g in range(N_GROUPS):
        h = jax.nn.silu(jnp.einsum('td,edf->tef', xt, wg[g])) * jnp.einsum('td,edf->tef', xt, wu[g])
        out = out + jnp.einsum('tef,efd->td', h * combine[:, g, :, None], wd[g])
    return out.reshape(B, S, D)


def setup_inputs(seed: int = 0) -> dict:
    key = jax.random.key(seed)
    ks = jax.random.split(key, 32)
    f32 = jnp.float32
    nrm = lambda k, shape, s: jax.random.normal(k, shape, f32) * s
    gain = lambda k, shape: 1.0 + 0.02 * jax.random.normal(k, shape, f32)
    a_c = jax.random.uniform(ks[10], (DEPTH, GROUP_WIDTH), f32, 0.9, 0.999)
    a_base = a_c ** (1.0 / LRU_C)
    rg_lambda = jnp.log(a_base) - jnp.log1p(-a_base)
    return {
        'x': nrm(ks[0], (BATCH, SEQ, D_MODEL), 1.0),
        'mem': nrm(ks[1], (BATCH, MEM_LEN, D_MODEL), 1.0),
        'norm1_g': gain(ks[2], (DEPTH, D_MODEL)),
        'w_in': nrm(ks[3], (DEPTH, D_MODEL, D_IN_PROJ), D_MODEL ** -0.5),
        'conv_w': nrm(ks[4], (DEPTH, CONV_WIDTH, GROUP_WIDTH), CONV_WIDTH ** -0.5),
        'conv_b': nrm(ks[5], (DEPTH, GROUP_WIDTH), 0.02),
        'rg_wa': nrm(ks[6], (DEPTH, LRU_HEADS, LRU_BLOCK, LRU_BLOCK), LRU_BLOCK ** -0.5),
        'rg_ba': nrm(ks[7], (DEPTH, GROUP_WIDTH), 0.02),
        'rg_wx': nrm(ks[8], (DEPTH, LRU_HEADS, LRU_BLOCK, LRU_BLOCK), LRU_BLOCK ** -0.5),
        'rg_bx': nrm(ks[9], (DEPTH, GROUP_WIDTH), 0.02),
        'rg_lambda': rg_lambda,
        'dq_norm_g': gain(ks[11], (DEPTH, DIFF_QK_DIM)),
        'dk_norm_g': gain(ks[12], (DEPTH, DIFF_QK_DIM)),
        'lambda_q1': nrm(ks[13], (DEPTH, DIFF_QK_DIM), 0.1),
        'lambda_k1': nrm(ks[14], (DEPTH, DIFF_QK_DIM), 0.1),
        'lambda_q2': nrm(ks[15], (DEPTH, DIFF_QK_DIM), 0.1),
        'lambda_k2': nrm(ks[16], (DEPTH, DIFF_QK_DIM), 0.1),
        'diff_head_norm_g': gain(ks[17], (DEPTH, DIFF_HEAD_DIM)),
        'mem_norm_g': gain(ks[18], (D_MODEL,)),
        'w_mem_kv': nrm(ks[19], (DEPTH, D_MODEL, 2 * GROUP_WIDTH), D_MODEL ** -0.5),
        'mq_norm_g': gain(ks[20], (DEPTH, MEM_HEAD_DIM)),
        'mk_norm_g': gain(ks[21], (DEPTH, MEM_HEAD_DIM)),
        'w_out': nrm(ks[22], (DEPTH, D_MIX, D_MODEL), D_MIX ** -0.5),
        'norm2_g': gain(ks[23], (DEPTH, D_MODEL)),
        'w_router_group': nrm(ks[24], (DEPTH, D_MODEL, N_GROUPS), D_MODEL ** -0.5),
        'b_router_group': nrm(ks[25], (DEPTH, N_GROUPS), 0.01),
        'w_router_expert': nrm(ks[26], (DEPTH, D_MODEL, N_EXPERTS), D_MODEL ** -0.5),
        'b_router_expert': nrm(ks[27], (DEPTH, N_EXPERTS), 0.01),
        'w_expert_gate': nrm(ks[28], (DEPTH, N_EXPERTS, D_MODEL, D_EXPERT), D_MODEL ** -0.5),
        'w_expert_up': nrm(ks[29], (DEPTH, N_EXPERTS, D_MODEL, D_EXPERT), D_MODEL ** -0.5),
        'w_expert_down': nrm(ks[30], (DEPTH, N_EXPERTS, D_EXPERT, D_MODEL), D_EXPERT ** -0.5),
    }


def reference(x, mem, norm1_g, w_in, conv_w, conv_b, rg_wa, rg_ba, rg_wx, rg_bx, rg_lambda,
              dq_norm_g, dk_norm_g, lambda_q1, lambda_k1, lambda_q2, lambda_k2, diff_head_norm_g,
              mem_norm_g, w_mem_kv, mq_norm_g, mk_norm_g, w_out, norm2_g,
              w_router_group, b_router_group, w_router_expert, b_router_expert,
              w_expert_gate, w_expert_up, w_expert_down):
    B, S, _ = x.shape
    M = mem.shape[1]
    memn = rms_norm(mem, mem_norm_g)
    for l in range(DEPTH):
        lam_init = 0.8 - 0.6 * math.exp(-0.3 * l)
        xn = rms_norm(x, norm1_g[l])
        u = xn @ w_in[l]

        y_lru = rg_lru_group(u[..., LRU_X_OFF:LRU_X_OFF + GROUP_WIDTH], u[..., LRU_G_OFF:LRU_G_OFF + GROUP_WIDTH],
                             conv_w[l], conv_b[l], rg_wa[l], rg_ba[l], rg_wx[l], rg_bx[l], rg_lambda[l])

        q = rms_norm(u[..., DIFF_Q_OFF:DIFF_Q_OFF + GROUP_WIDTH].reshape(B, S, DIFF_HEADS, 2, DIFF_QK_DIM), dq_norm_g[l])
        k = rms_norm(u[..., DIFF_K_OFF:DIFF_K_OFF + GROUP_WIDTH].reshape(B, S, DIFF_HEADS, 2, DIFF_QK_DIM), dk_norm_g[l])
        v = u[..., DIFF_V_OFF:DIFF_V_OFF + GROUP_WIDTH].reshape(B, S, DIFF_HEADS, DIFF_HEAD_DIM)
        lam = (jnp.exp(jnp.sum(lambda_q1[l].astype(jnp.float32) * lambda_k1[l].astype(jnp.float32)))
               - jnp.exp(jnp.sum(lambda_q2[l].astype(jnp.float32) * lambda_k2[l].astype(jnp.float32)))
               + lam_init)
        o = diff_attention(q, k, v, lam)
        y_diff = (rms_norm(o, diff_head_norm_g[l]) * (1.0 - lam_init)).reshape(B, S, GROUP_WIDTH)

        mq = rms_norm(u[..., MEM_Q_OFF:MEM_Q_OFF + GROUP_WIDTH].reshape(B, S, MEM_HEADS, MEM_HEAD_DIM), mq_norm_g[l])
        kv = memn @ w_mem_kv[l]
        mk = rms_norm(kv[..., :GROUP_WIDTH].reshape(B, M, MEM_HEADS, MEM_HEAD_DIM), mk_norm_g[l])
        mv = kv[..., GROUP_WIDTH:].reshape(B, M, MEM_HEADS, MEM_HEAD_DIM)
        y_mem = memory_attention(mq, mk, mv)

        x = x + jnp.concatenate([y_lru, y_diff, y_mem], axis=-1) @ w_out[l]

        x = x + hier_moe(rms_norm(x, norm2_g[l]), w_router_group[l], b_router_group[l],
                         w_router_expert[l], b_router_expert[l],
                         w_expert_gate[l], w_expert_up[l], w_expert_down[l])
    return x
```

```python
import functools
import math

import jax
import jax.numpy as jnp
from jax import lax
from jax.experimental import pallas as pl
from jax.experimental.pallas import tpu as pltpu

F32 = jnp.float32
BF16 = jnp.bfloat16

D_MODEL = 1024
GROUP_WIDTH = D_MODEL // 2
LRU_HEADS = 8
CONV_WIDTH = 4
LRU_C = 8.0
DIFF_HEADS = 4
DIFF_HEAD_DIM = GROUP_WIDTH // DIFF_HEADS
DIFF_QK_DIM = DIFF_HEAD_DIM // 2
MEM_HEADS = 4
MEM_HEAD_DIM = GROUP_WIDTH // MEM_HEADS
N_GROUPS = 4
EXPERTS_PER_GROUP = 8
N_EXPERTS = N_GROUPS * EXPERTS_PER_GROUP
D_EXPERT = 256
D_IN_PROJ = 6 * GROUP_WIDTH
EPS = 1e-6

LANES = 128
SUBLANES = 8
VMEM_LIMIT = 56 * 1024 * 1024
NEG_BIG = -1e30
ROUTER_EXPERT_LANE0 = 32


def _tile(n, pref):
    t = min(n, pref)
    assert n % t == 0, (n, t)
    return t


def _params(*sem):
    return pltpu.CompilerParams(dimension_semantics=sem, vmem_limit_bytes=VMEM_LIMIT)


def _rms_rows(x, g):
    return x * lax.rsqrt(jnp.mean(x * x, axis=-1, keepdims=True) + EPS) * g


def _lane_iota(shape):
    return lax.broadcasted_iota(jnp.int32, shape, len(shape) - 1)


def _half_head_norm(z, g):
    sq = z * z
    lo = _lane_iota(z.shape) < DIFF_QK_DIM
    s_all = jnp.sum(sq, axis=-1, keepdims=True)
    s_lo = jnp.sum(jnp.where(lo, sq, 0.0), axis=-1, keepdims=True)
    inv_lo = lax.rsqrt(s_lo * (1.0 / DIFF_QK_DIM) + EPS)
    inv_hi = lax.rsqrt((s_all - s_lo) * (1.0 / DIFF_QK_DIM) + EPS)
    return z * jnp.where(lo, inv_lo, inv_hi) * g


def _in_proj_kernel(x_ref, g1_ref, w_ref, qg_ref, kg_ref, mqg_ref, lru_ref, qkv_ref, mq_ref):
    xn = _rms_rows(x_ref[...], g1_ref[...]).astype(BF16)
    gw = GROUP_WIDTH
    lru_ref[...] = jnp.dot(xn, w_ref[:, 0:2 * gw], preferred_element_type=F32).astype(BF16)
    q = jnp.dot(xn, w_ref[:, 2 * gw:3 * gw], preferred_element_type=F32)
    k = jnp.dot(xn, w_ref[:, 3 * gw:4 * gw], preferred_element_type=F32)
    qscale = DIFF_QK_DIM ** -0.5
    for h in range(DIFF_HEADS):
        sl = slice(h * LANES, (h + 1) * LANES)
        qkv_ref[:, sl] = (_half_head_norm(q[:, sl], qg_ref[...]) * qscale).astype(BF16)
        qkv_ref[:, gw + h * LANES:gw + (h + 1) * LANES] = _half_head_norm(k[:, sl], kg_ref[...]).astype(BF16)
    qkv_ref[:, 2 * gw:3 * gw] = jnp.dot(xn, w_ref[:, 4 * gw:5 * gw], preferred_element_type=F32).astype(BF16)
    mq = jnp.dot(xn, w_ref[:, 5 * gw:6 * gw], preferred_element_type=F32)
    for h in range(MEM_HEADS):
        sl = slice(h * MEM_HEAD_DIM, (h + 1) * MEM_HEAD_DIM)
        mq_ref[:, sl] = _rms_rows(mq[:, sl], mqg_ref[...]).astype(BF16)


def _in_proj(x2, g1, w_in, qg, kg, mqg):
    t = x2.shape[0]
    tm = _tile(t, 512)
    row = lambda i: (i, 0)
    fixed = lambda i: (0, 0)
    return pl.pallas_call(
        _in_proj_kernel,
        grid=(t // tm,),
        in_specs=[pl.BlockSpec((tm, D_MODEL), row), pl.BlockSpec((1, D_MODEL), fixed),
                  pl.BlockSpec((D_MODEL, D_IN_PROJ), fixed), pl.BlockSpec((1, LANES), fixed),
                  pl.BlockSpec((1, LANES), fixed), pl.BlockSpec((1, MEM_HEAD_DIM), fixed)],
        out_specs=[pl.BlockSpec((tm, 2 * GROUP_WIDTH), row), pl.BlockSpec((tm, 3 * GROUP_WIDTH), row),
                   pl.BlockSpec((tm, GROUP_WIDTH), row)],
        out_shape=[jax.ShapeDtypeStruct((t, 2 * GROUP_WIDTH), BF16), jax.ShapeDtypeStruct((t, 3 * GROUP_WIDTH), BF16),
                   jax.ShapeDtypeStruct((t, GROUP_WIDTH), BF16)],
        compiler_params=_params("arbitrary"),
        name="in_proj",
    )(x2, g1, w_in, qg, kg, mqg)


def _softplus(z):
    return jnp.maximum(z, 0.0) + jnp.log(1.0 + jnp.exp(-jnp.abs(z)))


def _lru_kernel(u_ref, cw_ref, cb_ref, wa_ref, ba_ref, wx_ref, bx_ref, lam_ref, y_ref,
                ext_ref, hcar_ref, a_ref, b_ref, h_ref):
    ts = y_ref.shape[0]
    gw = GROUP_WIDTH
    hist = SUBLANES

    @pl.when(pl.program_id(1) == 0)
    def _():
        ext_ref[0:hist, :] = jnp.zeros((hist, gw), F32)
        hcar_ref[...] = jnp.zeros_like(hcar_ref)

    @pl.when(pl.program_id(1) != 0)
    def _():
        ext_ref[0:hist, :] = ext_ref[ts:ts + hist, :]

    ext_ref[hist:hist + ts, :] = u_ref[:, 0:gw].astype(F32)
    xc = cb_ref[...] + cw_ref[CONV_WIDTH - 1:CONV_WIDTH, :] * ext_ref[hist:hist + ts, :]
    for j in range(CONV_WIDTH - 1):
        off = hist - (CONV_WIDTH - 1) + j
        xc = xc + cw_ref[j:j + 1, :] * ext_ref[off:off + ts, :]
    xcb = xc.astype(BF16)
    r = jax.nn.sigmoid(jnp.dot(xcb, wa_ref[...], preferred_element_type=F32) + ba_ref[...])
    gate_i = jax.nn.sigmoid(jnp.dot(xcb, wx_ref[...], preferred_element_type=F32) + bx_ref[...])
    log_a = (-LRU_C * r) * _softplus(-lam_ref[...])
    a = jnp.exp(log_a)
    b = jnp.sqrt(1.0 - a * a) * (gate_i * xc)

    row = lax.broadcasted_iota(jnp.int32, (ts, gw), 0) & (SUBLANES - 1)
    d = 1
    while d < SUBLANES:
        keep = row >= d
        a_prev = pltpu.roll(a, d, axis=0)
        b_prev = pltpu.roll(b, d, axis=0)
        b = jnp.where(keep, a * b_prev + b, b)
        a = jnp.where(keep, a * a_prev, a)
        d *= 2
    a_ref[...] = a
    b_ref[...] = b

    def block(i, h):
        off = pl.multiple_of(i * SUBLANES, SUBLANES)
        hb = a_ref[pl.ds(off, SUBLANES), :] * h + b_ref[pl.ds(off, SUBLANES), :]
        h_ref[pl.ds(off, SUBLANES), :] = hb
        return jnp.broadcast_to(hb[SUBLANES - 1:SUBLANES, :], (SUBLANES, gw))

    hcar_ref[...] = lax.fori_loop(0, ts // SUBLANES, block, hcar_ref[...], unroll=8)
    y_ref[...] = (h_ref[...] * jax.nn.gelu(u_ref[:, gw:2 * gw].astype(F32))).astype(BF16)


def _lru(lru_in, b, s, conv_w, conv_b, wa_bd, ba, wx_bd, bx, lam):
    ts = _tile(s, 512)
    ns = s // ts
    gw = GROUP_WIDTH
    fixed = lambda bi, si: (0, 0)
    vec = pl.BlockSpec((1, gw), fixed)
    return pl.pallas_call(
        _lru_kernel,
        grid=(b, ns),
        in_specs=[pl.BlockSpec((ts, 2 * gw), lambda bi, si: (bi * ns + si, 0)),
                  pl.BlockSpec((CONV_WIDTH, gw), fixed), vec,
                  pl.BlockSpec((gw, gw), fixed), vec, pl.BlockSpec((gw, gw), fixed), vec, vec],
        out_specs=pl.BlockSpec((ts, gw), lambda bi, si: (bi * ns + si, 0)),
        out_shape=jax.ShapeDtypeStruct((b * s, gw), BF16),
        scratch_shapes=[pltpu.VMEM((ts + SUBLANES, gw), F32), pltpu.VMEM((SUBLANES, gw), F32),
                        pltpu.VMEM((ts, gw), F32), pltpu.VMEM((ts, gw), F32), pltpu.VMEM((ts, gw), F32)],
        compiler_params=_params("arbitrary", "arbitrary"),
        name="rg_lru",
    )(lru_in, conv_w, conv_b, wa_bd, ba, wx_bd, bx, lam)


def _diff_attn_kernel(lam_ref, q_ref, k_ref, v_ref, hg_ref, o_ref, *, tq, tk, out_scale):
    qi = pl.program_id(2)
    q = q_ref[...]
    lo = _lane_iota(q.shape) < DIFF_QK_DIM
    zero = jnp.zeros_like(q)
    qs = jnp.concatenate([jnp.where(lo, q, zero), jnp.where(lo, zero, q)], axis=0)

    def step(j, carry, masked):
        m, l, acc = carry
        off = pl.multiple_of(j * tk, tk)
        kt = k_ref[pl.ds(off, tk), :]
        vt = v_ref[pl.ds(off, tk), :]
        s = lax.dot_general(qs, kt, (((1,), (1,)), ((), ())), preferred_element_type=F32)
        if masked:
            r = lax.broadcasted_iota(jnp.int32, (2 * tq, tk), 0)
            qpos = qi * tq + jnp.where(r >= tq, r - tq, r)
            kpos = j * tk + lax.broadcasted_iota(jnp.int32, (2 * tq, tk), 1)
            s = jnp.where(kpos <= qpos, s, NEG_BIG)
        m_new = jnp.maximum(m, jnp.max(s, axis=-1, keepdims=True))
        alpha = jnp.exp(m - m_new)
        p = jnp.exp(s - m_new)
        l = alpha * l + jnp.sum(p, axis=-1, keepdims=True)
        acc = alpha * acc + jnp.dot(p.astype(BF16), vt, preferred_element_type=F32)
        return m_new, l, acc

    init = (jnp.full((2 * tq, 1), NEG_BIG, F32), jnp.zeros((2 * tq, 1), F32),
            jnp.zeros((2 * tq, DIFF_HEAD_DIM), F32))
    nfull = (qi * tq) // tk
    carry = lax.fori_loop(0, nfull, lambda j, c: step(j, c, False), init)
    _, l, acc = step(nfull, carry, True)
    o = acc / l
    o = o[0:tq] - lam_ref[0, 0] * o[tq:2 * tq]
    o_ref[...] = (_rms_rows(o, hg_ref[...]) * out_scale).astype(BF16)


def _diff_attn(qkv, lam, hg, b, s, lam_init):
    tq = _tile(s, 256)
    tk = _tile(s, 512)
    assert tk % tq == 0
    nq = s // tq
    hd = DIFF_HEAD_DIM
    kern = functools.partial(_diff_attn_kernel, tq=tq, tk=tk, out_scale=1.0 - lam_init)
    return pl.pallas_call(
        kern,
        grid=(b, DIFF_HEADS, nq),
        in_specs=[pl.BlockSpec(memory_space=pltpu.SMEM),
                  pl.BlockSpec((tq, hd), lambda bi, h, qi: (bi * nq + qi, h)),
                  pl.BlockSpec((s, hd), lambda bi, h, qi: (bi, DIFF_HEADS + h)),
                  pl.BlockSpec((s, hd), lambda bi, h, qi: (bi, 2 * DIFF_HEADS + h)),
                  pl.BlockSpec((1, hd), lambda bi, h, qi: (0, 0))],
        out_specs=pl.BlockSpec((tq, hd), lambda bi, h, qi: (bi * nq + qi, h)),
        out_shape=jax.ShapeDtypeStruct((b * s, GROUP_WIDTH), BF16),
        compiler_params=_params("arbitrary", "arbitrary", "arbitrary"),
        name="diff_attn",
    )(lam, qkv, qkv, qkv, hg)


def _mem_kv_kernel(mem_ref, mg_ref, w_ref, kg_ref, mk_ref, mv_ref):
    memn = _rms_rows(mem_ref[...], mg_ref[...]).astype(BF16)
    gw = GROUP_WIDTH
    k = jnp.dot(memn, w_ref[:, 0:gw], preferred_element_type=F32)
    for h in range(MEM_HEADS):
        sl = slice(h * MEM_HEAD_DIM, (h + 1) * MEM_HEAD_DIM)
        mk_ref[:, sl] = _rms_rows(k[:, sl], kg_ref[...]).astype(BF16)
    mv_ref[...] = jnp.dot(memn, w_ref[:, gw:2 * gw], preferred_element_type=F32).astype(BF16)


def _mem_kv(mem2, mg, w_kv, kg):
    r = mem2.shape[0]
    tm = _tile(r, 512)
    row = lambda i: (i, 0)
    fixed = lambda i: (0, 0)
    gw = GROUP_WIDTH
    return pl.pallas_call(
        _mem_kv_kernel,
        grid=(r // tm,),
        in_specs=[pl.BlockSpec((tm, D_MODEL), row), pl.BlockSpec((1, D_MODEL), fixed),
                  pl.BlockSpec((D_MODEL, 2 * gw), fixed), pl.BlockSpec((1, MEM_HEAD_DIM), fixed)],
        out_specs=[pl.BlockSpec((tm, gw), row), pl.BlockSpec((tm, gw), row)],
        out_shape=[jax.ShapeDtypeStruct((r, gw), BF16), jax.ShapeDtypeStruct((r, gw), BF16)],
        compiler_params=_params("arbitrary"),
        name="mem_kv",
    )(mem2, mg, w_kv, kg)


def _mem_attn_kernel(q_ref, k_ref, v_ref, o_ref):
    scale = MEM_HEAD_DIM ** -0.5
    for h in range(MEM_HEADS):
        sl = slice(h * MEM_HEAD_DIM, (h + 1) * MEM_HEAD_DIM)
        s = lax.dot_general(q_ref[:, sl], k_ref[:, sl], (((1,), (1,)), ((), ())),
                            preferred_element_type=F32) * scale
        p = jnp.exp(s - jnp.max(s, axis=-1, keepdims=True))
        o = jnp.dot(p.astype(BF16), v_ref[:, sl], preferred_element_type=F32)
        o_ref[:, sl] = (o / jnp.sum(p, axis=-1, keepdims=True)).astype(BF16)


def _mem_attn(mq, mk, mv, b, s, m):
    ts = _tile(s, 512)
    ns = s // ts
    gw = GROUP_WIDTH
    return pl.pallas_call(
        _mem_attn_kernel,
        grid=(b, ns),
        in_specs=[pl.BlockSpec((ts, gw), lambda bi, si: (bi * ns + si, 0)),
                  pl.BlockSpec((m, gw), lambda bi, si: (bi, 0)),
                  pl.BlockSpec((m, gw), lambda bi, si: (bi, 0))],
        out_specs=pl.BlockSpec((ts, gw), lambda bi, si: (bi * ns + si, 0)),
        out_shape=jax.ShapeDtypeStruct((b * s, gw), BF16),
        compiler_params=_params("arbitrary", "arbitrary"),
        name="mem_attn",
    )(mq, mk, mv)


def _out_proj_kernel(x_ref, yl_ref, yd_ref, ym_ref, w_ref, o_ref):
    y = jnp.concatenate([yl_ref[...], yd_ref[...], ym_ref[...]], axis=-1)
    o_ref[...] = x_ref[...] + jnp.dot(y, w_ref[...], preferred_element_type=F32)


def _out_proj(x2, y_lru, y_diff, y_mem, w_out):
    t = x2.shape[0]
    tm = _tile(t, 512)
    row = lambda i: (i, 0)
    gw = GROUP_WIDTH
    return pl.pallas_call(
        _out_proj_kernel,
        grid=(t // tm,),
        in_specs=[pl.BlockSpec((tm, D_MODEL), row), pl.BlockSpec((tm, gw), row), pl.BlockSpec((tm, gw), row),
                  pl.BlockSpec((tm, gw), row), pl.BlockSpec((3 * gw, D_MODEL), lambda i: (0, 0))],
        out_specs=pl.BlockSpec((tm, D_MODEL), row),
        out_shape=jax.ShapeDtypeStruct((t, D_MODEL), F32),
        compiler_params=_params("arbitrary"),
        name="out_proj",
    )(x2, y_lru, y_diff, y_mem, w_out)


def _route(logits):
    lane_i = _lane_iota(logits.shape)
    lane = lane_i.astype(F32)
    e0 = ROUTER_EXPERT_LANE0
    lg = jnp.where(lane_i < N_GROUPS, logits, NEG_BIG)
    mg = jnp.max(lg, axis=-1, keepdims=True)
    g_gate = 1.0 / jnp.sum(jnp.exp(lg - mg), axis=-1, keepdims=True)
    g_idx = jnp.min(jnp.where(lg == mg, lane, float(LANES)), axis=-1, keepdims=True)
    lane_group = ((lane_i - e0) >> 3).astype(F32)
    sel = (lane_i >= e0) & (lane_i < e0 + N_EXPERTS) & (lane_group == g_idx)
    le = jnp.where(sel, logits, NEG_BIG)
    m1 = jnp.max(le, axis=-1, keepdims=True)
    se = jnp.sum(jnp.where(sel, jnp.exp(le - m1), 0.0), axis=-1, keepdims=True)
    i1 = jnp.min(jnp.where(sel & (le == m1), lane, float(LANES)), axis=-1, keepdims=True)
    le2 = jnp.where(lane == i1, NEG_BIG, le)
    m2 = jnp.max(le2, axis=-1, keepdims=True)
    i2 = jnp.min(jnp.where(sel & (le2 == m2) & (lane != i1), lane, float(LANES)), axis=-1, keepdims=True)
    p1 = 1.0 / se
    p2 = jnp.exp(m2 - m1) / se
    tot = p1 + p2
    w = jnp.where(lane == i1, p1 / tot, jnp.where(lane == i2, p2 / tot, 0.0))
    return g_gate * w


def _moe_dense_kernel(x_ref, g2_ref, rhi_ref, rlo_ref, rb_ref, wg_ref, wu_ref, wd_ref, o_ref,
                      xn_ref, comb_ref, acc_ref):
    e = pl.program_id(1)

    @pl.when(e == 0)
    def _():
        xn = _rms_rows(x_ref[...], g2_ref[...])
        hi = xn.astype(BF16)
        lo = (xn - hi.astype(F32)).astype(BF16)
        xn_ref[...] = hi
        logits = (jnp.dot(hi, rhi_ref[...], preferred_element_type=F32)
                  + jnp.dot(lo, rhi_ref[...], preferred_element_type=F32)
                  + jnp.dot(hi, rlo_ref[...], preferred_element_type=F32) + rb_ref[...])
        comb_ref[...] = _route(logits)
        acc_ref[...] = jnp.zeros_like(acc_ref)

    comb = comb_ref[...]
    cw = jnp.sum(jnp.where(_lane_iota(comb.shape) == e + ROUTER_EXPERT_LANE0, comb, 0.0), axis=-1, keepdims=True)
    xn = xn_ref[...]
    hg = jnp.dot(xn, wg_ref[...], preferred_element_type=F32)
    hu = jnp.dot(xn, wu_ref[...], preferred_element_type=F32)
    h = (jax.nn.silu(hg) * hu * cw).astype(BF16)
    acc_ref[...] += jnp.dot(h, wd_ref[...], preferred_element_type=F32)

    @pl.when(e == N_EXPERTS - 1)
    def _():
        o_ref[...] = x_ref[...] + acc_ref[...]


def _moe_dense(x2, g2, r_hi, r_lo, r_b, wg, wu, wd):
    t = x2.shape[0]
    tm = _tile(t, 1024)
    row = lambda i, e: (i, 0)
    fixed = lambda i, e: (0, 0)
    return pl.pallas_call(
        _moe_dense_kernel,
        grid=(t // tm, N_EXPERTS),
        in_specs=[pl.BlockSpec((tm, D_MODEL), row), pl.BlockSpec((1, D_MODEL), fixed),
                  pl.BlockSpec((D_MODEL, LANES), fixed), pl.BlockSpec((D_MODEL, LANES), fixed),
                  pl.BlockSpec((1, LANES), fixed),
                  pl.BlockSpec((None, D_MODEL, D_EXPERT), lambda i, e: (e, 0, 0)),
                  pl.BlockSpec((None, D_MODEL, D_EXPERT), lambda i, e: (e, 0, 0)),
                  pl.BlockSpec((None, D_EXPERT, D_MODEL), lambda i, e: (e, 0, 0))],
        out_specs=pl.BlockSpec((tm, D_MODEL), row),
        out_shape=jax.ShapeDtypeStruct((t, D_MODEL), F32),
        scratch_shapes=[pltpu.VMEM((tm, D_MODEL), BF16), pltpu.VMEM((tm, LANES), F32),
                        pltpu.VMEM((tm, D_MODEL), F32)],
        compiler_params=_params("arbitrary", "arbitrary"),
        name="moe_dense",
    )(x2, g2, r_hi, r_lo, r_b, wg, wu, wd)


def _block_diag(w):
    h, n, _ = w.shape
    eye = jnp.eye(h, dtype=w.dtype)
    return (eye[:, None, :, None] * w[:, :, None, :]).reshape(h * n, h * n)


def _router_tables(w_rg, b_rg, w_re, b_re):
    e0 = ROUTER_EXPERT_LANE0
    w = jnp.zeros((D_MODEL, LANES), F32).at[:, 0:N_GROUPS].set(w_rg).at[:, e0:e0 + N_EXPERTS].set(w_re)
    b = jnp.zeros((1, LANES), F32).at[0, 0:N_GROUPS].set(b_rg).at[0, e0:e0 + N_EXPERTS].set(b_re)
    hi = w.astype(BF16)
    lo = (w - hi.astype(F32)).astype(BF16)
    return hi, lo, b


def kernel(x, mem, norm1_g, w_in, conv_w, conv_b, rg_wa, rg_ba, rg_wx, rg_bx, rg_lambda, dq_norm_g, dk_norm_g, lambda_q1, lambda_k1, lambda_q2, lambda_k2, diff_head_norm_g, mem_norm_g, w_mem_kv, mq_norm_g, mk_norm_g, w_out, norm2_g, w_router_group, b_router_group, w_router_expert, b_router_expert, w_expert_gate, w_expert_up, w_expert_down):
    b, s, d = x.shape
    m = mem.shape[1]
    depth = w_in.shape[0]
    x2 = x.reshape(b * s, d)
    mem2 = mem.reshape(b * m, d)
    vec = lambda v: v.reshape(1, -1).astype(F32)
    for l in range(depth):
        lam_init = 0.8 - 0.6 * math.exp(-0.3 * l)
        lam = (jnp.exp(jnp.sum(lambda_q1[l] * lambda_k1[l])) - jnp.exp(jnp.sum(lambda_q2[l] * lambda_k2[l]))
               + lam_init).reshape(1, 1).astype(F32)
        lru_in, qkv, mq = _in_proj(x2, vec(norm1_g[l]), w_in[l].astype(BF16),
                                   vec(jnp.tile(dq_norm_g[l], 2)), vec(jnp.tile(dk_norm_g[l], 2)), vec(mq_norm_g[l]))
        y_lru = _lru(lru_in, b, s, conv_w[l], vec(conv_b[l]), _block_diag(rg_wa[l]).astype(BF16), vec(rg_ba[l]),
                     _block_diag(rg_wx[l]).astype(BF16), vec(rg_bx[l]), vec(rg_lambda[l]))
        y_diff = _diff_attn(qkv, lam, vec(diff_head_norm_g[l]), b, s, lam_init)
        mk, mv = _mem_kv(mem2, vec(mem_norm_g), w_mem_kv[l].astype(BF16), vec(mk_norm_g[l]))
        y_mem = _mem_attn(mq, mk, mv, b, s, m)
        x2 = _out_proj(x2, y_lru, y_diff, y_mem, w_out[l].astype(BF16))
        r_hi, r_lo, r_b = _router_tables(w_router_group[l], b_router_group[l], w_router_expert[l], b_router_expert[l])
        x2 = _moe_dense(x2, vec(norm2_g[l]), r_hi, r_lo, r_b, w_expert_gate[l].astype(BF16),
                        w_expert_up[l].astype(BF16), w_expert_down[l].astype(BF16))
    return x2.reshape(b, s, d)
```

```python
import functools
import math

import jax
import jax.numpy as jnp
from jax import lax
from jax.experimental import pallas as pl
from jax.experimental.pallas import tpu as pltpu

F32 = jnp.float32
BF16 = jnp.bfloat16

D_MODEL = 1024
GROUP_WIDTH = D_MODEL // 2
LRU_HEADS = 8
CONV_WIDTH = 4
LRU_C = 8.0
DIFF_HEADS = 4
DIFF_HEAD_DIM = GROUP_WIDTH // DIFF_HEADS
DIFF_QK_DIM = DIFF_HEAD_DIM // 2
MEM_HEADS = 4
MEM_HEAD_DIM = GROUP_WIDTH // MEM_HEADS
N_GROUPS = 4
EXPERTS_PER_GROUP = 8
N_EXPERTS = N_GROUPS * EXPERTS_PER_GROUP
D_EXPERT = 256
D_IN_PROJ = 6 * GROUP_WIDTH
EPS = 1e-6

LANES = 128
SUBLANES = 8
VMEM_LIMIT = 56 * 1024 * 1024
NEG_BIG = -1e30
ROUTER_EXPERT_LANE0 = 32


def _tile(n, pref):
    t = min(n, pref)
    assert n % t == 0, (n, t)
    return t


def _params(*sem):
    return pltpu.CompilerParams(dimension_semantics=sem, vmem_limit_bytes=VMEM_LIMIT)


def _rms_rows(x, g):
    return x * lax.rsqrt(jnp.mean(x * x, axis=-1, keepdims=True) + EPS) * g


def _lane_iota(shape):
    return lax.broadcasted_iota(jnp.int32, shape, len(shape) - 1)


def _half_head_norm(z, g):
    sq = z * z
    lo = _lane_iota(z.shape) < DIFF_QK_DIM
    s_all = jnp.sum(sq, axis=-1, keepdims=True)
    s_lo = jnp.sum(jnp.where(lo, sq, 0.0), axis=-1, keepdims=True)
    inv_lo = lax.rsqrt(s_lo * (1.0 / DIFF_QK_DIM) + EPS)
    inv_hi = lax.rsqrt((s_all - s_lo) * (1.0 / DIFF_QK_DIM) + EPS)
    return z * jnp.where(lo, inv_lo, inv_hi) * g


def _in_proj_kernel(x_ref, g1_ref, w_ref, qg_ref, kg_ref, mqg_ref, lru_ref, qkv_ref, mq_ref):
    xn = _rms_rows(x_ref[...], g1_ref[...]).astype(BF16)
    gw = GROUP_WIDTH
    lru_ref[...] = jnp.dot(xn, w_ref[:, 0:2 * gw], preferred_element_type=F32).astype(BF16)
    q = jnp.dot(xn, w_ref[:, 2 * gw:3 * gw], preferred_element_type=F32)
    k = jnp.dot(xn, w_ref[:, 3 * gw:4 * gw], preferred_element_type=F32)
    qscale = DIFF_QK_DIM ** -0.5 * math.log2(math.e)
    for h in range(DIFF_HEADS):
        sl = slice(h * LANES, (h + 1) * LANES)
        qkv_ref[:, sl] = (_half_head_norm(q[:, sl], qg_ref[...]) * qscale).astype(BF16)
        qkv_ref[:, gw + h * LANES:gw + (h + 1) * LANES] = _half_head_norm(k[:, sl], kg_ref[...]).astype(BF16)
    qkv_ref[:, 2 * gw:3 * gw] = jnp.dot(xn, w_ref[:, 4 * gw:5 * gw], preferred_element_type=F32).astype(BF16)
    mq = jnp.dot(xn, w_ref[:, 5 * gw:6 * gw], preferred_element_type=F32)
    for h in range(MEM_HEADS):
        sl = slice(h * MEM_HEAD_DIM, (h + 1) * MEM_HEAD_DIM)
        mq_ref[:, sl] = _rms_rows(mq[:, sl], mqg_ref[...]).astype(BF16)


def _in_proj(x2, g1, w_in, qg, kg, mqg):
    t = x2.shape[0]
    tm = _tile(t, 512)
    row = lambda i: (i, 0)
    fixed = lambda i: (0, 0)
    return pl.pallas_call(
        _in_proj_kernel,
        grid=(t // tm,),
        in_specs=[pl.BlockSpec((tm, D_MODEL), row), pl.BlockSpec((1, D_MODEL), fixed),
                  pl.BlockSpec((D_MODEL, D_IN_PROJ), fixed), pl.BlockSpec((1, LANES), fixed),
                  pl.BlockSpec((1, LANES), fixed), pl.BlockSpec((1, MEM_HEAD_DIM), fixed)],
        out_specs=[pl.BlockSpec((tm, 2 * GROUP_WIDTH), row), pl.BlockSpec((tm, 3 * GROUP_WIDTH), row),
                   pl.BlockSpec((tm, GROUP_WIDTH), row)],
        out_shape=[jax.ShapeDtypeStruct((t, 2 * GROUP_WIDTH), BF16), jax.ShapeDtypeStruct((t, 3 * GROUP_WIDTH), BF16),
                   jax.ShapeDtypeStruct((t, GROUP_WIDTH), BF16)],
        compiler_params=_params("arbitrary"),
        name="in_proj",
    )(x2, g1, w_in, qg, kg, mqg)


def _softplus(z):
    return jnp.maximum(z, 0.0) + jnp.log(1.0 + jnp.exp(-jnp.abs(z)))


def _lru_kernel(u_ref, cw_ref, cb_ref, wa_ref, ba_ref, wx_ref, bx_ref, lam_ref, y_ref,
                ext_ref, hcar_ref, a_ref, b_ref, h_ref):
    ts = y_ref.shape[0]
    gw = GROUP_WIDTH
    hist = SUBLANES

    @pl.when(pl.program_id(1) == 0)
    def _():
        ext_ref[0:hist, :] = jnp.zeros((hist, gw), F32)
        hcar_ref[...] = jnp.zeros_like(hcar_ref)

    @pl.when(pl.program_id(1) != 0)
    def _():
        ext_ref[0:hist, :] = ext_ref[ts:ts + hist, :]

    ext_ref[hist:hist + ts, :] = u_ref[:, 0:gw].astype(F32)
    xc = cb_ref[...] + cw_ref[CONV_WIDTH - 1:CONV_WIDTH, :] * ext_ref[hist:hist + ts, :]
    for j in range(CONV_WIDTH - 1):
        off = hist - (CONV_WIDTH - 1) + j
        xc = xc + cw_ref[j:j + 1, :] * ext_ref[off:off + ts, :]
    xcb = xc.astype(BF16)
    r = jax.nn.sigmoid(jnp.dot(xcb, wa_ref[...], preferred_element_type=F32) + ba_ref[...])
    gate_i = jax.nn.sigmoid(jnp.dot(xcb, wx_ref[...], preferred_element_type=F32) + bx_ref[...])
    log_a = (-LRU_C * r) * _softplus(-lam_ref[...])
    a = jnp.exp(log_a)
    b = jnp.sqrt(1.0 - a * a) * (gate_i * xc)

    row = lax.broadcasted_iota(jnp.int32, (ts, gw), 0) & (SUBLANES - 1)
    d = 1
    while d < SUBLANES:
        keep = row >= d
        a_prev = pltpu.roll(a, d, axis=0)
        b_prev = pltpu.roll(b, d, axis=0)
        b = jnp.where(keep, a * b_prev + b, b)
        a = jnp.where(keep, a * a_prev, a)
        d *= 2
    a_ref[...] = a
    b_ref[...] = b

    def block(i, h):
        off = pl.multiple_of(i * SUBLANES, SUBLANES)
        hb = a_ref[pl.ds(off, SUBLANES), :] * h + b_ref[pl.ds(off, SUBLANES), :]
        h_ref[pl.ds(off, SUBLANES), :] = hb
        return jnp.broadcast_to(hb[SUBLANES - 1:SUBLANES, :], (SUBLANES, gw))

    hcar_ref[...] = lax.fori_loop(0, ts // SUBLANES, block, hcar_ref[...], unroll=8)
    y_ref[...] = (h_ref[...] * jax.nn.gelu(u_ref[:, gw:2 * gw].astype(F32))).astype(BF16)


def _lru(lru_in, b, s, conv_w, conv_b, wa_bd, ba, wx_bd, bx, lam):
    ts = _tile(s, 512)
    ns = s // ts
    gw = GROUP_WIDTH
    fixed = lambda bi, si: (0, 0)
    vec = pl.BlockSpec((1, gw), fixed)
    return pl.pallas_call(
        _lru_kernel,
        grid=(b, ns),
        in_specs=[pl.BlockSpec((ts, 2 * gw), lambda bi, si: (bi * ns + si, 0)),
                  pl.BlockSpec((CONV_WIDTH, gw), fixed), vec,
                  pl.BlockSpec((gw, gw), fixed), vec, pl.BlockSpec((gw, gw), fixed), vec, vec],
        out_specs=pl.BlockSpec((ts, gw), lambda bi, si: (bi * ns + si, 0)),
        out_shape=jax.ShapeDtypeStruct((b * s, gw), BF16),
        scratch_shapes=[pltpu.VMEM((ts + SUBLANES, gw), F32), pltpu.VMEM((SUBLANES, gw), F32),
                        pltpu.VMEM((ts, gw), F32), pltpu.VMEM((ts, gw), F32), pltpu.VMEM((ts, gw), F32)],
        compiler_params=_params("arbitrary", "arbitrary"),
        name="rg_lru",
    )(lru_in, conv_w, conv_b, wa_bd, ba, wx_bd, bx, lam)


ATTN_HEADS_PER_STEP = 2


def _diff_attn_kernel(lam_ref, q_ref, k_ref, v_ref, hg_ref, o_ref, *, t, out_scale):
    qi = pl.program_id(2)
    hd = DIFF_HEAD_DIM
    heads = range(ATTN_HEADS_PER_STEP)
    lo = _lane_iota((t, hd)) < DIFF_QK_DIM
    qs = []
    for h in heads:
        q = q_ref[:, h * hd:(h + 1) * hd]
        zero = jnp.zeros_like(q)
        qs.append(jnp.concatenate([jnp.where(lo, q, zero), jnp.where(lo, zero, q)], axis=0))

    def step(j, carry, masked):
        off = pl.multiple_of(j * t, t)
        out = []
        for h in heads:
            m, l, acc = carry[h]
            kt = k_ref[pl.ds(off, t), h * hd:(h + 1) * hd]
            vt = v_ref[pl.ds(off, t), h * hd:(h + 1) * hd]
            s = lax.dot_general(qs[h], kt, (((1,), (1,)), ((), ())), preferred_element_type=F32)
            if masked:
                r = lax.broadcasted_iota(jnp.int32, (2 * t, t), 0)
                c = lax.broadcasted_iota(jnp.int32, (2 * t, t), 1)
                s = jnp.where(c <= jnp.where(r >= t, r - t, r), s, NEG_BIG)
            m_new = jnp.maximum(m, jnp.max(s, axis=-1, keepdims=True))
            alpha = jnp.exp2(m - m_new)
            p = jnp.exp2(s - m_new)
            l = alpha * l + jnp.sum(p, axis=-1, keepdims=True)
            acc = alpha * acc + jnp.dot(p.astype(BF16), vt, preferred_element_type=F32)
            out.append((m_new, l, acc))
        return tuple(out)

    init = tuple((jnp.full((2 * t, 1), NEG_BIG, F32), jnp.zeros((2 * t, 1), F32),
                  jnp.zeros((2 * t, hd), F32)) for _ in heads)
    carry = lax.fori_loop(0, qi, lambda j, c: step(j, c, False), init)
    carry = step(qi, carry, True)
    for h in heads:
        _, l, acc = carry[h]
        o = acc / l
        o = o[0:t] - lam_ref[0, 0] * o[t:2 * t]
        o_ref[:, h * hd:(h + 1) * hd] = (_rms_rows(o, hg_ref[...]) * out_scale).astype(BF16)


def _diff_attn(qkv, lam, hg, b, s, lam_init):
    t = _tile(s, 512)
    nq = s // t
    hp = ATTN_HEADS_PER_STEP
    w = hp * DIFF_HEAD_DIM
    ng = DIFF_HEADS // hp
    kern = functools.partial(_diff_attn_kernel, t=t, out_scale=1.0 - lam_init)
    return pl.pallas_call(
        kern,
        grid=(b, ng, nq),
        in_specs=[pl.BlockSpec(memory_space=pltpu.SMEM),
                  pl.BlockSpec((t, w), lambda bi, g, qi: (bi * nq + qi, g)),
                  pl.BlockSpec((s, w), lambda bi, g, qi: (bi, ng + g)),
                  pl.BlockSpec((s, w), lambda bi, g, qi: (bi, 2 * ng + g)),
                  pl.BlockSpec((1, DIFF_HEAD_DIM), lambda bi, g, qi: (0, 0))],
        out_specs=pl.BlockSpec((t, w), lambda bi, g, qi: (bi * nq + qi, g)),
        out_shape=jax.ShapeDtypeStruct((b * s, GROUP_WIDTH), BF16),
        compiler_params=_params("arbitrary", "arbitrary", "arbitrary"),
        name="diff_attn",
    )(lam, qkv, qkv, qkv, hg)


def _mem_kv_kernel(mem_ref, mg_ref, w_ref, kg_ref, mk_ref, mv_ref):
    memn = _rms_rows(mem_ref[...], mg_ref[...]).astype(BF16)
    gw = GROUP_WIDTH
    k = jnp.dot(memn, w_ref[:, 0:gw], preferred_element_type=F32)
    for h in range(MEM_HEADS):
        sl = slice(h * MEM_HEAD_DIM, (h + 1) * MEM_HEAD_DIM)
        mk_ref[:, sl] = _rms_rows(k[:, sl], kg_ref[...]).astype(BF16)
    mv_ref[...] = jnp.dot(memn, w_ref[:, gw:2 * gw], preferred_element_type=F32).astype(BF16)


def _mem_kv(mem2, mg, w_kv, kg):
    r = mem2.shape[0]
    tm = _tile(r, 512)
    row = lambda i: (i, 0)
    fixed = lambda i: (0, 0)
    gw = GROUP_WIDTH
    return pl.pallas_call(
        _mem_kv_kernel,
        grid=(r // tm,),
        in_specs=[pl.BlockSpec((tm, D_MODEL), row), pl.BlockSpec((1, D_MODEL), fixed),
                  pl.BlockSpec((D_MODEL, 2 * gw), fixed), pl.BlockSpec((1, MEM_HEAD_DIM), fixed)],
        out_specs=[pl.BlockSpec((tm, gw), row), pl.BlockSpec((tm, gw), row)],
        out_shape=[jax.ShapeDtypeStruct((r, gw), BF16), jax.ShapeDtypeStruct((r, gw), BF16)],
        compiler_params=_params("arbitrary"),
        name="mem_kv",
    )(mem2, mg, w_kv, kg)


def _mem_attn_kernel(q_ref, k_ref, v_ref, o_ref):
    scale = MEM_HEAD_DIM ** -0.5
    for h in range(MEM_HEADS):
        sl = slice(h * MEM_HEAD_DIM, (h + 1) * MEM_HEAD_DIM)
        s = lax.dot_general(q_ref[:, sl], k_ref[:, sl], (((1,), (1,)), ((), ())),
                            preferred_element_type=F32) * scale
        p = jnp.exp(s - jnp.max(s, axis=-1, keepdims=True))
        o = jnp.dot(p.astype(BF16), v_ref[:, sl], preferred_element_type=F32)
        o_ref[:, sl] = (o / jnp.sum(p, axis=-1, keepdims=True)).astype(BF16)


def _mem_attn(mq, mk, mv, b, s, m):
    ts = _tile(s, 512)
    ns = s // ts
    gw = GROUP_WIDTH
    return pl.pallas_call(
        _mem_attn_kernel,
        grid=(b, ns),
        in_specs=[pl.BlockSpec((ts, gw), lambda bi, si: (bi * ns + si, 0)),
                  pl.BlockSpec((m, gw), lambda bi, si: (bi, 0)),
                  pl.BlockSpec((m, gw), lambda bi, si: (bi, 0))],
        out_specs=pl.BlockSpec((ts, gw), lambda bi, si: (bi * ns + si, 0)),
        out_shape=jax.ShapeDtypeStruct((b * s, gw), BF16),
        compiler_params=_params("arbitrary", "arbitrary"),
        name="mem_attn",
    )(mq, mk, mv)


def _out_proj_kernel(x_ref, yl_ref, yd_ref, ym_ref, w_ref, o_ref):
    y = jnp.concatenate([yl_ref[...], yd_ref[...], ym_ref[...]], axis=-1)
    o_ref[...] = x_ref[...] + jnp.dot(y, w_ref[...], preferred_element_type=F32)


def _out_proj(x2, y_lru, y_diff, y_mem, w_out):
    t = x2.shape[0]
    tm = _tile(t, 512)
    row = lambda i: (i, 0)
    gw = GROUP_WIDTH
    return pl.pallas_call(
        _out_proj_kernel,
        grid=(t // tm,),
        in_specs=[pl.BlockSpec((tm, D_MODEL), row), pl.BlockSpec((tm, gw), row), pl.BlockSpec((tm, gw), row),
                  pl.BlockSpec((tm, gw), row), pl.BlockSpec((3 * gw, D_MODEL), lambda i: (0, 0))],
        out_specs=pl.BlockSpec((tm, D_MODEL), row),
        out_shape=jax.ShapeDtypeStruct((t, D_MODEL), F32),
        compiler_params=_params("arbitrary"),
        name="out_proj",
    )(x2, y_lru, y_diff, y_mem, w_out)


MOE_TILE = 256
GRANULE = 16
TILE_SLOTS = MOE_TILE // GRANULE + N_GROUPS
SORTED_ROWS = TILE_SLOTS * GRANULE
STEP_GRANULES = 16
STEP_ROWS = STEP_GRANULES * GRANULE
XS_WIDTH = D_MODEL + LANES
INFO_POS_LANE = 0


def _route_t(lt):
    row_i = lax.broadcasted_iota(jnp.int32, lt.shape, 0)
    row = row_i.astype(F32)
    e0 = ROUTER_EXPERT_LANE0
    lg = jnp.where(row_i < N_GROUPS, lt, NEG_BIG)
    mg = jnp.max(lg, axis=0, keepdims=True)
    g_gate = 1.0 / jnp.sum(jnp.exp(lg - mg), axis=0, keepdims=True)
    g_idx = jnp.min(jnp.where(lg == mg, row, float(LANES)), axis=0, keepdims=True)
    row_group = ((row_i - e0) >> 3).astype(F32)
    sel = (row_i >= e0) & (row_i < e0 + N_EXPERTS) & (row_group == g_idx)
    le = jnp.where(sel, lt, NEG_BIG)
    m1 = jnp.max(le, axis=0, keepdims=True)
    se = jnp.sum(jnp.where(sel, jnp.exp(le - m1), 0.0), axis=0, keepdims=True)
    i1 = jnp.min(jnp.where(sel & (le == m1), row, float(LANES)), axis=0, keepdims=True)
    le2 = jnp.where(row == i1, NEG_BIG, le)
    m2 = jnp.max(le2, axis=0, keepdims=True)
    i2 = jnp.min(jnp.where(sel & (le2 == m2) & (row != i1), row, float(LANES)), axis=0, keepdims=True)
    p1 = 1.0 / se
    p2 = jnp.exp(m2 - m1) / se
    tot = p1 + p2
    w = jnp.where(row == i1, p1 / tot, jnp.where(row == i2, p2 / tot, 0.0))
    return g_gate * w, g_idx


def _split3(c):
    hi = c.astype(BF16).astype(F32)
    r1 = c - hi
    mid = r1.astype(BF16).astype(F32)
    lo = (r1 - mid).astype(BF16).astype(F32)
    return hi, mid, lo


def _moe_sort_kernel(x_ref, g2_ref, rhi_ref, rlo_ref, rb_ref, xs_ref, info_ref, lens_ref):
    tl = x_ref.shape[0]
    xn = _rms_rows(x_ref[...], g2_ref[...])
    hi = xn.astype(BF16)
    lo = (xn - hi.astype(F32)).astype(BF16)
    logits = (jnp.dot(hi, rhi_ref[...], preferred_element_type=F32)
              + jnp.dot(lo, rhi_ref[...], preferred_element_type=F32)
              + jnp.dot(hi, rlo_ref[...], preferred_element_type=F32) + rb_ref[...])
    comb_t, g_idx = _route_t(logits.T)

    grow = lax.broadcasted_iota(jnp.int32, (SUBLANES, tl), 0).astype(F32)
    gt = jnp.where(grow == g_idx, 1.0, 0.0)
    earlier = (lax.broadcasted_iota(jnp.int32, (tl, tl), 0)
               < lax.broadcasted_iota(jnp.int32, (tl, tl), 1)).astype(BF16)
    before = jnp.dot(gt.astype(BF16), earlier, preferred_element_type=F32)
    rank = jnp.sum(gt * before, axis=0, keepdims=True)
    cnt = jnp.sum(gt, axis=1, keepdims=True)
    glen = jnp.floor((cnt + (GRANULE - 1)) * (1.0 / GRANULE))
    r8 = lax.broadcasted_iota(jnp.int32, (SUBLANES, 1), 0)
    start = jnp.zeros((SUBLANES, 1), F32)
    for g in range(1, N_GROUPS):
        start = jnp.where(r8 == g, jnp.sum(jnp.where(r8 < g, glen, 0.0), axis=0, keepdims=True), start)
    pos = jnp.sum(gt * (start * GRANULE), axis=0, keepdims=True) + rank
    lens_ref[...] = jnp.broadcast_to(glen, (SUBLANES, LANES))

    rows = lax.broadcasted_iota(jnp.int32, (LANES, tl), 0)
    info = jnp.where(rows == INFO_POS_LANE, pos, comb_t).T
    info_ref[...] = info
    lane = _lane_iota(info.shape)
    e0 = ROUTER_EXPERT_LANE0
    c_hi, c_mid, c_lo = _split3(jnp.where((lane >= e0) & (lane < e0 + N_EXPERTS), info, 0.0))
    aug = (c_hi + pltpu.roll(c_mid, N_EXPERTS, axis=1) + pltpu.roll(c_lo, 2 * N_EXPERTS, axis=1)).astype(BF16)
    perm = jnp.where(pos == lax.broadcasted_iota(jnp.int32, (SORTED_ROWS, tl), 0).astype(F32),
                     1.0, 0.0).astype(BF16)
    xs_ref[:, 0:D_MODEL] = jnp.dot(perm, hi, preferred_element_type=F32).astype(BF16)
    xs_ref[:, D_MODEL:XS_WIDTH] = jnp.dot(perm, aug, preferred_element_type=F32).astype(BF16)


def _moe_sort(x2, g2, r_hi, r_lo, r_b):
    t = x2.shape[0]
    assert t % MOE_TILE == 0
    nt = t // MOE_TILE
    row = lambda i: (i, 0)
    fixed = lambda i: (0, 0)
    return pl.pallas_call(
        _moe_sort_kernel,
        grid=(nt,),
        in_specs=[pl.BlockSpec((MOE_TILE, D_MODEL), row), pl.BlockSpec((1, D_MODEL), fixed),
                  pl.BlockSpec((D_MODEL, LANES), fixed), pl.BlockSpec((D_MODEL, LANES), fixed),
                  pl.BlockSpec((1, LANES), fixed)],
        out_specs=[pl.BlockSpec((SORTED_ROWS, XS_WIDTH), row), pl.BlockSpec((MOE_TILE, LANES), row),
                   pl.BlockSpec((None, SUBLANES, LANES), lambda i: (i, 0, 0))],
        out_shape=[jax.ShapeDtypeStruct((nt * SORTED_ROWS, XS_WIDTH), BF16),
                   jax.ShapeDtypeStruct((t, LANES), F32),
                   jax.ShapeDtypeStruct((nt, SUBLANES, LANES), F32)],
        compiler_params=_params("arbitrary"),
        name="moe_sort",
    )(x2, g2, r_hi, r_lo, r_b)


def _moe_expert_kernel(sg_ref, sv_ref, gi_ref, *refs):
    del gi_ref
    xs_refs = refs[:STEP_GRANULES]
    wg_ref, wu_ref, wd_ref, y_ref = refs[STEP_GRANULES:]
    s = pl.program_id(0)

    @pl.when(sv_ref[s] == 0)
    def _():
        y_ref[...] = jnp.zeros_like(y_ref)

    @pl.when(sv_ref[s] != 0)
    def _():
        rows = jnp.concatenate([r[...] for r in xs_refs], axis=0)
        x = rows[:, 0:D_MODEL]
        aug = rows[:, D_MODEL:XS_WIDTH].astype(F32)
        lane = _lane_iota(aug.shape)
        e0 = ROUTER_EXPERT_LANE0
        comb = jnp.where((lane >= e0) & (lane < e0 + N_EXPERTS),
                         aug + pltpu.roll(aug, LANES - N_EXPERTS, axis=1)
                         + pltpu.roll(aug, LANES - 2 * N_EXPERTS, axis=1), 0.0)
        first = e0 + sg_ref[s] * EXPERTS_PER_GROUP
        hs = []
        for e in range(EXPERTS_PER_GROUP):
            cw = jnp.sum(jnp.where(lane == first + e, comb, 0.0), axis=-1, keepdims=True)
            hg = jnp.dot(x, wg_ref[e], preferred_element_type=F32)
            hu = jnp.dot(x, wu_ref[e], preferred_element_type=F32)
            hs.append((jax.nn.silu(hg) * hu * cw).astype(BF16))
        h = jnp.concatenate(hs, axis=-1)
        y_ref[...] = jnp.dot(h, wd_ref[...], preferred_element_type=F32).astype(BF16)


def _granule_map(k, s, sg, sv, gi):
    return (gi[s * STEP_GRANULES + k], 0)


def _moe_experts(step_group, step_valid, gran_idx, xs, wg, wu, wd):
    nstep = step_group.shape[0]
    epg = EXPERTS_PER_GROUP
    wmap4 = lambda s, sg, sv, gi: (sg[s], 0, 0, 0)
    in_specs = [pl.BlockSpec((GRANULE, XS_WIDTH), functools.partial(_granule_map, k)) for k in range(STEP_GRANULES)]
    in_specs += [pl.BlockSpec((None, epg, D_MODEL, D_EXPERT), wmap4),
                 pl.BlockSpec((None, epg, D_MODEL, D_EXPERT), wmap4),
                 pl.BlockSpec((None, epg * D_EXPERT, D_MODEL), lambda s, sg, sv, gi: (sg[s], 0, 0))]
    return pl.pallas_call(
        _moe_expert_kernel,
        grid_spec=pltpu.PrefetchScalarGridSpec(
            num_scalar_prefetch=3, grid=(nstep,), in_specs=in_specs,
            out_specs=pl.BlockSpec((STEP_ROWS, D_MODEL), lambda s, sg, sv, gi: (s, 0))),
        out_shape=jax.ShapeDtypeStruct((nstep * STEP_ROWS, D_MODEL), BF16),
        compiler_params=_params("arbitrary"),
        name="moe_experts",
    )(step_group, step_valid, gran_idx, *([xs] * STEP_GRANULES), wg, wu, wd)


def _moe_combine_kernel(inv_ref, x_ref, info_ref, *refs):
    del inv_ref
    ys_refs = refs[:TILE_SLOTS]
    o_ref = refs[TILE_SLOTS]
    kpad = -SORTED_ROWS % LANES
    ys = jnp.concatenate([r[...] for r in ys_refs] + [jnp.zeros((kpad, D_MODEL), BF16)], axis=0)
    pos = info_ref[:, INFO_POS_LANE:INFO_POS_LANE + 1]
    unperm = jnp.where(pos == _lane_iota((MOE_TILE, SORTED_ROWS + kpad)).astype(F32), 1.0, 0.0).astype(BF16)
    o_ref[...] = x_ref[...] + jnp.dot(unperm, ys, preferred_element_type=F32)


def _slot_map(q, i, inv):
    return (inv[i * TILE_SLOTS + q], 0)


def _moe_combine(inv, x2, info, ys):
    t = x2.shape[0]
    nt = t // MOE_TILE
    row = lambda i, inv: (i, 0)
    in_specs = [pl.BlockSpec((MOE_TILE, D_MODEL), row), pl.BlockSpec((MOE_TILE, LANES), row)]
    in_specs += [pl.BlockSpec((GRANULE, D_MODEL), functools.partial(_slot_map, q)) for q in range(TILE_SLOTS)]
    return pl.pallas_call(
        _moe_combine_kernel,
        grid_spec=pltpu.PrefetchScalarGridSpec(
            num_scalar_prefetch=1, grid=(nt,), in_specs=in_specs,
            out_specs=pl.BlockSpec((MOE_TILE, D_MODEL), row)),
        out_shape=jax.ShapeDtypeStruct((t, D_MODEL), F32),
        compiler_params=_params("arbitrary"),
        name="moe_combine",
    )(inv, x2, info, *([ys] * TILE_SLOTS))


def _moe_tables(lens, nt):
    i32 = jnp.int32
    start = jnp.cumsum(lens, axis=1) - lens
    base = jnp.arange(nt, dtype=i32)[:, None] * TILE_SLOTS + start
    run_len = lens.T.reshape(-1)
    run_base = base.T.reshape(-1)
    run_end = jnp.cumsum(run_len)
    run_start = run_end - run_len
    n_g = jnp.sum(lens, axis=0)
    steps_g = (n_g + STEP_GRANULES - 1) // STEP_GRANULES
    step_end = jnp.cumsum(steps_g)
    step_off = step_end - steps_g
    gran_off = jnp.cumsum(n_g) - n_g
    nstep = (nt * (TILE_SLOTS - 1) + STEP_GRANULES - 1) // STEP_GRANULES + N_GROUPS
    s = jnp.arange(nstep, dtype=i32)
    count_le = lambda ends, v: jnp.sum((v[..., None] >= ends).astype(i32), axis=-1)
    sg = jnp.minimum(count_le(step_end, s), N_GROUPS - 1)
    sv = (s < step_end[-1]).astype(i32)
    jl = (s - step_off[sg])[:, None] * STEP_GRANULES + jnp.arange(STEP_GRANULES, dtype=i32)[None, :]
    ok = (sv[:, None] > 0) & (jl < n_g[sg][:, None])
    j = jnp.where(ok, gran_off[sg][:, None] + jl, 0)
    r = jnp.minimum(count_le(run_end, j), run_len.shape[0] - 1)
    gran = run_base[r] + (j - run_start[r])
    gran = jnp.where(ok, gran, jnp.where(sv[:, None] > 0, gran[:, 0:1], 0)).astype(i32)
    q = jnp.arange(TILE_SLOTS, dtype=i32)[None, :]
    cum = jnp.cumsum(lens, axis=1)
    gq = jnp.minimum(jnp.sum((q[:, :, None] >= cum[:, None, :]).astype(i32), axis=-1), N_GROUPS - 1)
    used = q < cum[:, -1:]
    ridx = gq * nt + jnp.arange(nt, dtype=i32)[:, None]
    jloc = run_start[ridx] + (q - jnp.take_along_axis(start, gq, axis=1)) - gran_off[gq]
    inv = jnp.where(used, step_off[gq] * STEP_GRANULES + jloc, 0).astype(i32)
    return sg, sv, gran.reshape(-1), inv.reshape(-1)


def _moe(x2, g2, r_hi, r_lo, r_b, wg, wu, wd):
    nt = x2.shape[0] // MOE_TILE
    xs, info, lens = _moe_sort(x2, g2, r_hi, r_lo, r_b)
    sg, sv, gran, inv = _moe_tables(lens[:, 0:N_GROUPS, 0].astype(jnp.int32), nt)
    ys = _moe_experts(sg, sv, gran, xs, wg, wu, wd)
    return _moe_combine(inv, x2, info, ys)


def _block_diag(w):
    h, n, _ = w.shape
    eye = jnp.eye(h, dtype=w.dtype)
    return (eye[:, None, :, None] * w[:, :, None, :]).reshape(h * n, h * n)


def _router_tables(w_rg, b_rg, w_re, b_re):
    e0 = ROUTER_EXPERT_LANE0
    w = jnp.zeros((D_MODEL, LANES), F32).at[:, 0:N_GROUPS].set(w_rg).at[:, e0:e0 + N_EXPERTS].set(w_re)
    b = jnp.zeros((1, LANES), F32).at[0, 0:N_GROUPS].set(b_rg).at[0, e0:e0 + N_EXPERTS].set(b_re)
    hi = w.astype(BF16)
    lo = (w - hi.astype(F32)).astype(BF16)
    return hi, lo, b


def kernel(x, mem, norm1_g, w_in, conv_w, conv_b, rg_wa, rg_ba, rg_wx, rg_bx, rg_lambda, dq_norm_g, dk_norm_g, lambda_q1, lambda_k1, lambda_q2, lambda_k2, diff_head_norm_g, mem_norm_g, w_mem_kv, mq_norm_g, mk_norm_g, w_out, norm2_g, w_router_group, b_router_group, w_router_expert, b_router_expert, w_expert_gate, w_expert_up, w_expert_down):
    b, s, d = x.shape
    m = mem.shape[1]
    depth = w_in.shape[0]
    x2 = x.reshape(b * s, d)
    mem2 = mem.reshape(b * m, d)
    vec = lambda v: v.reshape(1, -1).astype(F32)
    for l in range(depth):
        lam_init = 0.8 - 0.6 * math.exp(-0.3 * l)
        lam = (jnp.exp(jnp.sum(lambda_q1[l] * lambda_k1[l])) - jnp.exp(jnp.sum(lambda_q2[l] * lambda_k2[l]))
               + lam_init).reshape(1, 1).astype(F32)
        lru_in, qkv, mq = _in_proj(x2, vec(norm1_g[l]), w_in[l].astype(BF16),
                                   vec(jnp.tile(dq_norm_g[l], 2)), vec(jnp.tile(dk_norm_g[l], 2)), vec(mq_norm_g[l]))
        y_lru = _lru(lru_in, b, s, conv_w[l], vec(conv_b[l]), _block_diag(rg_wa[l]).astype(BF16), vec(rg_ba[l]),
                     _block_diag(rg_wx[l]).astype(BF16), vec(rg_bx[l]), vec(rg_lambda[l]))
        y_diff = _diff_attn(qkv, lam, vec(diff_head_norm_g[l]), b, s, lam_init)
        mk, mv = _mem_kv(mem2, vec(mem_norm_g), w_mem_kv[l].astype(BF16), vec(mk_norm_g[l]))
        y_mem = _mem_attn(mq, mk, mv, b, s, m)
        x2 = _out_proj(x2, y_lru, y_diff, y_mem, w_out[l].astype(BF16))
        r_hi, r_lo, r_b = _router_tables(w_router_group[l], b_router_group[l], w_router_expert[l], b_router_expert[l])
        epg = EXPERTS_PER_GROUP
        x2 = _moe(x2, vec(norm2_g[l]), r_hi, r_lo, r_b,
                  w_expert_gate[l].astype(BF16).reshape(N_GROUPS, epg, D_MODEL, D_EXPERT),
                  w_expert_up[l].astype(BF16).reshape(N_GROUPS, epg, D_MODEL, D_EXPERT),
                  w_expert_down[l].astype(BF16).reshape(N_GROUPS, epg * D_EXPERT, D_MODEL))
    return x2.reshape(b, s, d)
```

```python
import functools
import math

import jax
import jax.numpy as jnp
from jax import lax
from jax.experimental import pallas as pl
from jax.experimental.pallas import tpu as pltpu

F32 = jnp.float32
BF16 = jnp.bfloat16

D_MODEL = 1024
GROUP_WIDTH = D_MODEL // 2
LRU_HEADS = 8
CONV_WIDTH = 4
LRU_C = 8.0
DIFF_HEADS = 4
DIFF_HEAD_DIM = GROUP_WIDTH // DIFF_HEADS
DIFF_QK_DIM = DIFF_HEAD_DIM // 2
MEM_HEADS = 4
MEM_HEAD_DIM = GROUP_WIDTH // MEM_HEADS
N_GROUPS = 4
EXPERTS_PER_GROUP = 8
N_EXPERTS = N_GROUPS * EXPERTS_PER_GROUP
D_EXPERT = 256
D_IN_PROJ = 6 * GROUP_WIDTH
EPS = 1e-6

LANES = 128
SUBLANES = 8
VMEM_LIMIT = 56 * 1024 * 1024
NEG_BIG = -1e30
ROUTER_EXPERT_LANE0 = 32


def _tile(n, pref):
    t = min(n, pref)
    assert n % t == 0, (n, t)
    return t


def _params(*sem):
    return pltpu.CompilerParams(dimension_semantics=sem, vmem_limit_bytes=VMEM_LIMIT)


def _rms_rows(x, g):
    return x * lax.rsqrt(jnp.mean(x * x, axis=-1, keepdims=True) + EPS) * g


def _lane_iota(shape):
    return lax.broadcasted_iota(jnp.int32, shape, len(shape) - 1)


def _half_head_norm(z, g):
    sq = z * z
    lo = _lane_iota(z.shape) < DIFF_QK_DIM
    s_all = jnp.sum(sq, axis=-1, keepdims=True)
    s_lo = jnp.sum(jnp.where(lo, sq, 0.0), axis=-1, keepdims=True)
    inv_lo = lax.rsqrt(s_lo * (1.0 / DIFF_QK_DIM) + EPS)
    inv_hi = lax.rsqrt((s_all - s_lo) * (1.0 / DIFF_QK_DIM) + EPS)
    return z * jnp.where(lo, inv_lo, inv_hi) * g


SEQ_TILE = 512
_NT = (((1,), (1,)), ((), ()))


def _in_proj_kernel(x_ref, g1_ref, w_ref, wqt_ref, wvt_ref, qg_ref, kg_ref, mqg_ref,
                    lru_ref, k_ref, mq_ref, qt_ref, vt_ref):
    xn = _rms_rows(x_ref[...], g1_ref[...]).astype(BF16)
    gw = GROUP_WIDTH
    tm = xn.shape[0]
    lru_ref[...] = jnp.dot(xn, w_ref[:, 0:2 * gw], preferred_element_type=F32).astype(BF16)
    k = jnp.dot(xn, w_ref[:, 2 * gw:3 * gw], preferred_element_type=F32)
    for h in range(DIFF_HEADS):
        sl = slice(h * LANES, (h + 1) * LANES)
        k_ref[:, sl] = _half_head_norm(k[:, sl], kg_ref[...]).astype(BF16)
    mq = jnp.dot(xn, w_ref[:, 3 * gw:4 * gw], preferred_element_type=F32)
    for h in range(MEM_HEADS):
        sl = slice(h * MEM_HEAD_DIM, (h + 1) * MEM_HEAD_DIM)
        mq_ref[:, sl] = _rms_rows(mq[:, sl], mqg_ref[...]).astype(BF16)
    qt = lax.dot_general(wqt_ref[...], xn, _NT, preferred_element_type=F32)
    q3 = qt.reshape(gw // DIFF_QK_DIM, DIFF_QK_DIM, tm)
    q3 = q3 * lax.rsqrt(jnp.mean(q3 * q3, axis=1, keepdims=True) + EPS)
    qscale = DIFF_QK_DIM ** -0.5 * math.log2(math.e)
    qt_ref[...] = (q3.reshape(gw, tm) * (qg_ref[...] * qscale)).astype(BF16)
    vt_ref[...] = lax.dot_general(wvt_ref[...], xn, _NT, preferred_element_type=F32).astype(BF16)


def _in_proj(x2, b, s, g1, w_rows, wqt, wvt, qg_col, kg, mqg):
    t = x2.shape[0]
    tm = SEQ_TILE
    assert s % tm == 0
    ns = s // tm
    gw = GROUP_WIDTH
    row = lambda i: (i, 0)
    fixed = lambda i: (0, 0)
    fm = lambda i: (i // ns, i % ns, 0, 0)
    return pl.pallas_call(
        _in_proj_kernel,
        grid=(t // tm,),
        in_specs=[pl.BlockSpec((tm, D_MODEL), row), pl.BlockSpec((1, D_MODEL), fixed),
                  pl.BlockSpec((D_MODEL, 4 * gw), fixed), pl.BlockSpec((gw, D_MODEL), fixed),
                  pl.BlockSpec((gw, D_MODEL), fixed), pl.BlockSpec((gw, 1), fixed),
                  pl.BlockSpec((1, LANES), fixed), pl.BlockSpec((1, MEM_HEAD_DIM), fixed)],
        out_specs=[pl.BlockSpec((tm, 2 * gw), row), pl.BlockSpec((tm, gw), row), pl.BlockSpec((tm, gw), row),
                   pl.BlockSpec((None, None, gw, tm), fm), pl.BlockSpec((None, None, gw, tm), fm)],
        out_shape=[jax.ShapeDtypeStruct((t, 2 * gw), BF16), jax.ShapeDtypeStruct((t, gw), BF16),
                   jax.ShapeDtypeStruct((t, gw), BF16), jax.ShapeDtypeStruct((b, ns, gw, tm), BF16),
                   jax.ShapeDtypeStruct((b, ns, gw, tm), BF16)],
        compiler_params=_params("arbitrary"),
        name="in_proj",
    )(x2, g1, w_rows, wqt, wvt, qg_col, kg, mqg)


def _softplus(z):
    return jnp.maximum(z, 0.0) + jnp.log(1.0 + jnp.exp(-jnp.abs(z)))


def _lru_kernel(u_ref, cw_ref, cb_ref, wa_ref, ba_ref, wx_ref, bx_ref, lam_ref, y_ref,
                ext_ref, hcar_ref, a_ref, b_ref, h_ref):
    ts = y_ref.shape[0]
    gw = GROUP_WIDTH
    hist = SUBLANES

    @pl.when(pl.program_id(1) == 0)
    def _():
        ext_ref[0:hist, :] = jnp.zeros((hist, gw), F32)
        hcar_ref[...] = jnp.zeros_like(hcar_ref)

    @pl.when(pl.program_id(1) != 0)
    def _():
        ext_ref[0:hist, :] = ext_ref[ts:ts + hist, :]

    ext_ref[hist:hist + ts, :] = u_ref[:, 0:gw].astype(F32)
    xc = cb_ref[...] + cw_ref[CONV_WIDTH - 1:CONV_WIDTH, :] * ext_ref[hist:hist + ts, :]
    for j in range(CONV_WIDTH - 1):
        off = hist - (CONV_WIDTH - 1) + j
        xc = xc + cw_ref[j:j + 1, :] * ext_ref[off:off + ts, :]
    xcb = xc.astype(BF16)
    r = jax.nn.sigmoid(jnp.dot(xcb, wa_ref[...], preferred_element_type=F32) + ba_ref[...])
    gate_i = jax.nn.sigmoid(jnp.dot(xcb, wx_ref[...], preferred_element_type=F32) + bx_ref[...])
    log_a = (-LRU_C * r) * _softplus(-lam_ref[...])
    a = jnp.exp(log_a)
    b = jnp.sqrt(1.0 - a * a) * (gate_i * xc)

    row = lax.broadcasted_iota(jnp.int32, (ts, gw), 0) & (SUBLANES - 1)
    d = 1
    while d < SUBLANES:
        keep = row >= d
        a_prev = pltpu.roll(a, d, axis=0)
        b_prev = pltpu.roll(b, d, axis=0)
        b = jnp.where(keep, a * b_prev + b, b)
        a = jnp.where(keep, a * a_prev, a)
        d *= 2
    a_ref[...] = a
    b_ref[...] = b

    def block(i, h):
        off = pl.multiple_of(i * SUBLANES, SUBLANES)
        hb = a_ref[pl.ds(off, SUBLANES), :] * h + b_ref[pl.ds(off, SUBLANES), :]
        h_ref[pl.ds(off, SUBLANES), :] = hb
        return jnp.broadcast_to(hb[SUBLANES - 1:SUBLANES, :], (SUBLANES, gw))

    hcar_ref[...] = lax.fori_loop(0, ts // SUBLANES, block, hcar_ref[...], unroll=8)
    y_ref[...] = (h_ref[...] * jax.nn.gelu(u_ref[:, gw:2 * gw].astype(F32))).astype(BF16)


def _lru(lru_in, b, s, conv_w, conv_b, wa_bd, ba, wx_bd, bx, lam):
    ts = _tile(s, 512)
    ns = s // ts
    gw = GROUP_WIDTH
    fixed = lambda bi, si: (0, 0)
    vec = pl.BlockSpec((1, gw), fixed)
    return pl.pallas_call(
        _lru_kernel,
        grid=(b, ns),
        in_specs=[pl.BlockSpec((ts, 2 * gw), lambda bi, si: (bi * ns + si, 0)),
                  pl.BlockSpec((CONV_WIDTH, gw), fixed), vec,
                  pl.BlockSpec((gw, gw), fixed), vec, pl.BlockSpec((gw, gw), fixed), vec, vec],
        out_specs=pl.BlockSpec((ts, gw), lambda bi, si: (bi * ns + si, 0)),
        out_shape=jax.ShapeDtypeStruct((b * s, gw), BF16),
        scratch_shapes=[pltpu.VMEM((ts + SUBLANES, gw), F32), pltpu.VMEM((SUBLANES, gw), F32),
                        pltpu.VMEM((ts, gw), F32), pltpu.VMEM((ts, gw), F32), pltpu.VMEM((ts, gw), F32)],
        compiler_params=_params("arbitrary", "arbitrary"),
        name="rg_lru",
    )(lru_in, conv_w, conv_b, wa_bd, ba, wx_bd, bx, lam)


ATTN_HEADS_PER_STEP = 2


def _diff_attn_kernel(lam_ref, qt_ref, k_ref, vt_ref, hg_ref, o_ref,
                      qs_ref, sa_ref, sb_ref, m_ref, l_ref, acc_ref, *, t, out_scale):
    qi = pl.program_id(2)
    hd = DIFF_HEAD_DIM
    heads = range(ATTN_HEADS_PER_STEP)
    lo = lax.broadcasted_iota(jnp.int32, (hd, t), 0) < DIFF_QK_DIM
    for h in heads:
        q = qt_ref[h * hd:(h + 1) * hd, :]
        zero = jnp.zeros_like(q)
        qs_ref[h, :, 0:t] = jnp.where(lo, q, zero)
        qs_ref[h, :, t:2 * t] = jnp.where(lo, zero, q)
    m_ref[...] = jnp.full(m_ref.shape, NEG_BIG, F32)
    l_ref[...] = jnp.zeros(l_ref.shape, F32)
    acc_ref[...] = jnp.zeros(acc_ref.shape, F32)

    def scores(j, dst_ref):
        off = pl.multiple_of(j * t, t)
        for h in heads:
            dst_ref[h] = jnp.dot(k_ref[pl.ds(off, t), h * hd:(h + 1) * hd], qs_ref[h],
                                 preferred_element_type=F32)

    def update(j, src_ref, masked):
        for h in heads:
            s = src_ref[h]
            if masked:
                kpos = lax.broadcasted_iota(jnp.int32, (t, 2 * t), 0)
                c = lax.broadcasted_iota(jnp.int32, (t, 2 * t), 1)
                s = jnp.where(kpos <= jnp.where(c >= t, c - t, c), s, NEG_BIG)
            m = m_ref[h]
            m_new = jnp.maximum(m, jnp.max(s, axis=0, keepdims=True))
            alpha = jnp.exp2(m - m_new)
            p = jnp.exp2(s - m_new)
            m_ref[h] = m_new
            l_ref[h] = alpha * l_ref[h] + jnp.sum(p, axis=0, keepdims=True)
            vt = vt_ref[j, h * hd:(h + 1) * hd, :]
            acc_ref[h] = alpha * acc_ref[h] + jnp.dot(vt, p.astype(BF16), preferred_element_type=F32)

    scores(0, sa_ref)

    def pair(i, carry):
        scores(2 * i + 1, sb_ref)
        update(2 * i, sa_ref, False)
        scores(2 * i + 2, sa_ref)
        update(2 * i + 1, sb_ref, False)
        return carry

    lax.fori_loop(0, qi >> 1, pair, 0)

    @pl.when((qi & 1) == 1)
    def _():
        scores(qi, sb_ref)
        update(qi - 1, sa_ref, False)
        update(qi, sb_ref, True)

    @pl.when((qi & 1) == 0)
    def _():
        update(qi, sa_ref, True)

    for h in heads:
        o = acc_ref[h] / l_ref[h]
        o = o[:, 0:t] - lam_ref[0, 0] * o[:, t:2 * t]
        o = o * lax.rsqrt(jnp.mean(o * o, axis=0, keepdims=True) + EPS) * (hg_ref[...] * out_scale)
        o_ref[:, h * hd:(h + 1) * hd] = o.T.astype(BF16)


def _diff_attn(qt, k, vt, lam, hg_col, b, s, lam_init):
    t = SEQ_TILE
    nq = s // t
    hp = ATTN_HEADS_PER_STEP
    w = hp * DIFF_HEAD_DIM
    ng = DIFF_HEADS // hp
    kern = functools.partial(_diff_attn_kernel, t=t, out_scale=1.0 - lam_init)
    return pl.pallas_call(
        kern,
        grid=(b, ng, nq),
        in_specs=[pl.BlockSpec(memory_space=pltpu.SMEM),
                  pl.BlockSpec((None, None, w, t), lambda bi, g, qi: (bi, qi, g, 0)),
                  pl.BlockSpec((s, w), lambda bi, g, qi: (bi, g)),
                  pl.BlockSpec((None, nq, w, t), lambda bi, g, qi: (bi, 0, g, 0)),
                  pl.BlockSpec((DIFF_HEAD_DIM, 1), lambda bi, g, qi: (0, 0))],
        out_specs=pl.BlockSpec((t, w), lambda bi, g, qi: (bi * nq + qi, g)),
        out_shape=jax.ShapeDtypeStruct((b * s, GROUP_WIDTH), BF16),
        scratch_shapes=[pltpu.VMEM((hp, DIFF_HEAD_DIM, 2 * t), BF16),
                        pltpu.VMEM((hp, t, 2 * t), F32), pltpu.VMEM((hp, t, 2 * t), F32),
                        pltpu.VMEM((hp, 1, 2 * t), F32), pltpu.VMEM((hp, 1, 2 * t), F32),
                        pltpu.VMEM((hp, DIFF_HEAD_DIM, 2 * t), F32)],
        compiler_params=_params("arbitrary", "arbitrary", "arbitrary"),
        name="diff_attn",
    )(lam, qt, k, vt, hg_col)


def _mem_kv_kernel(mem_ref, mg_ref, w_ref, kg_ref, mk_ref, mv_ref):
    memn = _rms_rows(mem_ref[...], mg_ref[...]).astype(BF16)
    gw = GROUP_WIDTH
    k = jnp.dot(memn, w_ref[:, 0:gw], preferred_element_type=F32)
    for h in range(MEM_HEADS):
        sl = slice(h * MEM_HEAD_DIM, (h + 1) * MEM_HEAD_DIM)
        mk_ref[:, sl] = _rms_rows(k[:, sl], kg_ref[...]).astype(BF16)
    mv_ref[...] = jnp.dot(memn, w_ref[:, gw:2 * gw], preferred_element_type=F32).astype(BF16)


def _mem_kv(mem2, mg, w_kv, kg):
    r = mem2.shape[0]
    tm = _tile(r, 512)
    row = lambda i: (i, 0)
    fixed = lambda i: (0, 0)
    gw = GROUP_WIDTH
    return pl.pallas_call(
        _mem_kv_kernel,
        grid=(r // tm,),
        in_specs=[pl.BlockSpec((tm, D_MODEL), row), pl.BlockSpec((1, D_MODEL), fixed),
                  pl.BlockSpec((D_MODEL, 2 * gw), fixed), pl.BlockSpec((1, MEM_HEAD_DIM), fixed)],
        out_specs=[pl.BlockSpec((tm, gw), row), pl.BlockSpec((tm, gw), row)],
        out_shape=[jax.ShapeDtypeStruct((r, gw), BF16), jax.ShapeDtypeStruct((r, gw), BF16)],
        compiler_params=_params("arbitrary"),
        name="mem_kv",
    )(mem2, mg, w_kv, kg)


def _mem_attn_kernel(q_ref, k_ref, v_ref, o_ref):
    scale = MEM_HEAD_DIM ** -0.5
    for h in range(MEM_HEADS):
        sl = slice(h * MEM_HEAD_DIM, (h + 1) * MEM_HEAD_DIM)
        s = lax.dot_general(q_ref[:, sl], k_ref[:, sl], (((1,), (1,)), ((), ())),
                            preferred_element_type=F32) * scale
        p = jnp.exp(s - jnp.max(s, axis=-1, keepdims=True))
        o = jnp.dot(p.astype(BF16), v_ref[:, sl], preferred_element_type=F32)
        o_ref[:, sl] = (o / jnp.sum(p, axis=-1, keepdims=True)).astype(BF16)


def _mem_attn(mq, mk, mv, b, s, m):
    ts = _tile(s, 512)
    ns = s // ts
    gw = GROUP_WIDTH
    return pl.pallas_call(
        _mem_attn_kernel,
        grid=(b, ns),
        in_specs=[pl.BlockSpec((ts, gw), lambda bi, si: (bi * ns + si, 0)),
                  pl.BlockSpec((m, gw), lambda bi, si: (bi, 0)),
                  pl.BlockSpec((m, gw), lambda bi, si: (bi, 0))],
        out_specs=pl.BlockSpec((ts, gw), lambda bi, si: (bi * ns + si, 0)),
        out_shape=jax.ShapeDtypeStruct((b * s, gw), BF16),
        compiler_params=_params("arbitrary", "arbitrary"),
        name="mem_attn",
    )(mq, mk, mv)


def _out_proj_kernel(x_ref, yl_ref, yd_ref, ym_ref, w_ref, o_ref):
    y = jnp.concatenate([yl_ref[...], yd_ref[...], ym_ref[...]], axis=-1)
    o_ref[...] = x_ref[...] + jnp.dot(y, w_ref[...], preferred_element_type=F32)


def _out_proj(x2, y_lru, y_diff, y_mem, w_out):
    t = x2.shape[0]
    tm = _tile(t, 512)
    row = lambda i: (i, 0)
    gw = GROUP_WIDTH
    return pl.pallas_call(
        _out_proj_kernel,
        grid=(t // tm,),
        in_specs=[pl.BlockSpec((tm, D_MODEL), row), pl.BlockSpec((tm, gw), row), pl.BlockSpec((tm, gw), row),
                  pl.BlockSpec((tm, gw), row), pl.BlockSpec((3 * gw, D_MODEL), lambda i: (0, 0))],
        out_specs=pl.BlockSpec((tm, D_MODEL), row),
        out_shape=jax.ShapeDtypeStruct((t, D_MODEL), F32),
        compiler_params=_params("arbitrary"),
        name="out_proj",
    )(x2, y_lru, y_diff, y_mem, w_out)


MOE_TILE = 256
GRANULE = 16
TILE_SLOTS = MOE_TILE // GRANULE + N_GROUPS
SORTED_ROWS = TILE_SLOTS * GRANULE
STEP_GRANULES = 16
STEP_ROWS = STEP_GRANULES * GRANULE
XS_WIDTH = D_MODEL + LANES
INFO_POS_LANE = 0


def _route_t(lt):
    row_i = lax.broadcasted_iota(jnp.int32, lt.shape, 0)
    row = row_i.astype(F32)
    e0 = ROUTER_EXPERT_LANE0
    lg = jnp.where(row_i < N_GROUPS, lt, NEG_BIG)
    mg = jnp.max(lg, axis=0, keepdims=True)
    g_gate = 1.0 / jnp.sum(jnp.exp(lg - mg), axis=0, keepdims=True)
    g_idx = jnp.min(jnp.where(lg == mg, row, float(LANES)), axis=0, keepdims=True)
    row_group = ((row_i - e0) >> 3).astype(F32)
    sel = (row_i >= e0) & (row_i < e0 + N_EXPERTS) & (row_group == g_idx)
    le = jnp.where(sel, lt, NEG_BIG)
    m1 = jnp.max(le, axis=0, keepdims=True)
    se = jnp.sum(jnp.where(sel, jnp.exp(le - m1), 0.0), axis=0, keepdims=True)
    i1 = jnp.min(jnp.where(sel & (le == m1), row, float(LANES)), axis=0, keepdims=True)
    le2 = jnp.where(row == i1, NEG_BIG, le)
    m2 = jnp.max(le2, axis=0, keepdims=True)
    i2 = jnp.min(jnp.where(sel & (le2 == m2) & (row != i1), row, float(LANES)), axis=0, keepdims=True)
    p1 = 1.0 / se
    p2 = jnp.exp(m2 - m1) / se
    tot = p1 + p2
    w = jnp.where(row == i1, p1 / tot, jnp.where(row == i2, p2 / tot, 0.0))
    return g_gate * w, g_idx


def _split3(c):
    hi = c.astype(BF16).astype(F32)
    r1 = c - hi
    mid = r1.astype(BF16).astype(F32)
    lo = (r1 - mid).astype(BF16).astype(F32)
    return hi, mid, lo


def _moe_sort_kernel(x_ref, g2_ref, rhi_ref, rlo_ref, rb_ref, xs_ref, info_ref, lens_ref):
    tl = x_ref.shape[0]
    xn = _rms_rows(x_ref[...], g2_ref[...])
    hi = xn.astype(BF16)
    lo = (xn - hi.astype(F32)).astype(BF16)
    logits = (jnp.dot(hi, rhi_ref[...], preferred_element_type=F32)
              + jnp.dot(lo, rhi_ref[...], preferred_element_type=F32)
              + jnp.dot(hi, rlo_ref[...], preferred_element_type=F32) + rb_ref[...])
    comb_t, g_idx = _route_t(logits.T)

    grow = lax.broadcasted_iota(jnp.int32, (SUBLANES, tl), 0).astype(F32)
    gt = jnp.where(grow == g_idx, 1.0, 0.0)
    earlier = (lax.broadcasted_iota(jnp.int32, (tl, tl), 0)
               < lax.broadcasted_iota(jnp.int32, (tl, tl), 1)).astype(BF16)
    before = jnp.dot(gt.astype(BF16), earlier, preferred_element_type=F32)
    rank = jnp.sum(gt * before, axis=0, keepdims=True)
    cnt = jnp.sum(gt, axis=1, keepdims=True)
    glen = jnp.floor((cnt + (GRANULE - 1)) * (1.0 / GRANULE))
    r8 = lax.broadcasted_iota(jnp.int32, (SUBLANES, 1), 0)
    start = jnp.zeros((SUBLANES, 1), F32)
    for g in range(1, N_GROUPS):
        start = jnp.where(r8 == g, jnp.sum(jnp.where(r8 < g, glen, 0.0), axis=0, keepdims=True), start)
    pos = jnp.sum(gt * (start * GRANULE), axis=0, keepdims=True) + rank
    lens_ref[...] = jnp.broadcast_to(glen, (SUBLANES, LANES))

    rows = lax.broadcasted_iota(jnp.int32, (LANES, tl), 0)
    info = jnp.where(rows == INFO_POS_LANE, pos, comb_t).T
    info_ref[...] = info
    lane = _lane_iota(info.shape)
    e0 = ROUTER_EXPERT_LANE0
    c_hi, c_mid, c_lo = _split3(jnp.where((lane >= e0) & (lane < e0 + N_EXPERTS), info, 0.0))
    aug = (c_hi + pltpu.roll(c_mid, N_EXPERTS, axis=1) + pltpu.roll(c_lo, 2 * N_EXPERTS, axis=1)).astype(BF16)
    perm = jnp.where(pos == lax.broadcasted_iota(jnp.int32, (SORTED_ROWS, tl), 0).astype(F32),
                     1.0, 0.0).astype(BF16)
    xs_ref[:, 0:D_MODEL] = jnp.dot(perm, hi, preferred_element_type=F32).astype(BF16)
    xs_ref[:, D_MODEL:XS_WIDTH] = jnp.dot(perm, aug, preferred_element_type=F32).astype(BF16)


def _moe_sort(x2, g2, r_hi, r_lo, r_b):
    t = x2.shape[0]
    assert t % MOE_TILE == 0
    nt = t // MOE_TILE
    row = lambda i: (i, 0)
    fixed = lambda i: (0, 0)
    return pl.pallas_call(
        _moe_sort_kernel,
        grid=(nt,),
        in_specs=[pl.BlockSpec((MOE_TILE, D_MODEL), row), pl.BlockSpec((1, D_MODEL), fixed),
                  pl.BlockSpec((D_MODEL, LANES), fixed), pl.BlockSpec((D_MODEL, LANES), fixed),
                  pl.BlockSpec((1, LANES), fixed)],
        out_specs=[pl.BlockSpec((SORTED_ROWS, XS_WIDTH), row), pl.BlockSpec((MOE_TILE, LANES), row),
                   pl.BlockSpec((None, SUBLANES, LANES), lambda i: (i, 0, 0))],
        out_shape=[jax.ShapeDtypeStruct((nt * SORTED_ROWS, XS_WIDTH), BF16),
                   jax.ShapeDtypeStruct((t, LANES), F32),
                   jax.ShapeDtypeStruct((nt, SUBLANES, LANES), F32)],
        compiler_params=_params("arbitrary"),
        name="moe_sort",
    )(x2, g2, r_hi, r_lo, r_b)


def _moe_expert_kernel(sg_ref, sv_ref, gi_ref, *refs):
    del gi_ref
    xs_refs = refs[:STEP_GRANULES]
    wg_ref, wu_ref, wd_ref, y_ref = refs[STEP_GRANULES:]
    s = pl.program_id(0)

    @pl.when(sv_ref[s] == 0)
    def _():
        y_ref[...] = jnp.zeros_like(y_ref)

    @pl.when(sv_ref[s] != 0)
    def _():
        rows = jnp.concatenate([r[...] for r in xs_refs], axis=0)
        x = rows[:, 0:D_MODEL]
        aug = rows[:, D_MODEL:XS_WIDTH].astype(F32)
        lane = _lane_iota(aug.shape)
        e0 = ROUTER_EXPERT_LANE0
        comb = jnp.where((lane >= e0) & (lane < e0 + N_EXPERTS),
                         aug + pltpu.roll(aug, LANES - N_EXPERTS, axis=1)
                         + pltpu.roll(aug, LANES - 2 * N_EXPERTS, axis=1), 0.0)
        first = e0 + sg_ref[s] * EXPERTS_PER_GROUP
        hs = []
        for e in range(EXPERTS_PER_GROUP):
            cw = jnp.sum(jnp.where(lane == first + e, comb, 0.0), axis=-1, keepdims=True)
            hg = jnp.dot(x, wg_ref[e], preferred_element_type=F32)
            hu = jnp.dot(x, wu_ref[e], preferred_element_type=F32)
            hs.append((jax.nn.silu(hg) * hu * cw).astype(BF16))
        h = jnp.concatenate(hs, axis=-1)
        y_ref[...] = jnp.dot(h, wd_ref[...], preferred_element_type=F32).astype(BF16)


def _granule_map(k, s, sg, sv, gi):
    return (gi[s * STEP_GRANULES + k], 0)


def _moe_experts(step_group, step_valid, gran_idx, xs, wg, wu, wd):
    nstep = step_group.shape[0]
    epg = EXPERTS_PER_GROUP
    wmap4 = lambda s, sg, sv, gi: (sg[s], 0, 0, 0)
    in_specs = [pl.BlockSpec((GRANULE, XS_WIDTH), functools.partial(_granule_map, k)) for k in range(STEP_GRANULES)]
    in_specs += [pl.BlockSpec((None, epg, D_MODEL, D_EXPERT), wmap4),
                 pl.BlockSpec((None, epg, D_MODEL, D_EXPERT), wmap4),
                 pl.BlockSpec((None, epg * D_EXPERT, D_MODEL), lambda s, sg, sv, gi: (sg[s], 0, 0))]
    return pl.pallas_call(
        _moe_expert_kernel,
        grid_spec=pltpu.PrefetchScalarGridSpec(
            num_scalar_prefetch=3, grid=(nstep,), in_specs=in_specs,
            out_specs=pl.BlockSpec((STEP_ROWS, D_MODEL), lambda s, sg, sv, gi: (s, 0))),
        out_shape=jax.ShapeDtypeStruct((nstep * STEP_ROWS, D_MODEL), BF16),
        compiler_params=_params("arbitrary"),
        name="moe_experts",
    )(step_group, step_valid, gran_idx, *([xs] * STEP_GRANULES), wg, wu, wd)


def _moe_combine_kernel(inv_ref, x_ref, info_ref, *refs):
    del inv_ref
    ys_refs = refs[:TILE_SLOTS]
    o_ref = refs[TILE_SLOTS]
    kpad = -SORTED_ROWS % LANES
    ys = jnp.concatenate([r[...] for r in ys_refs] + [jnp.zeros((kpad, D_MODEL), BF16)], axis=0)
    pos = info_ref[:, INFO_POS_LANE:INFO_POS_LANE + 1]
    unperm = jnp.where(pos == _lane_iota((MOE_TILE, SORTED_ROWS + kpad)).astype(F32), 1.0, 0.0).astype(BF16)
    o_ref[...] = x_ref[...] + jnp.dot(unperm, ys, preferred_element_type=F32)


def _slot_map(q, i, inv):
    return (inv[i * TILE_SLOTS + q], 0)


def _moe_combine(inv, x2, info, ys):
    t = x2.shape[0]
    nt = t // MOE_TILE
    row = lambda i, inv: (i, 0)
    in_specs = [pl.BlockSpec((MOE_TILE, D_MODEL), row), pl.BlockSpec((MOE_TILE, LANES), row)]
    in_specs += [pl.BlockSpec((GRANULE, D_MODEL), functools.partial(_slot_map, q)) for q in range(TILE_SLOTS)]
    return pl.pallas_call(
        _moe_combine_kernel,
        grid_spec=pltpu.PrefetchScalarGridSpec(
            num_scalar_prefetch=1, grid=(nt,), in_specs=in_specs,
            out_specs=pl.BlockSpec((MOE_TILE, D_MODEL), row)),
        out_shape=jax.ShapeDtypeStruct((t, D_MODEL), F32),
        compiler_params=_params("arbitrary"),
        name="moe_combine",
    )(inv, x2, info, *([ys] * TILE_SLOTS))


def _moe_tables(lens, nt):
    i32 = jnp.int32
    start = jnp.cumsum(lens, axis=1) - lens
    base = jnp.arange(nt, dtype=i32)[:, None] * TILE_SLOTS + start
    run_len = lens.T.reshape(-1)
    run_base = base.T.reshape(-1)
    run_end = jnp.cumsum(run_len)
    run_start = run_end - run_len
    n_g = jnp.sum(lens, axis=0)
    steps_g = (n_g + STEP_GRANULES - 1) // STEP_GRANULES
    step_end = jnp.cumsum(steps_g)
    step_off = step_end - steps_g
    gran_off = jnp.cumsum(n_g) - n_g
    nstep = (nt * (TILE_SLOTS - 1) + STEP_GRANULES - 1) // STEP_GRANULES + N_GROUPS
    s = jnp.arange(nstep, dtype=i32)
    count_le = lambda ends, v: jnp.sum((v[..., None] >= ends).astype(i32), axis=-1)
    sg = jnp.minimum(count_le(step_end, s), N_GROUPS - 1)
    sv = (s < step_end[-1]).astype(i32)
    jl = (s - step_off[sg])[:, None] * STEP_GRANULES + jnp.arange(STEP_GRANULES, dtype=i32)[None, :]
    ok = (sv[:, None] > 0) & (jl < n_g[sg][:, None])
    j = jnp.where(ok, gran_off[sg][:, None] + jl, 0)
    r = jnp.minimum(count_le(run_end, j), run_len.shape[0] - 1)
    gran = run_base[r] + (j - run_start[r])
    gran = jnp.where(ok, gran, jnp.where(sv[:, None] > 0, gran[:, 0:1], 0)).astype(i32)
    q = jnp.arange(TILE_SLOTS, dtype=i32)[None, :]
    cum = jnp.cumsum(lens, axis=1)
    gq = jnp.minimum(jnp.sum((q[:, :, None] >= cum[:, None, :]).astype(i32), axis=-1), N_GROUPS - 1)
    used = q < cum[:, -1:]
    ridx = gq * nt + jnp.arange(nt, dtype=i32)[:, None]
    jloc = run_start[ridx] + (q - jnp.take_along_axis(start, gq, axis=1)) - gran_off[gq]
    inv = jnp.where(used, step_off[gq] * STEP_GRANULES + jloc, 0).astype(i32)
    return sg, sv, gran.reshape(-1), inv.reshape(-1)


def _moe(x2, g2, r_hi, r_lo, r_b, wg, wu, wd):
    nt = x2.shape[0] // MOE_TILE
    xs, info, lens = _moe_sort(x2, g2, r_hi, r_lo, r_b)
    sg, sv, gran, inv = _moe_tables(lens[:, 0:N_GROUPS, 0].astype(jnp.int32), nt)
    ys = _moe_experts(sg, sv, gran, xs, wg, wu, wd)
    return _moe_combine(inv, x2, info, ys)


def _block_diag(w):
    h, n, _ = w.shape
    eye = jnp.eye(h, dtype=w.dtype)
    return (eye[:, None, :, None] * w[:, :, None, :]).reshape(h * n, h * n)


def _router_tables(w_rg, b_rg, w_re, b_re):
    e0 = ROUTER_EXPERT_LANE0
    w = jnp.zeros((D_MODEL, LANES), F32).at[:, 0:N_GROUPS].set(w_rg).at[:, e0:e0 + N_EXPERTS].set(w_re)
    b = jnp.zeros((1, LANES), F32).at[0, 0:N_GROUPS].set(b_rg).at[0, e0:e0 + N_EXPERTS].set(b_re)
    hi = w.astype(BF16)
    lo = (w - hi.astype(F32)).astype(BF16)
    return hi, lo, b


def kernel(x, mem, norm1_g, w_in, conv_w, conv_b, rg_wa, rg_ba, rg_wx, rg_bx, rg_lambda, dq_norm_g, dk_norm_g, lambda_q1, lambda_k1, lambda_q2, lambda_k2, diff_head_norm_g, mem_norm_g, w_mem_kv, mq_norm_g, mk_norm_g, w_out, norm2_g, w_router_group, b_router_group, w_router_expert, b_router_expert, w_expert_gate, w_expert_up, w_expert_down):
    b, s, d = x.shape
    m = mem.shape[1]
    depth = w_in.shape[0]
    x2 = x.reshape(b * s, d)
    mem2 = mem.reshape(b * m, d)
    vec = lambda v: v.reshape(1, -1).astype(F32)
    for l in range(depth):
        lam_init = 0.8 - 0.6 * math.exp(-0.3 * l)
        lam = (jnp.exp(jnp.sum(lambda_q1[l] * lambda_k1[l])) - jnp.exp(jnp.sum(lambda_q2[l] * lambda_k2[l]))
               + lam_init).reshape(1, 1).astype(F32)
        gw = GROUP_WIDTH
        wl = w_in[l].astype(BF16)
        w_rows = jnp.concatenate([wl[:, 0:2 * gw], wl[:, 3 * gw:4 * gw], wl[:, 5 * gw:6 * gw]], axis=1)
        col = lambda v: v.reshape(-1, 1).astype(F32)
        lru_in, k, mq, qt, vt = _in_proj(x2, b, s, vec(norm1_g[l]), w_rows, wl[:, 2 * gw:3 * gw].T,
                                         wl[:, 4 * gw:5 * gw].T, col(jnp.tile(dq_norm_g[l], gw // DIFF_QK_DIM)),
                                         vec(jnp.tile(dk_norm_g[l], 2)), vec(mq_norm_g[l]))
        y_lru = _lru(lru_in, b, s, conv_w[l], vec(conv_b[l]), _block_diag(rg_wa[l]).astype(BF16), vec(rg_ba[l]),
                     _block_diag(rg_wx[l]).astype(BF16), vec(rg_bx[l]), vec(rg_lambda[l]))
        y_diff = _diff_attn(qt, k, vt, lam, col(diff_head_norm_g[l]), b, s, lam_init)
        mk, mv = _mem_kv(mem2, vec(mem_norm_g), w_mem_kv[l].astype(BF16), vec(mk_norm_g[l]))
        y_mem = _mem_attn(mq, mk, mv, b, s, m)
        x2 = _out_proj(x2, y_lru, y_diff, y_mem, w_out[l].astype(BF16))
        r_hi, r_lo, r_b = _router_tables(w_router_group[l], b_router_group[l], w_router_expert[l], b_router_expert[l])
        epg = EXPERTS_PER_GROUP
        x2 = _moe(x2, vec(norm2_g[l]), r_hi, r_lo, r_b,
                  w_expert_gate[l].astype(BF16).reshape(N_GROUPS, epg, D_MODEL, D_EXPERT),
                  w_expert_up[l].astype(BF16).reshape(N_GROUPS, epg, D_MODEL, D_EXPERT),
                  w_expert_down[l].astype(BF16).reshape(N_GROUPS, epg * D_EXPERT, D_MODEL))
    return x2.reshape(b, s, d)
```

```python
import functools
import math

import jax
import jax.numpy as jnp
from jax import lax
from jax.experimental import pallas as pl
from jax.experimental.pallas import tpu as pltpu

F32 = jnp.float32
BF16 = jnp.bfloat16

D_MODEL = 1024
GROUP_WIDTH = D_MODEL // 2
LRU_HEADS = 8
CONV_WIDTH = 4
LRU_C = 8.0
DIFF_HEADS = 4
DIFF_HEAD_DIM = GROUP_WIDTH // DIFF_HEADS
DIFF_QK_DIM = DIFF_HEAD_DIM // 2
MEM_HEADS = 4
MEM_HEAD_DIM = GROUP_WIDTH // MEM_HEADS
N_GROUPS = 4
EXPERTS_PER_GROUP = 8
N_EXPERTS = N_GROUPS * EXPERTS_PER_GROUP
D_EXPERT = 256
D_IN_PROJ = 6 * GROUP_WIDTH
EPS = 1e-6

LANES = 128
SUBLANES = 8
VMEM_LIMIT = 56 * 1024 * 1024
NEG_BIG = -1e30
ROUTER_EXPERT_LANE0 = 32


def _tile(n, pref):
    t = min(n, pref)
    assert n % t == 0, (n, t)
    return t


def _params(*sem):
    return pltpu.CompilerParams(dimension_semantics=sem, vmem_limit_bytes=VMEM_LIMIT)


def _rms_rows(x, g):
    return x * lax.rsqrt(jnp.mean(x * x, axis=-1, keepdims=True) + EPS) * g


def _lane_iota(shape):
    return lax.broadcasted_iota(jnp.int32, shape, len(shape) - 1)


def _half_head_norm(z, g):
    sq = z * z
    lo = _lane_iota(z.shape) < DIFF_QK_DIM
    s_all = jnp.sum(sq, axis=-1, keepdims=True)
    s_lo = jnp.sum(jnp.where(lo, sq, 0.0), axis=-1, keepdims=True)
    inv_lo = lax.rsqrt(s_lo * (1.0 / DIFF_QK_DIM) + EPS)
    inv_hi = lax.rsqrt((s_all - s_lo) * (1.0 / DIFF_QK_DIM) + EPS)
    return z * jnp.where(lo, inv_lo, inv_hi) * g


SEQ_TILE = 512
_NT = (((1,), (1,)), ((), ()))


def _in_proj_kernel(x_ref, g1_ref, w_ref, wqt_ref, wvt_ref, qg_ref, kg_ref, mqg_ref,
                    lru_ref, k_ref, mq_ref, qt_ref, vt_ref):
    xn = _rms_rows(x_ref[...], g1_ref[...]).astype(BF16)
    gw = GROUP_WIDTH
    tm = xn.shape[0]
    lru_ref[...] = jnp.dot(xn, w_ref[:, 0:2 * gw], preferred_element_type=F32).astype(BF16)
    k = jnp.dot(xn, w_ref[:, 2 * gw:3 * gw], preferred_element_type=F32)
    for h in range(DIFF_HEADS):
        sl = slice(h * LANES, (h + 1) * LANES)
        k_ref[:, sl] = _half_head_norm(k[:, sl], kg_ref[...]).astype(BF16)
    mq = jnp.dot(xn, w_ref[:, 3 * gw:4 * gw], preferred_element_type=F32)
    for h in range(MEM_HEADS):
        sl = slice(h * MEM_HEAD_DIM, (h + 1) * MEM_HEAD_DIM)
        mq_ref[:, sl] = _rms_rows(mq[:, sl], mqg_ref[...]).astype(BF16)
    qt = lax.dot_general(wqt_ref[...], xn, _NT, preferred_element_type=F32)
    q3 = qt.reshape(gw // DIFF_QK_DIM, DIFF_QK_DIM, tm)
    q3 = q3 * lax.rsqrt(jnp.mean(q3 * q3, axis=1, keepdims=True) + EPS)
    qscale = DIFF_QK_DIM ** -0.5 * math.log2(math.e)
    qt_ref[...] = (q3.reshape(gw, tm) * (qg_ref[...] * qscale)).astype(BF16)
    vt_ref[...] = lax.dot_general(wvt_ref[...], xn, _NT, preferred_element_type=F32).astype(BF16)


def _in_proj(x2, b, s, g1, w_rows, wqt, wvt, qg_col, kg, mqg):
    t = x2.shape[0]
    tm = SEQ_TILE
    assert s % tm == 0
    ns = s // tm
    gw = GROUP_WIDTH
    row = lambda i: (i, 0)
    fixed = lambda i: (0, 0)
    fm = lambda i: (i // ns, i % ns, 0, 0)
    return pl.pallas_call(
        _in_proj_kernel,
        grid=(t // tm,),
        in_specs=[pl.BlockSpec((tm, D_MODEL), row), pl.BlockSpec((1, D_MODEL), fixed),
                  pl.BlockSpec((D_MODEL, 4 * gw), fixed), pl.BlockSpec((gw, D_MODEL), fixed),
                  pl.BlockSpec((gw, D_MODEL), fixed), pl.BlockSpec((gw, 1), fixed),
                  pl.BlockSpec((1, LANES), fixed), pl.BlockSpec((1, MEM_HEAD_DIM), fixed)],
        out_specs=[pl.BlockSpec((tm, 2 * gw), row), pl.BlockSpec((tm, gw), row), pl.BlockSpec((tm, gw), row),
                   pl.BlockSpec((None, None, gw, tm), fm), pl.BlockSpec((None, None, gw, tm), fm)],
        out_shape=[jax.ShapeDtypeStruct((t, 2 * gw), BF16), jax.ShapeDtypeStruct((t, gw), BF16),
                   jax.ShapeDtypeStruct((t, gw), BF16), jax.ShapeDtypeStruct((b, ns, gw, tm), BF16),
                   jax.ShapeDtypeStruct((b, ns, gw, tm), BF16)],
        compiler_params=_params("arbitrary"),
        name="in_proj",
    )(x2, g1, w_rows, wqt, wvt, qg_col, kg, mqg)


def _softplus(z):
    return jnp.maximum(z, 0.0) + jnp.log(1.0 + jnp.exp(-jnp.abs(z)))


def _lru_kernel(u_ref, cw_ref, cb_ref, wa_ref, ba_ref, wx_ref, bx_ref, lam_ref, y_ref,
                ext_ref, hcar_ref, a_ref, b_ref, h_ref):
    ts = y_ref.shape[0]
    gw = GROUP_WIDTH
    hist = SUBLANES

    @pl.when(pl.program_id(1) == 0)
    def _():
        ext_ref[0:hist, :] = jnp.zeros((hist, gw), F32)
        hcar_ref[...] = jnp.zeros_like(hcar_ref)

    @pl.when(pl.program_id(1) != 0)
    def _():
        ext_ref[0:hist, :] = ext_ref[ts:ts + hist, :]

    ext_ref[hist:hist + ts, :] = u_ref[:, 0:gw].astype(F32)
    xc = cb_ref[...] + cw_ref[CONV_WIDTH - 1:CONV_WIDTH, :] * ext_ref[hist:hist + ts, :]
    for j in range(CONV_WIDTH - 1):
        off = hist - (CONV_WIDTH - 1) + j
        xc = xc + cw_ref[j:j + 1, :] * ext_ref[off:off + ts, :]
    xcb = xc.astype(BF16)
    r = jax.nn.sigmoid(jnp.dot(xcb, wa_ref[...], preferred_element_type=F32) + ba_ref[...])
    gate_i = jax.nn.sigmoid(jnp.dot(xcb, wx_ref[...], preferred_element_type=F32) + bx_ref[...])
    log_a = (-LRU_C * r) * _softplus(-lam_ref[...])
    a = jnp.exp(log_a)
    b = jnp.sqrt(1.0 - a * a) * (gate_i * xc)

    row = lax.broadcasted_iota(jnp.int32, (ts, gw), 0) & (SUBLANES - 1)
    d = 1
    while d < SUBLANES:
        keep = row >= d
        a_prev = pltpu.roll(a, d, axis=0)
        b_prev = pltpu.roll(b, d, axis=0)
        b = jnp.where(keep, a * b_prev + b, b)
        a = jnp.where(keep, a * a_prev, a)
        d *= 2
    a_ref[...] = a
    b_ref[...] = b

    def block(i, h):
        off = pl.multiple_of(i * SUBLANES, SUBLANES)
        hb = a_ref[pl.ds(off, SUBLANES), :] * h + b_ref[pl.ds(off, SUBLANES), :]
        h_ref[pl.ds(off, SUBLANES), :] = hb
        return jnp.broadcast_to(hb[SUBLANES - 1:SUBLANES, :], (SUBLANES, gw))

    hcar_ref[...] = lax.fori_loop(0, ts // SUBLANES, block, hcar_ref[...], unroll=8)
    y_ref[...] = (h_ref[...] * jax.nn.gelu(u_ref[:, gw:2 * gw].astype(F32))).astype(BF16)


def _lru(lru_in, b, s, conv_w, conv_b, wa_bd, ba, wx_bd, bx, lam):
    ts = _tile(s, 512)
    ns = s // ts
    gw = GROUP_WIDTH
    fixed = lambda bi, si: (0, 0)
    vec = pl.BlockSpec((1, gw), fixed)
    return pl.pallas_call(
        _lru_kernel,
        grid=(b, ns),
        in_specs=[pl.BlockSpec((ts, 2 * gw), lambda bi, si: (bi * ns + si, 0)),
                  pl.BlockSpec((CONV_WIDTH, gw), fixed), vec,
                  pl.BlockSpec((gw, gw), fixed), vec, pl.BlockSpec((gw, gw), fixed), vec, vec],
        out_specs=pl.BlockSpec((ts, gw), lambda bi, si: (bi * ns + si, 0)),
        out_shape=jax.ShapeDtypeStruct((b * s, gw), BF16),
        scratch_shapes=[pltpu.VMEM((ts + SUBLANES, gw), F32), pltpu.VMEM((SUBLANES, gw), F32),
                        pltpu.VMEM((ts, gw), F32), pltpu.VMEM((ts, gw), F32), pltpu.VMEM((ts, gw), F32)],
        compiler_params=_params("arbitrary", "arbitrary"),
        name="rg_lru",
    )(lru_in, conv_w, conv_b, wa_bd, ba, wx_bd, bx, lam)


ATTN_HEADS_PER_STEP = 2
ATTN_MIN_SUM = 2.0 ** -40


def _score_bound(dq_gain, dk_gain):
    qscale = DIFF_QK_DIM ** -0.5 * math.log2(math.e)
    return 1.02 * qscale * DIFF_QK_DIM * jnp.max(jnp.abs(dq_gain)) * jnp.max(jnp.abs(dk_gain))


def _diff_attn_kernel(lam_ref, qt_ref, k_ref, vt_ref, hg_ref, o_ref,
                      qs_ref, sa_ref, sb_ref, m_ref, l_ref, acc_ref, *, t, out_scale):
    qi = pl.program_id(2)
    hd = DIFF_HEAD_DIM
    heads = range(ATTN_HEADS_PER_STEP)
    lo = lax.broadcasted_iota(jnp.int32, (hd, t), 0) < DIFF_QK_DIM
    for h in heads:
        q = qt_ref[h * hd:(h + 1) * hd, :]
        zero = jnp.zeros_like(q)
        qs_ref[h, :, 0:t] = jnp.where(lo, q, zero)
        qs_ref[h, :, t:2 * t] = jnp.where(lo, zero, q)

    def causal(x, fill):
        kpos = lax.broadcasted_iota(jnp.int32, (t, 2 * t), 0)
        c = lax.broadcasted_iota(jnp.int32, (t, 2 * t), 1)
        return jnp.where(kpos <= jnp.where(c >= t, c - t, c), x, fill)

    def score(j, h):
        off = pl.multiple_of(j * t, t)
        return jnp.dot(k_ref[pl.ds(off, t), h * hd:(h + 1) * hd], qs_ref[h],
                       preferred_element_type=F32)

    bound = lam_ref[0, 1]
    l_ref[...] = jnp.zeros(l_ref.shape, F32)
    acc_ref[...] = jnp.zeros(acc_ref.shape, F32)

    def fast_update(j, masked):
        for h in heads:
            p = jnp.exp2(score(j, h) - bound)
            if masked:
                p = causal(p, 0.0)
            l_ref[h] += jnp.sum(p, axis=0, keepdims=True)
            acc_ref[h] += jnp.dot(vt_ref[j, h * hd:(h + 1) * hd, :], p.astype(BF16),
                                  preferred_element_type=F32)

    def fast_pair(i, carry):
        fast_update(2 * i, False)
        fast_update(2 * i + 1, False)
        return carry

    lax.fori_loop(0, qi >> 1, fast_pair, 0)

    @pl.when((qi & 1) == 1)
    def _():
        fast_update(qi - 1, False)

    fast_update(qi, True)

    @pl.when(jnp.min(l_ref[...]) < ATTN_MIN_SUM)
    def _():
        m_ref[...] = jnp.full(m_ref.shape, NEG_BIG, F32)
        l_ref[...] = jnp.zeros(l_ref.shape, F32)
        acc_ref[...] = jnp.zeros(acc_ref.shape, F32)

        def scores(j, dst_ref):
            for h in heads:
                dst_ref[h] = score(j, h)

        def update(j, src_ref, masked):
            for h in heads:
                s = src_ref[h]
                if masked:
                    s = causal(s, NEG_BIG)
                m = m_ref[h]
                m_new = jnp.maximum(m, jnp.max(s, axis=0, keepdims=True))
                alpha = jnp.exp2(m - m_new)
                p = jnp.exp2(s - m_new)
                m_ref[h] = m_new
                l_ref[h] = alpha * l_ref[h] + jnp.sum(p, axis=0, keepdims=True)
                vt = vt_ref[j, h * hd:(h + 1) * hd, :]
                acc_ref[h] = alpha * acc_ref[h] + jnp.dot(vt, p.astype(BF16), preferred_element_type=F32)

        scores(0, sa_ref)

        def pair(i, carry):
            scores(2 * i + 1, sb_ref)
            update(2 * i, sa_ref, False)
            scores(2 * i + 2, sa_ref)
            update(2 * i + 1, sb_ref, False)
            return carry

        lax.fori_loop(0, qi >> 1, pair, 0)

        @pl.when((qi & 1) == 1)
        def _():
            scores(qi, sb_ref)
            update(qi - 1, sa_ref, False)
            update(qi, sb_ref, True)

        @pl.when((qi & 1) == 0)
        def _():
            update(qi, sa_ref, True)

    for h in heads:
        o = acc_ref[h] / l_ref[h]
        o = o[:, 0:t] - lam_ref[0, 0] * o[:, t:2 * t]
        o = o * lax.rsqrt(jnp.mean(o * o, axis=0, keepdims=True) + EPS) * (hg_ref[...] * out_scale)
        o_ref[:, h * hd:(h + 1) * hd] = o.T.astype(BF16)


def _diff_attn(qt, k, vt, lam, hg_col, b, s, lam_init):
    t = SEQ_TILE
    nq = s // t
    hp = ATTN_HEADS_PER_STEP
    w = hp * DIFF_HEAD_DIM
    ng = DIFF_HEADS // hp
    kern = functools.partial(_diff_attn_kernel, t=t, out_scale=1.0 - lam_init)
    return pl.pallas_call(
        kern,
        grid=(b, ng, nq),
        in_specs=[pl.BlockSpec(memory_space=pltpu.SMEM),
                  pl.BlockSpec((None, None, w, t), lambda bi, g, qi: (bi, qi, g, 0)),
                  pl.BlockSpec((s, w), lambda bi, g, qi: (bi, g)),
                  pl.BlockSpec((None, nq, w, t), lambda bi, g, qi: (bi, 0, g, 0)),
                  pl.BlockSpec((DIFF_HEAD_DIM, 1), lambda bi, g, qi: (0, 0))],
        out_specs=pl.BlockSpec((t, w), lambda bi, g, qi: (bi * nq + qi, g)),
        out_shape=jax.ShapeDtypeStruct((b * s, GROUP_WIDTH), BF16),
        scratch_shapes=[pltpu.VMEM((hp, DIFF_HEAD_DIM, 2 * t), BF16),
                        pltpu.VMEM((hp, t, 2 * t), F32), pltpu.VMEM((hp, t, 2 * t), F32),
                        pltpu.VMEM((hp, 1, 2 * t), F32), pltpu.VMEM((hp, 1, 2 * t), F32),
                        pltpu.VMEM((hp, DIFF_HEAD_DIM, 2 * t), F32)],
        compiler_params=_params("arbitrary", "arbitrary", "arbitrary"),
        name="diff_attn",
    )(lam, qt, k, vt, hg_col)


def _mem_kv_kernel(mem_ref, mg_ref, w_ref, kg_ref, mk_ref, mv_ref):
    memn = _rms_rows(mem_ref[...], mg_ref[...]).astype(BF16)
    gw = GROUP_WIDTH
    k = jnp.dot(memn, w_ref[:, 0:gw], preferred_element_type=F32)
    for h in range(MEM_HEADS):
        sl = slice(h * MEM_HEAD_DIM, (h + 1) * MEM_HEAD_DIM)
        mk_ref[:, sl] = _rms_rows(k[:, sl], kg_ref[...]).astype(BF16)
    mv_ref[...] = jnp.dot(memn, w_ref[:, gw:2 * gw], preferred_element_type=F32).astype(BF16)


def _mem_kv(mem2, mg, w_kv, kg):
    r = mem2.shape[0]
    tm = _tile(r, 512)
    row = lambda i: (i, 0)
    fixed = lambda i: (0, 0)
    gw = GROUP_WIDTH
    return pl.pallas_call(
        _mem_kv_kernel,
        grid=(r // tm,),
        in_specs=[pl.BlockSpec((tm, D_MODEL), row), pl.BlockSpec((1, D_MODEL), fixed),
                  pl.BlockSpec((D_MODEL, 2 * gw), fixed), pl.BlockSpec((1, MEM_HEAD_DIM), fixed)],
        out_specs=[pl.BlockSpec((tm, gw), row), pl.BlockSpec((tm, gw), row)],
        out_shape=[jax.ShapeDtypeStruct((r, gw), BF16), jax.ShapeDtypeStruct((r, gw), BF16)],
        compiler_params=_params("arbitrary"),
        name="mem_kv",
    )(mem2, mg, w_kv, kg)


def _mem_attn_kernel(q_ref, k_ref, v_ref, o_ref):
    scale = MEM_HEAD_DIM ** -0.5
    for h in range(MEM_HEADS):
        sl = slice(h * MEM_HEAD_DIM, (h + 1) * MEM_HEAD_DIM)
        s = lax.dot_general(q_ref[:, sl], k_ref[:, sl], (((1,), (1,)), ((), ())),
                            preferred_element_type=F32) * scale
        p = jnp.exp(s - jnp.max(s, axis=-1, keepdims=True))
        o = jnp.dot(p.astype(BF16), v_ref[:, sl], preferred_element_type=F32)
        o_ref[:, sl] = (o / jnp.sum(p, axis=-1, keepdims=True)).astype(BF16)


def _mem_attn(mq, mk, mv, b, s, m):
    ts = _tile(s, 512)
    ns = s // ts
    gw = GROUP_WIDTH
    return pl.pallas_call(
        _mem_attn_kernel,
        grid=(b, ns),
        in_specs=[pl.BlockSpec((ts, gw), lambda bi, si: (bi * ns + si, 0)),
                  pl.BlockSpec((m, gw), lambda bi, si: (bi, 0)),
                  pl.BlockSpec((m, gw), lambda bi, si: (bi, 0))],
        out_specs=pl.BlockSpec((ts, gw), lambda bi, si: (bi * ns + si, 0)),
        out_shape=jax.ShapeDtypeStruct((b * s, gw), BF16),
        compiler_params=_params("arbitrary", "arbitrary"),
        name="mem_attn",
    )(mq, mk, mv)


def _out_proj_kernel(x_ref, yl_ref, yd_ref, ym_ref, w_ref, o_ref):
    y = jnp.concatenate([yl_ref[...], yd_ref[...], ym_ref[...]], axis=-1)
    o_ref[...] = x_ref[...] + jnp.dot(y, w_ref[...], preferred_element_type=F32)


def _out_proj(x2, y_lru, y_diff, y_mem, w_out):
    t = x2.shape[0]
    tm = _tile(t, 512)
    row = lambda i: (i, 0)
    gw = GROUP_WIDTH
    return pl.pallas_call(
        _out_proj_kernel,
        grid=(t // tm,),
        in_specs=[pl.BlockSpec((tm, D_MODEL), row), pl.BlockSpec((tm, gw), row), pl.BlockSpec((tm, gw), row),
                  pl.BlockSpec((tm, gw), row), pl.BlockSpec((3 * gw, D_MODEL), lambda i: (0, 0))],
        out_specs=pl.BlockSpec((tm, D_MODEL), row),
        out_shape=jax.ShapeDtypeStruct((t, D_MODEL), F32),
        compiler_params=_params("arbitrary"),
        name="out_proj",
    )(x2, y_lru, y_diff, y_mem, w_out)


MOE_TILE = 256
GRANULE = 16
TILE_SLOTS = MOE_TILE // GRANULE + N_GROUPS
SORTED_ROWS = TILE_SLOTS * GRANULE
STEP_GRANULES = 16
STEP_ROWS = STEP_GRANULES * GRANULE
XS_WIDTH = D_MODEL + LANES
INFO_POS_LANE = 0


def _route_t(lt):
    row_i = lax.broadcasted_iota(jnp.int32, lt.shape, 0)
    row = row_i.astype(F32)
    e0 = ROUTER_EXPERT_LANE0
    lg = jnp.where(row_i < N_GROUPS, lt, NEG_BIG)
    mg = jnp.max(lg, axis=0, keepdims=True)
    g_gate = 1.0 / jnp.sum(jnp.exp(lg - mg), axis=0, keepdims=True)
    g_idx = jnp.min(jnp.where(lg == mg, row, float(LANES)), axis=0, keepdims=True)
    row_group = ((row_i - e0) >> 3).astype(F32)
    sel = (row_i >= e0) & (row_i < e0 + N_EXPERTS) & (row_group == g_idx)
    le = jnp.where(sel, lt, NEG_BIG)
    m1 = jnp.max(le, axis=0, keepdims=True)
    se = jnp.sum(jnp.where(sel, jnp.exp(le - m1), 0.0), axis=0, keepdims=True)
    i1 = jnp.min(jnp.where(sel & (le == m1), row, float(LANES)), axis=0, keepdims=True)
    le2 = jnp.where(row == i1, NEG_BIG, le)
    m2 = jnp.max(le2, axis=0, keepdims=True)
    i2 = jnp.min(jnp.where(sel & (le2 == m2) & (row != i1), row, float(LANES)), axis=0, keepdims=True)
    p1 = 1.0 / se
    p2 = jnp.exp(m2 - m1) / se
    tot = p1 + p2
    w = jnp.where(row == i1, p1 / tot, jnp.where(row == i2, p2 / tot, 0.0))
    return g_gate * w, g_idx


def _split3(c):
    hi = c.astype(BF16).astype(F32)
    r1 = c - hi
    mid = r1.astype(BF16).astype(F32)
    lo = (r1 - mid).astype(BF16).astype(F32)
    return hi, mid, lo


def _moe_sort_kernel(x_ref, g2_ref, rhi_ref, rlo_ref, rb_ref, xs_ref, info_ref, lens_ref):
    tl = x_ref.shape[0]
    xn = _rms_rows(x_ref[...], g2_ref[...])
    hi = xn.astype(BF16)
    lo = (xn - hi.astype(F32)).astype(BF16)
    logits = (jnp.dot(hi, rhi_ref[...], preferred_element_type=F32)
              + jnp.dot(lo, rhi_ref[...], preferred_element_type=F32)
              + jnp.dot(hi, rlo_ref[...], preferred_element_type=F32) + rb_ref[...])
    comb_t, g_idx = _route_t(logits.T)

    grow = lax.broadcasted_iota(jnp.int32, (SUBLANES, tl), 0).astype(F32)
    gt = jnp.where(grow == g_idx, 1.0, 0.0)
    earlier = (lax.broadcasted_iota(jnp.int32, (tl, tl), 0)
               < lax.broadcasted_iota(jnp.int32, (tl, tl), 1)).astype(BF16)
    before = jnp.dot(gt.astype(BF16), earlier, preferred_element_type=F32)
    rank = jnp.sum(gt * before, axis=0, keepdims=True)
    cnt = jnp.sum(gt, axis=1, keepdims=True)
    glen = jnp.floor((cnt + (GRANULE - 1)) * (1.0 / GRANULE))
    r8 = lax.broadcasted_iota(jnp.int32, (SUBLANES, 1), 0)
    start = jnp.zeros((SUBLANES, 1), F32)
    for g in range(1, N_GROUPS):
        start = jnp.where(r8 == g, jnp.sum(jnp.where(r8 < g, glen, 0.0), axis=0, keepdims=True), start)
    pos = jnp.sum(gt * (start * GRANULE), axis=0, keepdims=True) + rank
    lens_ref[...] = jnp.broadcast_to(glen, (SUBLANES, LANES))

    rows = lax.broadcasted_iota(jnp.int32, (LANES, tl), 0)
    info = jnp.where(rows == INFO_POS_LANE, pos, comb_t).T
    info_ref[...] = info
    lane = _lane_iota(info.shape)
    e0 = ROUTER_EXPERT_LANE0
    c_hi, c_mid, c_lo = _split3(jnp.where((lane >= e0) & (lane < e0 + N_EXPERTS), info, 0.0))
    aug = (c_hi + pltpu.roll(c_mid, N_EXPERTS, axis=1) + pltpu.roll(c_lo, 2 * N_EXPERTS, axis=1)).astype(BF16)
    perm = jnp.where(pos == lax.broadcasted_iota(jnp.int32, (SORTED_ROWS, tl), 0).astype(F32),
                     1.0, 0.0).astype(BF16)
    xs_ref[:, 0:D_MODEL] = jnp.dot(perm, hi, preferred_element_type=F32).astype(BF16)
    xs_ref[:, D_MODEL:XS_WIDTH] = jnp.dot(perm, aug, preferred_element_type=F32).astype(BF16)


def _moe_sort(x2, g2, r_hi, r_lo, r_b):
    t = x2.shape[0]
    assert t % MOE_TILE == 0
    nt = t // MOE_TILE
    row = lambda i: (i, 0)
    fixed = lambda i: (0, 0)
    return pl.pallas_call(
        _moe_sort_kernel,
        grid=(nt,),
        in_specs=[pl.BlockSpec((MOE_TILE, D_MODEL), row), pl.BlockSpec((1, D_MODEL), fixed),
                  pl.BlockSpec((D_MODEL, LANES), fixed), pl.BlockSpec((D_MODEL, LANES), fixed),
                  pl.BlockSpec((1, LANES), fixed)],
        out_specs=[pl.BlockSpec((SORTED_ROWS, XS_WIDTH), row), pl.BlockSpec((MOE_TILE, LANES), row),
                   pl.BlockSpec((None, SUBLANES, LANES), lambda i: (i, 0, 0))],
        out_shape=[jax.ShapeDtypeStruct((nt * SORTED_ROWS, XS_WIDTH), BF16),
                   jax.ShapeDtypeStruct((t, LANES), F32),
                   jax.ShapeDtypeStruct((nt, SUBLANES, LANES), F32)],
        compiler_params=_params("arbitrary"),
        name="moe_sort",
    )(x2, g2, r_hi, r_lo, r_b)


def _moe_expert_kernel(sg_ref, sv_ref, gi_ref, *refs):
    del gi_ref
    xs_refs = refs[:STEP_GRANULES]
    wg_ref, wu_ref, wd_ref, y_ref = refs[STEP_GRANULES:]
    s = pl.program_id(0)

    @pl.when(sv_ref[s] == 0)
    def _():
        y_ref[...] = jnp.zeros_like(y_ref)

    @pl.when(sv_ref[s] != 0)
    def _():
        rows = jnp.concatenate([r[...] for r in xs_refs], axis=0)
        x = rows[:, 0:D_MODEL]
        aug = rows[:, D_MODEL:XS_WIDTH].astype(F32)
        lane = _lane_iota(aug.shape)
        e0 = ROUTER_EXPERT_LANE0
        comb = jnp.where((lane >= e0) & (lane < e0 + N_EXPERTS),
                         aug + pltpu.roll(aug, LANES - N_EXPERTS, axis=1)
                         + pltpu.roll(aug, LANES - 2 * N_EXPERTS, axis=1), 0.0)
        first = e0 + sg_ref[s] * EXPERTS_PER_GROUP
        hs = []
        for e in range(EXPERTS_PER_GROUP):
            cw = jnp.sum(jnp.where(lane == first + e, comb, 0.0), axis=-1, keepdims=True)
            hg = jnp.dot(x, wg_ref[e], preferred_element_type=F32)
            hu = jnp.dot(x, wu_ref[e], preferred_element_type=F32)
            hs.append((jax.nn.silu(hg) * hu * cw).astype(BF16))
        h = jnp.concatenate(hs, axis=-1)
        y_ref[...] = jnp.dot(h, wd_ref[...], preferred_element_type=F32).astype(BF16)


def _granule_map(k, s, sg, sv, gi):
    return (gi[s * STEP_GRANULES + k], 0)


def _moe_experts(step_group, step_valid, gran_idx, xs, wg, wu, wd):
    nstep = step_group.shape[0]
    epg = EXPERTS_PER_GROUP
    wmap4 = lambda s, sg, sv, gi: (sg[s], 0, 0, 0)
    in_specs = [pl.BlockSpec((GRANULE, XS_WIDTH), functools.partial(_granule_map, k)) for k in range(STEP_GRANULES)]
    in_specs += [pl.BlockSpec((None, epg, D_MODEL, D_EXPERT), wmap4),
                 pl.BlockSpec((None, epg, D_MODEL, D_EXPERT), wmap4),
                 pl.BlockSpec((None, epg * D_EXPERT, D_MODEL), lambda s, sg, sv, gi: (sg[s], 0, 0))]
    return pl.pallas_call(
        _moe_expert_kernel,
        grid_spec=pltpu.PrefetchScalarGridSpec(
            num_scalar_prefetch=3, grid=(nstep,), in_specs=in_specs,
            out_specs=pl.BlockSpec((STEP_ROWS, D_MODEL), lambda s, sg, sv, gi: (s, 0))),
        out_shape=jax.ShapeDtypeStruct((nstep * STEP_ROWS, D_MODEL), BF16),
        compiler_params=_params("arbitrary"),
        name="moe_experts",
    )(step_group, step_valid, gran_idx, *([xs] * STEP_GRANULES), wg, wu, wd)


def _moe_combine_kernel(inv_ref, x_ref, info_ref, *refs):
    del inv_ref
    ys_refs = refs[:TILE_SLOTS]
    o_ref = refs[TILE_SLOTS]
    kpad = -SORTED_ROWS % LANES
    ys = jnp.concatenate([r[...] for r in ys_refs] + [jnp.zeros((kpad, D_MODEL), BF16)], axis=0)
    pos = info_ref[:, INFO_POS_LANE:INFO_POS_LANE + 1]
    unperm = jnp.where(pos == _lane_iota((MOE_TILE, SORTED_ROWS + kpad)).astype(F32), 1.0, 0.0).astype(BF16)
    o_ref[...] = x_ref[...] + jnp.dot(unperm, ys, preferred_element_type=F32)


def _slot_map(q, i, inv):
    return (inv[i * TILE_SLOTS + q], 0)


def _moe_combine(inv, x2, info, ys):
    t = x2.shape[0]
    nt = t // MOE_TILE
    row = lambda i, inv: (i, 0)
    in_specs = [pl.BlockSpec((MOE_TILE, D_MODEL), row), pl.BlockSpec((MOE_TILE, LANES), row)]
    in_specs += [pl.BlockSpec((GRANULE, D_MODEL), functools.partial(_slot_map, q)) for q in range(TILE_SLOTS)]
    return pl.pallas_call(
        _moe_combine_kernel,
        grid_spec=pltpu.PrefetchScalarGridSpec(
            num_scalar_prefetch=1, grid=(nt,), in_specs=in_specs,
            out_specs=pl.BlockSpec((MOE_TILE, D_MODEL), row)),
        out_shape=jax.ShapeDtypeStruct((t, D_MODEL), F32),
        compiler_params=_params("arbitrary"),
        name="moe_combine",
    )(inv, x2, info, *([ys] * TILE_SLOTS))


def _moe_tables(lens, nt):
    i32 = jnp.int32
    start = jnp.cumsum(lens, axis=1) - lens
    base = jnp.arange(nt, dtype=i32)[:, None] * TILE_SLOTS + start
    run_len = lens.T.reshape(-1)
    run_base = base.T.reshape(-1)
    run_end = jnp.cumsum(run_len)
    run_start = run_end - run_len
    n_g = jnp.sum(lens, axis=0)
    steps_g = (n_g + STEP_GRANULES - 1) // STEP_GRANULES
    step_end = jnp.cumsum(steps_g)
    step_off = step_end - steps_g
    gran_off = jnp.cumsum(n_g) - n_g
    nstep = (nt * (TILE_SLOTS - 1) + STEP_GRANULES - 1) // STEP_GRANULES + N_GROUPS
    s = jnp.arange(nstep, dtype=i32)
    count_le = lambda ends, v: jnp.sum((v[..., None] >= ends).astype(i32), axis=-1)
    sg = jnp.minimum(count_le(step_end, s), N_GROUPS - 1)
    sv = (s < step_end[-1]).astype(i32)
    jl = (s - step_off[sg])[:, None] * STEP_GRANULES + jnp.arange(STEP_GRANULES, dtype=i32)[None, :]
    ok = (sv[:, None] > 0) & (jl < n_g[sg][:, None])
    j = jnp.where(ok, gran_off[sg][:, None] + jl, 0)
    r = jnp.minimum(count_le(run_end, j), run_len.shape[0] - 1)
    gran = run_base[r] + (j - run_start[r])
    gran = jnp.where(ok, gran, jnp.where(sv[:, None] > 0, gran[:, 0:1], 0)).astype(i32)
    q = jnp.arange(TILE_SLOTS, dtype=i32)[None, :]
    cum = jnp.cumsum(lens, axis=1)
    gq = jnp.minimum(jnp.sum((q[:, :, None] >= cum[:, None, :]).astype(i32), axis=-1), N_GROUPS - 1)
    used = q < cum[:, -1:]
    ridx = gq * nt + jnp.arange(nt, dtype=i32)[:, None]
    jloc = run_start[ridx] + (q - jnp.take_along_axis(start, gq, axis=1)) - gran_off[gq]
    inv = jnp.where(used, step_off[gq] * STEP_GRANULES + jloc, 0).astype(i32)
    return sg, sv, gran.reshape(-1), inv.reshape(-1)


def _moe(x2, g2, r_hi, r_lo, r_b, wg, wu, wd):
    nt = x2.shape[0] // MOE_TILE
    xs, info, lens = _moe_sort(x2, g2, r_hi, r_lo, r_b)
    sg, sv, gran, inv = _moe_tables(lens[:, 0:N_GROUPS, 0].astype(jnp.int32), nt)
    ys = _moe_experts(sg, sv, gran, xs, wg, wu, wd)
    return _moe_combine(inv, x2, info, ys)


def _block_diag(w):
    h, n, _ = w.shape
    eye = jnp.eye(h, dtype=w.dtype)
    return (eye[:, None, :, None] * w[:, :, None, :]).reshape(h * n, h * n)


def _router_tables(w_rg, b_rg, w_re, b_re):
    e0 = ROUTER_EXPERT_LANE0
    w = jnp.zeros((D_MODEL, LANES), F32).at[:, 0:N_GROUPS].set(w_rg).at[:, e0:e0 + N_EXPERTS].set(w_re)
    b = jnp.zeros((1, LANES), F32).at[0, 0:N_GROUPS].set(b_rg).at[0, e0:e0 + N_EXPERTS].set(b_re)
    hi = w.astype(BF16)
    lo = (w - hi.astype(F32)).astype(BF16)
    return hi, lo, b


def kernel(x, mem, norm1_g, w_in, conv_w, conv_b, rg_wa, rg_ba, rg_wx, rg_bx, rg_lambda, dq_norm_g, dk_norm_g, lambda_q1, lambda_k1, lambda_q2, lambda_k2, diff_head_norm_g, mem_norm_g, w_mem_kv, mq_norm_g, mk_norm_g, w_out, norm2_g, w_router_group, b_router_group, w_router_expert, b_router_expert, w_expert_gate, w_expert_up, w_expert_down):
    b, s, d = x.shape
    m = mem.shape[1]
    depth = w_in.shape[0]
    x2 = x.reshape(b * s, d)
    mem2 = mem.reshape(b * m, d)
    vec = lambda v: v.reshape(1, -1).astype(F32)
    for l in range(depth):
        lam_init = 0.8 - 0.6 * math.exp(-0.3 * l)
        lam = (jnp.exp(jnp.sum(lambda_q1[l] * lambda_k1[l])) - jnp.exp(jnp.sum(lambda_q2[l] * lambda_k2[l]))
               + lam_init)
        lam = jnp.stack([lam, _score_bound(dq_norm_g[l], dk_norm_g[l])]).reshape(1, 2).astype(F32)
        gw = GROUP_WIDTH
        wl = w_in[l].astype(BF16)
        w_rows = jnp.concatenate([wl[:, 0:2 * gw], wl[:, 3 * gw:4 * gw], wl[:, 5 * gw:6 * gw]], axis=1)
        col = lambda v: v.reshape(-1, 1).astype(F32)
        lru_in, k, mq, qt, vt = _in_proj(x2, b, s, vec(norm1_g[l]), w_rows, wl[:, 2 * gw:3 * gw].T,
                                         wl[:, 4 * gw:5 * gw].T, col(jnp.tile(dq_norm_g[l], gw // DIFF_QK_DIM)),
                                         vec(jnp.tile(dk_norm_g[l], 2)), vec(mq_norm_g[l]))
        y_lru = _lru(lru_in, b, s, conv_w[l], vec(conv_b[l]), _block_diag(rg_wa[l]).astype(BF16), vec(rg_ba[l]),
                     _block_diag(rg_wx[l]).astype(BF16), vec(rg_bx[l]), vec(rg_lambda[l]))
        y_diff = _diff_attn(qt, k, vt, lam, col(diff_head_norm_g[l]), b, s, lam_init)
        mk, mv = _mem_kv(mem2, vec(mem_norm_g), w_mem_kv[l].astype(BF16), vec(mk_norm_g[l]))
        y_mem = _mem_attn(mq, mk, mv, b, s, m)
        x2 = _out_proj(x2, y_lru, y_diff, y_mem, w_out[l].astype(BF16))
        r_hi, r_lo, r_b = _router_tables(w_router_group[l], b_router_group[l], w_router_expert[l], b_router_expert[l])
        epg = EXPERTS_PER_GROUP
        x2 = _moe(x2, vec(norm2_g[l]), r_hi, r_lo, r_b,
                  w_expert_gate[l].astype(BF16).reshape(N_GROUPS, epg, D_MODEL, D_EXPERT),
                  w_expert_up[l].astype(BF16).reshape(N_GROUPS, epg, D_MODEL, D_EXPERT),
                  w_expert_down[l].astype(BF16).reshape(N_GROUPS, epg * D_EXPERT, D_MODEL))
    return x2.reshape(b, s, d)
```

```python
import functools
import math

import jax
import jax.numpy as jnp
from jax import lax
from jax.experimental import pallas as pl
from jax.experimental.pallas import tpu as pltpu

F32 = jnp.float32
BF16 = jnp.bfloat16

D_MODEL = 1024
GROUP_WIDTH = D_MODEL // 2
LRU_HEADS = 8
CONV_WIDTH = 4
LRU_C = 8.0
DIFF_HEADS = 4
DIFF_HEAD_DIM = GROUP_WIDTH // DIFF_HEADS
DIFF_QK_DIM = DIFF_HEAD_DIM // 2
MEM_HEADS = 4
MEM_HEAD_DIM = GROUP_WIDTH // MEM_HEADS
N_GROUPS = 4
EXPERTS_PER_GROUP = 8
N_EXPERTS = N_GROUPS * EXPERTS_PER_GROUP
D_EXPERT = 256
D_IN_PROJ = 6 * GROUP_WIDTH
EPS = 1e-6

LANES = 128
SUBLANES = 8
VMEM_LIMIT = 56 * 1024 * 1024
NEG_BIG = -1e30
ROUTER_EXPERT_LANE0 = 32


def _tile(n, pref):
    t = min(n, pref)
    assert n % t == 0, (n, t)
    return t


def _params(*sem):
    return pltpu.CompilerParams(dimension_semantics=sem, vmem_limit_bytes=VMEM_LIMIT)


def _rms_rows(x, g):
    return x * lax.rsqrt(jnp.mean(x * x, axis=-1, keepdims=True) + EPS) * g


def _lane_iota(shape):
    return lax.broadcasted_iota(jnp.int32, shape, len(shape) - 1)


def _half_head_norm(z, g):
    sq = z * z
    lo = _lane_iota(z.shape) < DIFF_QK_DIM
    s_all = jnp.sum(sq, axis=-1, keepdims=True)
    s_lo = jnp.sum(jnp.where(lo, sq, 0.0), axis=-1, keepdims=True)
    inv_lo = lax.rsqrt(s_lo * (1.0 / DIFF_QK_DIM) + EPS)
    inv_hi = lax.rsqrt((s_all - s_lo) * (1.0 / DIFF_QK_DIM) + EPS)
    return z * jnp.where(lo, inv_lo, inv_hi) * g


SEQ_TILE = 512
_NT = (((1,), (1,)), ((), ()))


def _in_proj_kernel(x_ref, g1_ref, w_ref, wqt_ref, wvt_ref, qg_ref, kg_ref, mqg_ref, mk_ref, mv_ref,
                    lru_ref, k_ref, ym_ref, qt_ref, vt_ref):
    xn = _rms_rows(x_ref[...], g1_ref[...]).astype(BF16)
    gw = GROUP_WIDTH
    tm = xn.shape[0]
    heads = range(MEM_HEADS)
    hsl = [slice(h * MEM_HEAD_DIM, (h + 1) * MEM_HEAD_DIM) for h in heads]
    mq = jnp.dot(xn, w_ref[:, 3 * gw:4 * gw], preferred_element_type=F32)
    lru_ref[...] = jnp.dot(xn, w_ref[:, 0:2 * gw], preferred_element_type=F32).astype(BF16)
    sc = [lax.dot_general(_rms_rows(mq[:, hsl[h]], mqg_ref[...]).astype(BF16), mk_ref[:, hsl[h]], _NT,
                          preferred_element_type=F32) * MEM_HEAD_DIM ** -0.5 for h in heads]
    k = jnp.dot(xn, w_ref[:, 2 * gw:3 * gw], preferred_element_type=F32)
    p = [jnp.exp(sc[h] - jnp.max(sc[h], axis=-1, keepdims=True)) for h in heads]
    qt = lax.dot_general(wqt_ref[...], xn, _NT, preferred_element_type=F32)
    o = [jnp.dot(p[h].astype(BF16), mv_ref[:, hsl[h]], preferred_element_type=F32) for h in heads]
    vt_ref[...] = lax.dot_general(wvt_ref[...], xn, _NT, preferred_element_type=F32).astype(BF16)
    for h in range(DIFF_HEADS):
        sl = slice(h * LANES, (h + 1) * LANES)
        k_ref[:, sl] = _half_head_norm(k[:, sl], kg_ref[...]).astype(BF16)
    for h in heads:
        ym_ref[:, hsl[h]] = (o[h] / jnp.sum(p[h], axis=-1, keepdims=True)).astype(BF16)
    q3 = qt.reshape(gw // DIFF_QK_DIM, DIFF_QK_DIM, tm)
    q3 = q3 * lax.rsqrt(jnp.mean(q3 * q3, axis=1, keepdims=True) + EPS)
    qscale = DIFF_QK_DIM ** -0.5 * math.log2(math.e)
    qt_ref[...] = (q3.reshape(gw, tm) * (qg_ref[...] * qscale)).astype(BF16)


def _in_proj(x2, b, s, g1, w_rows, wqt, wvt, qg_col, kg, mqg, mk, mv):
    t = x2.shape[0]
    tm = SEQ_TILE
    assert s % tm == 0
    ns = s // tm
    m = mk.shape[0] // b
    gw = GROUP_WIDTH
    row = lambda i: (i, 0)
    fixed = lambda i: (0, 0)
    fm = lambda i: (i // ns, i % ns, 0, 0)
    mem = pl.BlockSpec((m, gw), lambda i: (i // ns, 0))
    return pl.pallas_call(
        _in_proj_kernel,
        grid=(t // tm,),
        in_specs=[pl.BlockSpec((tm, D_MODEL), row), pl.BlockSpec((1, D_MODEL), fixed),
                  pl.BlockSpec((D_MODEL, 4 * gw), fixed), pl.BlockSpec((gw, D_MODEL), fixed),
                  pl.BlockSpec((gw, D_MODEL), fixed), pl.BlockSpec((gw, 1), fixed),
                  pl.BlockSpec((1, LANES), fixed), pl.BlockSpec((1, MEM_HEAD_DIM), fixed), mem, mem],
        out_specs=[pl.BlockSpec((tm, 2 * gw), row), pl.BlockSpec((tm, gw), row), pl.BlockSpec((tm, gw), row),
                   pl.BlockSpec((None, None, gw, tm), fm), pl.BlockSpec((None, None, gw, tm), fm)],
        out_shape=[jax.ShapeDtypeStruct((t, 2 * gw), BF16), jax.ShapeDtypeStruct((t, gw), BF16),
                   jax.ShapeDtypeStruct((t, gw), BF16), jax.ShapeDtypeStruct((b, ns, gw, tm), BF16),
                   jax.ShapeDtypeStruct((b, ns, gw, tm), BF16)],
        compiler_params=_params("arbitrary"),
        name="in_proj",
    )(x2, g1, w_rows, wqt, wvt, qg_col, kg, mqg, mk, mv)


def _softplus(z):
    return jnp.maximum(z, 0.0) + jnp.log(1.0 + jnp.exp(-jnp.abs(z)))


def _lru_kernel(u_ref, cw_ref, cb_ref, wa_ref, ba_ref, wx_ref, bx_ref, lam_ref, y_ref,
                ext_ref, hcar_ref, a_ref, b_ref, h_ref):
    ts = y_ref.shape[0]
    gw = GROUP_WIDTH
    hist = SUBLANES

    @pl.when(pl.program_id(1) == 0)
    def _():
        ext_ref[0:hist, :] = jnp.zeros((hist, gw), F32)
        hcar_ref[...] = jnp.zeros_like(hcar_ref)

    @pl.when(pl.program_id(1) != 0)
    def _():
        ext_ref[0:hist, :] = ext_ref[ts:ts + hist, :]

    ext_ref[hist:hist + ts, :] = u_ref[:, 0:gw].astype(F32)
    xc = cb_ref[...] + cw_ref[CONV_WIDTH - 1:CONV_WIDTH, :] * ext_ref[hist:hist + ts, :]
    for j in range(CONV_WIDTH - 1):
        off = hist - (CONV_WIDTH - 1) + j
        xc = xc + cw_ref[j:j + 1, :] * ext_ref[off:off + ts, :]
    xcb = xc.astype(BF16)
    r = jax.nn.sigmoid(jnp.dot(xcb, wa_ref[...], preferred_element_type=F32) + ba_ref[...])
    gate_i = jax.nn.sigmoid(jnp.dot(xcb, wx_ref[...], preferred_element_type=F32) + bx_ref[...])
    log_a = (-LRU_C * r) * _softplus(-lam_ref[...])
    a = jnp.exp(log_a)
    om = 1.0 - a * a
    b = om * lax.rsqrt(jnp.maximum(om, 1e-30)) * (gate_i * xc)

    row = lax.broadcasted_iota(jnp.int32, (ts, gw), 0) & (SUBLANES - 1)
    d = 1
    while d < SUBLANES:
        keep = row >= d
        a_prev = pltpu.roll(a, d, axis=0)
        b_prev = pltpu.roll(b, d, axis=0)
        b = jnp.where(keep, a * b_prev + b, b)
        a = jnp.where(keep, a * a_prev, a)
        d *= 2
    a_ref[...] = a
    b_ref[...] = b

    def block(i, h):
        off = pl.multiple_of(i * SUBLANES, SUBLANES)
        hb = a_ref[pl.ds(off, SUBLANES), :] * h + b_ref[pl.ds(off, SUBLANES), :]
        h_ref[pl.ds(off, SUBLANES), :] = hb
        return jnp.broadcast_to(hb[SUBLANES - 1:SUBLANES, :], (SUBLANES, gw))

    hcar_ref[...] = lax.fori_loop(0, ts // SUBLANES, block, hcar_ref[...], unroll=8)
    y_ref[...] = (h_ref[...] * jax.nn.gelu(u_ref[:, gw:2 * gw].astype(F32))).astype(BF16)


def _lru(lru_in, b, s, conv_w, conv_b, wa_bd, ba, wx_bd, bx, lam):
    ts = _tile(s, 512)
    ns = s // ts
    gw = GROUP_WIDTH
    fixed = lambda bi, si: (0, 0)
    vec = pl.BlockSpec((1, gw), fixed)
    return pl.pallas_call(
        _lru_kernel,
        grid=(b, ns),
        in_specs=[pl.BlockSpec((ts, 2 * gw), lambda bi, si: (bi * ns + si, 0)),
                  pl.BlockSpec((CONV_WIDTH, gw), fixed), vec,
                  pl.BlockSpec((gw, gw), fixed), vec, pl.BlockSpec((gw, gw), fixed), vec, vec],
        out_specs=pl.BlockSpec((ts, gw), lambda bi, si: (bi * ns + si, 0)),
        out_shape=jax.ShapeDtypeStruct((b * s, gw), BF16),
        scratch_shapes=[pltpu.VMEM((ts + SUBLANES, gw), F32), pltpu.VMEM((SUBLANES, gw), F32),
                        pltpu.VMEM((ts, gw), F32), pltpu.VMEM((ts, gw), F32), pltpu.VMEM((ts, gw), F32)],
        compiler_params=_params("arbitrary", "arbitrary"),
        name="rg_lru",
    )(lru_in, conv_w, conv_b, wa_bd, ba, wx_bd, bx, lam)


ATTN_HEADS_PER_STEP = 2
ATTN_MIN_SUM = 2.0 ** -40


def _score_bound(dq_gain, dk_gain):
    qscale = DIFF_QK_DIM ** -0.5 * math.log2(math.e)
    return 1.02 * qscale * DIFF_QK_DIM * jnp.max(jnp.abs(dq_gain)) * jnp.max(jnp.abs(dk_gain))


def _diff_attn_kernel(lam_ref, qt_ref, k_ref, vt_ref, hg_ref, o_ref,
                      qs_ref, sa_ref, sb_ref, m_ref, l_ref, acc_ref, *, t, out_scale):
    qi = pl.program_id(2)
    hd = DIFF_HEAD_DIM
    heads = range(ATTN_HEADS_PER_STEP)
    lo = lax.broadcasted_iota(jnp.int32, (hd, t), 0) < DIFF_QK_DIM
    for h in heads:
        q = qt_ref[h * hd:(h + 1) * hd, :]
        zero = jnp.zeros_like(q)
        qs_ref[h, :, 0:t] = jnp.where(lo, q, zero)
        qs_ref[h, :, t:2 * t] = jnp.where(lo, zero, q)

    def causal(x, fill):
        kpos = lax.broadcasted_iota(jnp.int32, (t, 2 * t), 0)
        c = lax.broadcasted_iota(jnp.int32, (t, 2 * t), 1)
        return jnp.where(kpos <= jnp.where(c >= t, c - t, c), x, fill)

    def score(j, h):
        off = pl.multiple_of(j * t, t)
        return jnp.dot(k_ref[pl.ds(off, t), h * hd:(h + 1) * hd], qs_ref[h],
                       preferred_element_type=F32)

    bound = lam_ref[0, 1]
    l_ref[...] = jnp.zeros(l_ref.shape, F32)
    acc_ref[...] = jnp.zeros(acc_ref.shape, F32)

    def fast_update(j, masked):
        for h in heads:
            p = jnp.exp2(score(j, h) - bound)
            if masked:
                p = causal(p, 0.0)
            l_ref[h] += jnp.sum(p, axis=0, keepdims=True)
            acc_ref[h] += jnp.dot(vt_ref[j, h * hd:(h + 1) * hd, :], p.astype(BF16),
                                  preferred_element_type=F32)

    def fast_pair(i, carry):
        fast_update(2 * i, False)
        fast_update(2 * i + 1, False)
        return carry

    lax.fori_loop(0, qi >> 1, fast_pair, 0)

    @pl.when((qi & 1) == 1)
    def _():
        fast_update(qi - 1, False)

    fast_update(qi, True)

    @pl.when(jnp.min(l_ref[...]) < ATTN_MIN_SUM)
    def _():
        m_ref[...] = jnp.full(m_ref.shape, NEG_BIG, F32)
        l_ref[...] = jnp.zeros(l_ref.shape, F32)
        acc_ref[...] = jnp.zeros(acc_ref.shape, F32)

        def scores(j, dst_ref):
            for h in heads:
                dst_ref[h] = score(j, h)

        def update(j, src_ref, masked):
            for h in heads:
                s = src_ref[h]
                if masked:
                    s = causal(s, NEG_BIG)
                m = m_ref[h]
                m_new = jnp.maximum(m, jnp.max(s, axis=0, keepdims=True))
                alpha = jnp.exp2(m - m_new)
                p = jnp.exp2(s - m_new)
                m_ref[h] = m_new
                l_ref[h] = alpha * l_ref[h] + jnp.sum(p, axis=0, keepdims=True)
                vt = vt_ref[j, h * hd:(h + 1) * hd, :]
                acc_ref[h] = alpha * acc_ref[h] + jnp.dot(vt, p.astype(BF16), preferred_element_type=F32)

        scores(0, sa_ref)

        def pair(i, carry):
            scores(2 * i + 1, sb_ref)
            update(2 * i, sa_ref, False)
            scores(2 * i + 2, sa_ref)
            update(2 * i + 1, sb_ref, False)
            return carry

        lax.fori_loop(0, qi >> 1, pair, 0)

        @pl.when((qi & 1) == 1)
        def _():
            scores(qi, sb_ref)
            update(qi - 1, sa_ref, False)
            update(qi, sb_ref, True)

        @pl.when((qi & 1) == 0)
        def _():
            update(qi, sa_ref, True)

    for h in heads:
        o = acc_ref[h] / l_ref[h]
        o = o[:, 0:t] - lam_ref[0, 0] * o[:, t:2 * t]
        o = o * lax.rsqrt(jnp.mean(o * o, axis=0, keepdims=True) + EPS) * (hg_ref[...] * out_scale)
        o_ref[:, h * hd:(h + 1) * hd] = o.T.astype(BF16)


def _diff_attn(qt, k, vt, lam, hg_col, b, s, lam_init):
    t = SEQ_TILE
    nq = s // t
    hp = ATTN_HEADS_PER_STEP
    w = hp * DIFF_HEAD_DIM
    ng = DIFF_HEADS // hp
    kern = functools.partial(_diff_attn_kernel, t=t, out_scale=1.0 - lam_init)
    return pl.pallas_call(
        kern,
        grid=(b, ng, nq),
        in_specs=[pl.BlockSpec(memory_space=pltpu.SMEM),
                  pl.BlockSpec((None, None, w, t), lambda bi, g, qi: (bi, qi, g, 0)),
                  pl.BlockSpec((s, w), lambda bi, g, qi: (bi, g)),
                  pl.BlockSpec((None, nq, w, t), lambda bi, g, qi: (bi, 0, g, 0)),
                  pl.BlockSpec((DIFF_HEAD_DIM, 1), lambda bi, g, qi: (0, 0))],
        out_specs=pl.BlockSpec((t, w), lambda bi, g, qi: (bi * nq + qi, g)),
        out_shape=jax.ShapeDtypeStruct((b * s, GROUP_WIDTH), BF16),
        scratch_shapes=[pltpu.VMEM((hp, DIFF_HEAD_DIM, 2 * t), BF16),
                        pltpu.VMEM((hp, t, 2 * t), F32), pltpu.VMEM((hp, t, 2 * t), F32),
                        pltpu.VMEM((hp, 1, 2 * t), F32), pltpu.VMEM((hp, 1, 2 * t), F32),
                        pltpu.VMEM((hp, DIFF_HEAD_DIM, 2 * t), F32)],
        compiler_params=_params("arbitrary", "arbitrary", "arbitrary"),
        name="diff_attn",
    )(lam, qt, k, vt, hg_col)


def _mem_kv_kernel(mem_ref, mg_ref, w_ref, kg_ref, mk_ref, mv_ref):
    memn = _rms_rows(mem_ref[...], mg_ref[...]).astype(BF16)
    gw = GROUP_WIDTH
    k = jnp.dot(memn, w_ref[:, 0:gw], preferred_element_type=F32)
    for h in range(MEM_HEADS):
        sl = slice(h * MEM_HEAD_DIM, (h + 1) * MEM_HEAD_DIM)
        mk_ref[:, sl] = _rms_rows(k[:, sl], kg_ref[...]).astype(BF16)
    mv_ref[...] = jnp.dot(memn, w_ref[:, gw:2 * gw], preferred_element_type=F32).astype(BF16)


def _mem_kv(mem2, mg, w_kv, kg):
    r = mem2.shape[0]
    tm = _tile(r, 512)
    row = lambda i: (i, 0)
    fixed = lambda i: (0, 0)
    gw = GROUP_WIDTH
    return pl.pallas_call(
        _mem_kv_kernel,
        grid=(r // tm,),
        in_specs=[pl.BlockSpec((tm, D_MODEL), row), pl.BlockSpec((1, D_MODEL), fixed),
                  pl.BlockSpec((D_MODEL, 2 * gw), fixed), pl.BlockSpec((1, MEM_HEAD_DIM), fixed)],
        out_specs=[pl.BlockSpec((tm, gw), row), pl.BlockSpec((tm, gw), row)],
        out_shape=[jax.ShapeDtypeStruct((r, gw), BF16), jax.ShapeDtypeStruct((r, gw), BF16)],
        compiler_params=_params("arbitrary"),
        name="mem_kv",
    )(mem2, mg, w_kv, kg)


MOE_TILE = 256
GRANULE = 16
TILE_SLOTS = MOE_TILE // GRANULE + N_GROUPS
SORTED_ROWS = TILE_SLOTS * GRANULE
STEP_GRANULES = 16
STEP_ROWS = STEP_GRANULES * GRANULE
XS_WIDTH = D_MODEL + LANES
INFO_POS_LANE = 0


def _route_t(lt):
    row_i = lax.broadcasted_iota(jnp.int32, lt.shape, 0)
    row = row_i.astype(F32)
    e0 = ROUTER_EXPERT_LANE0
    lg = jnp.where(row_i < N_GROUPS, lt, NEG_BIG)
    mg = jnp.max(lg, axis=0, keepdims=True)
    g_gate = 1.0 / jnp.sum(jnp.exp(lg - mg), axis=0, keepdims=True)
    g_idx = jnp.min(jnp.where(lg == mg, row, float(LANES)), axis=0, keepdims=True)
    row_group = ((row_i - e0) >> 3).astype(F32)
    sel = (row_i >= e0) & (row_i < e0 + N_EXPERTS) & (row_group == g_idx)
    le = jnp.where(sel, lt, NEG_BIG)
    m1 = jnp.max(le, axis=0, keepdims=True)
    se = jnp.sum(jnp.where(sel, jnp.exp(le - m1), 0.0), axis=0, keepdims=True)
    i1 = jnp.min(jnp.where(sel & (le == m1), row, float(LANES)), axis=0, keepdims=True)
    le2 = jnp.where(row == i1, NEG_BIG, le)
    m2 = jnp.max(le2, axis=0, keepdims=True)
    i2 = jnp.min(jnp.where(sel & (le2 == m2) & (row != i1), row, float(LANES)), axis=0, keepdims=True)
    p1 = 1.0 / se
    p2 = jnp.exp(m2 - m1) / se
    tot = p1 + p2
    w = jnp.where(row == i1, p1 / tot, jnp.where(row == i2, p2 / tot, 0.0))
    return g_gate * w, g_idx


def _split3(c):
    hi = c.astype(BF16).astype(F32)
    r1 = c - hi
    mid = r1.astype(BF16).astype(F32)
    lo = (r1 - mid).astype(BF16).astype(F32)
    return hi, mid, lo


def _sort_logits(x1, g2, rt_hi, rt_lo, rb_col):
    xn = _rms_rows(x1, g2)
    hi = xn.astype(BF16)
    lo = (xn - hi.astype(F32)).astype(BF16)
    logits_t = (lax.dot_general(rt_hi, hi, _NT, preferred_element_type=F32)
                + lax.dot_general(rt_hi, lo, _NT, preferred_element_type=F32)
                + lax.dot_general(rt_lo, hi, _NT, preferred_element_type=F32) + rb_col)
    return hi, logits_t


def _sort_rank(logits_t, earlier):
    tl = logits_t.shape[1]
    comb_t, g_idx = _route_t(logits_t)
    grow = lax.broadcasted_iota(jnp.int32, (SUBLANES, tl), 0).astype(F32)
    gt = jnp.where(grow == g_idx, 1.0, 0.0)
    before = jnp.dot(gt.astype(BF16), earlier, preferred_element_type=F32)
    return comb_t, gt, before


def _sort_emit(hi, comb_t, gt, before):
    tl = hi.shape[0]
    rank = jnp.sum(gt * before, axis=0, keepdims=True)
    cnt = jnp.sum(gt, axis=1, keepdims=True)
    glen = jnp.floor((cnt + (GRANULE - 1)) * (1.0 / GRANULE))
    r8 = lax.broadcasted_iota(jnp.int32, (SUBLANES, 1), 0)
    start = jnp.zeros((SUBLANES, 1), F32)
    for g in range(1, N_GROUPS):
        start = jnp.where(r8 == g, jnp.sum(jnp.where(r8 < g, glen, 0.0), axis=0, keepdims=True), start)
    pos = jnp.sum(gt * (start * GRANULE), axis=0, keepdims=True) + rank

    rows = lax.broadcasted_iota(jnp.int32, (LANES, tl), 0)
    info = jnp.where(rows == INFO_POS_LANE, pos, comb_t).T
    lane = _lane_iota(info.shape)
    e0 = ROUTER_EXPERT_LANE0
    c_hi, c_mid, c_lo = _split3(jnp.where((lane >= e0) & (lane < e0 + N_EXPERTS), info, 0.0))
    aug = (c_hi + pltpu.roll(c_mid, N_EXPERTS, axis=1) + pltpu.roll(c_lo, 2 * N_EXPERTS, axis=1)).astype(BF16)
    perm = jnp.where(pos == lax.broadcasted_iota(jnp.int32, (SORTED_ROWS, tl), 0).astype(F32),
                     1.0, 0.0).astype(BF16)
    xs = jnp.dot(perm, hi, preferred_element_type=F32).astype(BF16)
    xs_aug = jnp.dot(perm, aug, preferred_element_type=F32).astype(BF16)
    return xs, xs_aug, info, glen


def _out_sort_kernel(x_ref, yl_ref, yd_ref, ym_ref, w_ref, g2_ref, rthi_ref, rtlo_ref, rb_ref, earlier_ref,
                     o_ref, xs_ref, info_ref, lens_ref, x1s_ref):
    @pl.when(pl.program_id(0) == 0)
    def _():
        x1s_ref[...] = jnp.zeros_like(x1s_ref)

    tm = x1s_ref.shape[0]
    subs = range(tm // MOE_TILE)

    def project(c, nchunks):
        r = slice(c * (tm // nchunks), (c + 1) * (tm // nchunks))
        y = jnp.concatenate([yl_ref[r, :], yd_ref[r, :], ym_ref[r, :]], axis=-1)
        x1 = x_ref[r, :] + jnp.dot(y, w_ref[...], preferred_element_type=F32)
        o_ref[r, :] = x1
        return r, x1

    fresh = [project(0, 2)]
    s1 = [_sort_logits(x1s_ref[sub * MOE_TILE:(sub + 1) * MOE_TILE, :], g2_ref[...], rthi_ref[...],
                       rtlo_ref[...], rb_ref[...]) for sub in subs]
    for r, x1 in fresh:
        x1s_ref[r, :] = x1
    fresh = [project(1, 2)]
    s2 = [_sort_rank(s1[sub][1], earlier_ref[...]) for sub in subs]
    for r, x1 in fresh:
        x1s_ref[r, :] = x1
    for sub in subs:
        xs, xs_aug, info, glen = _sort_emit(s1[sub][0], *s2[sub])
        r0 = sub * SORTED_ROWS
        xs_ref[r0:r0 + SORTED_ROWS, 0:D_MODEL] = xs
        xs_ref[r0:r0 + SORTED_ROWS, D_MODEL:XS_WIDTH] = xs_aug
        info_ref[sub * MOE_TILE:(sub + 1) * MOE_TILE, :] = info
        lens_ref[sub] = jnp.broadcast_to(glen, (SUBLANES, LANES))


def _out_sort(x2, y_lru, y_diff, y_mem, w_out, g2, rt_hi, rt_lo, rb_col):
    t = x2.shape[0]
    tm = SEQ_TILE
    assert t % tm == 0 and tm % MOE_TILE == 0
    sub = tm // MOE_TILE
    nt = t // MOE_TILE
    n = t // tm
    cur = lambda i: (jnp.minimum(i, n - 1), 0)
    prev = lambda i: (jnp.maximum(i - 1, 0), 0)
    fixed = lambda i: (0, 0)
    gw = GROUP_WIDTH
    earlier = jnp.triu(jnp.ones((MOE_TILE, MOE_TILE), BF16), k=1)
    return pl.pallas_call(
        _out_sort_kernel,
        grid=(n + 1,),
        in_specs=[pl.BlockSpec((tm, D_MODEL), cur), pl.BlockSpec((tm, gw), cur), pl.BlockSpec((tm, gw), cur),
                  pl.BlockSpec((tm, gw), cur), pl.BlockSpec((3 * gw, D_MODEL), fixed),
                  pl.BlockSpec((1, D_MODEL), fixed), pl.BlockSpec((LANES, D_MODEL), fixed),
                  pl.BlockSpec((LANES, D_MODEL), fixed), pl.BlockSpec((LANES, 1), fixed),
                  pl.BlockSpec((MOE_TILE, MOE_TILE), fixed)],
        out_specs=[pl.BlockSpec((tm, D_MODEL), cur), pl.BlockSpec((sub * SORTED_ROWS, XS_WIDTH), prev),
                   pl.BlockSpec((tm, LANES), prev),
                   pl.BlockSpec((sub, SUBLANES, LANES), lambda i: (jnp.maximum(i - 1, 0), 0, 0))],
        out_shape=[jax.ShapeDtypeStruct((t, D_MODEL), F32),
                   jax.ShapeDtypeStruct((nt * SORTED_ROWS, XS_WIDTH), BF16),
                   jax.ShapeDtypeStruct((t, LANES), F32),
                   jax.ShapeDtypeStruct((nt, SUBLANES, LANES), F32)],
        scratch_shapes=[pltpu.VMEM((tm, D_MODEL), F32)],
        compiler_params=_params("arbitrary"),
        name="out_proj_sort",
    )(x2, y_lru, y_diff, y_mem, w_out, g2, rt_hi, rt_lo, rb_col, earlier)


def _moe_expert_kernel(sg_ref, sv_ref, gi_ref, *refs):
    del gi_ref
    xs_refs = refs[:STEP_GRANULES]
    wg_ref, wu_ref, wd_ref, y_ref = refs[STEP_GRANULES:]
    s = pl.program_id(0)

    @pl.when(sv_ref[s] == 0)
    def _():
        y_ref[...] = jnp.zeros_like(y_ref)

    @pl.when(sv_ref[s] != 0)
    def _():
        rows = jnp.concatenate([r[...] for r in xs_refs], axis=0)
        x = rows[:, 0:D_MODEL]
        aug = rows[:, D_MODEL:XS_WIDTH].astype(F32)
        lane = _lane_iota(aug.shape)
        e0 = ROUTER_EXPERT_LANE0
        comb = jnp.where((lane >= e0) & (lane < e0 + N_EXPERTS),
                         aug + pltpu.roll(aug, LANES - N_EXPERTS, axis=1)
                         + pltpu.roll(aug, LANES - 2 * N_EXPERTS, axis=1), 0.0)
        first = e0 + sg_ref[s] * EXPERTS_PER_GROUP
        hs = []
        for e in range(EXPERTS_PER_GROUP):
            cw = jnp.sum(jnp.where(lane == first + e, comb, 0.0), axis=-1, keepdims=True)
            hg = jnp.dot(x, wg_ref[e], preferred_element_type=F32)
            hu = jnp.dot(x, wu_ref[e], preferred_element_type=F32)
            hs.append((jax.nn.silu(hg) * hu * cw).astype(BF16))
        h = jnp.concatenate(hs, axis=-1)
        y_ref[...] = jnp.dot(h, wd_ref[...], preferred_element_type=F32).astype(BF16)


def _granule_map(k, s, sg, sv, gi):
    return (gi[s * STEP_GRANULES + k], 0)


def _moe_experts(step_group, step_valid, gran_idx, xs, wg, wu, wd):
    nstep = step_group.shape[0]
    epg = EXPERTS_PER_GROUP
    wmap4 = lambda s, sg, sv, gi: (sg[s], 0, 0, 0)
    in_specs = [pl.BlockSpec((GRANULE, XS_WIDTH), functools.partial(_granule_map, k)) for k in range(STEP_GRANULES)]
    in_specs += [pl.BlockSpec((None, epg, D_MODEL, D_EXPERT), wmap4),
                 pl.BlockSpec((None, epg, D_MODEL, D_EXPERT), wmap4),
                 pl.BlockSpec((None, epg * D_EXPERT, D_MODEL), lambda s, sg, sv, gi: (sg[s], 0, 0))]
    return pl.pallas_call(
        _moe_expert_kernel,
        grid_spec=pltpu.PrefetchScalarGridSpec(
            num_scalar_prefetch=3, grid=(nstep,), in_specs=in_specs,
            out_specs=pl.BlockSpec((STEP_ROWS, D_MODEL), lambda s, sg, sv, gi: (s, 0))),
        out_shape=jax.ShapeDtypeStruct((nstep * STEP_ROWS, D_MODEL), BF16),
        compiler_params=_params("arbitrary"),
        name="moe_experts",
    )(step_group, step_valid, gran_idx, *([xs] * STEP_GRANULES), wg, wu, wd)


def _moe_combine_kernel(inv_ref, x_ref, info_ref, *refs):
    del inv_ref
    ys_refs = refs[:TILE_SLOTS]
    o_ref = refs[TILE_SLOTS]
    kpad = -SORTED_ROWS % LANES
    ys = jnp.concatenate([r[...] for r in ys_refs] + [jnp.zeros((kpad, D_MODEL), BF16)], axis=0)
    pos = info_ref[:, INFO_POS_LANE:INFO_POS_LANE + 1]
    unperm = jnp.where(pos == _lane_iota((MOE_TILE, SORTED_ROWS + kpad)).astype(F32), 1.0, 0.0).astype(BF16)
    o_ref[...] = x_ref[...] + jnp.dot(unperm, ys, preferred_element_type=F32)


def _slot_map(q, i, inv):
    return (inv[i * TILE_SLOTS + q], 0)


def _moe_combine(inv, x2, info, ys):
    t = x2.shape[0]
    nt = t // MOE_TILE
    row = lambda i, inv: (i, 0)
    in_specs = [pl.BlockSpec((MOE_TILE, D_MODEL), row), pl.BlockSpec((MOE_TILE, LANES), row)]
    in_specs += [pl.BlockSpec((GRANULE, D_MODEL), functools.partial(_slot_map, q)) for q in range(TILE_SLOTS)]
    return pl.pallas_call(
        _moe_combine_kernel,
        grid_spec=pltpu.PrefetchScalarGridSpec(
            num_scalar_prefetch=1, grid=(nt,), in_specs=in_specs,
            out_specs=pl.BlockSpec((MOE_TILE, D_MODEL), row)),
        out_shape=jax.ShapeDtypeStruct((t, D_MODEL), F32),
        compiler_params=_params("arbitrary"),
        name="moe_combine",
    )(inv, x2, info, *([ys] * TILE_SLOTS))


def _moe_tables(lens, nt):
    i32 = jnp.int32
    start = jnp.cumsum(lens, axis=1) - lens
    base = jnp.arange(nt, dtype=i32)[:, None] * TILE_SLOTS + start
    run_len = lens.T.reshape(-1)
    run_base = base.T.reshape(-1)
    run_end = jnp.cumsum(run_len)
    run_start = run_end - run_len
    n_g = jnp.sum(lens, axis=0)
    steps_g = (n_g + STEP_GRANULES - 1) // STEP_GRANULES
    step_end = jnp.cumsum(steps_g)
    step_off = step_end - steps_g
    gran_off = jnp.cumsum(n_g) - n_g
    nstep = (nt * (TILE_SLOTS - 1) + STEP_GRANULES - 1) // STEP_GRANULES + N_GROUPS
    s = jnp.arange(nstep, dtype=i32)
    count_le = lambda ends, v: jnp.sum((v[..., None] >= ends).astype(i32), axis=-1)
    sg = jnp.minimum(count_le(step_end, s), N_GROUPS - 1)
    sv = (s < step_end[-1]).astype(i32)
    jl = (s - step_off[sg])[:, None] * STEP_GRANULES + jnp.arange(STEP_GRANULES, dtype=i32)[None, :]
    ok = (sv[:, None] > 0) & (jl < n_g[sg][:, None])
    j = jnp.where(ok, gran_off[sg][:, None] + jl, 0)
    r = jnp.minimum(count_le(run_end, j), run_len.shape[0] - 1)
    gran = run_base[r] + (j - run_start[r])
    gran = jnp.where(ok, gran, jnp.where(sv[:, None] > 0, gran[:, 0:1], 0)).astype(i32)
    q = jnp.arange(TILE_SLOTS, dtype=i32)[None, :]
    cum = jnp.cumsum(lens, axis=1)
    gq = jnp.minimum(jnp.sum((q[:, :, None] >= cum[:, None, :]).astype(i32), axis=-1), N_GROUPS - 1)
    used = q < cum[:, -1:]
    ridx = gq * nt + jnp.arange(nt, dtype=i32)[:, None]
    jloc = run_start[ridx] + (q - jnp.take_along_axis(start, gq, axis=1)) - gran_off[gq]
    inv = jnp.where(used, step_off[gq] * STEP_GRANULES + jloc, 0).astype(i32)
    return sg, sv, gran.reshape(-1), inv.reshape(-1)


def _out_proj_moe(x2, y_lru, y_diff, y_mem, w_out, g2, rt_hi, rt_lo, rb_col, wg, wu, wd):
    nt = x2.shape[0] // MOE_TILE
    x1, xs, info, lens = _out_sort(x2, y_lru, y_diff, y_mem, w_out, g2, rt_hi, rt_lo, rb_col)
    sg, sv, gran, inv = _moe_tables(lens[:, 0:N_GROUPS, 0].astype(jnp.int32), nt)
    ys = _moe_experts(sg, sv, gran, xs, wg, wu, wd)
    return _moe_combine(inv, x1, info, ys)


def _block_diag(w):
    h, n, _ = w.shape
    eye = jnp.eye(h, dtype=w.dtype)
    return (eye[:, None, :, None] * w[:, :, None, :]).reshape(h * n, h * n)


def _router_tables(w_rg, b_rg, w_re, b_re):
    e0 = ROUTER_EXPERT_LANE0
    w = jnp.zeros((LANES, D_MODEL), F32).at[0:N_GROUPS].set(w_rg.T).at[e0:e0 + N_EXPERTS].set(w_re.T)
    b = jnp.zeros((LANES, 1), F32).at[0:N_GROUPS, 0].set(b_rg).at[e0:e0 + N_EXPERTS, 0].set(b_re)
    hi = w.astype(BF16)
    lo = (w - hi.astype(F32)).astype(BF16)
    return hi, lo, b


def kernel(x, mem, norm1_g, w_in, conv_w, conv_b, rg_wa, rg_ba, rg_wx, rg_bx, rg_lambda, dq_norm_g, dk_norm_g, lambda_q1, lambda_k1, lambda_q2, lambda_k2, diff_head_norm_g, mem_norm_g, w_mem_kv, mq_norm_g, mk_norm_g, w_out, norm2_g, w_router_group, b_router_group, w_router_expert, b_router_expert, w_expert_gate, w_expert_up, w_expert_down):
    b, s, d = x.shape
    m = mem.shape[1]
    depth = w_in.shape[0]
    x2 = x.reshape(b * s, d)
    mem2 = mem.reshape(b * m, d)
    vec = lambda v: v.reshape(1, -1).astype(F32)
    for l in range(depth):
        lam_init = 0.8 - 0.6 * math.exp(-0.3 * l)
        lam = (jnp.exp(jnp.sum(lambda_q1[l] * lambda_k1[l])) - jnp.exp(jnp.sum(lambda_q2[l] * lambda_k2[l]))
               + lam_init)
        lam = jnp.stack([lam, _score_bound(dq_norm_g[l], dk_norm_g[l])]).reshape(1, 2).astype(F32)
        gw = GROUP_WIDTH
        wl = w_in[l].astype(BF16)
        w_rows = jnp.concatenate([wl[:, 0:2 * gw], wl[:, 3 * gw:4 * gw], wl[:, 5 * gw:6 * gw]], axis=1)
        col = lambda v: v.reshape(-1, 1).astype(F32)
        mk, mv = _mem_kv(mem2, vec(mem_norm_g), w_mem_kv[l].astype(BF16), vec(mk_norm_g[l]))
        lru_in, k, y_mem, qt, vt = _in_proj(x2, b, s, vec(norm1_g[l]), w_rows, wl[:, 2 * gw:3 * gw].T,
                                            wl[:, 4 * gw:5 * gw].T, col(jnp.tile(dq_norm_g[l], gw // DIFF_QK_DIM)),
                                            vec(jnp.tile(dk_norm_g[l], 2)), vec(mq_norm_g[l]), mk, mv)
        y_lru = _lru(lru_in, b, s, conv_w[l], vec(conv_b[l]), _block_diag(rg_wa[l]).astype(BF16), vec(rg_ba[l]),
                     _block_diag(rg_wx[l]).astype(BF16), vec(rg_bx[l]), vec(rg_lambda[l]))
        y_diff = _diff_attn(qt, k, vt, lam, col(diff_head_norm_g[l]), b, s, lam_init)
        rt_hi, rt_lo, rb_col = _router_tables(w_router_group[l], b_router_group[l], w_router_expert[l],
                                              b_router_expert[l])
        epg = EXPERTS_PER_GROUP
        x2 = _out_proj_moe(x2, y_lru, y_diff, y_mem, w_out[l].astype(BF16), vec(norm2_g[l]), rt_hi, rt_lo, rb_col,
                           w_expert_gate[l].astype(BF16).reshape(N_GROUPS, epg, D_MODEL, D_EXPERT),
                           w_expert_up[l].astype(BF16).reshape(N_GROUPS, epg, D_MODEL, D_EXPERT),
                           w_expert_down[l].astype(BF16).reshape(N_GROUPS, epg * D_EXPERT, D_MODEL))
    return x2.reshape(b, s, d)
```

```python
import functools
import math

import jax
import jax.numpy as jnp
from jax import lax
from jax.experimental import pallas as pl
from jax.experimental.pallas import tpu as pltpu

F32 = jnp.float32
BF16 = jnp.bfloat16

D_MODEL = 1024
GROUP_WIDTH = D_MODEL // 2
LRU_HEADS = 8
CONV_WIDTH = 4
LRU_C = 8.0
DIFF_HEADS = 4
DIFF_HEAD_DIM = GROUP_WIDTH // DIFF_HEADS
DIFF_QK_DIM = DIFF_HEAD_DIM // 2
MEM_HEADS = 4
MEM_HEAD_DIM = GROUP_WIDTH // MEM_HEADS
N_GROUPS = 4
EXPERTS_PER_GROUP = 8
N_EXPERTS = N_GROUPS * EXPERTS_PER_GROUP
D_EXPERT = 256
D_IN_PROJ = 6 * GROUP_WIDTH
EPS = 1e-6

LANES = 128
SUBLANES = 8
VMEM_LIMIT = 56 * 1024 * 1024
NEG_BIG = -1e30
ROUTER_EXPERT_LANE0 = 32


def _tile(n, pref):
    t = min(n, pref)
    assert n % t == 0, (n, t)
    return t


def _params(*sem):
    return pltpu.CompilerParams(dimension_semantics=sem, vmem_limit_bytes=VMEM_LIMIT)


def _lspec(l, tail, blk=None):
    idx = (l,) + tuple(blk if blk is not None else (0,) * len(tail))
    return pl.BlockSpec((None,) + tuple(tail), lambda *_: idx)


def _rms_rows(x, g):
    return x * lax.rsqrt(jnp.mean(x * x, axis=-1, keepdims=True) + EPS) * g


def _lane_iota(shape):
    return lax.broadcasted_iota(jnp.int32, shape, len(shape) - 1)


def _half_head_norm(z, g):
    sq = z * z
    lo = _lane_iota(z.shape) < DIFF_QK_DIM
    s_all = jnp.sum(sq, axis=-1, keepdims=True)
    s_lo = jnp.sum(jnp.where(lo, sq, 0.0), axis=-1, keepdims=True)
    inv_lo = lax.rsqrt(s_lo * (1.0 / DIFF_QK_DIM) + EPS)
    inv_hi = lax.rsqrt((s_all - s_lo) * (1.0 / DIFF_QK_DIM) + EPS)
    return z * jnp.where(lo, inv_lo, inv_hi) * g


SEQ_TILE = 512
_NT = (((1,), (1,)), ((), ()))


def _in_proj_kernel(x_ref, g1_ref, wl_ref, wk_ref, wmq_ref, wqt_ref, wvt_ref, qg_ref, kg_ref, mqg_ref,
                    mk_ref, mv_ref, lru_ref, k_ref, ym_ref, qt_ref, vt_ref):
    xn = _rms_rows(x_ref[...], g1_ref[...]).astype(BF16)
    gw = GROUP_WIDTH
    tm = xn.shape[0]
    heads = range(MEM_HEADS)
    hsl = [slice(h * MEM_HEAD_DIM, (h + 1) * MEM_HEAD_DIM) for h in heads]
    mq = jnp.dot(xn, wmq_ref[...], preferred_element_type=F32)
    lru_ref[...] = jnp.dot(xn, wl_ref[...], preferred_element_type=F32).astype(BF16)
    sc = [lax.dot_general(_rms_rows(mq[:, hsl[h]], mqg_ref[...]).astype(BF16), mk_ref[:, hsl[h]], _NT,
                          preferred_element_type=F32) * MEM_HEAD_DIM ** -0.5 for h in heads]
    k = jnp.dot(xn, wk_ref[...], preferred_element_type=F32)
    p = [jnp.exp(sc[h] - jnp.max(sc[h], axis=-1, keepdims=True)) for h in heads]
    qt = lax.dot_general(wqt_ref[...], xn, _NT, preferred_element_type=F32)
    o = [jnp.dot(p[h].astype(BF16), mv_ref[:, hsl[h]], preferred_element_type=F32) for h in heads]
    vt_ref[...] = lax.dot_general(wvt_ref[...], xn, _NT, preferred_element_type=F32).astype(BF16)
    for h in range(DIFF_HEADS):
        sl = slice(h * LANES, (h + 1) * LANES)
        k_ref[:, sl] = _half_head_norm(k[:, sl], kg_ref[...]).astype(BF16)
    for h in heads:
        ym_ref[:, hsl[h]] = (o[h] / jnp.sum(p[h], axis=-1, keepdims=True)).astype(BF16)
    q3 = qt.reshape(gw // DIFF_QK_DIM, DIFF_QK_DIM, tm)
    q3 = q3 * lax.rsqrt(jnp.mean(q3 * q3, axis=1, keepdims=True) + EPS)
    qscale = DIFF_QK_DIM ** -0.5 * math.log2(math.e)
    qt_ref[...] = (q3.reshape(gw, tm) * (qg_ref[...] * qscale)).astype(BF16)


def _in_proj(x2, b, s, l, g1, w_in, w_in_t, qg_col, kg, mqg, mk, mv):
    t = x2.shape[0]
    tm = SEQ_TILE
    assert s % tm == 0
    ns = s // tm
    m = mk.shape[0] // b
    gw = GROUP_WIDTH
    row = lambda i: (i, 0)
    fm = lambda i: (i // ns, i % ns, 0, 0)
    mem = pl.BlockSpec((m, gw), lambda i: (i // ns, 0))
    return pl.pallas_call(
        _in_proj_kernel,
        grid=(t // tm,),
        in_specs=[pl.BlockSpec((tm, D_MODEL), row), _lspec(l, (1, D_MODEL)),
                  _lspec(l, (D_MODEL, 2 * gw), (0, 0)), _lspec(l, (D_MODEL, gw), (0, 3)),
                  _lspec(l, (D_MODEL, gw), (0, 5)), _lspec(l, (gw, D_MODEL), (2, 0)),
                  _lspec(l, (gw, D_MODEL), (4, 0)), _lspec(l, (gw, 1)),
                  _lspec(l, (1, LANES)), _lspec(l, (1, MEM_HEAD_DIM)), mem, mem],
        out_specs=[pl.BlockSpec((tm, 2 * gw), row), pl.BlockSpec((tm, gw), row), pl.BlockSpec((tm, gw), row),
                   pl.BlockSpec((None, None, gw, tm), fm), pl.BlockSpec((None, None, gw, tm), fm)],
        out_shape=[jax.ShapeDtypeStruct((t, 2 * gw), BF16), jax.ShapeDtypeStruct((t, gw), BF16),
                   jax.ShapeDtypeStruct((t, gw), BF16), jax.ShapeDtypeStruct((b, ns, gw, tm), BF16),
                   jax.ShapeDtypeStruct((b, ns, gw, tm), BF16)],
        compiler_params=_params("arbitrary"),
        name="in_proj",
    )(x2, g1, w_in, w_in, w_in, w_in_t, w_in_t, qg_col, kg, mqg, mk, mv)


def _softplus(z):
    return jnp.maximum(z, 0.0) + jnp.log(1.0 + jnp.exp(-jnp.abs(z)))


def _lru_kernel(u_ref, cw_ref, cb_ref, wa_ref, ba_ref, wx_ref, bx_ref, lam_ref, y_ref,
                ext_ref, hcar_ref, a_ref, b_ref, h_ref):
    ts = y_ref.shape[0]
    gw = GROUP_WIDTH
    hist = SUBLANES

    @pl.when(pl.program_id(1) == 0)
    def _():
        ext_ref[0:hist, :] = jnp.zeros((hist, gw), F32)
        hcar_ref[...] = jnp.zeros_like(hcar_ref)

    @pl.when(pl.program_id(1) != 0)
    def _():
        ext_ref[0:hist, :] = ext_ref[ts:ts + hist, :]

    ext_ref[hist:hist + ts, :] = u_ref[:, 0:gw].astype(F32)
    xc = cb_ref[...] + cw_ref[CONV_WIDTH - 1:CONV_WIDTH, :] * ext_ref[hist:hist + ts, :]
    for j in range(CONV_WIDTH - 1):
        off = hist - (CONV_WIDTH - 1) + j
        xc = xc + cw_ref[j:j + 1, :] * ext_ref[off:off + ts, :]
    xcb = xc.astype(BF16)
    r = jax.nn.sigmoid(jnp.dot(xcb, wa_ref[...], preferred_element_type=F32) + ba_ref[...])
    gate_i = jax.nn.sigmoid(jnp.dot(xcb, wx_ref[...], preferred_element_type=F32) + bx_ref[...])
    log_a = (-LRU_C * r) * _softplus(-lam_ref[...])
    a = jnp.exp(log_a)
    om = 1.0 - a * a
    b = om * lax.rsqrt(jnp.maximum(om, 1e-30)) * (gate_i * xc)

    row = lax.broadcasted_iota(jnp.int32, (ts, gw), 0) & (SUBLANES - 1)
    d = 1
    while d < SUBLANES:
        keep = row >= d
        a_prev = pltpu.roll(a, d, axis=0)
        b_prev = pltpu.roll(b, d, axis=0)
        b = jnp.where(keep, a * b_prev + b, b)
        a = jnp.where(keep, a * a_prev, a)
        d *= 2
    a_ref[...] = a
    b_ref[...] = b

    def block(i, h):
        off = pl.multiple_of(i * SUBLANES, SUBLANES)
        hb = a_ref[pl.ds(off, SUBLANES), :] * h + b_ref[pl.ds(off, SUBLANES), :]
        h_ref[pl.ds(off, SUBLANES), :] = hb
        return jnp.broadcast_to(hb[SUBLANES - 1:SUBLANES, :], (SUBLANES, gw))

    hcar_ref[...] = lax.fori_loop(0, ts // SUBLANES, block, hcar_ref[...], unroll=8)
    y_ref[...] = (h_ref[...] * jax.nn.gelu(u_ref[:, gw:2 * gw].astype(F32))).astype(BF16)


def _lru(lru_in, b, s, l, conv_w, conv_b, wa_bd, ba, wx_bd, bx, lam):
    ts = _tile(s, 512)
    ns = s // ts
    gw = GROUP_WIDTH
    vec = _lspec(l, (1, gw))
    return pl.pallas_call(
        _lru_kernel,
        grid=(b, ns),
        in_specs=[pl.BlockSpec((ts, 2 * gw), lambda bi, si: (bi * ns + si, 0)),
                  _lspec(l, (CONV_WIDTH, gw)), vec,
                  _lspec(l, (gw, gw)), vec, _lspec(l, (gw, gw)), vec, vec],
        out_specs=pl.BlockSpec((ts, gw), lambda bi, si: (bi * ns + si, 0)),
        out_shape=jax.ShapeDtypeStruct((b * s, gw), BF16),
        scratch_shapes=[pltpu.VMEM((ts + SUBLANES, gw), F32), pltpu.VMEM((SUBLANES, gw), F32),
                        pltpu.VMEM((ts, gw), F32), pltpu.VMEM((ts, gw), F32), pltpu.VMEM((ts, gw), F32)],
        compiler_params=_params("arbitrary", "arbitrary"),
        name="rg_lru",
    )(lru_in, conv_w, conv_b, wa_bd, ba, wx_bd, bx, lam)


ATTN_HEADS_PER_STEP = 2
ATTN_MIN_SUM = 2.0 ** -40


def _score_bound(dq_gain, dk_gain):
    qscale = DIFF_QK_DIM ** -0.5 * math.log2(math.e)
    return (1.02 * qscale * DIFF_QK_DIM) * jnp.max(jnp.abs(dq_gain), axis=-1) * jnp.max(jnp.abs(dk_gain), axis=-1)


def _diff_attn_kernel(lam_ref, qt_ref, k_ref, vt_ref, hg_ref, o_ref,
                      qs_ref, sa_ref, sb_ref, m_ref, l_ref, acc_ref, *, t, layer, out_scale):
    qi = pl.program_id(2)
    hd = DIFF_HEAD_DIM
    heads = range(ATTN_HEADS_PER_STEP)
    lo = lax.broadcasted_iota(jnp.int32, (hd, t), 0) < DIFF_QK_DIM
    for h in heads:
        q = qt_ref[h * hd:(h + 1) * hd, :]
        zero = jnp.zeros_like(q)
        qs_ref[h, :, 0:t] = jnp.where(lo, q, zero)
        qs_ref[h, :, t:2 * t] = jnp.where(lo, zero, q)

    def causal(x, fill):
        kpos = lax.broadcasted_iota(jnp.int32, (t, 2 * t), 0)
        c = lax.broadcasted_iota(jnp.int32, (t, 2 * t), 1)
        return jnp.where(kpos <= jnp.where(c >= t, c - t, c), x, fill)

    def score(j, h):
        off = pl.multiple_of(j * t, t)
        return jnp.dot(k_ref[pl.ds(off, t), h * hd:(h + 1) * hd], qs_ref[h],
                       preferred_element_type=F32)

    bound = lam_ref[layer, 1]
    l_ref[...] = jnp.zeros(l_ref.shape, F32)
    acc_ref[...] = jnp.zeros(acc_ref.shape, F32)

    def fast_update(j, masked):
        for h in heads:
            p = jnp.exp2(score(j, h) - bound)
            if masked:
                p = causal(p, 0.0)
            l_ref[h] += jnp.sum(p, axis=0, keepdims=True)
            acc_ref[h] += jnp.dot(vt_ref[j, h * hd:(h + 1) * hd, :], p.astype(BF16),
                                  preferred_element_type=F32)

    def fast_pair(i, carry):
        fast_update(2 * i, False)
        fast_update(2 * i + 1, False)
        return carry

    lax.fori_loop(0, qi >> 1, fast_pair, 0)

    @pl.when((qi & 1) == 1)
    def _():
        fast_update(qi - 1, False)

    fast_update(qi, True)

    @pl.when(jnp.min(l_ref[...]) < ATTN_MIN_SUM)
    def _():
        m_ref[...] = jnp.full(m_ref.shape, NEG_BIG, F32)
        l_ref[...] = jnp.zeros(l_ref.shape, F32)
        acc_ref[...] = jnp.zeros(acc_ref.shape, F32)

        def scores(j, dst_ref):
            for h in heads:
                dst_ref[h] = score(j, h)

        def update(j, src_ref, masked):
            for h in heads:
                s = src_ref[h]
                if masked:
                    s = causal(s, NEG_BIG)
                m = m_ref[h]
                m_new = jnp.maximum(m, jnp.max(s, axis=0, keepdims=True))
                alpha = jnp.exp2(m - m_new)
                p = jnp.exp2(s - m_new)
                m_ref[h] = m_new
                l_ref[h] = alpha * l_ref[h] + jnp.sum(p, axis=0, keepdims=True)
                vt = vt_ref[j, h * hd:(h + 1) * hd, :]
                acc_ref[h] = alpha * acc_ref[h] + jnp.dot(vt, p.astype(BF16), preferred_element_type=F32)

        scores(0, sa_ref)

        def pair(i, carry):
            scores(2 * i + 1, sb_ref)
            update(2 * i, sa_ref, False)
            scores(2 * i + 2, sa_ref)
            update(2 * i + 1, sb_ref, False)
            return carry

        lax.fori_loop(0, qi >> 1, pair, 0)

        @pl.when((qi & 1) == 1)
        def _():
            scores(qi, sb_ref)
            update(qi - 1, sa_ref, False)
            update(qi, sb_ref, True)

        @pl.when((qi & 1) == 0)
        def _():
            update(qi, sa_ref, True)

    for h in heads:
        o = acc_ref[h] / l_ref[h]
        o = o[:, 0:t] - lam_ref[layer, 0] * o[:, t:2 * t]
        o = o * lax.rsqrt(jnp.mean(o * o, axis=0, keepdims=True) + EPS) * (hg_ref[...] * out_scale)
        o_ref[:, h * hd:(h + 1) * hd] = o.T.astype(BF16)


def _diff_attn(qt, k, vt, lam, hg_col, b, s, l, lam_init):
    t = SEQ_TILE
    nq = s // t
    hp = ATTN_HEADS_PER_STEP
    w = hp * DIFF_HEAD_DIM
    ng = DIFF_HEADS // hp
    kern = functools.partial(_diff_attn_kernel, t=t, layer=l, out_scale=1.0 - lam_init)
    return pl.pallas_call(
        kern,
        grid=(b, ng, nq),
        in_specs=[pl.BlockSpec(memory_space=pltpu.SMEM),
                  pl.BlockSpec((None, None, w, t), lambda bi, g, qi: (bi, qi, g, 0)),
                  pl.BlockSpec((s, w), lambda bi, g, qi: (bi, g)),
                  pl.BlockSpec((None, nq, w, t), lambda bi, g, qi: (bi, 0, g, 0)),
                  _lspec(l, (DIFF_HEAD_DIM, 1))],
        out_specs=pl.BlockSpec((t, w), lambda bi, g, qi: (bi * nq + qi, g)),
        out_shape=jax.ShapeDtypeStruct((b * s, GROUP_WIDTH), BF16),
        scratch_shapes=[pltpu.VMEM((hp, DIFF_HEAD_DIM, 2 * t), BF16),
                        pltpu.VMEM((hp, t, 2 * t), F32), pltpu.VMEM((hp, t, 2 * t), F32),
                        pltpu.VMEM((hp, 1, 2 * t), F32), pltpu.VMEM((hp, 1, 2 * t), F32),
                        pltpu.VMEM((hp, DIFF_HEAD_DIM, 2 * t), F32)],
        compiler_params=_params("arbitrary", "arbitrary", "arbitrary"),
        name="diff_attn",
    )(lam, qt, k, vt, hg_col)


def _mem_kv_kernel(mem_ref, mg_ref, w_ref, kg_ref, mk_ref, mv_ref):
    memn = _rms_rows(mem_ref[...], mg_ref[...]).astype(BF16)
    gw = GROUP_WIDTH
    k = jnp.dot(memn, w_ref[:, 0:gw], preferred_element_type=F32)
    for h in range(MEM_HEADS):
        sl = slice(h * MEM_HEAD_DIM, (h + 1) * MEM_HEAD_DIM)
        mk_ref[:, sl] = _rms_rows(k[:, sl], kg_ref[...]).astype(BF16)
    mv_ref[...] = jnp.dot(memn, w_ref[:, gw:2 * gw], preferred_element_type=F32).astype(BF16)


def _mem_kv(mem2, mg, l, w_kv, kg):
    r = mem2.shape[0]
    tm = _tile(r, 512)
    row = lambda i: (i, 0)
    gw = GROUP_WIDTH
    return pl.pallas_call(
        _mem_kv_kernel,
        grid=(r // tm,),
        in_specs=[pl.BlockSpec((tm, D_MODEL), row), pl.BlockSpec((1, D_MODEL), lambda i: (0, 0)),
                  _lspec(l, (D_MODEL, 2 * gw)), _lspec(l, (1, MEM_HEAD_DIM))],
        out_specs=[pl.BlockSpec((tm, gw), row), pl.BlockSpec((tm, gw), row)],
        out_shape=[jax.ShapeDtypeStruct((r, gw), BF16), jax.ShapeDtypeStruct((r, gw), BF16)],
        compiler_params=_params("arbitrary"),
        name="mem_kv",
    )(mem2, mg, w_kv, kg)


MOE_TILE = 256
GRANULE = 16
TILE_SLOTS = MOE_TILE // GRANULE + N_GROUPS
SORTED_ROWS = TILE_SLOTS * GRANULE
STEP_GRANULES = 16
STEP_ROWS = STEP_GRANULES * GRANULE
XS_WIDTH = D_MODEL + LANES
INFO_POS_LANE = 0


def _route_t(lt):
    row_i = lax.broadcasted_iota(jnp.int32, lt.shape, 0)
    row = row_i.astype(F32)
    e0 = ROUTER_EXPERT_LANE0
    lg = jnp.where(row_i < N_GROUPS, lt, NEG_BIG)
    mg = jnp.max(lg, axis=0, keepdims=True)
    g_gate = 1.0 / jnp.sum(jnp.exp(lg - mg), axis=0, keepdims=True)
    g_idx = jnp.min(jnp.where(lg == mg, row, float(LANES)), axis=0, keepdims=True)
    row_group = ((row_i - e0) >> 3).astype(F32)
    sel = (row_i >= e0) & (row_i < e0 + N_EXPERTS) & (row_group == g_idx)
    le = jnp.where(sel, lt, NEG_BIG)
    m1 = jnp.max(le, axis=0, keepdims=True)
    se = jnp.sum(jnp.where(sel, jnp.exp(le - m1), 0.0), axis=0, keepdims=True)
    i1 = jnp.min(jnp.where(sel & (le == m1), row, float(LANES)), axis=0, keepdims=True)
    le2 = jnp.where(row == i1, NEG_BIG, le)
    m2 = jnp.max(le2, axis=0, keepdims=True)
    i2 = jnp.min(jnp.where(sel & (le2 == m2) & (row != i1), row, float(LANES)), axis=0, keepdims=True)
    p1 = 1.0 / se
    p2 = jnp.exp(m2 - m1) / se
    tot = p1 + p2
    w = jnp.where(row == i1, p1 / tot, jnp.where(row == i2, p2 / tot, 0.0))
    return g_gate * w, g_idx


def _split3(c):
    hi = c.astype(BF16).astype(F32)
    r1 = c - hi
    mid = r1.astype(BF16).astype(F32)
    lo = (r1 - mid).astype(BF16).astype(F32)
    return hi, mid, lo


def _sort_logits(x1, g2, rt_hi, rt_lo, rb_col):
    xn = _rms_rows(x1, g2)
    hi = xn.astype(BF16)
    lo = (xn - hi.astype(F32)).astype(BF16)
    logits_t = (lax.dot_general(rt_hi, hi, _NT, preferred_element_type=F32)
                + lax.dot_general(rt_hi, lo, _NT, preferred_element_type=F32)
                + lax.dot_general(rt_lo, hi, _NT, preferred_element_type=F32) + rb_col)
    return hi, logits_t


def _sort_rank(logits_t, earlier):
    tl = logits_t.shape[1]
    comb_t, g_idx = _route_t(logits_t)
    grow = lax.broadcasted_iota(jnp.int32, (SUBLANES, tl), 0).astype(F32)
    gt = jnp.where(grow == g_idx, 1.0, 0.0)
    before = jnp.dot(gt.astype(BF16), earlier, preferred_element_type=F32)
    return comb_t, gt, before


def _sort_emit(hi, comb_t, gt, before):
    tl = hi.shape[0]
    rank = jnp.sum(gt * before, axis=0, keepdims=True)
    cnt = jnp.sum(gt, axis=1, keepdims=True)
    glen = jnp.floor((cnt + (GRANULE - 1)) * (1.0 / GRANULE))
    r8 = lax.broadcasted_iota(jnp.int32, (SUBLANES, 1), 0)
    start = jnp.zeros((SUBLANES, 1), F32)
    for g in range(1, N_GROUPS):
        start = jnp.where(r8 == g, jnp.sum(jnp.where(r8 < g, glen, 0.0), axis=0, keepdims=True), start)
    pos = jnp.sum(gt * (start * GRANULE), axis=0, keepdims=True) + rank

    rows = lax.broadcasted_iota(jnp.int32, (LANES, tl), 0)
    info = jnp.where(rows == INFO_POS_LANE, pos, comb_t).T
    lane = _lane_iota(info.shape)
    e0 = ROUTER_EXPERT_LANE0
    c_hi, c_mid, c_lo = _split3(jnp.where((lane >= e0) & (lane < e0 + N_EXPERTS), info, 0.0))
    aug = (c_hi + pltpu.roll(c_mid, N_EXPERTS, axis=1) + pltpu.roll(c_lo, 2 * N_EXPERTS, axis=1)).astype(BF16)
    perm = jnp.where(pos == lax.broadcasted_iota(jnp.int32, (SORTED_ROWS, tl), 0).astype(F32),
                     1.0, 0.0).astype(BF16)
    xs = jnp.dot(perm, hi, preferred_element_type=F32).astype(BF16)
    xs_aug = jnp.dot(perm, aug, preferred_element_type=F32).astype(BF16)
    return xs, xs_aug, info, glen


def _out_sort_kernel(x_ref, yl_ref, yd_ref, ym_ref, w_ref, g2_ref, rthi_ref, rtlo_ref, rb_ref, earlier_ref,
                     o_ref, xs_ref, info_ref, lens_ref, x1s_ref):
    @pl.when(pl.program_id(0) == 0)
    def _():
        x1s_ref[...] = jnp.zeros_like(x1s_ref)

    tm = x1s_ref.shape[0]
    subs = range(tm // MOE_TILE)

    def project(c, nchunks):
        r = slice(c * (tm // nchunks), (c + 1) * (tm // nchunks))
        y = jnp.concatenate([yl_ref[r, :], yd_ref[r, :], ym_ref[r, :]], axis=-1)
        x1 = x_ref[r, :] + jnp.dot(y, w_ref[...], preferred_element_type=F32)
        o_ref[r, :] = x1
        return r, x1

    fresh = [project(0, 2)]
    s1 = [_sort_logits(x1s_ref[sub * MOE_TILE:(sub + 1) * MOE_TILE, :], g2_ref[...], rthi_ref[...],
                       rtlo_ref[...], rb_ref[...]) for sub in subs]
    for r, x1 in fresh:
        x1s_ref[r, :] = x1
    fresh = [project(1, 2)]
    s2 = [_sort_rank(s1[sub][1], earlier_ref[...]) for sub in subs]
    for r, x1 in fresh:
        x1s_ref[r, :] = x1
    for sub in subs:
        xs, xs_aug, info, glen = _sort_emit(s1[sub][0], *s2[sub])
        r0 = sub * SORTED_ROWS
        xs_ref[r0:r0 + SORTED_ROWS, 0:D_MODEL] = xs
        xs_ref[r0:r0 + SORTED_ROWS, D_MODEL:XS_WIDTH] = xs_aug
        info_ref[sub * MOE_TILE:(sub + 1) * MOE_TILE, :] = info
        lens_ref[sub] = jnp.broadcast_to(glen, (SUBLANES, LANES))


def _out_sort(x2, y_lru, y_diff, y_mem, l, w_out, g2, rt_hi, rt_lo, rb_col, earlier):
    t = x2.shape[0]
    tm = SEQ_TILE
    assert t % tm == 0 and tm % MOE_TILE == 0
    sub = tm // MOE_TILE
    nt = t // MOE_TILE
    n = t // tm
    cur = lambda i: (jnp.minimum(i, n - 1), 0)
    prev = lambda i: (jnp.maximum(i - 1, 0), 0)
    gw = GROUP_WIDTH
    return pl.pallas_call(
        _out_sort_kernel,
        grid=(n + 1,),
        in_specs=[pl.BlockSpec((tm, D_MODEL), cur), pl.BlockSpec((tm, gw), cur), pl.BlockSpec((tm, gw), cur),
                  pl.BlockSpec((tm, gw), cur), _lspec(l, (3 * gw, D_MODEL)),
                  _lspec(l, (1, D_MODEL)), _lspec(l, (LANES, D_MODEL)),
                  _lspec(l, (LANES, D_MODEL)), _lspec(l, (LANES, 1)),
                  pl.BlockSpec((MOE_TILE, MOE_TILE), lambda i: (0, 0))],
        out_specs=[pl.BlockSpec((tm, D_MODEL), cur), pl.BlockSpec((sub * SORTED_ROWS, XS_WIDTH), prev),
                   pl.BlockSpec((tm, LANES), prev),
                   pl.BlockSpec((sub, SUBLANES, LANES), lambda i: (jnp.maximum(i - 1, 0), 0, 0))],
        out_shape=[jax.ShapeDtypeStruct((t, D_MODEL), F32),
                   jax.ShapeDtypeStruct((nt * SORTED_ROWS, XS_WIDTH), BF16),
                   jax.ShapeDtypeStruct((t, LANES), F32),
                   jax.ShapeDtypeStruct((nt, SUBLANES, LANES), F32)],
        scratch_shapes=[pltpu.VMEM((tm, D_MODEL), F32)],
        compiler_params=_params("arbitrary"),
        name="out_proj_sort",
    )(x2, y_lru, y_diff, y_mem, w_out, g2, rt_hi, rt_lo, rb_col, earlier)


def _moe_expert_kernel(sg_ref, sv_ref, gi_ref, *refs):
    del gi_ref
    xs_refs = refs[:STEP_GRANULES]
    wg_ref, wu_ref, wd_ref, y_ref = refs[STEP_GRANULES:]
    s = pl.program_id(0)

    @pl.when(sv_ref[s] == 0)
    def _():
        y_ref[...] = jnp.zeros_like(y_ref)

    @pl.when(sv_ref[s] != 0)
    def _():
        rows = jnp.concatenate([r[...] for r in xs_refs], axis=0)
        x = rows[:, 0:D_MODEL]
        aug = rows[:, D_MODEL:XS_WIDTH].astype(F32)
        lane = _lane_iota(aug.shape)
        e0 = ROUTER_EXPERT_LANE0
        comb = jnp.where((lane >= e0) & (lane < e0 + N_EXPERTS),
                         aug + pltpu.roll(aug, LANES - N_EXPERTS, axis=1)
                         + pltpu.roll(aug, LANES - 2 * N_EXPERTS, axis=1), 0.0)
        first = e0 + sg_ref[s] * EXPERTS_PER_GROUP
        hs = []
        for e in range(EXPERTS_PER_GROUP):
            cw = jnp.sum(jnp.where(lane == first + e, comb, 0.0), axis=-1, keepdims=True)
            hg = jnp.dot(x, wg_ref[e], preferred_element_type=F32)
            hu = jnp.dot(x, wu_ref[e], preferred_element_type=F32)
            hs.append((jax.nn.silu(hg) * hu * cw).astype(BF16))
        h = jnp.concatenate(hs, axis=-1)
        y_ref[...] = jnp.dot(h, wd_ref[...], preferred_element_type=F32).astype(BF16)


def _granule_map(k, s, sg, sv, gi):
    return (gi[s * STEP_GRANULES + k], 0)


def _moe_experts(step_group, step_valid, gran_idx, xs, l, wg, wu, wd):
    nstep = step_group.shape[0]
    epg = EXPERTS_PER_GROUP
    wmap4 = lambda s, sg, sv, gi: (l * N_GROUPS + sg[s], 0, 0, 0)
    in_specs = [pl.BlockSpec((GRANULE, XS_WIDTH), functools.partial(_granule_map, k)) for k in range(STEP_GRANULES)]
    in_specs += [pl.BlockSpec((None, epg, D_MODEL, D_EXPERT), wmap4),
                 pl.BlockSpec((None, epg, D_MODEL, D_EXPERT), wmap4),
                 pl.BlockSpec((None, epg * D_EXPERT, D_MODEL), lambda s, sg, sv, gi: (l * N_GROUPS + sg[s], 0, 0))]
    return pl.pallas_call(
        _moe_expert_kernel,
        grid_spec=pltpu.PrefetchScalarGridSpec(
            num_scalar_prefetch=3, grid=(nstep,), in_specs=in_specs,
            out_specs=pl.BlockSpec((STEP_ROWS, D_MODEL), lambda s, sg, sv, gi: (s, 0))),
        out_shape=jax.ShapeDtypeStruct((nstep * STEP_ROWS, D_MODEL), BF16),
        compiler_params=_params("arbitrary"),
        name="moe_experts",
    )(step_group, step_valid, gran_idx, *([xs] * STEP_GRANULES), wg, wu, wd)


def _moe_combine_kernel(inv_ref, x_ref, info_ref, *refs):
    del inv_ref
    ys_refs = refs[:TILE_SLOTS]
    o_ref = refs[TILE_SLOTS]
    kpad = -SORTED_ROWS % LANES
    ys = jnp.concatenate([r[...] for r in ys_refs] + [jnp.zeros((kpad, D_MODEL), BF16)], axis=0)
    pos = info_ref[:, INFO_POS_LANE:INFO_POS_LANE + 1]
    unperm = jnp.where(pos == _lane_iota((MOE_TILE, SORTED_ROWS + kpad)).astype(F32), 1.0, 0.0).astype(BF16)
    o_ref[...] = x_ref[...] + jnp.dot(unperm, ys, preferred_element_type=F32)


def _slot_map(q, i, inv):
    return (inv[i * TILE_SLOTS + q], 0)


def _moe_combine(inv, x2, info, ys):
    t = x2.shape[0]
    nt = t // MOE_TILE
    row = lambda i, inv: (i, 0)
    in_specs = [pl.BlockSpec((MOE_TILE, D_MODEL), row), pl.BlockSpec((MOE_TILE, LANES), row)]
    in_specs += [pl.BlockSpec((GRANULE, D_MODEL), functools.partial(_slot_map, q)) for q in range(TILE_SLOTS)]
    return pl.pallas_call(
        _moe_combine_kernel,
        grid_spec=pltpu.PrefetchScalarGridSpec(
            num_scalar_prefetch=1, grid=(nt,), in_specs=in_specs,
            out_specs=pl.BlockSpec((MOE_TILE, D_MODEL), row)),
        out_shape=jax.ShapeDtypeStruct((t, D_MODEL), F32),
        compiler_params=_params("arbitrary"),
        name="moe_combine",
    )(inv, x2, info, *([ys] * TILE_SLOTS))


def _moe_tables(lens, nt):
    i32 = jnp.int32
    ng = N_GROUPS
    garange = jnp.arange(ng, dtype=i32)
    cum = jnp.cumsum(lens, axis=1)
    start = cum - lens
    base = jnp.arange(nt, dtype=i32)[:, None] * TILE_SLOTS + start
    run_len = lens.T.reshape(-1)
    run_base = base.T.reshape(-1)
    run_end = jnp.cumsum(run_len)
    run_start = run_end - run_len
    n_g = jnp.sum(lens, axis=0)
    steps_g = (n_g + STEP_GRANULES - 1) // STEP_GRANULES
    step_end = jnp.cumsum(steps_g)
    step_off = step_end - steps_g
    gran_off = jnp.cumsum(n_g) - n_g
    nstep = (nt * (TILE_SLOTS - 1) + STEP_GRANULES - 1) // STEP_GRANULES + ng
    s = jnp.arange(nstep, dtype=i32)
    sg = jnp.minimum(jnp.sum((s[:, None] >= step_end[None, :]).astype(i32), axis=-1), ng - 1)
    goh = sg[:, None] == garange[None, :]
    pick = lambda v: jnp.sum(jnp.where(goh, v[None, :], 0), axis=-1)
    sv = s < step_end[-1]
    jl = (s - pick(step_off))[:, None] * STEP_GRANULES + jnp.arange(STEP_GRANULES, dtype=i32)[None, :]
    ok = sv[:, None] & (jl < pick(n_g)[:, None])
    j = jnp.where(ok, pick(gran_off)[:, None] + jl, 0)
    inrun = (j[..., None] >= run_start) & (j[..., None] < run_end)
    gran = j + jnp.sum(jnp.where(inrun, run_base - run_start, 0), axis=-1)
    gran = jnp.where(ok, gran, jnp.where(sv[:, None], gran[:, 0:1], 0)).astype(i32)
    q = jnp.arange(TILE_SLOTS, dtype=i32)[None, :]
    gq = jnp.minimum(jnp.sum((q[:, :, None] >= cum[:, None, :]).astype(i32), axis=-1), ng - 1)
    qoh = gq[..., None] == garange
    used = q < cum[:, -1:]
    per_g = run_start.reshape(ng, nt).T - start + (step_off * STEP_GRANULES - gran_off)[None, :]
    inv = jnp.where(used, q + jnp.sum(jnp.where(qoh, per_g[:, None, :], 0), axis=-1), 0).astype(i32)
    return sg, sv.astype(i32), gran.reshape(-1), inv.reshape(-1)


def _out_proj_moe(x2, y_lru, y_diff, y_mem, l, w_out, g2, rt_hi, rt_lo, rb_col, earlier, wg, wu, wd):
    nt = x2.shape[0] // MOE_TILE
    x1, xs, info, lens = _out_sort(x2, y_lru, y_diff, y_mem, l, w_out, g2, rt_hi, rt_lo, rb_col, earlier)
    sg, sv, gran, inv = _moe_tables(lens[:, 0:N_GROUPS, 0].astype(jnp.int32), nt)
    ys = _moe_experts(sg, sv, gran, xs, l, wg, wu, wd)
    return _moe_combine(inv, x1, info, ys)


def _block_diag(w):
    depth, h, n, _ = w.shape
    eye = jnp.eye(h, dtype=w.dtype)
    return (eye[None, :, None, :, None] * w[:, :, :, None, :]).reshape(depth, h * n, h * n)


def _router_tables(w_rg, b_rg, w_re, b_re):
    depth = w_rg.shape[0]
    e0 = ROUTER_EXPERT_LANE0
    pad = lambda rows, width: jnp.zeros((depth, rows, width), F32)
    w = jnp.concatenate([jnp.swapaxes(w_rg, 1, 2), pad(e0 - N_GROUPS, D_MODEL), jnp.swapaxes(w_re, 1, 2),
                         pad(LANES - e0 - N_EXPERTS, D_MODEL)], axis=1)
    bias = jnp.concatenate([b_rg[:, :, None], pad(e0 - N_GROUPS, 1), b_re[:, :, None],
                            pad(LANES - e0 - N_EXPERTS, 1)], axis=1)
    hi = w.astype(BF16)
    lo = (w - hi.astype(F32)).astype(BF16)
    return hi, lo, bias


def kernel(x, mem, norm1_g, w_in, conv_w, conv_b, rg_wa, rg_ba, rg_wx, rg_bx, rg_lambda, dq_norm_g, dk_norm_g, lambda_q1, lambda_k1, lambda_q2, lambda_k2, diff_head_norm_g, mem_norm_g, w_mem_kv, mq_norm_g, mk_norm_g, w_out, norm2_g, w_router_group, b_router_group, w_router_expert, b_router_expert, w_expert_gate, w_expert_up, w_expert_down):
    b, s, d = x.shape
    m = mem.shape[1]
    depth = w_in.shape[0]
    gw = GROUP_WIDTH
    epg = EXPERTS_PER_GROUP
    x2 = x.reshape(b * s, d)
    mem2 = mem.reshape(b * m, d)
    row = lambda v: v.reshape(depth, 1, -1).astype(F32)
    col = lambda v: v.reshape(depth, -1, 1).astype(F32)
    lam_inits = [0.8 - 0.6 * math.exp(-0.3 * l) for l in range(depth)]
    lam = (jnp.exp(jnp.sum(lambda_q1 * lambda_k1, axis=-1)) - jnp.exp(jnp.sum(lambda_q2 * lambda_k2, axis=-1))
           + jnp.asarray(lam_inits, F32))
    lam = jnp.stack([lam, _score_bound(dq_norm_g, dk_norm_g)], axis=1).astype(F32)
    w_in_bf = w_in.astype(BF16)
    w_in_t = jnp.swapaxes(w_in_bf, 1, 2)
    qg_col = col(jnp.tile(dq_norm_g, (1, gw // DIFF_QK_DIM)))
    kg = row(jnp.tile(dk_norm_g, (1, 2)))
    wa_bd = _block_diag(rg_wa).astype(BF16)
    wx_bd = _block_diag(rg_wx).astype(BF16)
    w_kv_bf = w_mem_kv.astype(BF16)
    w_out_bf = w_out.astype(BF16)
    rt_hi, rt_lo, rb_col = _router_tables(w_router_group, b_router_group, w_router_expert, b_router_expert)
    earlier = jnp.triu(jnp.ones((MOE_TILE, MOE_TILE), BF16), k=1)
    wg = w_expert_gate.astype(BF16).reshape(depth * N_GROUPS, epg, D_MODEL, D_EXPERT)
    wu = w_expert_up.astype(BF16).reshape(depth * N_GROUPS, epg, D_MODEL, D_EXPERT)
    wd = w_expert_down.astype(BF16).reshape(depth * N_GROUPS, epg * D_EXPERT, D_MODEL)
    g1, g2, mqg, mkg = row(norm1_g), row(norm2_g), row(mq_norm_g), row(mk_norm_g)
    conv_b3, ba, bx, lru_lam, hg_col = row(conv_b), row(rg_ba), row(rg_bx), row(rg_lambda), col(diff_head_norm_g)
    mem_g = mem_norm_g.reshape(1, -1).astype(F32)
    for l in range(depth):
        mk, mv = _mem_kv(mem2, mem_g, l, w_kv_bf, mkg)
        lru_in, k, y_mem, qt, vt = _in_proj(x2, b, s, l, g1, w_in_bf, w_in_t, qg_col, kg, mqg, mk, mv)
        y_lru = _lru(lru_in, b, s, l, conv_w, conv_b3, wa_bd, ba, wx_bd, bx, lru_lam)
        y_diff = _diff_attn(qt, k, vt, lam, hg_col, b, s, l, lam_inits[l])
        x2 = _out_proj_moe(x2, y_lru, y_diff, y_mem, l, w_out_bf, g2, rt_hi, rt_lo, rb_col, earlier, wg, wu, wd)
    return x2.reshape(b, s, d)
```

```python
import functools
import math

import jax
import jax.numpy as jnp
from jax import lax
from jax.experimental import pallas as pl
from jax.experimental.pallas import tpu as pltpu

F32 = jnp.float32
BF16 = jnp.bfloat16

D_MODEL = 1024
GROUP_WIDTH = D_MODEL // 2
LRU_HEADS = 8
CONV_WIDTH = 4
LRU_C = 8.0
DIFF_HEADS = 4
DIFF_HEAD_DIM = GROUP_WIDTH // DIFF_HEADS
DIFF_QK_DIM = DIFF_HEAD_DIM // 2
MEM_HEADS = 4
MEM_HEAD_DIM = GROUP_WIDTH // MEM_HEADS
N_GROUPS = 4
EXPERTS_PER_GROUP = 8
N_EXPERTS = N_GROUPS * EXPERTS_PER_GROUP
D_EXPERT = 256
D_IN_PROJ = 6 * GROUP_WIDTH
EPS = 1e-6

LANES = 128
SUBLANES = 8
VMEM_LIMIT = 56 * 1024 * 1024
NEG_BIG = -1e30
ROUTER_EXPERT_LANE0 = 32


def _tile(n, pref):
    t = min(n, pref)
    assert n % t == 0, (n, t)
    return t


def _params(*sem):
    return pltpu.CompilerParams(dimension_semantics=sem, vmem_limit_bytes=VMEM_LIMIT)


def _lspec(l, tail, blk=None):
    idx = (l,) + tuple(blk if blk is not None else (0,) * len(tail))
    return pl.BlockSpec((None,) + tuple(tail), lambda *_: idx)


def _rms_rows(x, g):
    return x * lax.rsqrt(jnp.mean(x * x, axis=-1, keepdims=True) + EPS) * g


def _lane_iota(shape):
    return lax.broadcasted_iota(jnp.int32, shape, len(shape) - 1)


def _half_head_norm(z, g):
    sq = z * z
    lo = _lane_iota(z.shape) < DIFF_QK_DIM
    s_all = jnp.sum(sq, axis=-1, keepdims=True)
    s_lo = jnp.sum(jnp.where(lo, sq, 0.0), axis=-1, keepdims=True)
    inv_lo = lax.rsqrt(s_lo * (1.0 / DIFF_QK_DIM) + EPS)
    inv_hi = lax.rsqrt((s_all - s_lo) * (1.0 / DIFF_QK_DIM) + EPS)
    return z * jnp.where(lo, inv_lo, inv_hi) * g


SEQ_TILE = 512
_NT = (((1,), (1,)), ((), ()))


def _in_proj_body(x, g1_ref, wl_ref, wk_ref, wmq_ref, wqt_ref, wvt_ref, qg_ref, kg_ref, mqg_ref,
                  mk_ref, mv_ref, lru_ref, k_ref, ym_ref, qt_ref, vt_ref):
    xn = _rms_rows(x, g1_ref[...]).astype(BF16)
    gw = GROUP_WIDTH
    tm = xn.shape[0]
    heads = range(MEM_HEADS)
    hsl = [slice(h * MEM_HEAD_DIM, (h + 1) * MEM_HEAD_DIM) for h in heads]
    mq = jnp.dot(xn, wmq_ref[...], preferred_element_type=F32)
    lru_ref[...] = jnp.dot(xn, wl_ref[...], preferred_element_type=F32).astype(BF16)
    sc = [lax.dot_general(_rms_rows(mq[:, hsl[h]], mqg_ref[...]).astype(BF16), mk_ref[:, hsl[h]], _NT,
                          preferred_element_type=F32) * MEM_HEAD_DIM ** -0.5 for h in heads]
    k = jnp.dot(xn, wk_ref[...], preferred_element_type=F32)
    p = [jnp.exp(sc[h] - jnp.max(sc[h], axis=-1, keepdims=True)) for h in heads]
    qt = lax.dot_general(wqt_ref[...], xn, _NT, preferred_element_type=F32)
    o = [jnp.dot(p[h].astype(BF16), mv_ref[:, hsl[h]], preferred_element_type=F32) for h in heads]
    vt_ref[...] = lax.dot_general(wvt_ref[...], xn, _NT, preferred_element_type=F32).astype(BF16)
    for h in range(DIFF_HEADS):
        sl = slice(h * LANES, (h + 1) * LANES)
        k_ref[:, sl] = _half_head_norm(k[:, sl], kg_ref[...]).astype(BF16)
    for h in heads:
        ym_ref[:, hsl[h]] = (o[h] / jnp.sum(p[h], axis=-1, keepdims=True)).astype(BF16)
    q3 = qt.reshape(gw // DIFF_QK_DIM, DIFF_QK_DIM, tm)
    q3 = q3 * lax.rsqrt(jnp.mean(q3 * q3, axis=1, keepdims=True) + EPS)
    qscale = DIFF_QK_DIM ** -0.5 * math.log2(math.e)
    qt_ref[...] = (q3.reshape(gw, tm) * (qg_ref[...] * qscale)).astype(BF16)


def _in_proj_kernel(x_ref, *refs):
    _in_proj_body(x_ref[...], *refs)


def _unsort(info, ys_refs):
    kpad = -SORTED_ROWS % LANES
    ys = jnp.concatenate([r[...] for r in ys_refs] + [jnp.zeros((kpad, D_MODEL), BF16)], axis=0)
    pos = info[:, INFO_POS_LANE:INFO_POS_LANE + 1]
    unperm = jnp.where(pos == _lane_iota((MOE_TILE, SORTED_ROWS + kpad)).astype(F32), 1.0, 0.0).astype(BF16)
    return jnp.dot(unperm, ys, preferred_element_type=F32)


def _combine_in_proj_kernel(inv_ref, x1_ref, info_ref, *refs):
    del inv_ref
    nsub = x1_ref.shape[0] // MOE_TILE
    ys_refs, rest = refs[:nsub * TILE_SLOTS], refs[nsub * TILE_SLOTS:]
    params, x_out_ref, outs = rest[:-6], rest[-6], rest[-5:]
    x = jnp.concatenate(
        [x1_ref[u * MOE_TILE:(u + 1) * MOE_TILE, :]
         + _unsort(info_ref[u * MOE_TILE:(u + 1) * MOE_TILE, :], ys_refs[u * TILE_SLOTS:(u + 1) * TILE_SLOTS])
         for u in range(nsub)], axis=0)
    x_out_ref[...] = x
    _in_proj_body(x, *params, *outs)


def _slot_map(q, sub, i, inv):
    return (inv[(i * sub[0] + sub[1]) * TILE_SLOTS + q], 0)


def _in_proj(x2, b, s, l, g1, w_in, w_in_t, qg_col, kg, mqg, mk, mv, moe=None):
    t = x2.shape[0]
    tm = SEQ_TILE
    assert s % tm == 0
    ns = s // tm
    m = mk.shape[0] // b
    gw = GROUP_WIDTH
    row = lambda i, *_: (i, 0)
    fm = lambda i, *_: (i // ns, i % ns, 0, 0)
    mem = pl.BlockSpec((m, gw), lambda i, *_: (i // ns, 0))
    in_specs = [_lspec(l, (1, D_MODEL)),
                _lspec(l, (D_MODEL, 2 * gw), (0, 0)), _lspec(l, (D_MODEL, gw), (0, 3)),
                _lspec(l, (D_MODEL, gw), (0, 5)), _lspec(l, (gw, D_MODEL), (2, 0)),
                _lspec(l, (gw, D_MODEL), (4, 0)), _lspec(l, (gw, 1)),
                _lspec(l, (1, LANES)), _lspec(l, (1, MEM_HEAD_DIM)), mem, mem]
    operands = (g1, w_in, w_in, w_in, w_in_t, w_in_t, qg_col, kg, mqg, mk, mv)
    out_specs = [pl.BlockSpec((tm, 2 * gw), row), pl.BlockSpec((tm, gw), row), pl.BlockSpec((tm, gw), row),
                 pl.BlockSpec((None, None, gw, tm), fm), pl.BlockSpec((None, None, gw, tm), fm)]
    out_shape = [jax.ShapeDtypeStruct((t, 2 * gw), BF16), jax.ShapeDtypeStruct((t, gw), BF16),
                 jax.ShapeDtypeStruct((t, gw), BF16), jax.ShapeDtypeStruct((b, ns, gw, tm), BF16),
                 jax.ShapeDtypeStruct((b, ns, gw, tm), BF16)]
    xspec = pl.BlockSpec((tm, D_MODEL), row)
    if moe is None:
        return pl.pallas_call(
            _in_proj_kernel, grid=(t // tm,), in_specs=[xspec] + in_specs, out_specs=out_specs,
            out_shape=out_shape, compiler_params=_params("arbitrary"), name="in_proj",
        )(x2, *operands)
    inv, info, ys = moe
    nsub = tm // MOE_TILE
    gran = [pl.BlockSpec((GRANULE, D_MODEL), functools.partial(_slot_map, q, (nsub, u)))
            for u in range(nsub) for q in range(TILE_SLOTS)]
    return pl.pallas_call(
        _combine_in_proj_kernel,
        grid_spec=pltpu.PrefetchScalarGridSpec(
            num_scalar_prefetch=1, grid=(t // tm,),
            in_specs=[xspec, pl.BlockSpec((tm, LANES), row)] + gran + in_specs,
            out_specs=[xspec] + out_specs),
        out_shape=[jax.ShapeDtypeStruct((t, D_MODEL), F32)] + out_shape,
        compiler_params=_params("arbitrary"),
        name="combine_in_proj",
    )(inv, x2, info, *([ys] * (nsub * TILE_SLOTS)), *operands)


def _softplus(z):
    return jnp.maximum(z, 0.0) + jnp.log(1.0 + jnp.exp(-jnp.abs(z)))


def _lru_kernel(u_ref, cw_ref, cb_ref, wa_ref, ba_ref, wx_ref, bx_ref, lam_ref, y_ref,
                ext_ref, hcar_ref, a_ref, b_ref, h_ref):
    ts = y_ref.shape[0]
    gw = GROUP_WIDTH
    hist = SUBLANES

    @pl.when(pl.program_id(1) == 0)
    def _():
        ext_ref[0:hist, :] = jnp.zeros((hist, gw), F32)
        hcar_ref[...] = jnp.zeros_like(hcar_ref)

    @pl.when(pl.program_id(1) != 0)
    def _():
        ext_ref[0:hist, :] = ext_ref[ts:ts + hist, :]

    ext_ref[hist:hist + ts, :] = u_ref[:, 0:gw].astype(F32)
    xc = cb_ref[...] + cw_ref[CONV_WIDTH - 1:CONV_WIDTH, :] * ext_ref[hist:hist + ts, :]
    for j in range(CONV_WIDTH - 1):
        off = hist - (CONV_WIDTH - 1) + j
        xc = xc + cw_ref[j:j + 1, :] * ext_ref[off:off + ts, :]
    xcb = xc.astype(BF16)
    r = jax.nn.sigmoid(jnp.dot(xcb, wa_ref[...], preferred_element_type=F32) + ba_ref[...])
    gate_i = jax.nn.sigmoid(jnp.dot(xcb, wx_ref[...], preferred_element_type=F32) + bx_ref[...])
    log_a = (-LRU_C * r) * _softplus(-lam_ref[...])
    a = jnp.exp(log_a)
    om = 1.0 - a * a
    b = om * lax.rsqrt(jnp.maximum(om, 1e-30)) * (gate_i * xc)

    row = lax.broadcasted_iota(jnp.int32, (ts, gw), 0) & (SUBLANES - 1)
    d = 1
    while d < SUBLANES:
        keep = row >= d
        a_prev = pltpu.roll(a, d, axis=0)
        b_prev = pltpu.roll(b, d, axis=0)
        b = jnp.where(keep, a * b_prev + b, b)
        a = jnp.where(keep, a * a_prev, a)
        d *= 2
    a_ref[...] = a
    b_ref[...] = b

    def block(i, h):
        off = pl.multiple_of(i * SUBLANES, SUBLANES)
        hb = a_ref[pl.ds(off, SUBLANES), :] * h + b_ref[pl.ds(off, SUBLANES), :]
        h_ref[pl.ds(off, SUBLANES), :] = hb
        return jnp.broadcast_to(hb[SUBLANES - 1:SUBLANES, :], (SUBLANES, gw))

    hcar_ref[...] = lax.fori_loop(0, ts // SUBLANES, block, hcar_ref[...], unroll=8)
    y_ref[...] = (h_ref[...] * jax.nn.gelu(u_ref[:, gw:2 * gw].astype(F32))).astype(BF16)


def _lru(lru_in, b, s, l, conv_w, conv_b, wa_bd, ba, wx_bd, bx, lam):
    ts = _tile(s, 512)
    ns = s // ts
    gw = GROUP_WIDTH
    vec = _lspec(l, (1, gw))
    return pl.pallas_call(
        _lru_kernel,
        grid=(b, ns),
        in_specs=[pl.BlockSpec((ts, 2 * gw), lambda bi, si: (bi * ns + si, 0)),
                  _lspec(l, (CONV_WIDTH, gw)), vec,
                  _lspec(l, (gw, gw)), vec, _lspec(l, (gw, gw)), vec, vec],
        out_specs=pl.BlockSpec((ts, gw), lambda bi, si: (bi * ns + si, 0)),
        out_shape=jax.ShapeDtypeStruct((b * s, gw), BF16),
        scratch_shapes=[pltpu.VMEM((ts + SUBLANES, gw), F32), pltpu.VMEM((SUBLANES, gw), F32),
                        pltpu.VMEM((ts, gw), F32), pltpu.VMEM((ts, gw), F32), pltpu.VMEM((ts, gw), F32)],
        compiler_params=_params("arbitrary", "arbitrary"),
        name="rg_lru",
    )(lru_in, conv_w, conv_b, wa_bd, ba, wx_bd, bx, lam)


ATTN_HEADS_PER_STEP = 2
ATTN_MIN_SUM = 2.0 ** -40


def _score_bound(dq_gain, dk_gain):
    qscale = DIFF_QK_DIM ** -0.5 * math.log2(math.e)
    return (1.02 * qscale * DIFF_QK_DIM) * jnp.max(jnp.abs(dq_gain), axis=-1) * jnp.max(jnp.abs(dk_gain), axis=-1)


def _diff_attn_kernel(lam_ref, qt_ref, k_ref, vt_ref, hg_ref, o_ref,
                      qs_ref, sa_ref, sb_ref, m_ref, l_ref, acc_ref, *, t, layer, out_scale):
    qi = pl.program_id(2)
    hd = DIFF_HEAD_DIM
    heads = range(ATTN_HEADS_PER_STEP)
    lo = lax.broadcasted_iota(jnp.int32, (hd, t), 0) < DIFF_QK_DIM
    for h in heads:
        q = qt_ref[h * hd:(h + 1) * hd, :]
        zero = jnp.zeros_like(q)
        qs_ref[h, :, 0:t] = jnp.where(lo, q, zero)
        qs_ref[h, :, t:2 * t] = jnp.where(lo, zero, q)

    def causal(x, fill):
        kpos = lax.broadcasted_iota(jnp.int32, (t, 2 * t), 0)
        c = lax.broadcasted_iota(jnp.int32, (t, 2 * t), 1)
        return jnp.where(kpos <= jnp.where(c >= t, c - t, c), x, fill)

    def score(j, h):
        off = pl.multiple_of(j * t, t)
        return jnp.dot(k_ref[pl.ds(off, t), h * hd:(h + 1) * hd], qs_ref[h],
                       preferred_element_type=F32)

    bound = lam_ref[layer, 1]
    l_ref[...] = jnp.zeros(l_ref.shape, F32)
    acc_ref[...] = jnp.zeros(acc_ref.shape, F32)

    def fast_update(j, masked):
        for h in heads:
            p = jnp.exp2(score(j, h) - bound)
            if masked:
                p = causal(p, 0.0)
            l_ref[h] += jnp.sum(p, axis=0, keepdims=True)
            acc_ref[h] += jnp.dot(vt_ref[j, h * hd:(h + 1) * hd, :], p.astype(BF16),
                                  preferred_element_type=F32)

    def fast_pair(i, carry):
        fast_update(2 * i, False)
        fast_update(2 * i + 1, False)
        return carry

    lax.fori_loop(0, qi >> 1, fast_pair, 0)

    @pl.when((qi & 1) == 1)
    def _():
        fast_update(qi - 1, False)

    fast_update(qi, True)

    @pl.when(jnp.min(l_ref[...]) < ATTN_MIN_SUM)
    def _():
        m_ref[...] = jnp.full(m_ref.shape, NEG_BIG, F32)
        l_ref[...] = jnp.zeros(l_ref.shape, F32)
        acc_ref[...] = jnp.zeros(acc_ref.shape, F32)

        def scores(j, dst_ref):
            for h in heads:
                dst_ref[h] = score(j, h)

        def update(j, src_ref, masked):
            for h in heads:
                s = src_ref[h]
                if masked:
                    s = causal(s, NEG_BIG)
                m = m_ref[h]
                m_new = jnp.maximum(m, jnp.max(s, axis=0, keepdims=True))
                alpha = jnp.exp2(m - m_new)
                p = jnp.exp2(s - m_new)
                m_ref[h] = m_new
                l_ref[h] = alpha * l_ref[h] + jnp.sum(p, axis=0, keepdims=True)
                vt = vt_ref[j, h * hd:(h + 1) * hd, :]
                acc_ref[h] = alpha * acc_ref[h] + jnp.dot(vt, p.astype(BF16), preferred_element_type=F32)

        scores(0, sa_ref)

        def pair(i, carry):
            scores(2 * i + 1, sb_ref)
            update(2 * i, sa_ref, False)
            scores(2 * i + 2, sa_ref)
            update(2 * i + 1, sb_ref, False)
            return carry

        lax.fori_loop(0, qi >> 1, pair, 0)

        @pl.when((qi & 1) == 1)
        def _():
            scores(qi, sb_ref)
            update(qi - 1, sa_ref, False)
            update(qi, sb_ref, True)

        @pl.when((qi & 1) == 0)
        def _():
            update(qi, sa_ref, True)

    for h in heads:
        o = acc_ref[h] / l_ref[h]
        o = o[:, 0:t] - lam_ref[layer, 0] * o[:, t:2 * t]
        o = o * lax.rsqrt(jnp.mean(o * o, axis=0, keepdims=True) + EPS) * (hg_ref[...] * out_scale)
        o_ref[:, h * hd:(h + 1) * hd] = o.T.astype(BF16)


def _diff_attn(qt, k, vt, lam, hg_col, b, s, l, lam_init):
    t = SEQ_TILE
    nq = s // t
    hp = ATTN_HEADS_PER_STEP
    w = hp * DIFF_HEAD_DIM
    ng = DIFF_HEADS // hp
    kern = functools.partial(_diff_attn_kernel, t=t, layer=l, out_scale=1.0 - lam_init)
    return pl.pallas_call(
        kern,
        grid=(b, ng, nq),
        in_specs=[pl.BlockSpec(memory_space=pltpu.SMEM),
                  pl.BlockSpec((None, None, w, t), lambda bi, g, qi: (bi, qi, g, 0)),
                  pl.BlockSpec((s, w), lambda bi, g, qi: (bi, g)),
                  pl.BlockSpec((None, nq, w, t), lambda bi, g, qi: (bi, 0, g, 0)),
                  _lspec(l, (DIFF_HEAD_DIM, 1))],
        out_specs=pl.BlockSpec((t, w), lambda bi, g, qi: (bi * nq + qi, g)),
        out_shape=jax.ShapeDtypeStruct((b * s, GROUP_WIDTH), BF16),
        scratch_shapes=[pltpu.VMEM((hp, DIFF_HEAD_DIM, 2 * t), BF16),
                        pltpu.VMEM((hp, t, 2 * t), F32), pltpu.VMEM((hp, t, 2 * t), F32),
                        pltpu.VMEM((hp, 1, 2 * t), F32), pltpu.VMEM((hp, 1, 2 * t), F32),
                        pltpu.VMEM((hp, DIFF_HEAD_DIM, 2 * t), F32)],
        compiler_params=_params("arbitrary", "arbitrary", "arbitrary"),
        name="diff_attn",
    )(lam, qt, k, vt, hg_col)


def _mem_kv_kernel(mem_ref, mg_ref, w_ref, kg_ref, mk_ref, mv_ref):
    memn = _rms_rows(mem_ref[...], mg_ref[...]).astype(BF16)
    gw = GROUP_WIDTH
    k = jnp.dot(memn, w_ref[:, 0:gw], preferred_element_type=F32)
    for h in range(MEM_HEADS):
        sl = slice(h * MEM_HEAD_DIM, (h + 1) * MEM_HEAD_DIM)
        mk_ref[:, sl] = _rms_rows(k[:, sl], kg_ref[...]).astype(BF16)
    mv_ref[...] = jnp.dot(memn, w_ref[:, gw:2 * gw], preferred_element_type=F32).astype(BF16)


def _mem_kv(mem2, mg, l, w_kv, kg):
    r = mem2.shape[0]
    tm = _tile(r, 512)
    row = lambda i: (i, 0)
    gw = GROUP_WIDTH
    return pl.pallas_call(
        _mem_kv_kernel,
        grid=(r // tm,),
        in_specs=[pl.BlockSpec((tm, D_MODEL), row), pl.BlockSpec((1, D_MODEL), lambda i: (0, 0)),
                  _lspec(l, (D_MODEL, 2 * gw)), _lspec(l, (1, MEM_HEAD_DIM))],
        out_specs=[pl.BlockSpec((tm, gw), row), pl.BlockSpec((tm, gw), row)],
        out_shape=[jax.ShapeDtypeStruct((r, gw), BF16), jax.ShapeDtypeStruct((r, gw), BF16)],
        compiler_params=_params("arbitrary"),
        name="mem_kv",
    )(mem2, mg, w_kv, kg)


MOE_TILE = 256
GRANULE = 16
TILE_SLOTS = MOE_TILE // GRANULE + N_GROUPS
SORTED_ROWS = TILE_SLOTS * GRANULE
STEP_GRANULES = 32
STEP_ROWS = STEP_GRANULES * GRANULE
XS_WIDTH = D_MODEL + LANES
INFO_POS_LANE = 0


def _route_t(lt):
    row_i = lax.broadcasted_iota(jnp.int32, lt.shape, 0)
    row = row_i.astype(F32)
    e0 = ROUTER_EXPERT_LANE0
    lg = jnp.where(row_i < N_GROUPS, lt, NEG_BIG)
    mg = jnp.max(lg, axis=0, keepdims=True)
    g_gate = 1.0 / jnp.sum(jnp.exp(lg - mg), axis=0, keepdims=True)
    g_idx = jnp.min(jnp.where(lg == mg, row, float(LANES)), axis=0, keepdims=True)
    row_group = ((row_i - e0) >> 3).astype(F32)
    sel = (row_i >= e0) & (row_i < e0 + N_EXPERTS) & (row_group == g_idx)
    le = jnp.where(sel, lt, NEG_BIG)
    m1 = jnp.max(le, axis=0, keepdims=True)
    se = jnp.sum(jnp.where(sel, jnp.exp(le - m1), 0.0), axis=0, keepdims=True)
    i1 = jnp.min(jnp.where(sel & (le == m1), row, float(LANES)), axis=0, keepdims=True)
    le2 = jnp.where(row == i1, NEG_BIG, le)
    m2 = jnp.max(le2, axis=0, keepdims=True)
    i2 = jnp.min(jnp.where(sel & (le2 == m2) & (row != i1), row, float(LANES)), axis=0, keepdims=True)
    p1 = 1.0 / se
    p2 = jnp.exp(m2 - m1) / se
    tot = p1 + p2
    w = jnp.where(row == i1, p1 / tot, jnp.where(row == i2, p2 / tot, 0.0))
    return g_gate * w, g_idx


def _split3(c):
    hi = c.astype(BF16).astype(F32)
    r1 = c - hi
    mid = r1.astype(BF16).astype(F32)
    lo = (r1 - mid).astype(BF16).astype(F32)
    return hi, mid, lo


def _sort_logits(x1, g2, rt_hi, rt_lo, rb_col):
    xn = _rms_rows(x1, g2)
    hi = xn.astype(BF16)
    lo = (xn - hi.astype(F32)).astype(BF16)
    logits_t = (lax.dot_general(rt_hi, hi, _NT, preferred_element_type=F32)
                + lax.dot_general(rt_hi, lo, _NT, preferred_element_type=F32)
                + lax.dot_general(rt_lo, hi, _NT, preferred_element_type=F32) + rb_col)
    return hi, logits_t


def _sort_rank(logits_t, earlier):
    tl = logits_t.shape[1]
    comb_t, g_idx = _route_t(logits_t)
    grow = lax.broadcasted_iota(jnp.int32, (SUBLANES, tl), 0).astype(F32)
    gt = jnp.where(grow == g_idx, 1.0, 0.0)
    before = jnp.dot(gt.astype(BF16), earlier, preferred_element_type=F32)
    return comb_t, gt, before


def _sort_emit(hi, comb_t, gt, before):
    tl = hi.shape[0]
    rank = jnp.sum(gt * before, axis=0, keepdims=True)
    cnt = jnp.sum(gt, axis=1, keepdims=True)
    glen = jnp.floor((cnt + (GRANULE - 1)) * (1.0 / GRANULE))
    r8 = lax.broadcasted_iota(jnp.int32, (SUBLANES, 1), 0)
    start = jnp.zeros((SUBLANES, 1), F32)
    for g in range(1, N_GROUPS):
        start = jnp.where(r8 == g, jnp.sum(jnp.where(r8 < g, glen, 0.0), axis=0, keepdims=True), start)
    pos = jnp.sum(gt * (start * GRANULE), axis=0, keepdims=True) + rank

    rows = lax.broadcasted_iota(jnp.int32, (LANES, tl), 0)
    info = jnp.where(rows == INFO_POS_LANE, pos, comb_t).T
    lane = _lane_iota(info.shape)
    e0 = ROUTER_EXPERT_LANE0
    c_hi, c_mid, c_lo = _split3(jnp.where((lane >= e0) & (lane < e0 + N_EXPERTS), info, 0.0))
    aug = (c_hi + pltpu.roll(c_mid, N_EXPERTS, axis=1) + pltpu.roll(c_lo, 2 * N_EXPERTS, axis=1)).astype(BF16)
    perm = jnp.where(pos == lax.broadcasted_iota(jnp.int32, (SORTED_ROWS, tl), 0).astype(F32),
                     1.0, 0.0).astype(BF16)
    xs = jnp.dot(perm, hi, preferred_element_type=F32).astype(BF16)
    xs_aug = jnp.dot(perm, aug, preferred_element_type=F32).astype(BF16)
    return xs, xs_aug, info, glen


def _out_sort_kernel(x_ref, yl_ref, yd_ref, ym_ref, w_ref, g2_ref, rthi_ref, rtlo_ref, rb_ref, earlier_ref,
                     o_ref, xs_ref, info_ref, lens_ref, x1s_ref):
    @pl.when(pl.program_id(0) == 0)
    def _():
        x1s_ref[...] = jnp.zeros_like(x1s_ref)

    tm = x1s_ref.shape[0]
    subs = range(tm // MOE_TILE)

    def project(c, nchunks):
        r = slice(c * (tm // nchunks), (c + 1) * (tm // nchunks))
        y = jnp.concatenate([yl_ref[r, :], yd_ref[r, :], ym_ref[r, :]], axis=-1)
        x1 = x_ref[r, :] + jnp.dot(y, w_ref[...], preferred_element_type=F32)
        o_ref[r, :] = x1
        return r, x1

    fresh = [project(0, 2)]
    s1 = [_sort_logits(x1s_ref[sub * MOE_TILE:(sub + 1) * MOE_TILE, :], g2_ref[...], rthi_ref[...],
                       rtlo_ref[...], rb_ref[...]) for sub in subs]
    for r, x1 in fresh:
        x1s_ref[r, :] = x1
    fresh = [project(1, 2)]
    s2 = [_sort_rank(s1[sub][1], earlier_ref[...]) for sub in subs]
    for r, x1 in fresh:
        x1s_ref[r, :] = x1
    for sub in subs:
        xs, xs_aug, info, glen = _sort_emit(s1[sub][0], *s2[sub])
        r0 = sub * SORTED_ROWS
        xs_ref[r0:r0 + SORTED_ROWS, 0:D_MODEL] = xs
        xs_ref[r0:r0 + SORTED_ROWS, D_MODEL:XS_WIDTH] = xs_aug
        info_ref[sub * MOE_TILE:(sub + 1) * MOE_TILE, :] = info
        lens_ref[sub] = jnp.broadcast_to(glen, (SUBLANES, LANES))


def _out_sort(x2, y_lru, y_diff, y_mem, l, w_out, g2, rt_hi, rt_lo, rb_col, earlier):
    t = x2.shape[0]
    tm = SEQ_TILE
    assert t % tm == 0 and tm % MOE_TILE == 0
    sub = tm // MOE_TILE
    nt = t // MOE_TILE
    n = t // tm
    cur = lambda i: (jnp.minimum(i, n - 1), 0)
    prev = lambda i: (jnp.maximum(i - 1, 0), 0)
    gw = GROUP_WIDTH
    return pl.pallas_call(
        _out_sort_kernel,
        grid=(n + 1,),
        in_specs=[pl.BlockSpec((tm, D_MODEL), cur), pl.BlockSpec((tm, gw), cur), pl.BlockSpec((tm, gw), cur),
                  pl.BlockSpec((tm, gw), cur), _lspec(l, (3 * gw, D_MODEL)),
                  _lspec(l, (1, D_MODEL)), _lspec(l, (LANES, D_MODEL)),
                  _lspec(l, (LANES, D_MODEL)), _lspec(l, (LANES, 1)),
                  pl.BlockSpec((MOE_TILE, MOE_TILE), lambda i: (0, 0))],
        out_specs=[pl.BlockSpec((tm, D_MODEL), cur), pl.BlockSpec((sub * SORTED_ROWS, XS_WIDTH), prev),
                   pl.BlockSpec((tm, LANES), prev),
                   pl.BlockSpec((sub, SUBLANES, LANES), lambda i: (jnp.maximum(i - 1, 0), 0, 0))],
        out_shape=[jax.ShapeDtypeStruct((t, D_MODEL), F32),
                   jax.ShapeDtypeStruct((nt * SORTED_ROWS, XS_WIDTH), BF16),
                   jax.ShapeDtypeStruct((t, LANES), F32),
                   jax.ShapeDtypeStruct((nt, SUBLANES, LANES), F32)],
        scratch_shapes=[pltpu.VMEM((tm, D_MODEL), F32)],
        compiler_params=_params("arbitrary"),
        name="out_proj_sort",
    )(x2, y_lru, y_diff, y_mem, w_out, g2, rt_hi, rt_lo, rb_col, earlier)


def _moe_expert_kernel(sg_ref, sv_ref, gi_ref, *refs):
    del gi_ref
    xs_refs = refs[:STEP_GRANULES]
    wg_ref, wu_ref, wd_ref, y_ref = refs[STEP_GRANULES:]
    s = pl.program_id(0)

    @pl.when(sv_ref[s] == 0)
    def _():
        y_ref[...] = jnp.zeros_like(y_ref)

    @pl.when(sv_ref[s] != 0)
    def _():
        rows = jnp.concatenate([r[...] for r in xs_refs], axis=0)
        x = rows[:, 0:D_MODEL]
        aug = rows[:, D_MODEL:XS_WIDTH].astype(F32)
        lane = _lane_iota(aug.shape)
        e0 = ROUTER_EXPERT_LANE0
        comb = jnp.where((lane >= e0) & (lane < e0 + N_EXPERTS),
                         aug + pltpu.roll(aug, LANES - N_EXPERTS, axis=1)
                         + pltpu.roll(aug, LANES - 2 * N_EXPERTS, axis=1), 0.0)
        first = e0 + sg_ref[s] * EXPERTS_PER_GROUP
        hs = []
        for e in range(EXPERTS_PER_GROUP):
            cw = jnp.sum(jnp.where(lane == first + e, comb, 0.0), axis=-1, keepdims=True)
            hg = jnp.dot(x, wg_ref[e], preferred_element_type=F32)
            hu = jnp.dot(x, wu_ref[e], preferred_element_type=F32)
            hs.append((jax.nn.silu(hg) * hu * cw).astype(BF16))
        h = jnp.concatenate(hs, axis=-1)
        y_ref[...] = jnp.dot(h, wd_ref[...], preferred_element_type=F32).astype(BF16)


def _granule_map(k, s, sg, sv, gi):
    return (gi[s * STEP_GRANULES + k], 0)


def _moe_experts(step_group, step_valid, gran_idx, xs, l, wg, wu, wd):
    nstep = step_group.shape[0]
    epg = EXPERTS_PER_GROUP
    wmap4 = lambda s, sg, sv, gi: (l * N_GROUPS + sg[s], 0, 0, 0)
    in_specs = [pl.BlockSpec((GRANULE, XS_WIDTH), functools.partial(_granule_map, k)) for k in range(STEP_GRANULES)]
    in_specs += [pl.BlockSpec((None, epg, D_MODEL, D_EXPERT), wmap4),
                 pl.BlockSpec((None, epg, D_MODEL, D_EXPERT), wmap4),
                 pl.BlockSpec((None, epg * D_EXPERT, D_MODEL), lambda s, sg, sv, gi: (l * N_GROUPS + sg[s], 0, 0))]
    return pl.pallas_call(
        _moe_expert_kernel,
        grid_spec=pltpu.PrefetchScalarGridSpec(
            num_scalar_prefetch=3, grid=(nstep,), in_specs=in_specs,
            out_specs=pl.BlockSpec((STEP_ROWS, D_MODEL), lambda s, sg, sv, gi: (s, 0))),
        out_shape=jax.ShapeDtypeStruct((nstep * STEP_ROWS, D_MODEL), BF16),
        compiler_params=_params("arbitrary"),
        name="moe_experts",
    )(step_group, step_valid, gran_idx, *([xs] * STEP_GRANULES), wg, wu, wd)


def _moe_combine_kernel(inv_ref, x_ref, info_ref, *refs):
    del inv_ref
    ys_refs = refs[:TILE_SLOTS]
    o_ref = refs[TILE_SLOTS]
    o_ref[...] = x_ref[...] + _unsort(info_ref[...], ys_refs)


def _moe_combine(inv, x2, info, ys):
    t = x2.shape[0]
    nt = t // MOE_TILE
    row = lambda i, inv: (i, 0)
    in_specs = [pl.BlockSpec((MOE_TILE, D_MODEL), row), pl.BlockSpec((MOE_TILE, LANES), row)]
    in_specs += [pl.BlockSpec((GRANULE, D_MODEL), functools.partial(_slot_map, q, (1, 0)))
                 for q in range(TILE_SLOTS)]
    return pl.pallas_call(
        _moe_combine_kernel,
        grid_spec=pltpu.PrefetchScalarGridSpec(
            num_scalar_prefetch=1, grid=(nt,), in_specs=in_specs,
            out_specs=pl.BlockSpec((MOE_TILE, D_MODEL), row)),
        out_shape=jax.ShapeDtypeStruct((t, D_MODEL), F32),
        compiler_params=_params("arbitrary"),
        name="moe_combine",
    )(inv, x2, info, *([ys] * TILE_SLOTS))


def _moe_tables(lens, nt):
    i32 = jnp.int32
    ng = N_GROUPS
    garange = jnp.arange(ng, dtype=i32)
    cum = jnp.cumsum(lens, axis=1)
    start = cum - lens
    base = jnp.arange(nt, dtype=i32)[:, None] * TILE_SLOTS + start
    run_len = lens.T.reshape(-1)
    run_base = base.T.reshape(-1)
    run_end = jnp.cumsum(run_len)
    run_start = run_end - run_len
    n_g = jnp.sum(lens, axis=0)
    steps_g = (n_g + STEP_GRANULES - 1) // STEP_GRANULES
    step_end = jnp.cumsum(steps_g)
    step_off = step_end - steps_g
    gran_off = jnp.cumsum(n_g) - n_g
    nstep = (nt * (TILE_SLOTS - 1) + STEP_GRANULES - 1) // STEP_GRANULES + ng
    s = jnp.arange(nstep, dtype=i32)
    sg = jnp.minimum(jnp.sum((s[:, None] >= step_end[None, :]).astype(i32), axis=-1), ng - 1)
    goh = sg[:, None] == garange[None, :]
    pick = lambda v: jnp.sum(jnp.where(goh, v[None, :], 0), axis=-1)
    sv = s < step_end[-1]
    jl = (s - pick(step_off))[:, None] * STEP_GRANULES + jnp.arange(STEP_GRANULES, dtype=i32)[None, :]
    ok = sv[:, None] & (jl < pick(n_g)[:, None])
    j = jnp.where(ok, pick(gran_off)[:, None] + jl, 0)
    inrun = (j[..., None] >= run_start) & (j[..., None] < run_end)
    gran = j + jnp.sum(jnp.where(inrun, run_base - run_start, 0), axis=-1)
    gran = jnp.where(ok, gran, jnp.where(sv[:, None], gran[:, 0:1], 0)).astype(i32)
    q = jnp.arange(TILE_SLOTS, dtype=i32)[None, :]
    gq = jnp.minimum(jnp.sum((q[:, :, None] >= cum[:, None, :]).astype(i32), axis=-1), ng - 1)
    qoh = gq[..., None] == garange
    used = q < cum[:, -1:]
    per_g = run_start.reshape(ng, nt).T - start + (step_off * STEP_GRANULES - gran_off)[None, :]
    inv = jnp.where(used, q + jnp.sum(jnp.where(qoh, per_g[:, None, :], 0), axis=-1), 0).astype(i32)
    return sg, sv.astype(i32), gran.reshape(-1), inv.reshape(-1)


def _out_proj_moe(x2, y_lru, y_diff, y_mem, l, w_out, g2, rt_hi, rt_lo, rb_col, earlier, wg, wu, wd):
    nt = x2.shape[0] // MOE_TILE
    x1, xs, info, lens = _out_sort(x2, y_lru, y_diff, y_mem, l, w_out, g2, rt_hi, rt_lo, rb_col, earlier)
    sg, sv, gran, inv = _moe_tables(lens[:, 0:N_GROUPS, 0].astype(jnp.int32), nt)
    ys = _moe_experts(sg, sv, gran, xs, l, wg, wu, wd)
    return x1, (inv, info, ys)


def _block_diag(w):
    depth, h, n, _ = w.shape
    eye = jnp.eye(h, dtype=w.dtype)
    return (eye[None, :, None, :, None] * w[:, :, :, None, :]).reshape(depth, h * n, h * n)


def _router_tables(w_rg, b_rg, w_re, b_re):
    depth = w_rg.shape[0]
    e0 = ROUTER_EXPERT_LANE0
    pad = lambda rows, width: jnp.zeros((depth, rows, width), F32)
    w = jnp.concatenate([jnp.swapaxes(w_rg, 1, 2), pad(e0 - N_GROUPS, D_MODEL), jnp.swapaxes(w_re, 1, 2),
                         pad(LANES - e0 - N_EXPERTS, D_MODEL)], axis=1)
    bias = jnp.concatenate([b_rg[:, :, None], pad(e0 - N_GROUPS, 1), b_re[:, :, None],
                            pad(LANES - e0 - N_EXPERTS, 1)], axis=1)
    hi = w.astype(BF16)
    lo = (w - hi.astype(F32)).astype(BF16)
    return hi, lo, bias


def kernel(x, mem, norm1_g, w_in, conv_w, conv_b, rg_wa, rg_ba, rg_wx, rg_bx, rg_lambda, dq_norm_g, dk_norm_g, lambda_q1, lambda_k1, lambda_q2, lambda_k2, diff_head_norm_g, mem_norm_g, w_mem_kv, mq_norm_g, mk_norm_g, w_out, norm2_g, w_router_group, b_router_group, w_router_expert, b_router_expert, w_expert_gate, w_expert_up, w_expert_down):
    b, s, d = x.shape
    m = mem.shape[1]
    depth = w_in.shape[0]
    gw = GROUP_WIDTH
    epg = EXPERTS_PER_GROUP
    x2 = x.reshape(b * s, d)
    mem2 = mem.reshape(b * m, d)
    row = lambda v: v.reshape(depth, 1, -1).astype(F32)
    col = lambda v: v.reshape(depth, -1, 1).astype(F32)
    lam_inits = [0.8 - 0.6 * math.exp(-0.3 * l) for l in range(depth)]
    lam = (jnp.exp(jnp.sum(lambda_q1 * lambda_k1, axis=-1)) - jnp.exp(jnp.sum(lambda_q2 * lambda_k2, axis=-1))
           + jnp.asarray(lam_inits, F32))
    lam = jnp.stack([lam, _score_bound(dq_norm_g, dk_norm_g)], axis=1).astype(F32)
    w_in_bf = w_in.astype(BF16)
    w_in_t = jnp.swapaxes(w_in_bf, 1, 2)
    qg_col = col(jnp.tile(dq_norm_g, (1, gw // DIFF_QK_DIM)))
    kg = row(jnp.tile(dk_norm_g, (1, 2)))
    wa_bd = _block_diag(rg_wa).astype(BF16)
    wx_bd = _block_diag(rg_wx).astype(BF16)
    w_kv_bf = w_mem_kv.astype(BF16)
    w_out_bf = w_out.astype(BF16)
    rt_hi, rt_lo, rb_col = _router_tables(w_router_group, b_router_group, w_router_expert, b_router_expert)
    earlier = jnp.triu(jnp.ones((MOE_TILE, MOE_TILE), BF16), k=1)
    wg = w_expert_gate.astype(BF16).reshape(depth * N_GROUPS, epg, D_MODEL, D_EXPERT)
    wu = w_expert_up.astype(BF16).reshape(depth * N_GROUPS, epg, D_MODEL, D_EXPERT)
    wd = w_expert_down.astype(BF16).reshape(depth * N_GROUPS, epg * D_EXPERT, D_MODEL)
    g1, g2, mqg, mkg = row(norm1_g), row(norm2_g), row(mq_norm_g), row(mk_norm_g)
    conv_b3, ba, bx, lru_lam, hg_col = row(conv_b), row(rg_ba), row(rg_bx), row(rg_lambda), col(diff_head_norm_g)
    mem_g = mem_norm_g.reshape(1, -1).astype(F32)
    moe = None
    for l in range(depth):
        mk, mv = _mem_kv(mem2, mem_g, l, w_kv_bf, mkg)
        proj = _in_proj(x2, b, s, l, g1, w_in_bf, w_in_t, qg_col, kg, mqg, mk, mv, moe)
        if moe is not None:
            x2, *proj = proj
        lru_in, k, y_mem, qt, vt = proj
        y_lru = _lru(lru_in, b, s, l, conv_w, conv_b3, wa_bd, ba, wx_bd, bx, lru_lam)
        y_diff = _diff_attn(qt, k, vt, lam, hg_col, b, s, l, lam_inits[l])
        x2, moe = _out_proj_moe(x2, y_lru, y_diff, y_mem, l, w_out_bf, g2, rt_hi, rt_lo, rb_col, earlier,
                                wg, wu, wd)
    return _moe_combine(moe[0], x2, moe[1], moe[2]).reshape(b, s, d)
```

```python
import functools
import math

import jax
import jax.numpy as jnp
from jax import lax
from jax.experimental import pallas as pl
from jax.experimental.pallas import tpu as pltpu

F32 = jnp.float32
BF16 = jnp.bfloat16

D_MODEL = 1024
GROUP_WIDTH = D_MODEL // 2
LRU_HEADS = 8
CONV_WIDTH = 4
LRU_C = 8.0
DIFF_HEADS = 4
DIFF_HEAD_DIM = GROUP_WIDTH // DIFF_HEADS
DIFF_QK_DIM = DIFF_HEAD_DIM // 2
MEM_HEADS = 4
MEM_HEAD_DIM = GROUP_WIDTH // MEM_HEADS
N_GROUPS = 4
EXPERTS_PER_GROUP = 8
N_EXPERTS = N_GROUPS * EXPERTS_PER_GROUP
D_EXPERT = 256
D_IN_PROJ = 6 * GROUP_WIDTH
EPS = 1e-6

LANES = 128
SUBLANES = 8
VMEM_LIMIT = 56 * 1024 * 1024
NEG_BIG = -1e30
ROUTER_EXPERT_LANE0 = 32


def _tile(n, pref):
    t = min(n, pref)
    assert n % t == 0, (n, t)
    return t


def _params(*sem):
    return pltpu.CompilerParams(dimension_semantics=sem, vmem_limit_bytes=VMEM_LIMIT)


def _lspec(l, tail, blk=None):
    idx = (l,) + tuple(blk if blk is not None else (0,) * len(tail))
    return pl.BlockSpec((None,) + tuple(tail), lambda *_: idx)


def _rms_rows(x, g):
    return x * lax.rsqrt(jnp.mean(x * x, axis=-1, keepdims=True) + EPS) * g


def _lane_iota(shape):
    return lax.broadcasted_iota(jnp.int32, shape, len(shape) - 1)


def _half_head_norm(z, g):
    sq = z * z
    lo = _lane_iota(z.shape) < DIFF_QK_DIM
    s_all = jnp.sum(sq, axis=-1, keepdims=True)
    s_lo = jnp.sum(jnp.where(lo, sq, 0.0), axis=-1, keepdims=True)
    inv_lo = lax.rsqrt(s_lo * (1.0 / DIFF_QK_DIM) + EPS)
    inv_hi = lax.rsqrt((s_all - s_lo) * (1.0 / DIFF_QK_DIM) + EPS)
    return z * jnp.where(lo, inv_lo, inv_hi) * g


SEQ_TILE = 512
_NT = (((1,), (1,)), ((), ()))


N_PROJ_PARAMS = 18
N_PROJ_OUTS = 5


def _softplus(z):
    return jnp.maximum(z, 0.0) + jnp.log(1.0 + jnp.exp(-jnp.abs(z)))


def _in_proj_body(x, ns, g1_ref, wl_ref, wk_ref, wmq_ref, wqt_ref, wvt_ref, qg_ref, kg_ref, mqg_ref,
                  mk_ref, mv_ref, cw_ref, cb_ref, wa_ref, ba_ref, wx_ref, bx_ref, llam_ref,
                  ylru_ref, k_ref, ym_ref, qt_ref, vt_ref, ext_ref, hcar_ref, a_ref, b_ref, h_ref):
    gw = GROUP_WIDTH
    tm = x.shape[0]
    hist = SUBLANES
    first = lax.rem(pl.program_id(0), ns) == 0

    @pl.when(first)
    def _():
        ext_ref[0:hist, :] = jnp.zeros((hist, gw), F32)
        hcar_ref[...] = jnp.zeros_like(hcar_ref)

    @pl.when(jnp.logical_not(first))
    def _():
        ext_ref[0:hist, :] = ext_ref[tm:tm + hist, :]

    xn = _rms_rows(x, g1_ref[...]).astype(BF16)
    heads = range(MEM_HEADS)
    hsl = [slice(h * MEM_HEAD_DIM, (h + 1) * MEM_HEAD_DIM) for h in heads]
    mq = jnp.dot(xn, wmq_ref[...], preferred_element_type=F32)
    u = jnp.dot(xn, wl_ref[...], preferred_element_type=F32)
    ext_ref[hist:hist + tm, :] = u[:, 0:gw]
    xc = cb_ref[...] + cw_ref[CONV_WIDTH - 1:CONV_WIDTH, :] * u[:, 0:gw]
    for j in range(CONV_WIDTH - 1):
        off = hist - (CONV_WIDTH - 1) + j
        xc = xc + cw_ref[j:j + 1, :] * ext_ref[off:off + tm, :]
    xcb = xc.astype(BF16)
    sc = [lax.dot_general(_rms_rows(mq[:, hsl[h]], mqg_ref[...]).astype(BF16), mk_ref[:, hsl[h]], _NT,
                          preferred_element_type=F32) * MEM_HEAD_DIM ** -0.5 for h in heads]
    k = jnp.dot(xn, wk_ref[...], preferred_element_type=F32)
    r_pre = jnp.dot(xcb, wa_ref[...], preferred_element_type=F32)
    i_pre = jnp.dot(xcb, wx_ref[...], preferred_element_type=F32)
    p = [jnp.exp(sc[h] - jnp.max(sc[h], axis=-1, keepdims=True)) for h in heads]
    qt = lax.dot_general(wqt_ref[...], xn, _NT, preferred_element_type=F32)
    o = [jnp.dot(p[h].astype(BF16), mv_ref[:, hsl[h]], preferred_element_type=F32) for h in heads]
    vt_ref[...] = lax.dot_general(wvt_ref[...], xn, _NT, preferred_element_type=F32).astype(BF16)

    r = jax.nn.sigmoid(r_pre + ba_ref[...])
    gate_i = jax.nn.sigmoid(i_pre + bx_ref[...])
    a = jnp.exp((-LRU_C * r) * _softplus(-llam_ref[...]))
    om = 1.0 - a * a
    b = om * lax.rsqrt(jnp.maximum(om, 1e-30)) * (gate_i * xc)
    row = lax.broadcasted_iota(jnp.int32, (tm, gw), 0) & (SUBLANES - 1)
    d = 1
    while d < SUBLANES:
        keep = row >= d
        a_prev = pltpu.roll(a, d, axis=0)
        b_prev = pltpu.roll(b, d, axis=0)
        b = jnp.where(keep, a * b_prev + b, b)
        a = jnp.where(keep, a * a_prev, a)
        d *= 2
    a_ref[...] = a
    b_ref[...] = b
    gated = jax.nn.gelu(u[:, gw:2 * gw])

    for h in range(DIFF_HEADS):
        sl = slice(h * LANES, (h + 1) * LANES)
        k_ref[:, sl] = _half_head_norm(k[:, sl], kg_ref[...]).astype(BF16)
    for h in heads:
        ym_ref[:, hsl[h]] = (o[h] / jnp.sum(p[h], axis=-1, keepdims=True)).astype(BF16)
    q3 = qt.reshape(gw // DIFF_QK_DIM, DIFF_QK_DIM, tm)
    q3 = q3 * lax.rsqrt(jnp.mean(q3 * q3, axis=1, keepdims=True) + EPS)
    qscale = DIFF_QK_DIM ** -0.5 * math.log2(math.e)
    qt_ref[...] = (q3.reshape(gw, tm) * (qg_ref[...] * qscale)).astype(BF16)

    def block(i, hprev):
        off = pl.multiple_of(i * SUBLANES, SUBLANES)
        hb = a_ref[pl.ds(off, SUBLANES), :] * hprev + b_ref[pl.ds(off, SUBLANES), :]
        h_ref[pl.ds(off, SUBLANES), :] = hb
        return jnp.broadcast_to(hb[SUBLANES - 1:SUBLANES, :], (SUBLANES, gw))

    hcar_ref[...] = lax.fori_loop(0, tm // SUBLANES, block, hcar_ref[...], unroll=8)
    ylru_ref[...] = (h_ref[...] * gated).astype(BF16)


def _in_proj_kernel(x_ref, *refs, ns):
    _in_proj_body(x_ref[...], ns, *refs)


def _unsort(info, ys_refs):
    kpad = -SORTED_ROWS % LANES
    ys = jnp.concatenate([r[...] for r in ys_refs] + [jnp.zeros((kpad, D_MODEL), BF16)], axis=0)
    pos = info[:, INFO_POS_LANE:INFO_POS_LANE + 1]
    unperm = jnp.where(pos == _lane_iota((MOE_TILE, SORTED_ROWS + kpad)).astype(F32), 1.0, 0.0).astype(BF16)
    return jnp.dot(unperm, ys, preferred_element_type=F32)


def _combine_in_proj_kernel(inv_ref, x1_ref, info_ref, *refs, ns):
    del inv_ref
    nsub = x1_ref.shape[0] // MOE_TILE
    ys_refs, rest = refs[:nsub * TILE_SLOTS], refs[nsub * TILE_SLOTS:]
    params, x_out_ref, outs = rest[:N_PROJ_PARAMS], rest[N_PROJ_PARAMS], rest[N_PROJ_PARAMS + 1:]
    x = jnp.concatenate(
        [x1_ref[u * MOE_TILE:(u + 1) * MOE_TILE, :]
         + _unsort(info_ref[u * MOE_TILE:(u + 1) * MOE_TILE, :], ys_refs[u * TILE_SLOTS:(u + 1) * TILE_SLOTS])
         for u in range(nsub)], axis=0)
    x_out_ref[...] = x
    _in_proj_body(x, ns, *params, *outs)


def _slot_map(q, sub, i, inv):
    return (inv[(i * sub[0] + sub[1]) * TILE_SLOTS + q], 0)


def _in_proj(x2, b, s, l, g1, w_in, w_in_t, qg_col, kg, mqg, mk, mv, lru, moe=None):
    t = x2.shape[0]
    tm = SEQ_TILE
    assert s % tm == 0
    ns = s // tm
    m = mk.shape[0] // b
    gw = GROUP_WIDTH
    row = lambda i, *_: (i, 0)
    fm = lambda i, *_: (i // ns, i % ns, 0, 0)
    mem = pl.BlockSpec((m, gw), lambda i, *_: (i // ns, 0))
    in_specs = [_lspec(l, (1, D_MODEL)),
                _lspec(l, (D_MODEL, 2 * gw), (0, 0)), _lspec(l, (D_MODEL, gw), (0, 3)),
                _lspec(l, (D_MODEL, gw), (0, 5)), _lspec(l, (gw, D_MODEL), (2, 0)),
                _lspec(l, (gw, D_MODEL), (4, 0)), _lspec(l, (gw, 1)),
                _lspec(l, (1, LANES)), _lspec(l, (1, MEM_HEAD_DIM)), mem, mem]
    vec = _lspec(l, (1, gw))
    in_specs += [_lspec(l, (CONV_WIDTH, gw)), vec, _lspec(l, (gw, gw)), vec, _lspec(l, (gw, gw)), vec, vec]
    operands = (g1, w_in, w_in, w_in, w_in_t, w_in_t, qg_col, kg, mqg, mk, mv) + tuple(lru)
    assert len(operands) == N_PROJ_PARAMS
    out_specs = [pl.BlockSpec((tm, gw), row), pl.BlockSpec((tm, gw), row), pl.BlockSpec((tm, gw), row),
                 pl.BlockSpec((None, None, gw, tm), fm), pl.BlockSpec((None, None, gw, tm), fm)]
    out_shape = [jax.ShapeDtypeStruct((t, gw), BF16), jax.ShapeDtypeStruct((t, gw), BF16),
                 jax.ShapeDtypeStruct((t, gw), BF16), jax.ShapeDtypeStruct((b, ns, gw, tm), BF16),
                 jax.ShapeDtypeStruct((b, ns, gw, tm), BF16)]
    assert len(out_specs) == N_PROJ_OUTS
    scratch = [pltpu.VMEM((tm + SUBLANES, gw), F32), pltpu.VMEM((SUBLANES, gw), F32),
               pltpu.VMEM((tm, gw), F32), pltpu.VMEM((tm, gw), F32), pltpu.VMEM((tm, gw), F32)]
    xspec = pl.BlockSpec((tm, D_MODEL), row)
    if moe is None:
        return pl.pallas_call(
            functools.partial(_in_proj_kernel, ns=ns), grid=(t // tm,), in_specs=[xspec] + in_specs,
            out_specs=out_specs, out_shape=out_shape, scratch_shapes=scratch,
            compiler_params=_params("arbitrary"), name="in_proj",
        )(x2, *operands)
    inv, info, ys = moe
    nsub = tm // MOE_TILE
    gran = [pl.BlockSpec((GRANULE, D_MODEL), functools.partial(_slot_map, q, (nsub, u)))
            for u in range(nsub) for q in range(TILE_SLOTS)]
    return pl.pallas_call(
        functools.partial(_combine_in_proj_kernel, ns=ns),
        grid_spec=pltpu.PrefetchScalarGridSpec(
            num_scalar_prefetch=1, grid=(t // tm,),
            in_specs=[xspec, pl.BlockSpec((tm, LANES), row)] + gran + in_specs,
            out_specs=[xspec] + out_specs, scratch_shapes=scratch),
        out_shape=[jax.ShapeDtypeStruct((t, D_MODEL), F32)] + out_shape,
        compiler_params=_params("arbitrary"),
        name="combine_in_proj",
    )(inv, x2, info, *([ys] * (nsub * TILE_SLOTS)), *operands)


ATTN_HEADS_PER_STEP = 2
ATTN_MIN_SUM = 2.0 ** -40


def _score_bound(dq_gain, dk_gain):
    qscale = DIFF_QK_DIM ** -0.5 * math.log2(math.e)
    return (1.02 * qscale * DIFF_QK_DIM) * jnp.max(jnp.abs(dq_gain), axis=-1) * jnp.max(jnp.abs(dk_gain), axis=-1)


def _diff_attn_kernel(lam_ref, qt_ref, k_ref, vt_ref, hg_ref, o_ref,
                      qs_ref, sa_ref, sb_ref, m_ref, l_ref, acc_ref, *, t, layer, out_scale):
    qi = pl.program_id(2)
    hd = DIFF_HEAD_DIM
    heads = range(ATTN_HEADS_PER_STEP)
    lo = lax.broadcasted_iota(jnp.int32, (hd, t), 0) < DIFF_QK_DIM
    for h in heads:
        q = qt_ref[h * hd:(h + 1) * hd, :]
        zero = jnp.zeros_like(q)
        qs_ref[h, :, 0:t] = jnp.where(lo, q, zero)
        qs_ref[h, :, t:2 * t] = jnp.where(lo, zero, q)

    def causal(x, fill):
        kpos = lax.broadcasted_iota(jnp.int32, (t, 2 * t), 0)
        c = lax.broadcasted_iota(jnp.int32, (t, 2 * t), 1)
        return jnp.where(kpos <= jnp.where(c >= t, c - t, c), x, fill)

    def score(j, h):
        off = pl.multiple_of(j * t, t)
        return jnp.dot(k_ref[pl.ds(off, t), h * hd:(h + 1) * hd], qs_ref[h],
                       preferred_element_type=F32)

    bound = lam_ref[layer, 1]
    l_ref[...] = jnp.zeros(l_ref.shape, F32)
    acc_ref[...] = jnp.zeros(acc_ref.shape, F32)

    def fast_update(j, masked):
        for h in heads:
            p = jnp.exp2(score(j, h) - bound)
            if masked:
                p = causal(p, 0.0)
            l_ref[h] += jnp.sum(p, axis=0, keepdims=True)
            acc_ref[h] += jnp.dot(vt_ref[j, h * hd:(h + 1) * hd, :], p.astype(BF16),
                                  preferred_element_type=F32)

    def fast_pair(i, carry):
        fast_update(2 * i, False)
        fast_update(2 * i + 1, False)
        return carry

    lax.fori_loop(0, qi >> 1, fast_pair, 0)

    @pl.when((qi & 1) == 1)
    def _():
        fast_update(qi - 1, False)

    fast_update(qi, True)

    @pl.when(jnp.min(l_ref[...]) < ATTN_MIN_SUM)
    def _():
        m_ref[...] = jnp.full(m_ref.shape, NEG_BIG, F32)
        l_ref[...] = jnp.zeros(l_ref.shape, F32)
        acc_ref[...] = jnp.zeros(acc_ref.shape, F32)

        def scores(j, dst_ref):
            for h in heads:
                dst_ref[h] = score(j, h)

        def update(j, src_ref, masked):
            for h in heads:
                s = src_ref[h]
                if masked:
                    s = causal(s, NEG_BIG)
                m = m_ref[h]
                m_new = jnp.maximum(m, jnp.max(s, axis=0, keepdims=True))
                alpha = jnp.exp2(m - m_new)
                p = jnp.exp2(s - m_new)
                m_ref[h] = m_new
                l_ref[h] = alpha * l_ref[h] + jnp.sum(p, axis=0, keepdims=True)
                vt = vt_ref[j, h * hd:(h + 1) * hd, :]
                acc_ref[h] = alpha * acc_ref[h] + jnp.dot(vt, p.astype(BF16), preferred_element_type=F32)

        scores(0, sa_ref)

        def pair(i, carry):
            scores(2 * i + 1, sb_ref)
            update(2 * i, sa_ref, False)
            scores(2 * i + 2, sa_ref)
            update(2 * i + 1, sb_ref, False)
            return carry

        lax.fori_loop(0, qi >> 1, pair, 0)

        @pl.when((qi & 1) == 1)
        def _():
            scores(qi, sb_ref)
            update(qi - 1, sa_ref, False)
            update(qi, sb_ref, True)

        @pl.when((qi & 1) == 0)
        def _():
            update(qi, sa_ref, True)

    for h in heads:
        o = acc_ref[h] / l_ref[h]
        o = o[:, 0:t] - lam_ref[layer, 0] * o[:, t:2 * t]
        o = o * lax.rsqrt(jnp.mean(o * o, axis=0, keepdims=True) + EPS) * (hg_ref[...] * out_scale)
        o_ref[:, h * hd:(h + 1) * hd] = o.T.astype(BF16)


def _diff_attn(qt, k, vt, lam, hg_col, b, s, l, lam_init):
    t = SEQ_TILE
    nq = s // t
    hp = ATTN_HEADS_PER_STEP
    w = hp * DIFF_HEAD_DIM
    ng = DIFF_HEADS // hp
    kern = functools.partial(_diff_attn_kernel, t=t, layer=l, out_scale=1.0 - lam_init)
    return pl.pallas_call(
        kern,
        grid=(b, ng, nq),
        in_specs=[pl.BlockSpec(memory_space=pltpu.SMEM),
                  pl.BlockSpec((None, None, w, t), lambda bi, g, qi: (bi, qi, g, 0)),
                  pl.BlockSpec((s, w), lambda bi, g, qi: (bi, g)),
                  pl.BlockSpec((None, nq, w, t), lambda bi, g, qi: (bi, 0, g, 0)),
                  _lspec(l, (DIFF_HEAD_DIM, 1))],
        out_specs=pl.BlockSpec((t, w), lambda bi, g, qi: (bi * nq + qi, g)),
        out_shape=jax.ShapeDtypeStruct((b * s, GROUP_WIDTH), BF16),
        scratch_shapes=[pltpu.VMEM((hp, DIFF_HEAD_DIM, 2 * t), BF16),
                        pltpu.VMEM((hp, t, 2 * t), F32), pltpu.VMEM((hp, t, 2 * t), F32),
                        pltpu.VMEM((hp, 1, 2 * t), F32), pltpu.VMEM((hp, 1, 2 * t), F32),
                        pltpu.VMEM((hp, DIFF_HEAD_DIM, 2 * t), F32)],
        compiler_params=_params("arbitrary", "arbitrary", "arbitrary"),
        name="diff_attn",
    )(lam, qt, k, vt, hg_col)


def _mem_kv_kernel(mem_ref, mg_ref, w_ref, kg_ref, mk_ref, mv_ref):
    memn = _rms_rows(mem_ref[...], mg_ref[...]).astype(BF16)
    gw = GROUP_WIDTH
    k = jnp.dot(memn, w_ref[:, 0:gw], preferred_element_type=F32)
    for h in range(MEM_HEADS):
        sl = slice(h * MEM_HEAD_DIM, (h + 1) * MEM_HEAD_DIM)
        mk_ref[:, sl] = _rms_rows(k[:, sl], kg_ref[...]).astype(BF16)
    mv_ref[...] = jnp.dot(memn, w_ref[:, gw:2 * gw], preferred_element_type=F32).astype(BF16)


def _mem_kv(mem2, mg, l, w_kv, kg):
    r = mem2.shape[0]
    tm = _tile(r, 512)
    row = lambda i: (i, 0)
    gw = GROUP_WIDTH
    return pl.pallas_call(
        _mem_kv_kernel,
        grid=(r // tm,),
        in_specs=[pl.BlockSpec((tm, D_MODEL), row), pl.BlockSpec((1, D_MODEL), lambda i: (0, 0)),
                  _lspec(l, (D_MODEL, 2 * gw)), _lspec(l, (1, MEM_HEAD_DIM))],
        out_specs=[pl.BlockSpec((tm, gw), row), pl.BlockSpec((tm, gw), row)],
        out_shape=[jax.ShapeDtypeStruct((r, gw), BF16), jax.ShapeDtypeStruct((r, gw), BF16)],
        compiler_params=_params("arbitrary"),
        name="mem_kv",
    )(mem2, mg, w_kv, kg)


MOE_TILE = 256
GRANULE = 16
TILE_SLOTS = MOE_TILE // GRANULE + N_GROUPS
SORTED_ROWS = TILE_SLOTS * GRANULE
STEP_GRANULES = 32
STEP_ROWS = STEP_GRANULES * GRANULE
XS_WIDTH = D_MODEL + LANES
INFO_POS_LANE = 0


def _route_t(lt):
    row_i = lax.broadcasted_iota(jnp.int32, lt.shape, 0)
    row = row_i.astype(F32)
    e0 = ROUTER_EXPERT_LANE0
    lg = jnp.where(row_i < N_GROUPS, lt, NEG_BIG)
    mg = jnp.max(lg, axis=0, keepdims=True)
    g_gate = 1.0 / jnp.sum(jnp.exp(lg - mg), axis=0, keepdims=True)
    g_idx = jnp.min(jnp.where(lg == mg, row, float(LANES)), axis=0, keepdims=True)
    row_group = ((row_i - e0) >> 3).astype(F32)
    sel = (row_i >= e0) & (row_i < e0 + N_EXPERTS) & (row_group == g_idx)
    le = jnp.where(sel, lt, NEG_BIG)
    m1 = jnp.max(le, axis=0, keepdims=True)
    se = jnp.sum(jnp.where(sel, jnp.exp(le - m1), 0.0), axis=0, keepdims=True)
    i1 = jnp.min(jnp.where(sel & (le == m1), row, float(LANES)), axis=0, keepdims=True)
    le2 = jnp.where(row == i1, NEG_BIG, le)
    m2 = jnp.max(le2, axis=0, keepdims=True)
    i2 = jnp.min(jnp.where(sel & (le2 == m2) & (row != i1), row, float(LANES)), axis=0, keepdims=True)
    p1 = 1.0 / se
    p2 = jnp.exp(m2 - m1) / se
    tot = p1 + p2
    w = jnp.where(row == i1, p1 / tot, jnp.where(row == i2, p2 / tot, 0.0))
    return g_gate * w, g_idx


def _split3(c):
    hi = c.astype(BF16).astype(F32)
    r1 = c - hi
    mid = r1.astype(BF16).astype(F32)
    lo = (r1 - mid).astype(BF16).astype(F32)
    return hi, mid, lo


def _sort_logits(x1, g2, rt_hi, rt_lo, rb_col):
    xn = _rms_rows(x1, g2)
    hi = xn.astype(BF16)
    lo = (xn - hi.astype(F32)).astype(BF16)
    logits_t = (lax.dot_general(rt_hi, hi, _NT, preferred_element_type=F32)
                + lax.dot_general(rt_hi, lo, _NT, preferred_element_type=F32)
                + lax.dot_general(rt_lo, hi, _NT, preferred_element_type=F32) + rb_col)
    return hi, logits_t


def _sort_rank(logits_t, earlier):
    tl = logits_t.shape[1]
    comb_t, g_idx = _route_t(logits_t)
    grow = lax.broadcasted_iota(jnp.int32, (SUBLANES, tl), 0).astype(F32)
    gt = jnp.where(grow == g_idx, 1.0, 0.0)
    before = jnp.dot(gt.astype(BF16), earlier, preferred_element_type=F32)
    return comb_t, gt, before


def _sort_emit(hi, comb_t, gt, before):
    tl = hi.shape[0]
    rank = jnp.sum(gt * before, axis=0, keepdims=True)
    cnt = jnp.sum(gt, axis=1, keepdims=True)
    glen = jnp.floor((cnt + (GRANULE - 1)) * (1.0 / GRANULE))
    r8 = lax.broadcasted_iota(jnp.int32, (SUBLANES, 1), 0)
    start = jnp.zeros((SUBLANES, 1), F32)
    for g in range(1, N_GROUPS):
        start = jnp.where(r8 == g, jnp.sum(jnp.where(r8 < g, glen, 0.0), axis=0, keepdims=True), start)
    pos = jnp.sum(gt * (start * GRANULE), axis=0, keepdims=True) + rank

    rows = lax.broadcasted_iota(jnp.int32, (LANES, tl), 0)
    info = jnp.where(rows == INFO_POS_LANE, pos, comb_t).T
    lane = _lane_iota(info.shape)
    e0 = ROUTER_EXPERT_LANE0
    c_hi, c_mid, c_lo = _split3(jnp.where((lane >= e0) & (lane < e0 + N_EXPERTS), info, 0.0))
    aug = (c_hi + pltpu.roll(c_mid, N_EXPERTS, axis=1) + pltpu.roll(c_lo, 2 * N_EXPERTS, axis=1)).astype(BF16)
    perm = jnp.where(pos == lax.broadcasted_iota(jnp.int32, (SORTED_ROWS, tl), 0).astype(F32),
                     1.0, 0.0).astype(BF16)
    xs = jnp.dot(perm, hi, preferred_element_type=F32).astype(BF16)
    xs_aug = jnp.dot(perm, aug, preferred_element_type=F32).astype(BF16)
    return xs, xs_aug, info, glen


def _out_sort_kernel(x_ref, yl_ref, yd_ref, ym_ref, w_ref, g2_ref, rthi_ref, rtlo_ref, rb_ref, earlier_ref,
                     o_ref, xs_ref, info_ref, lens_ref, x1s_ref):
    @pl.when(pl.program_id(0) == 0)
    def _():
        x1s_ref[...] = jnp.zeros_like(x1s_ref)

    tm = x1s_ref.shape[0]
    subs = range(tm // MOE_TILE)

    def project(c, nchunks):
        r = slice(c * (tm // nchunks), (c + 1) * (tm // nchunks))
        y = jnp.concatenate([yl_ref[r, :], yd_ref[r, :], ym_ref[r, :]], axis=-1)
        x1 = x_ref[r, :] + jnp.dot(y, w_ref[...], preferred_element_type=F32)
        o_ref[r, :] = x1
        return r, x1

    fresh = [project(0, 2)]
    s1 = [_sort_logits(x1s_ref[sub * MOE_TILE:(sub + 1) * MOE_TILE, :], g2_ref[...], rthi_ref[...],
                       rtlo_ref[...], rb_ref[...]) for sub in subs]
    for r, x1 in fresh:
        x1s_ref[r, :] = x1
    fresh = [project(1, 2)]
    s2 = [_sort_rank(s1[sub][1], earlier_ref[...]) for sub in subs]
    for r, x1 in fresh:
        x1s_ref[r, :] = x1
    for sub in subs:
        xs, xs_aug, info, glen = _sort_emit(s1[sub][0], *s2[sub])
        r0 = sub * SORTED_ROWS
        xs_ref[r0:r0 + SORTED_ROWS, 0:D_MODEL] = xs
        xs_ref[r0:r0 + SORTED_ROWS, D_MODEL:XS_WIDTH] = xs_aug
        info_ref[sub * MOE_TILE:(sub + 1) * MOE_TILE, :] = info
        lens_ref[sub] = jnp.broadcast_to(glen, (SUBLANES, LANES))


def _out_sort(x2, y_lru, y_diff, y_mem, l, w_out, g2, rt_hi, rt_lo, rb_col, earlier):
    t = x2.shape[0]
    tm = SEQ_TILE
    assert t % tm == 0 and tm % MOE_TILE == 0
    sub = tm // MOE_TILE
    nt = t // MOE_TILE
    n = t // tm
    cur = lambda i: (jnp.minimum(i, n - 1), 0)
    prev = lambda i: (jnp.maximum(i - 1, 0), 0)
    gw = GROUP_WIDTH
    return pl.pallas_call(
        _out_sort_kernel,
        grid=(n + 1,),
        in_specs=[pl.BlockSpec((tm, D_MODEL), cur), pl.BlockSpec((tm, gw), cur), pl.BlockSpec((tm, gw), cur),
                  pl.BlockSpec((tm, gw), cur), _lspec(l, (3 * gw, D_MODEL)),
                  _lspec(l, (1, D_MODEL)), _lspec(l, (LANES, D_MODEL)),
                  _lspec(l, (LANES, D_MODEL)), _lspec(l, (LANES, 1)),
                  pl.BlockSpec((MOE_TILE, MOE_TILE), lambda i: (0, 0))],
        out_specs=[pl.BlockSpec((tm, D_MODEL), cur), pl.BlockSpec((sub * SORTED_ROWS, XS_WIDTH), prev),
                   pl.BlockSpec((tm, LANES), prev),
                   pl.BlockSpec((sub, SUBLANES, LANES), lambda i: (jnp.maximum(i - 1, 0), 0, 0))],
        out_shape=[jax.ShapeDtypeStruct((t, D_MODEL), F32),
                   jax.ShapeDtypeStruct((nt * SORTED_ROWS, XS_WIDTH), BF16),
                   jax.ShapeDtypeStruct((t, LANES), F32),
                   jax.ShapeDtypeStruct((nt, SUBLANES, LANES), F32)],
        scratch_shapes=[pltpu.VMEM((tm, D_MODEL), F32)],
        compiler_params=_params("arbitrary"),
        name="out_proj_sort",
    )(x2, y_lru, y_diff, y_mem, w_out, g2, rt_hi, rt_lo, rb_col, earlier)


def _moe_expert_kernel(sg_ref, sv_ref, gi_ref, *refs):
    del gi_ref
    xs_refs = refs[:STEP_GRANULES]
    wg_ref, wu_ref, wd_ref, y_ref = refs[STEP_GRANULES:]
    s = pl.program_id(0)

    @pl.when(sv_ref[s] == 0)
    def _():
        y_ref[...] = jnp.zeros_like(y_ref)

    @pl.when(sv_ref[s] != 0)
    def _():
        rows = jnp.concatenate([r[...] for r in xs_refs], axis=0)
        x = rows[:, 0:D_MODEL]
        aug = rows[:, D_MODEL:XS_WIDTH].astype(F32)
        lane = _lane_iota(aug.shape)
        e0 = ROUTER_EXPERT_LANE0
        comb = jnp.where((lane >= e0) & (lane < e0 + N_EXPERTS),
                         aug + pltpu.roll(aug, LANES - N_EXPERTS, axis=1)
                         + pltpu.roll(aug, LANES - 2 * N_EXPERTS, axis=1), 0.0)
        first = e0 + sg_ref[s] * EXPERTS_PER_GROUP
        hs = []
        for e in range(EXPERTS_PER_GROUP):
            cw = jnp.sum(jnp.where(lane == first + e, comb, 0.0), axis=-1, keepdims=True)
            hg = jnp.dot(x, wg_ref[e], preferred_element_type=F32)
            hu = jnp.dot(x, wu_ref[e], preferred_element_type=F32)
            hs.append((jax.nn.silu(hg) * hu * cw).astype(BF16))
        h = jnp.concatenate(hs, axis=-1)
        y_ref[...] = jnp.dot(h, wd_ref[...], preferred_element_type=F32).astype(BF16)


def _granule_map(k, s, sg, sv, gi):
    return (gi[s * STEP_GRANULES + k], 0)


def _moe_experts(step_group, step_valid, gran_idx, xs, l, wg, wu, wd):
    nstep = step_group.shape[0]
    epg = EXPERTS_PER_GROUP
    wmap4 = lambda s, sg, sv, gi: (l * N_GROUPS + sg[s], 0, 0, 0)
    in_specs = [pl.BlockSpec((GRANULE, XS_WIDTH), functools.partial(_granule_map, k)) for k in range(STEP_GRANULES)]
    in_specs += [pl.BlockSpec((None, epg, D_MODEL, D_EXPERT), wmap4),
                 pl.BlockSpec((None, epg, D_MODEL, D_EXPERT), wmap4),
                 pl.BlockSpec((None, epg * D_EXPERT, D_MODEL), lambda s, sg, sv, gi: (l * N_GROUPS + sg[s], 0, 0))]
    return pl.pallas_call(
        _moe_expert_kernel,
        grid_spec=pltpu.PrefetchScalarGridSpec(
            num_scalar_prefetch=3, grid=(nstep,), in_specs=in_specs,
            out_specs=pl.BlockSpec((STEP_ROWS, D_MODEL), lambda s, sg, sv, gi: (s, 0))),
        out_shape=jax.ShapeDtypeStruct((nstep * STEP_ROWS, D_MODEL), BF16),
        compiler_params=_params("arbitrary"),
        name="moe_experts",
    )(step_group, step_valid, gran_idx, *([xs] * STEP_GRANULES), wg, wu, wd)


def _moe_combine_kernel(inv_ref, x_ref, info_ref, *refs):
    del inv_ref
    ys_refs = refs[:TILE_SLOTS]
    o_ref = refs[TILE_SLOTS]
    o_ref[...] = x_ref[...] + _unsort(info_ref[...], ys_refs)


def _moe_combine(inv, x2, info, ys):
    t = x2.shape[0]
    nt = t // MOE_TILE
    row = lambda i, inv: (i, 0)
    in_specs = [pl.BlockSpec((MOE_TILE, D_MODEL), row), pl.BlockSpec((MOE_TILE, LANES), row)]
    in_specs += [pl.BlockSpec((GRANULE, D_MODEL), functools.partial(_slot_map, q, (1, 0)))
                 for q in range(TILE_SLOTS)]
    return pl.pallas_call(
        _moe_combine_kernel,
        grid_spec=pltpu.PrefetchScalarGridSpec(
            num_scalar_prefetch=1, grid=(nt,), in_specs=in_specs,
            out_specs=pl.BlockSpec((MOE_TILE, D_MODEL), row)),
        out_shape=jax.ShapeDtypeStruct((t, D_MODEL), F32),
        compiler_params=_params("arbitrary"),
        name="moe_combine",
    )(inv, x2, info, *([ys] * TILE_SLOTS))


def _moe_tables(lens, nt):
    i32 = jnp.int32
    ng = N_GROUPS
    garange = jnp.arange(ng, dtype=i32)
    cum = jnp.cumsum(lens, axis=1)
    start = cum - lens
    base = jnp.arange(nt, dtype=i32)[:, None] * TILE_SLOTS + start
    run_len = lens.T.reshape(-1)
    run_base = base.T.reshape(-1)
    run_end = jnp.cumsum(run_len)
    run_start = run_end - run_len
    n_g = jnp.sum(lens, axis=0)
    steps_g = (n_g + STEP_GRANULES - 1) // STEP_GRANULES
    step_end = jnp.cumsum(steps_g)
    step_off = step_end - steps_g
    gran_off = jnp.cumsum(n_g) - n_g
    nstep = (nt * (TILE_SLOTS - 1) + STEP_GRANULES - 1) // STEP_GRANULES + ng
    s = jnp.arange(nstep, dtype=i32)
    sg = jnp.minimum(jnp.sum((s[:, None] >= step_end[None, :]).astype(i32), axis=-1), ng - 1)
    goh = sg[:, None] == garange[None, :]
    pick = lambda v: jnp.sum(jnp.where(goh, v[None, :], 0), axis=-1)
    sv = s < step_end[-1]
    jl = (s - pick(step_off))[:, None] * STEP_GRANULES + jnp.arange(STEP_GRANULES, dtype=i32)[None, :]
    ok = sv[:, None] & (jl < pick(n_g)[:, None])
    j = jnp.where(ok, pick(gran_off)[:, None] + jl, 0)
    inrun = (j[..., None] >= run_start) & (j[..., None] < run_end)
    gran = j + jnp.sum(jnp.where(inrun, run_base - run_start, 0), axis=-1)
    gran = jnp.where(ok, gran, jnp.where(sv[:, None], gran[:, 0:1], 0)).astype(i32)
    q = jnp.arange(TILE_SLOTS, dtype=i32)[None, :]
    gq = jnp.minimum(jnp.sum((q[:, :, None] >= cum[:, None, :]).astype(i32), axis=-1), ng - 1)
    qoh = gq[..., None] == garange
    used = q < cum[:, -1:]
    per_g = run_start.reshape(ng, nt).T - start + (step_off * STEP_GRANULES - gran_off)[None, :]
    inv = jnp.where(used, q + jnp.sum(jnp.where(qoh, per_g[:, None, :], 0), axis=-1), 0).astype(i32)
    return sg, sv.astype(i32), gran.reshape(-1), inv.reshape(-1)


def _out_proj_moe(x2, y_lru, y_diff, y_mem, l, w_out, g2, rt_hi, rt_lo, rb_col, earlier, wg, wu, wd):
    nt = x2.shape[0] // MOE_TILE
    x1, xs, info, lens = _out_sort(x2, y_lru, y_diff, y_mem, l, w_out, g2, rt_hi, rt_lo, rb_col, earlier)
    sg, sv, gran, inv = _moe_tables(lens[:, 0:N_GROUPS, 0].astype(jnp.int32), nt)
    ys = _moe_experts(sg, sv, gran, xs, l, wg, wu, wd)
    return x1, (inv, info, ys)


def _block_diag(w):
    depth, h, n, _ = w.shape
    eye = jnp.eye(h, dtype=w.dtype)
    return (eye[None, :, None, :, None] * w[:, :, :, None, :]).reshape(depth, h * n, h * n)


def _router_tables(w_rg, b_rg, w_re, b_re):
    depth = w_rg.shape[0]
    e0 = ROUTER_EXPERT_LANE0
    pad = lambda rows, width: jnp.zeros((depth, rows, width), F32)
    w = jnp.concatenate([jnp.swapaxes(w_rg, 1, 2), pad(e0 - N_GROUPS, D_MODEL), jnp.swapaxes(w_re, 1, 2),
                         pad(LANES - e0 - N_EXPERTS, D_MODEL)], axis=1)
    bias = jnp.concatenate([b_rg[:, :, None], pad(e0 - N_GROUPS, 1), b_re[:, :, None],
                            pad(LANES - e0 - N_EXPERTS, 1)], axis=1)
    hi = w.astype(BF16)
    lo = (w - hi.astype(F32)).astype(BF16)
    return hi, lo, bias


def kernel(x, mem, norm1_g, w_in, conv_w, conv_b, rg_wa, rg_ba, rg_wx, rg_bx, rg_lambda, dq_norm_g, dk_norm_g, lambda_q1, lambda_k1, lambda_q2, lambda_k2, diff_head_norm_g, mem_norm_g, w_mem_kv, mq_norm_g, mk_norm_g, w_out, norm2_g, w_router_group, b_router_group, w_router_expert, b_router_expert, w_expert_gate, w_expert_up, w_expert_down):
    b, s, d = x.shape
    m = mem.shape[1]
    depth = w_in.shape[0]
    gw = GROUP_WIDTH
    epg = EXPERTS_PER_GROUP
    x2 = x.reshape(b * s, d)
    mem2 = mem.reshape(b * m, d)
    row = lambda v: v.reshape(depth, 1, -1).astype(F32)
    col = lambda v: v.reshape(depth, -1, 1).astype(F32)
    lam_inits = [0.8 - 0.6 * math.exp(-0.3 * l) for l in range(depth)]
    lam = (jnp.exp(jnp.sum(lambda_q1 * lambda_k1, axis=-1)) - jnp.exp(jnp.sum(lambda_q2 * lambda_k2, axis=-1))
           + jnp.asarray(lam_inits, F32))
    lam = jnp.stack([lam, _score_bound(dq_norm_g, dk_norm_g)], axis=1).astype(F32)
    w_in_bf = w_in.astype(BF16)
    w_in_t = jnp.swapaxes(w_in_bf, 1, 2)
    qg_col = col(jnp.tile(dq_norm_g, (1, gw // DIFF_QK_DIM)))
    kg = row(jnp.tile(dk_norm_g, (1, 2)))
    wa_bd = _block_diag(rg_wa).astype(BF16)
    wx_bd = _block_diag(rg_wx).astype(BF16)
    w_kv_bf = w_mem_kv.astype(BF16)
    w_out_bf = w_out.astype(BF16)
    rt_hi, rt_lo, rb_col = _router_tables(w_router_group, b_router_group, w_router_expert, b_router_expert)
    earlier = jnp.triu(jnp.ones((MOE_TILE, MOE_TILE), BF16), k=1)
    wg = w_expert_gate.astype(BF16).reshape(depth * N_GROUPS, epg, D_MODEL, D_EXPERT)
    wu = w_expert_up.astype(BF16).reshape(depth * N_GROUPS, epg, D_MODEL, D_EXPERT)
    wd = w_expert_down.astype(BF16).reshape(depth * N_GROUPS, epg * D_EXPERT, D_MODEL)
    g1, g2, mqg, mkg = row(norm1_g), row(norm2_g), row(mq_norm_g), row(mk_norm_g)
    conv_b3, ba, bx, lru_lam, hg_col = row(conv_b), row(rg_ba), row(rg_bx), row(rg_lambda), col(diff_head_norm_g)
    mem_g = mem_norm_g.reshape(1, -1).astype(F32)
    moe = None
    for l in range(depth):
        mk, mv = _mem_kv(mem2, mem_g, l, w_kv_bf, mkg)
        proj = _in_proj(x2, b, s, l, g1, w_in_bf, w_in_t, qg_col, kg, mqg, mk, mv,
                        (conv_w, conv_b3, wa_bd, ba, wx_bd, bx, lru_lam), moe)
        if moe is not None:
            x2, *proj = proj
        y_lru, k, y_mem, qt, vt = proj
        y_diff = _diff_attn(qt, k, vt, lam, hg_col, b, s, l, lam_inits[l])
        x2, moe = _out_proj_moe(x2, y_lru, y_diff, y_mem, l, w_out_bf, g2, rt_hi, rt_lo, rb_col, earlier,
                                wg, wu, wd)
    return _moe_combine(moe[0], x2, moe[1], moe[2]).reshape(b, s, d)
```

```python
import functools
import math

import jax
import jax.numpy as jnp
from jax import lax
from jax.experimental import pallas as pl
from jax.experimental.pallas import tpu as pltpu

F32 = jnp.float32
BF16 = jnp.bfloat16

D_MODEL = 1024
GROUP_WIDTH = D_MODEL // 2
LRU_HEADS = 8
CONV_WIDTH = 4
LRU_C = 8.0
DIFF_HEADS = 4
DIFF_HEAD_DIM = GROUP_WIDTH // DIFF_HEADS
DIFF_QK_DIM = DIFF_HEAD_DIM // 2
MEM_HEADS = 4
MEM_HEAD_DIM = GROUP_WIDTH // MEM_HEADS
N_GROUPS = 4
EXPERTS_PER_GROUP = 8
N_EXPERTS = N_GROUPS * EXPERTS_PER_GROUP
D_EXPERT = 256
D_IN_PROJ = 6 * GROUP_WIDTH
EPS = 1e-6

LANES = 128
SUBLANES = 8
VMEM_LIMIT = 56 * 1024 * 1024
NEG_BIG = -1e30
ROUTER_EXPERT_LANE0 = 32


def _tile(n, pref):
    t = min(n, pref)
    assert n % t == 0, (n, t)
    return t


def _params(*sem):
    return pltpu.CompilerParams(dimension_semantics=sem, vmem_limit_bytes=VMEM_LIMIT)


def _lspec(l, tail, blk=None):
    idx = (l,) + tuple(blk if blk is not None else (0,) * len(tail))
    return pl.BlockSpec((None,) + tuple(tail), lambda *_: idx)


def _rms_rows(x, g):
    return x * lax.rsqrt(jnp.mean(x * x, axis=-1, keepdims=True) + EPS) * g


def _lane_iota(shape):
    return lax.broadcasted_iota(jnp.int32, shape, len(shape) - 1)


def _half_head_norm(z, g):
    sq = z * z
    lo = _lane_iota(z.shape) < DIFF_QK_DIM
    s_all = jnp.sum(sq, axis=-1, keepdims=True)
    s_lo = jnp.sum(jnp.where(lo, sq, 0.0), axis=-1, keepdims=True)
    inv_lo = lax.rsqrt(s_lo * (1.0 / DIFF_QK_DIM) + EPS)
    inv_hi = lax.rsqrt((s_all - s_lo) * (1.0 / DIFF_QK_DIM) + EPS)
    return z * jnp.where(lo, inv_lo, inv_hi) * g


SEQ_TILE = 512
_NT = (((1,), (1,)), ((), ()))


N_PROJ_PARAMS = 18
N_PROJ_OUTS = 5


def _softplus(z):
    return jnp.maximum(z, 0.0) + jnp.log(1.0 + jnp.exp(-jnp.abs(z)))


def _in_proj_body(x, ns, g1_ref, wl_ref, wk_ref, wmq_ref, wqt_ref, wvt_ref, qg_ref, kg_ref, mqg_ref,
                  mk_ref, mv_ref, cw_ref, cb_ref, wa_ref, ba_ref, wx_ref, bx_ref, llam_ref,
                  ylru_ref, k_ref, ym_ref, qt_ref, vt_ref, ext_ref, hcar_ref, a_ref, b_ref, h_ref):
    gw = GROUP_WIDTH
    tm = x.shape[0]
    hist = SUBLANES
    first = lax.rem(pl.program_id(0), ns) == 0

    @pl.when(first)
    def _():
        ext_ref[0:hist, :] = jnp.zeros((hist, gw), F32)
        hcar_ref[...] = jnp.zeros_like(hcar_ref)

    @pl.when(jnp.logical_not(first))
    def _():
        ext_ref[0:hist, :] = ext_ref[tm:tm + hist, :]

    xn = _rms_rows(x, g1_ref[...]).astype(BF16)
    heads = range(MEM_HEADS)
    hsl = [slice(h * MEM_HEAD_DIM, (h + 1) * MEM_HEAD_DIM) for h in heads]
    mq = jnp.dot(xn, wmq_ref[...], preferred_element_type=F32)
    u = jnp.dot(xn, wl_ref[...], preferred_element_type=F32)
    ext_ref[hist:hist + tm, :] = u[:, 0:gw]
    xc = cb_ref[...] + cw_ref[CONV_WIDTH - 1:CONV_WIDTH, :] * u[:, 0:gw]
    for j in range(CONV_WIDTH - 1):
        off = hist - (CONV_WIDTH - 1) + j
        xc = xc + cw_ref[j:j + 1, :] * ext_ref[off:off + tm, :]
    xcb = xc.astype(BF16)
    sc = [lax.dot_general(_rms_rows(mq[:, hsl[h]], mqg_ref[...]).astype(BF16), mk_ref[:, hsl[h]], _NT,
                          preferred_element_type=F32) * MEM_HEAD_DIM ** -0.5 for h in heads]
    k = jnp.dot(xn, wk_ref[...], preferred_element_type=F32)
    r_pre = jnp.dot(xcb, wa_ref[...], preferred_element_type=F32)
    i_pre = jnp.dot(xcb, wx_ref[...], preferred_element_type=F32)
    p = [jnp.exp(sc[h] - jnp.max(sc[h], axis=-1, keepdims=True)) for h in heads]
    qt = lax.dot_general(wqt_ref[...], xn, _NT, preferred_element_type=F32)
    o = [jnp.dot(p[h].astype(BF16), mv_ref[:, hsl[h]], preferred_element_type=F32) for h in heads]
    vt_ref[...] = lax.dot_general(wvt_ref[...], xn, _NT, preferred_element_type=F32).astype(BF16)

    r = jax.nn.sigmoid(r_pre + ba_ref[...])
    gate_i = jax.nn.sigmoid(i_pre + bx_ref[...])
    a = jnp.exp((-LRU_C * r) * _softplus(-llam_ref[...]))
    om = 1.0 - a * a
    b = om * lax.rsqrt(jnp.maximum(om, 1e-30)) * (gate_i * xc)
    a = a.reshape(tm // SUBLANES, SUBLANES, gw)
    b = b.reshape(tm // SUBLANES, SUBLANES, gw)
    row = lax.broadcasted_iota(jnp.int32, a.shape, 1)
    d = 1
    while d < SUBLANES:
        keep = row >= d
        a_prev = pltpu.roll(a, d, axis=1)
        b_prev = pltpu.roll(b, d, axis=1)
        b = jnp.where(keep, a * b_prev + b, b)
        a = jnp.where(keep, a * a_prev, a)
        d *= 2
    a_ref[...] = a.reshape(tm, gw)
    b_ref[...] = b.reshape(tm, gw)
    gated = jax.nn.gelu(u[:, gw:2 * gw])

    for h in range(DIFF_HEADS):
        sl = slice(h * LANES, (h + 1) * LANES)
        k_ref[:, sl] = _half_head_norm(k[:, sl], kg_ref[...]).astype(BF16)
    for h in heads:
        ym_ref[:, hsl[h]] = (o[h] / jnp.sum(p[h], axis=-1, keepdims=True)).astype(BF16)
    q3 = qt.reshape(gw // DIFF_QK_DIM, DIFF_QK_DIM, tm)
    q3 = q3 * lax.rsqrt(jnp.mean(q3 * q3, axis=1, keepdims=True) + EPS)
    qscale = DIFF_QK_DIM ** -0.5 * math.log2(math.e)
    qt_ref[...] = (q3.reshape(gw, tm) * (qg_ref[...] * qscale)).astype(BF16)

    def block(i, hprev):
        off = pl.multiple_of(i * SUBLANES, SUBLANES)
        hb = a_ref[pl.ds(off, SUBLANES), :] * hprev + b_ref[pl.ds(off, SUBLANES), :]
        h_ref[pl.ds(off, SUBLANES), :] = hb
        return jnp.broadcast_to(hb[SUBLANES - 1:SUBLANES, :], (SUBLANES, gw))

    hcar_ref[...] = lax.fori_loop(0, tm // SUBLANES, block, hcar_ref[...], unroll=8)
    ylru_ref[...] = (h_ref[...] * gated).astype(BF16)


def _in_proj_kernel(x_ref, *refs, ns):
    _in_proj_body(x_ref[...], ns, *refs)


def _unsort(info, ys_refs):
    kpad = -SORTED_ROWS % LANES
    ys = jnp.concatenate([r[...] for r in ys_refs] + [jnp.zeros((kpad, D_MODEL), BF16)], axis=0)
    pos = info[:, INFO_POS_LANE:INFO_POS_LANE + 1]
    unperm = jnp.where(pos == _lane_iota((MOE_TILE, SORTED_ROWS + kpad)).astype(F32), 1.0, 0.0).astype(BF16)
    return jnp.dot(unperm, ys, preferred_element_type=F32)


def _combine_in_proj_kernel(inv_ref, x1_ref, info_ref, *refs, ns):
    del inv_ref
    nsub = x1_ref.shape[0] // MOE_TILE
    ys_refs, rest = refs[:nsub * TILE_SLOTS], refs[nsub * TILE_SLOTS:]
    params, x_out_ref, outs = rest[:N_PROJ_PARAMS], rest[N_PROJ_PARAMS], rest[N_PROJ_PARAMS + 1:]
    x = jnp.concatenate(
        [x1_ref[u * MOE_TILE:(u + 1) * MOE_TILE, :]
         + _unsort(info_ref[u * MOE_TILE:(u + 1) * MOE_TILE, :], ys_refs[u * TILE_SLOTS:(u + 1) * TILE_SLOTS])
         for u in range(nsub)], axis=0)
    x_out_ref[...] = x
    _in_proj_body(x, ns, *params, *outs)


def _slot_map(q, sub, i, inv):
    return (inv[(i * sub[0] + sub[1]) * TILE_SLOTS + q], 0)


def _in_proj(x2, b, s, l, g1, w_in, w_in_t, qg_col, kg, mqg, mk, mv, lru, moe=None):
    t = x2.shape[0]
    tm = SEQ_TILE
    assert s % tm == 0
    ns = s // tm
    m = mk.shape[0] // b
    gw = GROUP_WIDTH
    row = lambda i, *_: (i, 0)
    fm = lambda i, *_: (i // ns, i % ns, 0, 0)
    mem = pl.BlockSpec((m, gw), lambda i, *_: (i // ns, 0))
    in_specs = [_lspec(l, (1, D_MODEL)),
                _lspec(l, (D_MODEL, 2 * gw), (0, 0)), _lspec(l, (D_MODEL, gw), (0, 3)),
                _lspec(l, (D_MODEL, gw), (0, 5)), _lspec(l, (gw, D_MODEL), (2, 0)),
                _lspec(l, (gw, D_MODEL), (4, 0)), _lspec(l, (gw, 1)),
                _lspec(l, (1, LANES)), _lspec(l, (1, MEM_HEAD_DIM)), mem, mem]
    vec = _lspec(l, (1, gw))
    in_specs += [_lspec(l, (CONV_WIDTH, gw)), vec, _lspec(l, (gw, gw)), vec, _lspec(l, (gw, gw)), vec, vec]
    operands = (g1, w_in, w_in, w_in, w_in_t, w_in_t, qg_col, kg, mqg, mk, mv) + tuple(lru)
    assert len(operands) == N_PROJ_PARAMS
    out_specs = [pl.BlockSpec((tm, gw), row), pl.BlockSpec((tm, gw), row), pl.BlockSpec((tm, gw), row),
                 pl.BlockSpec((None, None, gw, tm), fm), pl.BlockSpec((None, None, gw, tm), fm)]
    out_shape = [jax.ShapeDtypeStruct((t, gw), BF16), jax.ShapeDtypeStruct((t, gw), BF16),
                 jax.ShapeDtypeStruct((t, gw), BF16), jax.ShapeDtypeStruct((b, ns, gw, tm), BF16),
                 jax.ShapeDtypeStruct((b, ns, gw, tm), BF16)]
    assert len(out_specs) == N_PROJ_OUTS
    scratch = [pltpu.VMEM((tm + SUBLANES, gw), F32), pltpu.VMEM((SUBLANES, gw), F32),
               pltpu.VMEM((tm, gw), F32), pltpu.VMEM((tm, gw), F32), pltpu.VMEM((tm, gw), F32)]
    xspec = pl.BlockSpec((tm, D_MODEL), row)
    if moe is None:
        return pl.pallas_call(
            functools.partial(_in_proj_kernel, ns=ns), grid=(t // tm,), in_specs=[xspec] + in_specs,
            out_specs=out_specs, out_shape=out_shape, scratch_shapes=scratch,
            compiler_params=_params("arbitrary"), name="in_proj",
        )(x2, *operands)
    inv, info, ys = moe
    nsub = tm // MOE_TILE
    gran = [pl.BlockSpec((GRANULE, D_MODEL), functools.partial(_slot_map, q, (nsub, u)))
            for u in range(nsub) for q in range(TILE_SLOTS)]
    return pl.pallas_call(
        functools.partial(_combine_in_proj_kernel, ns=ns),
        grid_spec=pltpu.PrefetchScalarGridSpec(
            num_scalar_prefetch=1, grid=(t // tm,),
            in_specs=[xspec, pl.BlockSpec((tm, LANES), row)] + gran + in_specs,
            out_specs=[xspec] + out_specs, scratch_shapes=scratch),
        out_shape=[jax.ShapeDtypeStruct((t, D_MODEL), F32)] + out_shape,
        compiler_params=_params("arbitrary"),
        name="combine_in_proj",
    )(inv, x2, info, *([ys] * (nsub * TILE_SLOTS)), *operands)


ATTN_HEADS_PER_STEP = 2
ATTN_MIN_SUM = 2.0 ** -40


def _score_bound(dq_gain, dk_gain):
    qscale = DIFF_QK_DIM ** -0.5 * math.log2(math.e)
    return (1.02 * qscale * DIFF_QK_DIM) * jnp.max(jnp.abs(dq_gain), axis=-1) * jnp.max(jnp.abs(dk_gain), axis=-1)


def _diff_attn_kernel(lam_ref, qt_ref, k_ref, vt_ref, hg_ref, o_ref,
                      qs_ref, sa_ref, sb_ref, m_ref, l_ref, acc_ref, *, t, layer, out_scale):
    qi = pl.program_id(2)
    hd = DIFF_HEAD_DIM
    heads = range(ATTN_HEADS_PER_STEP)
    lo = lax.broadcasted_iota(jnp.int32, (hd, t), 0) < DIFF_QK_DIM
    for h in heads:
        q = qt_ref[h * hd:(h + 1) * hd, :]
        zero = jnp.zeros_like(q)
        qs_ref[h, :, 0:t] = jnp.where(lo, q, zero)
        qs_ref[h, :, t:2 * t] = jnp.where(lo, zero, q)

    def causal(x, fill):
        kpos = lax.broadcasted_iota(jnp.int32, (t, 2 * t), 0)
        c = lax.broadcasted_iota(jnp.int32, (t, 2 * t), 1)
        return jnp.where(kpos <= jnp.where(c >= t, c - t, c), x, fill)

    def score(j, h):
        off = pl.multiple_of(j * t, t)
        return jnp.dot(k_ref[pl.ds(off, t), h * hd:(h + 1) * hd], qs_ref[h],
                       preferred_element_type=F32)

    bound = lam_ref[layer, 1]
    l_ref[...] = jnp.zeros(l_ref.shape, F32)
    acc_ref[...] = jnp.zeros(acc_ref.shape, F32)

    def fast_update(j, masked):
        for h in heads:
            p = jnp.exp2(score(j, h) - bound)
            if masked:
                p = causal(p, 0.0)
            l_ref[h] += jnp.sum(p, axis=0, keepdims=True)
            acc_ref[h] += jnp.dot(vt_ref[j, h * hd:(h + 1) * hd, :], p.astype(BF16),
                                  preferred_element_type=F32)

    def fast_pair(i, carry):
        fast_update(2 * i, False)
        fast_update(2 * i + 1, False)
        return carry

    lax.fori_loop(0, qi >> 1, fast_pair, 0)

    @pl.when((qi & 1) == 1)
    def _():
        fast_update(qi - 1, False)

    fast_update(qi, True)

    @pl.when(jnp.min(l_ref[...]) < ATTN_MIN_SUM)
    def _():
        m_ref[...] = jnp.full(m_ref.shape, NEG_BIG, F32)
        l_ref[...] = jnp.zeros(l_ref.shape, F32)
        acc_ref[...] = jnp.zeros(acc_ref.shape, F32)

        def scores(j, dst_ref):
            for h in heads:
                dst_ref[h] = score(j, h)

        def update(j, src_ref, masked):
            for h in heads:
                s = src_ref[h]
                if masked:
                    s = causal(s, NEG_BIG)
                m = m_ref[h]
                m_new = jnp.maximum(m, jnp.max(s, axis=0, keepdims=True))
                alpha = jnp.exp2(m - m_new)
                p = jnp.exp2(s - m_new)
                m_ref[h] = m_new
                l_ref[h] = alpha * l_ref[h] + jnp.sum(p, axis=0, keepdims=True)
                vt = vt_ref[j, h * hd:(h + 1) * hd, :]
                acc_ref[h] = alpha * acc_ref[h] + jnp.dot(vt, p.astype(BF16), preferred_element_type=F32)

        scores(0, sa_ref)

        def pair(i, carry):
            scores(2 * i + 1, sb_ref)
            update(2 * i, sa_ref, False)
            scores(2 * i + 2, sa_ref)
            update(2 * i + 1, sb_ref, False)
            return carry

        lax.fori_loop(0, qi >> 1, pair, 0)

        @pl.when((qi & 1) == 1)
        def _():
            scores(qi, sb_ref)
            update(qi - 1, sa_ref, False)
            update(qi, sb_ref, True)

        @pl.when((qi & 1) == 0)
        def _():
            update(qi, sa_ref, True)

    for h in heads:
        o = acc_ref[h] / l_ref[h]
        o = o[:, 0:t] - lam_ref[layer, 0] * o[:, t:2 * t]
        o = o * lax.rsqrt(jnp.mean(o * o, axis=0, keepdims=True) + EPS) * (hg_ref[...] * out_scale)
        o_ref[:, h * hd:(h + 1) * hd] = o.T.astype(BF16)


def _diff_attn(qt, k, vt, lam, hg_col, b, s, l, lam_init):
    t = SEQ_TILE
    nq = s // t
    hp = ATTN_HEADS_PER_STEP
    w = hp * DIFF_HEAD_DIM
    ng = DIFF_HEADS // hp
    kern = functools.partial(_diff_attn_kernel, t=t, layer=l, out_scale=1.0 - lam_init)
    return pl.pallas_call(
        kern,
        grid=(b, ng, nq),
        in_specs=[pl.BlockSpec(memory_space=pltpu.SMEM),
                  pl.BlockSpec((None, None, w, t), lambda bi, g, qi: (bi, qi, g, 0)),
                  pl.BlockSpec((s, w), lambda bi, g, qi: (bi, g)),
                  pl.BlockSpec((None, nq, w, t), lambda bi, g, qi: (bi, 0, g, 0)),
                  _lspec(l, (DIFF_HEAD_DIM, 1))],
        out_specs=pl.BlockSpec((t, w), lambda bi, g, qi: (bi * nq + qi, g)),
        out_shape=jax.ShapeDtypeStruct((b * s, GROUP_WIDTH), BF16),
        scratch_shapes=[pltpu.VMEM((hp, DIFF_HEAD_DIM, 2 * t), BF16),
                        pltpu.VMEM((hp, t, 2 * t), F32), pltpu.VMEM((hp, t, 2 * t), F32),
                        pltpu.VMEM((hp, 1, 2 * t), F32), pltpu.VMEM((hp, 1, 2 * t), F32),
                        pltpu.VMEM((hp, DIFF_HEAD_DIM, 2 * t), F32)],
        compiler_params=_params("arbitrary", "arbitrary", "arbitrary"),
        name="diff_attn",
    )(lam, qt, k, vt, hg_col)


def _mem_kv_kernel(mem_ref, mg_ref, w_ref, kg_ref, mk_ref, mv_ref):
    memn = _rms_rows(mem_ref[...], mg_ref[...]).astype(BF16)
    gw = GROUP_WIDTH
    k = jnp.dot(memn, w_ref[:, 0:gw], preferred_element_type=F32)
    for h in range(MEM_HEADS):
        sl = slice(h * MEM_HEAD_DIM, (h + 1) * MEM_HEAD_DIM)
        mk_ref[:, sl] = _rms_rows(k[:, sl], kg_ref[...]).astype(BF16)
    mv_ref[...] = jnp.dot(memn, w_ref[:, gw:2 * gw], preferred_element_type=F32).astype(BF16)


def _mem_kv(mem2, mg, l, w_kv, kg):
    r = mem2.shape[0]
    tm = _tile(r, 512)
    row = lambda i: (i, 0)
    gw = GROUP_WIDTH
    return pl.pallas_call(
        _mem_kv_kernel,
        grid=(r // tm,),
        in_specs=[pl.BlockSpec((tm, D_MODEL), row), pl.BlockSpec((1, D_MODEL), lambda i: (0, 0)),
                  _lspec(l, (D_MODEL, 2 * gw)), _lspec(l, (1, MEM_HEAD_DIM))],
        out_specs=[pl.BlockSpec((tm, gw), row), pl.BlockSpec((tm, gw), row)],
        out_shape=[jax.ShapeDtypeStruct((r, gw), BF16), jax.ShapeDtypeStruct((r, gw), BF16)],
        compiler_params=_params("arbitrary"),
        name="mem_kv",
    )(mem2, mg, w_kv, kg)


MOE_TILE = 256
GRANULE = 16
TILE_SLOTS = MOE_TILE // GRANULE + N_GROUPS
SORTED_ROWS = TILE_SLOTS * GRANULE
STEP_GRANULES = 32
STEP_ROWS = STEP_GRANULES * GRANULE
XS_WIDTH = D_MODEL + LANES
INFO_POS_LANE = 0


def _route_t(lt):
    row_i = lax.broadcasted_iota(jnp.int32, lt.shape, 0)
    row = row_i.astype(F32)
    e0 = ROUTER_EXPERT_LANE0
    lg = jnp.where(row_i < N_GROUPS, lt, NEG_BIG)
    mg = jnp.max(lg, axis=0, keepdims=True)
    g_gate = 1.0 / jnp.sum(jnp.exp(lg - mg), axis=0, keepdims=True)
    g_idx = jnp.min(jnp.where(lg == mg, row, float(LANES)), axis=0, keepdims=True)
    row_group = ((row_i - e0) >> 3).astype(F32)
    sel = (row_i >= e0) & (row_i < e0 + N_EXPERTS) & (row_group == g_idx)
    le = jnp.where(sel, lt, NEG_BIG)
    m1 = jnp.max(le, axis=0, keepdims=True)
    se = jnp.sum(jnp.where(sel, jnp.exp(le - m1), 0.0), axis=0, keepdims=True)
    i1 = jnp.min(jnp.where(sel & (le == m1), row, float(LANES)), axis=0, keepdims=True)
    le2 = jnp.where(row == i1, NEG_BIG, le)
    m2 = jnp.max(le2, axis=0, keepdims=True)
    i2 = jnp.min(jnp.where(sel & (le2 == m2) & (row != i1), row, float(LANES)), axis=0, keepdims=True)
    p1 = 1.0 / se
    p2 = jnp.exp(m2 - m1) / se
    tot = p1 + p2
    w = jnp.where(row == i1, p1 / tot, jnp.where(row == i2, p2 / tot, 0.0))
    return g_gate * w, g_idx


def _split3(c):
    hi = c.astype(BF16).astype(F32)
    r1 = c - hi
    mid = r1.astype(BF16).astype(F32)
    lo = (r1 - mid).astype(BF16).astype(F32)
    return hi, mid, lo


def _sort_logits(x1, g2, rt_hi, rt_lo, rb_col):
    xn = _rms_rows(x1, g2)
    hi = xn.astype(BF16)
    lo = (xn - hi.astype(F32)).astype(BF16)
    logits_t = (lax.dot_general(rt_hi, hi, _NT, preferred_element_type=F32)
                + lax.dot_general(rt_hi, lo, _NT, preferred_element_type=F32)
                + lax.dot_general(rt_lo, hi, _NT, preferred_element_type=F32) + rb_col)
    return hi, logits_t


def _sort_rank(logits_t, earlier):
    tl = logits_t.shape[1]
    comb_t, g_idx = _route_t(logits_t)
    grow = lax.broadcasted_iota(jnp.int32, (SUBLANES, tl), 0).astype(F32)
    gt = jnp.where(grow == g_idx, 1.0, 0.0)
    before = jnp.dot(gt.astype(BF16), earlier, preferred_element_type=F32)
    return comb_t, gt, before


def _sort_emit(hi, comb_t, gt, before):
    tl = hi.shape[0]
    rank = jnp.sum(gt * before, axis=0, keepdims=True)
    cnt = jnp.sum(gt, axis=1, keepdims=True)
    glen = jnp.floor((cnt + (GRANULE - 1)) * (1.0 / GRANULE))
    r8 = lax.broadcasted_iota(jnp.int32, (SUBLANES, 1), 0)
    start = jnp.zeros((SUBLANES, 1), F32)
    for g in range(1, N_GROUPS):
        start = jnp.where(r8 == g, jnp.sum(jnp.where(r8 < g, glen, 0.0), axis=0, keepdims=True), start)
    pos = jnp.sum(gt * (start * GRANULE), axis=0, keepdims=True) + rank

    rows = lax.broadcasted_iota(jnp.int32, (LANES, tl), 0)
    info = jnp.where(rows == INFO_POS_LANE, pos, comb_t).T
    lane = _lane_iota(info.shape)
    e0 = ROUTER_EXPERT_LANE0
    c_hi, c_mid, c_lo = _split3(jnp.where((lane >= e0) & (lane < e0 + N_EXPERTS), info, 0.0))
    aug = (c_hi + pltpu.roll(c_mid, N_EXPERTS, axis=1) + pltpu.roll(c_lo, 2 * N_EXPERTS, axis=1)).astype(BF16)
    perm = jnp.where(pos == lax.broadcasted_iota(jnp.int32, (SORTED_ROWS, tl), 0).astype(F32),
                     1.0, 0.0).astype(BF16)
    xs = jnp.dot(perm, hi, preferred_element_type=F32).astype(BF16)
    xs_aug = jnp.dot(perm, aug, preferred_element_type=F32).astype(BF16)
    return xs, xs_aug, info, glen


def _out_sort_kernel(x_ref, yl_ref, yd_ref, ym_ref, w_ref, g2_ref, rthi_ref, rtlo_ref, rb_ref, earlier_ref,
                     o_ref, xs_ref, info_ref, lens_ref, x1s_ref):
    @pl.when(pl.program_id(0) == 0)
    def _():
        x1s_ref[...] = jnp.zeros_like(x1s_ref)

    tm = x1s_ref.shape[0]
    subs = range(tm // MOE_TILE)

    def project(c, nchunks):
        r = slice(c * (tm // nchunks), (c + 1) * (tm // nchunks))
        y = jnp.concatenate([yl_ref[r, :], yd_ref[r, :], ym_ref[r, :]], axis=-1)
        x1 = x_ref[r, :] + jnp.dot(y, w_ref[...], preferred_element_type=F32)
        o_ref[r, :] = x1
        return r, x1

    fresh = [project(0, 2)]
    s1 = [_sort_logits(x1s_ref[sub * MOE_TILE:(sub + 1) * MOE_TILE, :], g2_ref[...], rthi_ref[...],
                       rtlo_ref[...], rb_ref[...]) for sub in subs]
    for r, x1 in fresh:
        x1s_ref[r, :] = x1
    fresh = [project(1, 2)]
    s2 = [_sort_rank(s1[sub][1], earlier_ref[...]) for sub in subs]
    for r, x1 in fresh:
        x1s_ref[r, :] = x1
    for sub in subs:
        xs, xs_aug, info, glen = _sort_emit(s1[sub][0], *s2[sub])
        r0 = sub * SORTED_ROWS
        xs_ref[r0:r0 + SORTED_ROWS, 0:D_MODEL] = xs
        xs_ref[r0:r0 + SORTED_ROWS, D_MODEL:XS_WIDTH] = xs_aug
        info_ref[sub * MOE_TILE:(sub + 1) * MOE_TILE, :] = info
        lens_ref[sub] = jnp.broadcast_to(glen, (SUBLANES, LANES))


def _out_sort(x2, y_lru, y_diff, y_mem, l, w_out, g2, rt_hi, rt_lo, rb_col, earlier):
    t = x2.shape[0]
    tm = SEQ_TILE
    assert t % tm == 0 and tm % MOE_TILE == 0
    sub = tm // MOE_TILE
    nt = t // MOE_TILE
    n = t // tm
    cur = lambda i: (jnp.minimum(i, n - 1), 0)
    prev = lambda i: (jnp.maximum(i - 1, 0), 0)
    gw = GROUP_WIDTH
    return pl.pallas_call(
        _out_sort_kernel,
        grid=(n + 1,),
        in_specs=[pl.BlockSpec((tm, D_MODEL), cur), pl.BlockSpec((tm, gw), cur), pl.BlockSpec((tm, gw), cur),
                  pl.BlockSpec((tm, gw), cur), _lspec(l, (3 * gw, D_MODEL)),
                  _lspec(l, (1, D_MODEL)), _lspec(l, (LANES, D_MODEL)),
                  _lspec(l, (LANES, D_MODEL)), _lspec(l, (LANES, 1)),
                  pl.BlockSpec((MOE_TILE, MOE_TILE), lambda i: (0, 0))],
        out_specs=[pl.BlockSpec((tm, D_MODEL), cur), pl.BlockSpec((sub * SORTED_ROWS, XS_WIDTH), prev),
                   pl.BlockSpec((tm, LANES), prev),
                   pl.BlockSpec((sub, SUBLANES, LANES), lambda i: (jnp.maximum(i - 1, 0), 0, 0))],
        out_shape=[jax.ShapeDtypeStruct((t, D_MODEL), F32),
                   jax.ShapeDtypeStruct((nt * SORTED_ROWS, XS_WIDTH), BF16),
                   jax.ShapeDtypeStruct((t, LANES), F32),
                   jax.ShapeDtypeStruct((nt, SUBLANES, LANES), F32)],
        scratch_shapes=[pltpu.VMEM((tm, D_MODEL), F32)],
        compiler_params=_params("arbitrary"),
        name="out_proj_sort",
    )(x2, y_lru, y_diff, y_mem, w_out, g2, rt_hi, rt_lo, rb_col, earlier)


def _moe_expert_kernel(sg_ref, sv_ref, gi_ref, *refs):
    del gi_ref
    xs_refs = refs[:STEP_GRANULES]
    wg_ref, wu_ref, wd_ref, y_ref = refs[STEP_GRANULES:]
    s = pl.program_id(0)

    @pl.when(sv_ref[s] == 0)
    def _():
        y_ref[...] = jnp.zeros_like(y_ref)

    @pl.when(sv_ref[s] != 0)
    def _():
        rows = jnp.concatenate([r[...] for r in xs_refs], axis=0)
        x = rows[:, 0:D_MODEL]
        aug = rows[:, D_MODEL:XS_WIDTH].astype(F32)
        lane = _lane_iota(aug.shape)
        e0 = ROUTER_EXPERT_LANE0
        comb = jnp.where((lane >= e0) & (lane < e0 + N_EXPERTS),
                         aug + pltpu.roll(aug, LANES - N_EXPERTS, axis=1)
                         + pltpu.roll(aug, LANES - 2 * N_EXPERTS, axis=1), 0.0)
        first = e0 + sg_ref[s] * EXPERTS_PER_GROUP
        hs = []
        for e in range(EXPERTS_PER_GROUP):
            cw = jnp.sum(jnp.where(lane == first + e, comb, 0.0), axis=-1, keepdims=True)
            hg = jnp.dot(x, wg_ref[e], preferred_element_type=F32)
            hu = jnp.dot(x, wu_ref[e], preferred_element_type=F32)
            hs.append((jax.nn.silu(hg) * hu * cw).astype(BF16))
        h = jnp.concatenate(hs, axis=-1)
        y_ref[...] = jnp.dot(h, wd_ref[...], preferred_element_type=F32).astype(BF16)


def _granule_map(k, s, sg, sv, gi):
    return (gi[s * STEP_GRANULES + k], 0)


def _moe_experts(step_group, step_valid, gran_idx, xs, l, wg, wu, wd):
    nstep = step_group.shape[0]
    epg = EXPERTS_PER_GROUP
    wmap4 = lambda s, sg, sv, gi: (l * N_GROUPS + sg[s], 0, 0, 0)
    in_specs = [pl.BlockSpec((GRANULE, XS_WIDTH), functools.partial(_granule_map, k)) for k in range(STEP_GRANULES)]
    in_specs += [pl.BlockSpec((None, epg, D_MODEL, D_EXPERT), wmap4),
                 pl.BlockSpec((None, epg, D_MODEL, D_EXPERT), wmap4),
                 pl.BlockSpec((None, epg * D_EXPERT, D_MODEL), lambda s, sg, sv, gi: (l * N_GROUPS + sg[s], 0, 0))]
    return pl.pallas_call(
        _moe_expert_kernel,
        grid_spec=pltpu.PrefetchScalarGridSpec(
            num_scalar_prefetch=3, grid=(nstep,), in_specs=in_specs,
            out_specs=pl.BlockSpec((STEP_ROWS, D_MODEL), lambda s, sg, sv, gi: (s, 0))),
        out_shape=jax.ShapeDtypeStruct((nstep * STEP_ROWS, D_MODEL), BF16),
        compiler_params=_params("arbitrary"),
        name="moe_experts",
    )(step_group, step_valid, gran_idx, *([xs] * STEP_GRANULES), wg, wu, wd)


def _moe_combine_kernel(inv_ref, x_ref, info_ref, *refs):
    del inv_ref
    ys_refs = refs[:TILE_SLOTS]
    o_ref = refs[TILE_SLOTS]
    o_ref[...] = x_ref[...] + _unsort(info_ref[...], ys_refs)


def _moe_combine(inv, x2, info, ys):
    t = x2.shape[0]
    nt = t // MOE_TILE
    row = lambda i, inv: (i, 0)
    in_specs = [pl.BlockSpec((MOE_TILE, D_MODEL), row), pl.BlockSpec((MOE_TILE, LANES), row)]
    in_specs += [pl.BlockSpec((GRANULE, D_MODEL), functools.partial(_slot_map, q, (1, 0)))
                 for q in range(TILE_SLOTS)]
    return pl.pallas_call(
        _moe_combine_kernel,
        grid_spec=pltpu.PrefetchScalarGridSpec(
            num_scalar_prefetch=1, grid=(nt,), in_specs=in_specs,
            out_specs=pl.BlockSpec((MOE_TILE, D_MODEL), row)),
        out_shape=jax.ShapeDtypeStruct((t, D_MODEL), F32),
        compiler_params=_params("arbitrary"),
        name="moe_combine",
    )(inv, x2, info, *([ys] * TILE_SLOTS))


def _moe_tables(lens, nt):
    i32 = jnp.int32
    ng = N_GROUPS
    garange = jnp.arange(ng, dtype=i32)
    cum = jnp.cumsum(lens, axis=1)
    start = cum - lens
    base = jnp.arange(nt, dtype=i32)[:, None] * TILE_SLOTS + start
    run_len = lens.T.reshape(-1)
    run_base = base.T.reshape(-1)
    run_end = jnp.cumsum(run_len)
    run_start = run_end - run_len
    n_g = jnp.sum(lens, axis=0)
    steps_g = (n_g + STEP_GRANULES - 1) // STEP_GRANULES
    step_end = jnp.cumsum(steps_g)
    step_off = step_end - steps_g
    gran_off = jnp.cumsum(n_g) - n_g
    nstep = (nt * (TILE_SLOTS - 1) + STEP_GRANULES - 1) // STEP_GRANULES + ng
    s = jnp.arange(nstep, dtype=i32)
    sg = jnp.minimum(jnp.sum((s[:, None] >= step_end[None, :]).astype(i32), axis=-1), ng - 1)
    goh = sg[:, None] == garange[None, :]
    pick = lambda v: jnp.sum(jnp.where(goh, v[None, :], 0), axis=-1)
    sv = s < step_end[-1]
    jl = (s - pick(step_off))[:, None] * STEP_GRANULES + jnp.arange(STEP_GRANULES, dtype=i32)[None, :]
    ok = sv[:, None] & (jl < pick(n_g)[:, None])
    j = jnp.where(ok, pick(gran_off)[:, None] + jl, 0)
    inrun = (j[..., None] >= run_start) & (j[..., None] < run_end)
    gran = j + jnp.sum(jnp.where(inrun, run_base - run_start, 0), axis=-1)
    gran = jnp.where(ok, gran, jnp.where(sv[:, None], gran[:, 0:1], 0)).astype(i32)
    q = jnp.arange(TILE_SLOTS, dtype=i32)[None, :]
    gq = jnp.minimum(jnp.sum((q[:, :, None] >= cum[:, None, :]).astype(i32), axis=-1), ng - 1)
    qoh = gq[..., None] == garange
    used = q < cum[:, -1:]
    per_g = run_start.reshape(ng, nt).T - start + (step_off * STEP_GRANULES - gran_off)[None, :]
    inv = jnp.where(used, q + jnp.sum(jnp.where(qoh, per_g[:, None, :], 0), axis=-1), 0).astype(i32)
    return sg, sv.astype(i32), gran.reshape(-1), inv.reshape(-1)


def _out_proj_moe(x2, y_lru, y_diff, y_mem, l, w_out, g2, rt_hi, rt_lo, rb_col, earlier, wg, wu, wd):
    nt = x2.shape[0] // MOE_TILE
    x1, xs, info, lens = _out_sort(x2, y_lru, y_diff, y_mem, l, w_out, g2, rt_hi, rt_lo, rb_col, earlier)
    sg, sv, gran, inv = _moe_tables(lens[:, 0:N_GROUPS, 0].astype(jnp.int32), nt)
    ys = _moe_experts(sg, sv, gran, xs, l, wg, wu, wd)
    return x1, (inv, info, ys)


def _block_diag(w):
    depth, h, n, _ = w.shape
    eye = jnp.eye(h, dtype=w.dtype)
    return (eye[None, :, None, :, None] * w[:, :, :, None, :]).reshape(depth, h * n, h * n)


def _router_tables(w_rg, b_rg, w_re, b_re):
    depth = w_rg.shape[0]
    e0 = ROUTER_EXPERT_LANE0
    pad = lambda rows, width: jnp.zeros((depth, rows, width), F32)
    w = jnp.concatenate([jnp.swapaxes(w_rg, 1, 2), pad(e0 - N_GROUPS, D_MODEL), jnp.swapaxes(w_re, 1, 2),
                         pad(LANES - e0 - N_EXPERTS, D_MODEL)], axis=1)
    bias = jnp.concatenate([b_rg[:, :, None], pad(e0 - N_GROUPS, 1), b_re[:, :, None],
                            pad(LANES - e0 - N_EXPERTS, 1)], axis=1)
    hi = w.astype(BF16)
    lo = (w - hi.astype(F32)).astype(BF16)
    return hi, lo, bias


def kernel(x, mem, norm1_g, w_in, conv_w, conv_b, rg_wa, rg_ba, rg_wx, rg_bx, rg_lambda, dq_norm_g, dk_norm_g, lambda_q1, lambda_k1, lambda_q2, lambda_k2, diff_head_norm_g, mem_norm_g, w_mem_kv, mq_norm_g, mk_norm_g, w_out, norm2_g, w_router_group, b_router_group, w_router_expert, b_router_expert, w_expert_gate, w_expert_up, w_expert_down):
    b, s, d = x.shape
    m = mem.shape[1]
    depth = w_in.shape[0]
    gw = GROUP_WIDTH
    epg = EXPERTS_PER_GROUP
    x2 = x.reshape(b * s, d)
    mem2 = mem.reshape(b * m, d)
    row = lambda v: v.reshape(depth, 1, -1).astype(F32)
    col = lambda v: v.reshape(depth, -1, 1).astype(F32)
    lam_inits = [0.8 - 0.6 * math.exp(-0.3 * l) for l in range(depth)]
    lam = (jnp.exp(jnp.sum(lambda_q1 * lambda_k1, axis=-1)) - jnp.exp(jnp.sum(lambda_q2 * lambda_k2, axis=-1))
           + jnp.asarray(lam_inits, F32))
    lam = jnp.stack([lam, _score_bound(dq_norm_g, dk_norm_g)], axis=1).astype(F32)
    w_in_bf = w_in.astype(BF16)
    w_in_t = jnp.swapaxes(w_in_bf, 1, 2)
    qg_col = col(jnp.tile(dq_norm_g, (1, gw // DIFF_QK_DIM)))
    kg = row(jnp.tile(dk_norm_g, (1, 2)))
    wa_bd = _block_diag(rg_wa).astype(BF16)
    wx_bd = _block_diag(rg_wx).astype(BF16)
    w_kv_bf = w_mem_kv.astype(BF16)
    w_out_bf = w_out.astype(BF16)
    rt_hi, rt_lo, rb_col = _router_tables(w_router_group, b_router_group, w_router_expert, b_router_expert)
    earlier = jnp.triu(jnp.ones((MOE_TILE, MOE_TILE), BF16), k=1)
    wg = w_expert_gate.astype(BF16).reshape(depth * N_GROUPS, epg, D_MODEL, D_EXPERT)
    wu = w_expert_up.astype(BF16).reshape(depth * N_GROUPS, epg, D_MODEL, D_EXPERT)
    wd = w_expert_down.astype(BF16).reshape(depth * N_GROUPS, epg * D_EXPERT, D_MODEL)
    g1, g2, mqg, mkg = row(norm1_g), row(norm2_g), row(mq_norm_g), row(mk_norm_g)
    conv_b3, ba, bx, lru_lam, hg_col = row(conv_b), row(rg_ba), row(rg_bx), row(rg_lambda), col(diff_head_norm_g)
    mem_g = mem_norm_g.reshape(1, -1).astype(F32)
    moe = None
    for l in range(depth):
        mk, mv = _mem_kv(mem2, mem_g, l, w_kv_bf, mkg)
        proj = _in_proj(x2, b, s, l, g1, w_in_bf, w_in_t, qg_col, kg, mqg, mk, mv,
                        (conv_w, conv_b3, wa_bd, ba, wx_bd, bx, lru_lam), moe)
        if moe is not None:
            x2, *proj = proj
        y_lru, k, y_mem, qt, vt = proj
        y_diff = _diff_attn(qt, k, vt, lam, hg_col, b, s, l, lam_inits[l])
        x2, moe = _out_proj_moe(x2, y_lru, y_diff, y_mem, l, w_out_bf, g2, rt_hi, rt_lo, rb_col, earlier,
                                wg, wu, wd)
    return _moe_combine(moe[0], x2, moe[1], moe[2]).reshape(b, s, d)
```

```python
import functools
import math

import jax
import jax.numpy as jnp
from jax import lax
from jax.experimental import pallas as pl
from jax.experimental.pallas import tpu as pltpu

F32 = jnp.float32
BF16 = jnp.bfloat16

D_MODEL = 1024
GROUP_WIDTH = D_MODEL // 2
LRU_HEADS = 8
CONV_WIDTH = 4
LRU_C = 8.0
DIFF_HEADS = 4
DIFF_HEAD_DIM = GROUP_WIDTH // DIFF_HEADS
DIFF_QK_DIM = DIFF_HEAD_DIM // 2
MEM_HEADS = 4
MEM_HEAD_DIM = GROUP_WIDTH // MEM_HEADS
N_GROUPS = 4
EXPERTS_PER_GROUP = 8
N_EXPERTS = N_GROUPS * EXPERTS_PER_GROUP
D_EXPERT = 256
D_IN_PROJ = 6 * GROUP_WIDTH
EPS = 1e-6

LANES = 128
SUBLANES = 8
VMEM_LIMIT = 56 * 1024 * 1024
NEG_BIG = -1e30
ROUTER_EXPERT_LANE0 = 32


def _tile(n, pref):
    t = min(n, pref)
    assert n % t == 0, (n, t)
    return t


def _params(*sem):
    return pltpu.CompilerParams(dimension_semantics=sem, vmem_limit_bytes=VMEM_LIMIT)


def _lspec(l, tail, blk=None):
    idx = (l,) + tuple(blk if blk is not None else (0,) * len(tail))
    return pl.BlockSpec((None,) + tuple(tail), lambda *_: idx)


def _rms_rows(x, g):
    return x * lax.rsqrt(jnp.mean(x * x, axis=-1, keepdims=True) + EPS) * g


def _lane_iota(shape):
    return lax.broadcasted_iota(jnp.int32, shape, len(shape) - 1)


def _half_head_norm(z, g):
    sq = z * z
    lo = _lane_iota(z.shape) < DIFF_QK_DIM
    s_all = jnp.sum(sq, axis=-1, keepdims=True)
    s_lo = jnp.sum(jnp.where(lo, sq, 0.0), axis=-1, keepdims=True)
    inv_lo = lax.rsqrt(s_lo * (1.0 / DIFF_QK_DIM) + EPS)
    inv_hi = lax.rsqrt((s_all - s_lo) * (1.0 / DIFF_QK_DIM) + EPS)
    return z * jnp.where(lo, inv_lo, inv_hi) * g


SEQ_TILE = 512
_NT = (((1,), (1,)), ((), ()))


N_PROJ_PARAMS = 18
N_PROJ_OUTS = 5


def _softplus(z):
    return jnp.maximum(z, 0.0) + jnp.log(1.0 + jnp.exp(-jnp.abs(z)))


def _in_proj_body(x, ns, g1_ref, wl_ref, wk_ref, wmq_ref, wqt_ref, wvt_ref, qg_ref, kg_ref, mqg_ref,
                  mk_ref, mv_ref, cw_ref, cb_ref, wa_ref, ba_ref, wx_ref, bx_ref, llam_ref,
                  ylru_ref, k_ref, ym_ref, qt_ref, vt_ref, ext_ref, hcar_ref, a_ref, b_ref, h_ref):
    gw = GROUP_WIDTH
    tm = x.shape[0]
    hist = SUBLANES
    first = lax.rem(pl.program_id(0), ns) == 0

    @pl.when(first)
    def _():
        ext_ref[0:hist, :] = jnp.zeros((hist, gw), F32)
        hcar_ref[...] = jnp.zeros_like(hcar_ref)

    @pl.when(jnp.logical_not(first))
    def _():
        ext_ref[0:hist, :] = ext_ref[tm:tm + hist, :]

    xn = _rms_rows(x, g1_ref[...]).astype(BF16)
    heads = range(MEM_HEADS)
    hsl = [slice(h * MEM_HEAD_DIM, (h + 1) * MEM_HEAD_DIM) for h in heads]
    mq = jnp.dot(xn, wmq_ref[...], preferred_element_type=F32)
    u = jnp.dot(xn, wl_ref[...], preferred_element_type=F32)
    ext_ref[hist:hist + tm, :] = u[:, 0:gw]
    xc = cb_ref[...] + cw_ref[CONV_WIDTH - 1:CONV_WIDTH, :] * u[:, 0:gw]
    for j in range(CONV_WIDTH - 1):
        off = hist - (CONV_WIDTH - 1) + j
        xc = xc + cw_ref[j:j + 1, :] * ext_ref[off:off + tm, :]
    xcb = xc.astype(BF16)
    sc = [lax.dot_general(_rms_rows(mq[:, hsl[h]], mqg_ref[...]).astype(BF16), mk_ref[:, hsl[h]], _NT,
                          preferred_element_type=F32) * MEM_HEAD_DIM ** -0.5 for h in heads]
    k = jnp.dot(xn, wk_ref[...], preferred_element_type=F32)
    r_pre = jnp.dot(xcb, wa_ref[...], preferred_element_type=F32)
    i_pre = jnp.dot(xcb, wx_ref[...], preferred_element_type=F32)
    p = [jnp.exp(sc[h] - jnp.max(sc[h], axis=-1, keepdims=True)) for h in heads]
    qt = lax.dot_general(wqt_ref[...], xn, _NT, preferred_element_type=F32)
    o = [jnp.dot(p[h].astype(BF16), mv_ref[:, hsl[h]], preferred_element_type=F32) for h in heads]
    vt_ref[...] = lax.dot_general(wvt_ref[...], xn, _NT, preferred_element_type=F32).astype(BF16)

    r = jax.nn.sigmoid(r_pre + ba_ref[...])
    gate_i = jax.nn.sigmoid(i_pre + bx_ref[...])
    a = jnp.exp((-LRU_C * r) * _softplus(-llam_ref[...]))
    om = 1.0 - a * a
    b = om * lax.rsqrt(jnp.maximum(om, 1e-30)) * (gate_i * xc)
    a = a.reshape(tm // SUBLANES, SUBLANES, gw)
    b = b.reshape(tm // SUBLANES, SUBLANES, gw)
    row = lax.broadcasted_iota(jnp.int32, a.shape, 1)
    d = 1
    while d < SUBLANES:
        keep = row >= d
        a_prev = pltpu.roll(a, d, axis=1)
        b_prev = pltpu.roll(b, d, axis=1)
        b = jnp.where(keep, a * b_prev + b, b)
        a = jnp.where(keep, a * a_prev, a)
        d *= 2
    a_ref[...] = a.reshape(tm, gw)
    b_ref[...] = b.reshape(tm, gw)
    gated = jax.nn.gelu(u[:, gw:2 * gw])

    for h in range(DIFF_HEADS):
        sl = slice(h * LANES, (h + 1) * LANES)
        k_ref[:, sl] = _half_head_norm(k[:, sl], kg_ref[...]).astype(BF16)
    for h in heads:
        ym_ref[:, hsl[h]] = (o[h] / jnp.sum(p[h], axis=-1, keepdims=True)).astype(BF16)
    q3 = qt.reshape(gw // DIFF_QK_DIM, DIFF_QK_DIM, tm)
    q3 = q3 * lax.rsqrt(jnp.mean(q3 * q3, axis=1, keepdims=True) + EPS)
    qscale = DIFF_QK_DIM ** -0.5 * math.log2(math.e)
    qt_ref[...] = (q3.reshape(gw, tm) * (qg_ref[...] * qscale)).astype(BF16)

    def block(i, hprev):
        off = pl.multiple_of(i * SUBLANES, SUBLANES)
        hb = a_ref[pl.ds(off, SUBLANES), :] * hprev + b_ref[pl.ds(off, SUBLANES), :]
        h_ref[pl.ds(off, SUBLANES), :] = hb
        return jnp.broadcast_to(hb[SUBLANES - 1:SUBLANES, :], (SUBLANES, gw))

    hcar_ref[...] = lax.fori_loop(0, tm // SUBLANES, block, hcar_ref[...], unroll=8)
    ylru_ref[...] = (h_ref[...] * gated).astype(BF16)


def _in_proj_kernel(x_ref, *refs, ns):
    _in_proj_body(x_ref[...], ns, *refs)


def _unsort(info, ys_refs):
    kpad = -SORTED_ROWS % LANES
    ys = jnp.concatenate([r[...] for r in ys_refs] + [jnp.zeros((kpad, D_MODEL), BF16)], axis=0)
    pos = info[:, INFO_POS_LANE:INFO_POS_LANE + 1]
    unperm = jnp.where(pos == _lane_iota((MOE_TILE, SORTED_ROWS + kpad)).astype(F32), 1.0, 0.0).astype(BF16)
    return jnp.dot(unperm, ys, preferred_element_type=F32)


def _combine_in_proj_kernel(inv_ref, x1_ref, info_ref, *refs, ns):
    del inv_ref
    nsub = x1_ref.shape[0] // MOE_TILE
    ys_refs, rest = refs[:nsub * TILE_SLOTS], refs[nsub * TILE_SLOTS:]
    params, x_out_ref, outs = rest[:N_PROJ_PARAMS], rest[N_PROJ_PARAMS], rest[N_PROJ_PARAMS + 1:]
    x = jnp.concatenate(
        [x1_ref[u * MOE_TILE:(u + 1) * MOE_TILE, :]
         + _unsort(info_ref[u * MOE_TILE:(u + 1) * MOE_TILE, :], ys_refs[u * TILE_SLOTS:(u + 1) * TILE_SLOTS])
         for u in range(nsub)], axis=0)
    x_out_ref[...] = x
    _in_proj_body(x, ns, *params, *outs)


def _slot_map(q, sub, i, inv):
    return (inv[(i * sub[0] + sub[1]) * TILE_SLOTS + q], 0)


def _in_proj(x2, b, s, l, g1, w_in, w_in_t, qg_col, kg, mqg, mk, mv, lru, moe=None):
    t = x2.shape[0]
    tm = SEQ_TILE
    assert s % tm == 0
    ns = s // tm
    m = mk.shape[0] // b
    gw = GROUP_WIDTH
    row = lambda i, *_: (i, 0)
    fm = lambda i, *_: (i // ns, i % ns, 0, 0)
    mem = pl.BlockSpec((m, gw), lambda i, *_: (i // ns, 0))
    in_specs = [_lspec(l, (1, D_MODEL)),
                _lspec(l, (D_MODEL, 2 * gw), (0, 0)), _lspec(l, (D_MODEL, gw), (0, 3)),
                _lspec(l, (D_MODEL, gw), (0, 5)), _lspec(l, (gw, D_MODEL), (2, 0)),
                _lspec(l, (gw, D_MODEL), (4, 0)), _lspec(l, (gw, 1)),
                _lspec(l, (1, LANES)), _lspec(l, (1, MEM_HEAD_DIM)), mem, mem]
    vec = _lspec(l, (1, gw))
    in_specs += [_lspec(l, (CONV_WIDTH, gw)), vec, _lspec(l, (gw, gw)), vec, _lspec(l, (gw, gw)), vec, vec]
    operands = (g1, w_in, w_in, w_in, w_in_t, w_in_t, qg_col, kg, mqg, mk, mv) + tuple(lru)
    assert len(operands) == N_PROJ_PARAMS
    out_specs = [pl.BlockSpec((tm, gw), row), pl.BlockSpec((tm, gw), row), pl.BlockSpec((tm, gw), row),
                 pl.BlockSpec((None, None, gw, tm), fm), pl.BlockSpec((None, None, gw, tm), fm)]
    out_shape = [jax.ShapeDtypeStruct((t, gw), BF16), jax.ShapeDtypeStruct((t, gw), BF16),
                 jax.ShapeDtypeStruct((t, gw), BF16), jax.ShapeDtypeStruct((b, ns, gw, tm), BF16),
                 jax.ShapeDtypeStruct((b, ns, gw, tm), BF16)]
    assert len(out_specs) == N_PROJ_OUTS
    scratch = [pltpu.VMEM((tm + SUBLANES, gw), F32), pltpu.VMEM((SUBLANES, gw), F32),
               pltpu.VMEM((tm, gw), F32), pltpu.VMEM((tm, gw), F32), pltpu.VMEM((tm, gw), F32)]
    xspec = pl.BlockSpec((tm, D_MODEL), row)
    if moe is None:
        return pl.pallas_call(
            functools.partial(_in_proj_kernel, ns=ns), grid=(t // tm,), in_specs=[xspec] + in_specs,
            out_specs=out_specs, out_shape=out_shape, scratch_shapes=scratch,
            compiler_params=_params("arbitrary"), name="in_proj",
        )(x2, *operands)
    inv, info, ys = moe
    nsub = tm // MOE_TILE
    gran = [pl.BlockSpec((GRANULE, D_MODEL), functools.partial(_slot_map, q, (nsub, u)))
            for u in range(nsub) for q in range(TILE_SLOTS)]
    return pl.pallas_call(
        functools.partial(_combine_in_proj_kernel, ns=ns),
        grid_spec=pltpu.PrefetchScalarGridSpec(
            num_scalar_prefetch=1, grid=(t // tm,),
            in_specs=[xspec, pl.BlockSpec((tm, LANES), row)] + gran + in_specs,
            out_specs=[xspec] + out_specs, scratch_shapes=scratch),
        out_shape=[jax.ShapeDtypeStruct((t, D_MODEL), F32)] + out_shape,
        compiler_params=_params("arbitrary"),
        name="combine_in_proj",
    )(inv, x2, info, *([ys] * (nsub * TILE_SLOTS)), *operands)


ATTN_HEADS_PER_STEP = 2
ATTN_MIN_SUM = 2.0 ** -40


def _score_bound(dq_gain, dk_gain):
    qscale = DIFF_QK_DIM ** -0.5 * math.log2(math.e)
    return (1.02 * qscale * DIFF_QK_DIM) * jnp.max(jnp.abs(dq_gain), axis=-1) * jnp.max(jnp.abs(dk_gain), axis=-1)


def _diff_attn_kernel(lam_ref, qt_ref, k_ref, vt_ref, hg_ref, o_ref,
                      qs_ref, sa_ref, sb_ref, m_ref, l_ref, acc_ref, *, t, layer, out_scale):
    qi = pl.program_id(2)
    hd = DIFF_HEAD_DIM
    heads = range(ATTN_HEADS_PER_STEP)
    lo = lax.broadcasted_iota(jnp.int32, (hd, t), 0) < DIFF_QK_DIM
    for h in heads:
        q = qt_ref[h * hd:(h + 1) * hd, :]
        zero = jnp.zeros_like(q)
        qs_ref[h, :, 0:t] = jnp.where(lo, q, zero)
        qs_ref[h, :, t:2 * t] = jnp.where(lo, zero, q)

    def causal(x, fill):
        kpos = lax.broadcasted_iota(jnp.int32, (t, 2 * t), 0)
        c = lax.broadcasted_iota(jnp.int32, (t, 2 * t), 1)
        return jnp.where(kpos <= jnp.where(c >= t, c - t, c), x, fill)

    def score(j, h):
        off = pl.multiple_of(j * t, t)
        return jnp.dot(k_ref[pl.ds(off, t), h * hd:(h + 1) * hd], qs_ref[h],
                       preferred_element_type=F32)

    bound = lam_ref[layer, 1]
    l_ref[...] = jnp.zeros(l_ref.shape, F32)
    acc_ref[...] = jnp.zeros(acc_ref.shape, F32)

    def fast_update(j, masked):
        for h in heads:
            p = jnp.exp2(score(j, h) - bound)
            if masked:
                p = causal(p, 0.0)
            l_ref[h] += jnp.sum(p, axis=0, keepdims=True)
            acc_ref[h] += jnp.dot(vt_ref[j, h * hd:(h + 1) * hd, :], p.astype(BF16),
                                  preferred_element_type=F32)

    def fast_pair(i, carry):
        fast_update(2 * i, False)
        fast_update(2 * i + 1, False)
        return carry

    lax.fori_loop(0, qi >> 1, fast_pair, 0)

    @pl.when((qi & 1) == 1)
    def _():
        fast_update(qi - 1, False)

    fast_update(qi, True)

    @pl.when(jnp.min(l_ref[...]) < ATTN_MIN_SUM)
    def _():
        m_ref[...] = jnp.full(m_ref.shape, NEG_BIG, F32)
        l_ref[...] = jnp.zeros(l_ref.shape, F32)
        acc_ref[...] = jnp.zeros(acc_ref.shape, F32)

        def scores(j, dst_ref):
            for h in heads:
                dst_ref[h] = score(j, h)

        def update(j, src_ref, masked):
            for h in heads:
                s = src_ref[h]
                if masked:
                    s = causal(s, NEG_BIG)
                m = m_ref[h]
                m_new = jnp.maximum(m, jnp.max(s, axis=0, keepdims=True))
                alpha = jnp.exp2(m - m_new)
                p = jnp.exp2(s - m_new)
                m_ref[h] = m_new
                l_ref[h] = alpha * l_ref[h] + jnp.sum(p, axis=0, keepdims=True)
                vt = vt_ref[j, h * hd:(h + 1) * hd, :]
                acc_ref[h] = alpha * acc_ref[h] + jnp.dot(vt, p.astype(BF16), preferred_element_type=F32)

        scores(0, sa_ref)

        def pair(i, carry):
            scores(2 * i + 1, sb_ref)
            update(2 * i, sa_ref, False)
            scores(2 * i + 2, sa_ref)
            update(2 * i + 1, sb_ref, False)
            return carry

        lax.fori_loop(0, qi >> 1, pair, 0)

        @pl.when((qi & 1) == 1)
        def _():
            scores(qi, sb_ref)
            update(qi - 1, sa_ref, False)
            update(qi, sb_ref, True)

        @pl.when((qi & 1) == 0)
        def _():
            update(qi, sa_ref, True)

    for h in heads:
        o = acc_ref[h] / l_ref[h]
        o = o[:, 0:t] - lam_ref[layer, 0] * o[:, t:2 * t]
        o = o * lax.rsqrt(jnp.mean(o * o, axis=0, keepdims=True) + EPS) * (hg_ref[...] * out_scale)
        o_ref[:, h * hd:(h + 1) * hd] = o.T.astype(BF16)


def _diff_attn(qt, k, vt, lam, hg_col, b, s, l, lam_init):
    t = SEQ_TILE
    nq = s // t
    hp = ATTN_HEADS_PER_STEP
    w = hp * DIFF_HEAD_DIM
    ng = DIFF_HEADS // hp
    kern = functools.partial(_diff_attn_kernel, t=t, layer=l, out_scale=1.0 - lam_init)
    return pl.pallas_call(
        kern,
        grid=(b, ng, nq),
        in_specs=[pl.BlockSpec(memory_space=pltpu.SMEM),
                  pl.BlockSpec((None, None, w, t), lambda bi, g, qi: (bi, qi, g, 0)),
                  pl.BlockSpec((s, w), lambda bi, g, qi: (bi, g)),
                  pl.BlockSpec((None, nq, w, t), lambda bi, g, qi: (bi, 0, g, 0)),
                  _lspec(l, (DIFF_HEAD_DIM, 1))],
        out_specs=pl.BlockSpec((t, w), lambda bi, g, qi: (bi * nq + qi, g)),
        out_shape=jax.ShapeDtypeStruct((b * s, GROUP_WIDTH), BF16),
        scratch_shapes=[pltpu.VMEM((hp, DIFF_HEAD_DIM, 2 * t), BF16),
                        pltpu.VMEM((hp, t, 2 * t), F32), pltpu.VMEM((hp, t, 2 * t), F32),
                        pltpu.VMEM((hp, 1, 2 * t), F32), pltpu.VMEM((hp, 1, 2 * t), F32),
                        pltpu.VMEM((hp, DIFF_HEAD_DIM, 2 * t), F32)],
        compiler_params=_params("arbitrary", "arbitrary", "arbitrary"),
        name="diff_attn",
    )(lam, qt, k, vt, hg_col)


def _mem_kv_kernel(mem_ref, mg_ref, w_ref, kg_ref, mk_ref, mv_ref):
    memn = _rms_rows(mem_ref[...], mg_ref[...]).astype(BF16)
    gw = GROUP_WIDTH
    k = jnp.dot(memn, w_ref[:, 0:gw], preferred_element_type=F32)
    for h in range(MEM_HEADS):
        sl = slice(h * MEM_HEAD_DIM, (h + 1) * MEM_HEAD_DIM)
        mk_ref[:, sl] = _rms_rows(k[:, sl], kg_ref[...]).astype(BF16)
    mv_ref[...] = jnp.dot(memn, w_ref[:, gw:2 * gw], preferred_element_type=F32).astype(BF16)


def _mem_kv(mem2, mg, l, w_kv, kg):
    r = mem2.shape[0]
    tm = _tile(r, 512)
    row = lambda i: (i, 0)
    gw = GROUP_WIDTH
    return pl.pallas_call(
        _mem_kv_kernel,
        grid=(r // tm,),
        in_specs=[pl.BlockSpec((tm, D_MODEL), row), pl.BlockSpec((1, D_MODEL), lambda i: (0, 0)),
                  _lspec(l, (D_MODEL, 2 * gw)), _lspec(l, (1, MEM_HEAD_DIM))],
        out_specs=[pl.BlockSpec((tm, gw), row), pl.BlockSpec((tm, gw), row)],
        out_shape=[jax.ShapeDtypeStruct((r, gw), BF16), jax.ShapeDtypeStruct((r, gw), BF16)],
        compiler_params=_params("arbitrary"),
        name="mem_kv",
    )(mem2, mg, w_kv, kg)


MOE_TILE = 256
GRANULE = 16
TILE_SLOTS = MOE_TILE // GRANULE + N_GROUPS
SORTED_ROWS = TILE_SLOTS * GRANULE
STEP_GRANULES = 32
STEP_ROWS = STEP_GRANULES * GRANULE
XS_WIDTH = D_MODEL + LANES
INFO_POS_LANE = 0


def _route_t(lt):
    row_i = lax.broadcasted_iota(jnp.int32, lt.shape, 0)
    row = row_i.astype(F32)
    e0 = ROUTER_EXPERT_LANE0
    lg = jnp.where(row_i < N_GROUPS, lt, NEG_BIG)
    mg = jnp.max(lg, axis=0, keepdims=True)
    g_gate = 1.0 / jnp.sum(jnp.exp(lg - mg), axis=0, keepdims=True)
    g_idx = jnp.min(jnp.where(lg == mg, row, float(LANES)), axis=0, keepdims=True)
    row_group = ((row_i - e0) >> 3).astype(F32)
    sel = (row_i >= e0) & (row_i < e0 + N_EXPERTS) & (row_group == g_idx)
    le = jnp.where(sel, lt, NEG_BIG)
    m1 = jnp.max(le, axis=0, keepdims=True)
    se = jnp.sum(jnp.where(sel, jnp.exp(le - m1), 0.0), axis=0, keepdims=True)
    i1 = jnp.min(jnp.where(sel & (le == m1), row, float(LANES)), axis=0, keepdims=True)
    le2 = jnp.where(row == i1, NEG_BIG, le)
    m2 = jnp.max(le2, axis=0, keepdims=True)
    i2 = jnp.min(jnp.where(sel & (le2 == m2) & (row != i1), row, float(LANES)), axis=0, keepdims=True)
    p1 = 1.0 / se
    p2 = jnp.exp(m2 - m1) / se
    tot = p1 + p2
    w = jnp.where(row == i1, p1 / tot, jnp.where(row == i2, p2 / tot, 0.0))
    return g_gate * w, g_idx


def _split3(c):
    hi = c.astype(BF16).astype(F32)
    r1 = c - hi
    mid = r1.astype(BF16).astype(F32)
    lo = (r1 - mid).astype(BF16).astype(F32)
    return hi, mid, lo


def _sort_logits(x1, g2, rt_hi, rt_lo, rb_col):
    xn = _rms_rows(x1, g2)
    hi = xn.astype(BF16)
    lo = (xn - hi.astype(F32)).astype(BF16)
    logits_t = (lax.dot_general(rt_hi, hi, _NT, preferred_element_type=F32)
                + lax.dot_general(rt_hi, lo, _NT, preferred_element_type=F32)
                + lax.dot_general(rt_lo, hi, _NT, preferred_element_type=F32) + rb_col)
    return hi, logits_t


def _sort_rank(logits_t, earlier):
    tl = logits_t.shape[1]
    comb_t, g_idx = _route_t(logits_t)
    grow = lax.broadcasted_iota(jnp.int32, (SUBLANES, tl), 0).astype(F32)
    gt = jnp.where(grow == g_idx, 1.0, 0.0)
    before = jnp.dot(gt.astype(BF16), earlier, preferred_element_type=F32)
    return comb_t, gt, before


def _sort_emit(hi, comb_t, gt, before):
    tl = hi.shape[0]
    rank = jnp.sum(gt * before, axis=0, keepdims=True)
    cnt = jnp.sum(gt, axis=1, keepdims=True)
    glen = jnp.floor((cnt + (GRANULE - 1)) * (1.0 / GRANULE))
    r8 = lax.broadcasted_iota(jnp.int32, (SUBLANES, 1), 0)
    start = jnp.zeros((SUBLANES, 1), F32)
    for g in range(1, N_GROUPS):
        start = jnp.where(r8 == g, jnp.sum(jnp.where(r8 < g, glen, 0.0), axis=0, keepdims=True), start)
    pos = jnp.sum(gt * (start * GRANULE), axis=0, keepdims=True) + rank

    rows = lax.broadcasted_iota(jnp.int32, (LANES, tl), 0)
    info = jnp.where(rows == INFO_POS_LANE, pos, comb_t).T
    lane = _lane_iota(info.shape)
    e0 = ROUTER_EXPERT_LANE0
    c_hi, c_mid, c_lo = _split3(jnp.where((lane >= e0) & (lane < e0 + N_EXPERTS), info, 0.0))
    aug = (c_hi + pltpu.roll(c_mid, N_EXPERTS, axis=1) + pltpu.roll(c_lo, 2 * N_EXPERTS, axis=1)).astype(BF16)
    perm = jnp.where(pos == lax.broadcasted_iota(jnp.int32, (SORTED_ROWS, tl), 0).astype(F32),
                     1.0, 0.0).astype(BF16)
    xs = jnp.dot(perm, hi, preferred_element_type=F32).astype(BF16)
    xs_aug = jnp.dot(perm, aug, preferred_element_type=F32).astype(BF16)
    return xs, xs_aug, info, glen


def _out_sort_kernel(x_ref, yl_ref, yd_ref, ym_ref, w_ref, g2_ref, rthi_ref, rtlo_ref, rb_ref, earlier_ref,
                     o_ref, xs_ref, info_ref, lens_ref, x1s_ref):
    @pl.when(pl.program_id(0) == 0)
    def _():
        x1s_ref[...] = jnp.zeros_like(x1s_ref)

    tm = x1s_ref.shape[0]
    subs = range(tm // MOE_TILE)

    def project(c, nchunks):
        r = slice(c * (tm // nchunks), (c + 1) * (tm // nchunks))
        y = jnp.concatenate([yl_ref[r, :], yd_ref[r, :], ym_ref[r, :]], axis=-1)
        x1 = x_ref[r, :] + jnp.dot(y, w_ref[...], preferred_element_type=F32)
        o_ref[r, :] = x1
        return r, x1

    fresh = [project(0, 2)]
    s1 = [_sort_logits(x1s_ref[sub * MOE_TILE:(sub + 1) * MOE_TILE, :], g2_ref[...], rthi_ref[...],
                       rtlo_ref[...], rb_ref[...]) for sub in subs]
    for r, x1 in fresh:
        x1s_ref[r, :] = x1
    fresh = [project(1, 2)]
    s2 = [_sort_rank(s1[sub][1], earlier_ref[...]) for sub in subs]
    for r, x1 in fresh:
        x1s_ref[r, :] = x1
    for sub in subs:
        xs, xs_aug, info, glen = _sort_emit(s1[sub][0], *s2[sub])
        r0 = sub * SORTED_ROWS
        xs_ref[r0:r0 + SORTED_ROWS, 0:D_MODEL] = xs
        xs_ref[r0:r0 + SORTED_ROWS, D_MODEL:XS_WIDTH] = xs_aug
        info_ref[sub * MOE_TILE:(sub + 1) * MOE_TILE, :] = info
        lens_ref[sub] = jnp.broadcast_to(glen, (SUBLANES, LANES))


def _out_sort(x2, y_lru, y_diff, y_mem, l, w_out, g2, rt_hi, rt_lo, rb_col, earlier):
    t = x2.shape[0]
    tm = SEQ_TILE
    assert t % tm == 0 and tm % MOE_TILE == 0
    sub = tm // MOE_TILE
    nt = t // MOE_TILE
    n = t // tm
    cur = lambda i: (jnp.minimum(i, n - 1), 0)
    prev = lambda i: (jnp.maximum(i - 1, 0), 0)
    gw = GROUP_WIDTH
    return pl.pallas_call(
        _out_sort_kernel,
        grid=(n + 1,),
        in_specs=[pl.BlockSpec((tm, D_MODEL), cur), pl.BlockSpec((tm, gw), cur), pl.BlockSpec((tm, gw), cur),
                  pl.BlockSpec((tm, gw), cur), _lspec(l, (3 * gw, D_MODEL)),
                  _lspec(l, (1, D_MODEL)), _lspec(l, (LANES, D_MODEL)),
                  _lspec(l, (LANES, D_MODEL)), _lspec(l, (LANES, 1)),
                  pl.BlockSpec((MOE_TILE, MOE_TILE), lambda i: (0, 0))],
        out_specs=[pl.BlockSpec((tm, D_MODEL), cur), pl.BlockSpec((sub * SORTED_ROWS, XS_WIDTH), prev),
                   pl.BlockSpec((tm, LANES), prev),
                   pl.BlockSpec((sub, SUBLANES, LANES), lambda i: (jnp.maximum(i - 1, 0), 0, 0))],
        out_shape=[jax.ShapeDtypeStruct((t, D_MODEL), F32),
                   jax.ShapeDtypeStruct((nt * SORTED_ROWS, XS_WIDTH), BF16),
                   jax.ShapeDtypeStruct((t, LANES), F32),
                   jax.ShapeDtypeStruct((nt, SUBLANES, LANES), F32)],
        scratch_shapes=[pltpu.VMEM((tm, D_MODEL), F32)],
        compiler_params=_params("arbitrary"),
        name="out_proj_sort",
    )(x2, y_lru, y_diff, y_mem, w_out, g2, rt_hi, rt_lo, rb_col, earlier)


def _moe_expert_kernel(sg_ref, sv_ref, gi_ref, xs_hbm, wg_ref, wu_ref, wd_ref, y_ref, xbuf_ref, sem_ref):
    s = pl.program_id(0)
    last = pl.num_programs(0) - 1
    slot = s & 1

    def granule_copy(step, k, dst_slot):
        g = gi_ref[step * STEP_GRANULES + k]
        return pltpu.make_async_copy(
            xs_hbm.at[pl.ds(pl.multiple_of(g * GRANULE, GRANULE), GRANULE), :],
            xbuf_ref.at[dst_slot, pl.ds(k * GRANULE, GRANULE), :], sem_ref.at[dst_slot])

    def start_fetch(step, dst_slot):
        for k in range(STEP_GRANULES):
            granule_copy(step, k, dst_slot).start()

    def wait_fetch(step, dst_slot):
        for k in range(STEP_GRANULES):
            granule_copy(step, k, dst_slot).wait()

    @pl.when(s == 0)
    def _():
        start_fetch(0, 0)

    @pl.when((s == 0) | (sv_ref[jnp.maximum(s - 1, 0)] != 0))
    def _():
        wait_fetch(s, slot)

    @pl.when(sv_ref[s] == 0)
    def _():
        y_ref[...] = jnp.zeros_like(y_ref)

    @pl.when(sv_ref[s] != 0)
    def _():
        nxt = jnp.minimum(s + 1, last)
        start_fetch(nxt, 1 - slot)
        rows = xbuf_ref[slot]
        x = rows[:, 0:D_MODEL]
        aug = rows[:, D_MODEL:XS_WIDTH].astype(F32)
        lane = _lane_iota(aug.shape)
        e0 = ROUTER_EXPERT_LANE0
        comb = jnp.where((lane >= e0) & (lane < e0 + N_EXPERTS),
                         aug + pltpu.roll(aug, LANES - N_EXPERTS, axis=1)
                         + pltpu.roll(aug, LANES - 2 * N_EXPERTS, axis=1), 0.0)
        first = e0 + sg_ref[s] * EXPERTS_PER_GROUP
        hs = []
        for e in range(EXPERTS_PER_GROUP):
            cw = jnp.sum(jnp.where(lane == first + e, comb, 0.0), axis=-1, keepdims=True)
            hg = jnp.dot(x, wg_ref[e], preferred_element_type=F32)
            hu = jnp.dot(x, wu_ref[e], preferred_element_type=F32)
            hs.append((jax.nn.silu(hg) * hu * cw).astype(BF16))
        h = jnp.concatenate(hs, axis=-1)
        y_ref[...] = jnp.dot(h, wd_ref[...], preferred_element_type=F32).astype(BF16)

        @pl.when(s == last)
        def _():
            wait_fetch(nxt, 1 - slot)


def _moe_experts(step_group, step_valid, gran_idx, xs, l, wg, wu, wd):
    nstep = step_group.shape[0]
    epg = EXPERTS_PER_GROUP
    wmap4 = lambda s, sg, sv, gi: (l * N_GROUPS + sg[s], 0, 0, 0)
    in_specs = [pl.BlockSpec(memory_space=pl.ANY),
                pl.BlockSpec((None, epg, D_MODEL, D_EXPERT), wmap4),
                pl.BlockSpec((None, epg, D_MODEL, D_EXPERT), wmap4),
                pl.BlockSpec((None, epg * D_EXPERT, D_MODEL), lambda s, sg, sv, gi: (l * N_GROUPS + sg[s], 0, 0))]
    return pl.pallas_call(
        _moe_expert_kernel,
        grid_spec=pltpu.PrefetchScalarGridSpec(
            num_scalar_prefetch=3, grid=(nstep,), in_specs=in_specs,
            out_specs=pl.BlockSpec((STEP_ROWS, D_MODEL), lambda s, sg, sv, gi: (s, 0)),
            scratch_shapes=[pltpu.VMEM((2, STEP_ROWS, XS_WIDTH), BF16), pltpu.SemaphoreType.DMA((2,))]),
        out_shape=jax.ShapeDtypeStruct((nstep * STEP_ROWS, D_MODEL), BF16),
        compiler_params=_params("arbitrary"),
        name="moe_experts",
    )(step_group, step_valid, gran_idx, xs, wg, wu, wd)


def _moe_combine_kernel(inv_ref, x_ref, info_ref, *refs):
    del inv_ref
    ys_refs = refs[:TILE_SLOTS]
    o_ref = refs[TILE_SLOTS]
    o_ref[...] = x_ref[...] + _unsort(info_ref[...], ys_refs)


def _moe_combine(inv, x2, info, ys):
    t = x2.shape[0]
    nt = t // MOE_TILE
    row = lambda i, inv: (i, 0)
    in_specs = [pl.BlockSpec((MOE_TILE, D_MODEL), row), pl.BlockSpec((MOE_TILE, LANES), row)]
    in_specs += [pl.BlockSpec((GRANULE, D_MODEL), functools.partial(_slot_map, q, (1, 0)))
                 for q in range(TILE_SLOTS)]
    return pl.pallas_call(
        _moe_combine_kernel,
        grid_spec=pltpu.PrefetchScalarGridSpec(
            num_scalar_prefetch=1, grid=(nt,), in_specs=in_specs,
            out_specs=pl.BlockSpec((MOE_TILE, D_MODEL), row)),
        out_shape=jax.ShapeDtypeStruct((t, D_MODEL), F32),
        compiler_params=_params("arbitrary"),
        name="moe_combine",
    )(inv, x2, info, *([ys] * TILE_SLOTS))


def _moe_tables(lens, nt):
    i32 = jnp.int32
    ng = N_GROUPS
    garange = jnp.arange(ng, dtype=i32)
    cum = jnp.cumsum(lens, axis=1)
    start = cum - lens
    base = jnp.arange(nt, dtype=i32)[:, None] * TILE_SLOTS + start
    run_len = lens.T.reshape(-1)
    run_base = base.T.reshape(-1)
    run_end = jnp.cumsum(run_len)
    run_start = run_end - run_len
    n_g = jnp.sum(lens, axis=0)
    steps_g = (n_g + STEP_GRANULES - 1) // STEP_GRANULES
    step_end = jnp.cumsum(steps_g)
    step_off = step_end - steps_g
    gran_off = jnp.cumsum(n_g) - n_g
    nstep = (nt * (TILE_SLOTS - 1) + STEP_GRANULES - 1) // STEP_GRANULES + ng
    s = jnp.arange(nstep, dtype=i32)
    sg = jnp.minimum(jnp.sum((s[:, None] >= step_end[None, :]).astype(i32), axis=-1), ng - 1)
    goh = sg[:, None] == garange[None, :]
    pick = lambda v: jnp.sum(jnp.where(goh, v[None, :], 0), axis=-1)
    sv = s < step_end[-1]
    jl = (s - pick(step_off))[:, None] * STEP_GRANULES + jnp.arange(STEP_GRANULES, dtype=i32)[None, :]
    ok = sv[:, None] & (jl < pick(n_g)[:, None])
    j = jnp.where(ok, pick(gran_off)[:, None] + jl, 0)
    inrun = (j[..., None] >= run_start) & (j[..., None] < run_end)
    gran = j + jnp.sum(jnp.where(inrun, run_base - run_start, 0), axis=-1)
    gran = jnp.where(ok, gran, jnp.where(sv[:, None], gran[:, 0:1], 0)).astype(i32)
    q = jnp.arange(TILE_SLOTS, dtype=i32)[None, :]
    gq = jnp.minimum(jnp.sum((q[:, :, None] >= cum[:, None, :]).astype(i32), axis=-1), ng - 1)
    qoh = gq[..., None] == garange
    used = q < cum[:, -1:]
    per_g = run_start.reshape(ng, nt).T - start + (step_off * STEP_GRANULES - gran_off)[None, :]
    inv = jnp.where(used, q + jnp.sum(jnp.where(qoh, per_g[:, None, :], 0), axis=-1), 0).astype(i32)
    return sg, sv.astype(i32), gran.reshape(-1), inv.reshape(-1)


def _out_proj_moe(x2, y_lru, y_diff, y_mem, l, w_out, g2, rt_hi, rt_lo, rb_col, earlier, wg, wu, wd):
    nt = x2.shape[0] // MOE_TILE
    x1, xs, info, lens = _out_sort(x2, y_lru, y_diff, y_mem, l, w_out, g2, rt_hi, rt_lo, rb_col, earlier)
    sg, sv, gran, inv = _moe_tables(lens[:, 0:N_GROUPS, 0].astype(jnp.int32), nt)
    ys = _moe_experts(sg, sv, gran, xs, l, wg, wu, wd)
    return x1, (inv, info, ys)


def _block_diag(w):
    depth, h, n, _ = w.shape
    eye = jnp.eye(h, dtype=w.dtype)
    return (eye[None, :, None, :, None] * w[:, :, :, None, :]).reshape(depth, h * n, h * n)


def _router_tables(w_rg, b_rg, w_re, b_re):
    depth = w_rg.shape[0]
    e0 = ROUTER_EXPERT_LANE0
    pad = lambda rows, width: jnp.zeros((depth, rows, width), F32)
    w = jnp.concatenate([jnp.swapaxes(w_rg, 1, 2), pad(e0 - N_GROUPS, D_MODEL), jnp.swapaxes(w_re, 1, 2),
                         pad(LANES - e0 - N_EXPERTS, D_MODEL)], axis=1)
    bias = jnp.concatenate([b_rg[:, :, None], pad(e0 - N_GROUPS, 1), b_re[:, :, None],
                            pad(LANES - e0 - N_EXPERTS, 1)], axis=1)
    hi = w.astype(BF16)
    lo = (w - hi.astype(F32)).astype(BF16)
    return hi, lo, bias


def kernel(x, mem, norm1_g, w_in, conv_w, conv_b, rg_wa, rg_ba, rg_wx, rg_bx, rg_lambda, dq_norm_g, dk_norm_g, lambda_q1, lambda_k1, lambda_q2, lambda_k2, diff_head_norm_g, mem_norm_g, w_mem_kv, mq_norm_g, mk_norm_g, w_out, norm2_g, w_router_group, b_router_group, w_router_expert, b_router_expert, w_expert_gate, w_expert_up, w_expert_down):
    b, s, d = x.shape
    m = mem.shape[1]
    depth = w_in.shape[0]
    gw = GROUP_WIDTH
    epg = EXPERTS_PER_GROUP
    x2 = x.reshape(b * s, d)
    mem2 = mem.reshape(b * m, d)
    row = lambda v: v.reshape(depth, 1, -1).astype(F32)
    col = lambda v: v.reshape(depth, -1, 1).astype(F32)
    lam_inits = [0.8 - 0.6 * math.exp(-0.3 * l) for l in range(depth)]
    lam = (jnp.exp(jnp.sum(lambda_q1 * lambda_k1, axis=-1)) - jnp.exp(jnp.sum(lambda_q2 * lambda_k2, axis=-1))
           + jnp.asarray(lam_inits, F32))
    lam = jnp.stack([lam, _score_bound(dq_norm_g, dk_norm_g)], axis=1).astype(F32)
    w_in_bf = w_in.astype(BF16)
    w_in_t = jnp.swapaxes(w_in_bf, 1, 2)
    qg_col = col(jnp.tile(dq_norm_g, (1, gw // DIFF_QK_DIM)))
    kg = row(jnp.tile(dk_norm_g, (1, 2)))
    wa_bd = _block_diag(rg_wa).astype(BF16)
    wx_bd = _block_diag(rg_wx).astype(BF16)
    w_kv_bf = w_mem_kv.astype(BF16)
    w_out_bf = w_out.astype(BF16)
    rt_hi, rt_lo, rb_col = _router_tables(w_router_group, b_router_group, w_router_expert, b_router_expert)
    earlier = jnp.triu(jnp.ones((MOE_TILE, MOE_TILE), BF16), k=1)
    wg = w_expert_gate.astype(BF16).reshape(depth * N_GROUPS, epg, D_MODEL, D_EXPERT)
    wu = w_expert_up.astype(BF16).reshape(depth * N_GROUPS, epg, D_MODEL, D_EXPERT)
    wd = w_expert_down.astype(BF16).reshape(depth * N_GROUPS, epg * D_EXPERT, D_MODEL)
    g1, g2, mqg, mkg = row(norm1_g), row(norm2_g), row(mq_norm_g), row(mk_norm_g)
    conv_b3, ba, bx, lru_lam, hg_col = row(conv_b), row(rg_ba), row(rg_bx), row(rg_lambda), col(diff_head_norm_g)
    mem_g = mem_norm_g.reshape(1, -1).astype(F32)
    moe = None
    for l in range(depth):
        mk, mv = _mem_kv(mem2, mem_g, l, w_kv_bf, mkg)
        proj = _in_proj(x2, b, s, l, g1, w_in_bf, w_in_t, qg_col, kg, mqg, mk, mv,
                        (conv_w, conv_b3, wa_bd, ba, wx_bd, bx, lru_lam), moe)
        if moe is not None:
            x2, *proj = proj
        y_lru, k, y_mem, qt, vt = proj
        y_diff = _diff_attn(qt, k, vt, lam, hg_col, b, s, l, lam_inits[l])
        x2, moe = _out_proj_moe(x2, y_lru, y_diff, y_mem, l, w_out_bf, g2, rt_hi, rt_lo, rb_col, earlier,
                                wg, wu, wd)
    return _moe_combine(moe[0], x2, moe[1], moe[2]).reshape(b, s, d)
```

```python
import functools
import math

import jax
import jax.numpy as jnp
from jax import lax
from jax.experimental import pallas as pl
from jax.experimental.pallas import tpu as pltpu

F32 = jnp.float32
BF16 = jnp.bfloat16

D_MODEL = 1024
GROUP_WIDTH = D_MODEL // 2
LRU_HEADS = 8
CONV_WIDTH = 4
LRU_C = 8.0
DIFF_HEADS = 4
DIFF_HEAD_DIM = GROUP_WIDTH // DIFF_HEADS
DIFF_QK_DIM = DIFF_HEAD_DIM // 2
MEM_HEADS = 4
MEM_HEAD_DIM = GROUP_WIDTH // MEM_HEADS
N_GROUPS = 4
EXPERTS_PER_GROUP = 8
N_EXPERTS = N_GROUPS * EXPERTS_PER_GROUP
D_EXPERT = 256
D_IN_PROJ = 6 * GROUP_WIDTH
EPS = 1e-6

LANES = 128
SUBLANES = 8
VMEM_LIMIT = 56 * 1024 * 1024
NEG_BIG = -1e30
ROUTER_EXPERT_LANE0 = 32


def _tile(n, pref):
    t = min(n, pref)
    assert n % t == 0, (n, t)
    return t


def _params(*sem):
    return pltpu.CompilerParams(dimension_semantics=sem, vmem_limit_bytes=VMEM_LIMIT)


def _lspec(l, tail, blk=None):
    idx = (l,) + tuple(blk if blk is not None else (0,) * len(tail))
    return pl.BlockSpec((None,) + tuple(tail), lambda *_: idx)


def _rms_rows(x, g):
    return x * lax.rsqrt(jnp.mean(x * x, axis=-1, keepdims=True) + EPS) * g


def _lane_iota(shape):
    return lax.broadcasted_iota(jnp.int32, shape, len(shape) - 1)


def _half_head_norm(z, g):
    sq = z * z
    lo = _lane_iota(z.shape) < DIFF_QK_DIM
    s_all = jnp.sum(sq, axis=-1, keepdims=True)
    s_lo = jnp.sum(jnp.where(lo, sq, 0.0), axis=-1, keepdims=True)
    inv_lo = lax.rsqrt(s_lo * (1.0 / DIFF_QK_DIM) + EPS)
    inv_hi = lax.rsqrt((s_all - s_lo) * (1.0 / DIFF_QK_DIM) + EPS)
    return z * jnp.where(lo, inv_lo, inv_hi) * g


SEQ_TILE = 512
_NT = (((1,), (1,)), ((), ()))


N_PROJ_PARAMS = 18
N_PROJ_OUTS = 5


def _softplus(z):
    return jnp.maximum(z, 0.0) + jnp.log(1.0 + jnp.exp(-jnp.abs(z)))


def _in_proj_body(x, ns, g1_ref, wl_ref, wk_ref, wmq_ref, wqt_ref, wvt_ref, qg_ref, kg_ref, mqg_ref,
                  mk_ref, mv_ref, cw_ref, cb_ref, wa_ref, ba_ref, wx_ref, bx_ref, llam_ref,
                  ylru_ref, k_ref, ym_ref, qt_ref, vt_ref, ext_ref, hcar_ref, a_ref, b_ref, h_ref):
    gw = GROUP_WIDTH
    tm = x.shape[0]
    hist = SUBLANES
    first = lax.rem(pl.program_id(0), ns) == 0

    @pl.when(first)
    def _():
        ext_ref[0:hist, :] = jnp.zeros((hist, gw), F32)
        hcar_ref[...] = jnp.zeros_like(hcar_ref)

    @pl.when(jnp.logical_not(first))
    def _():
        ext_ref[0:hist, :] = ext_ref[tm:tm + hist, :]

    xn = _rms_rows(x, g1_ref[...]).astype(BF16)
    heads = range(MEM_HEADS)
    hsl = [slice(h * MEM_HEAD_DIM, (h + 1) * MEM_HEAD_DIM) for h in heads]
    mq = jnp.dot(xn, wmq_ref[...], preferred_element_type=F32)
    u = jnp.dot(xn, wl_ref[...], preferred_element_type=F32)
    ext_ref[hist:hist + tm, :] = u[:, 0:gw]
    xc = cb_ref[...] + cw_ref[CONV_WIDTH - 1:CONV_WIDTH, :] * u[:, 0:gw]
    for j in range(CONV_WIDTH - 1):
        off = hist - (CONV_WIDTH - 1) + j
        xc = xc + cw_ref[j:j + 1, :] * ext_ref[off:off + tm, :]
    xcb = xc.astype(BF16)
    sc = [lax.dot_general(_rms_rows(mq[:, hsl[h]], mqg_ref[...]).astype(BF16), mk_ref[:, hsl[h]], _NT,
                          preferred_element_type=F32) * MEM_HEAD_DIM ** -0.5 for h in heads]
    k = jnp.dot(xn, wk_ref[...], preferred_element_type=F32)
    r_pre = jnp.dot(xcb, wa_ref[...], preferred_element_type=F32)
    i_pre = jnp.dot(xcb, wx_ref[...], preferred_element_type=F32)
    p = [jnp.exp(sc[h] - jnp.max(sc[h], axis=-1, keepdims=True)) for h in heads]
    qt = lax.dot_general(wqt_ref[...], xn, _NT, preferred_element_type=F32)
    o = [jnp.dot(p[h].astype(BF16), mv_ref[:, hsl[h]], preferred_element_type=F32) for h in heads]
    vt_ref[...] = lax.dot_general(wvt_ref[...], xn, _NT, preferred_element_type=F32).astype(BF16)

    r = jax.nn.sigmoid(r_pre + ba_ref[...])
    gate_i = jax.nn.sigmoid(i_pre + bx_ref[...])
    a = jnp.exp((-LRU_C * r) * _softplus(-llam_ref[...]))
    om = 1.0 - a * a
    b = om * lax.rsqrt(jnp.maximum(om, 1e-30)) * (gate_i * xc)
    a = a.reshape(tm // SUBLANES, SUBLANES, gw)
    b = b.reshape(tm // SUBLANES, SUBLANES, gw)
    row = lax.broadcasted_iota(jnp.int32, a.shape, 1)
    d = 1
    while d < SUBLANES:
        keep = row >= d
        a_prev = pltpu.roll(a, d, axis=1)
        b_prev = pltpu.roll(b, d, axis=1)
        b = jnp.where(keep, a * b_prev + b, b)
        a = jnp.where(keep, a * a_prev, a)
        d *= 2
    a_ref[...] = a.reshape(tm, gw)
    b_ref[...] = b.reshape(tm, gw)
    gated = jax.nn.gelu(u[:, gw:2 * gw])

    for h in range(DIFF_HEADS):
        sl = slice(h * LANES, (h + 1) * LANES)
        k_ref[:, sl] = _half_head_norm(k[:, sl], kg_ref[...]).astype(BF16)
    for h in heads:
        ym_ref[:, hsl[h]] = (o[h] / jnp.sum(p[h], axis=-1, keepdims=True)).astype(BF16)
    q3 = qt.reshape(gw // DIFF_QK_DIM, DIFF_QK_DIM, tm)
    q3 = q3 * lax.rsqrt(jnp.mean(q3 * q3, axis=1, keepdims=True) + EPS)
    qscale = DIFF_QK_DIM ** -0.5 * math.log2(math.e)
    qt_ref[...] = (q3.reshape(gw, tm) * (qg_ref[...] * qscale)).astype(BF16)

    def block(i, hprev):
        off = pl.multiple_of(i * SUBLANES, SUBLANES)
        hb = a_ref[pl.ds(off, SUBLANES), :] * hprev + b_ref[pl.ds(off, SUBLANES), :]
        h_ref[pl.ds(off, SUBLANES), :] = hb
        return jnp.broadcast_to(hb[SUBLANES - 1:SUBLANES, :], (SUBLANES, gw))

    hcar_ref[...] = lax.fori_loop(0, tm // SUBLANES, block, hcar_ref[...], unroll=8)
    ylru_ref[...] = (h_ref[...] * gated).astype(BF16)


def _in_proj_kernel(x_ref, *refs, ns):
    _in_proj_body(x_ref[...], ns, *refs)


def _granule_copies(tab_ref, step, n, src_hbm, buf_ref, slot, sem_ref, start):
    for k in range(n):
        g = tab_ref[step * n + k]
        cp = pltpu.make_async_copy(
            src_hbm.at[pl.ds(pl.multiple_of(g * GRANULE, GRANULE), GRANULE), :],
            buf_ref.at[slot, pl.ds(k * GRANULE, GRANULE), :], sem_ref.at[slot])
        if start:
            cp.start()
        else:
            cp.wait()


def _unsort(info, ys):
    kpad = -SORTED_ROWS % LANES
    ys = jnp.concatenate([ys, jnp.zeros((kpad, D_MODEL), BF16)], axis=0)
    pos = info[:, INFO_POS_LANE:INFO_POS_LANE + 1]
    unperm = jnp.where(pos == _lane_iota((MOE_TILE, SORTED_ROWS + kpad)).astype(F32), 1.0, 0.0).astype(BF16)
    return jnp.dot(unperm, ys, preferred_element_type=F32)


def _combined_tile(inv_ref, x1_ref, info_ref, ys_hbm, ybuf_ref, sem_ref):
    s = pl.program_id(0)
    last = pl.num_programs(0) - 1
    slot = s & 1
    nsub = x1_ref.shape[0] // MOE_TILE
    n = nsub * TILE_SLOTS

    @pl.when(s == 0)
    def _():
        _granule_copies(inv_ref, 0, n, ys_hbm, ybuf_ref, 0, sem_ref, True)

    _granule_copies(inv_ref, s, n, ys_hbm, ybuf_ref, slot, sem_ref, False)
    nxt = jnp.minimum(s + 1, last)
    _granule_copies(inv_ref, nxt, n, ys_hbm, ybuf_ref, 1 - slot, sem_ref, True)
    x = jnp.concatenate(
        [x1_ref[u * MOE_TILE:(u + 1) * MOE_TILE, :]
         + _unsort(info_ref[u * MOE_TILE:(u + 1) * MOE_TILE, :],
                   ybuf_ref[slot, u * SORTED_ROWS:(u + 1) * SORTED_ROWS, :])
         for u in range(nsub)], axis=0)

    @pl.when(s == last)
    def _():
        _granule_copies(inv_ref, nxt, n, ys_hbm, ybuf_ref, 1 - slot, sem_ref, False)

    return x


def _combine_in_proj_kernel(inv_ref, x1_ref, info_ref, ys_hbm, *refs, ns):
    params, x_out_ref, outs = refs[:N_PROJ_PARAMS], refs[N_PROJ_PARAMS], refs[N_PROJ_PARAMS + 1:-2]
    x = _combined_tile(inv_ref, x1_ref, info_ref, ys_hbm, *refs[-2:])
    x_out_ref[...] = x
    _in_proj_body(x, ns, *params, *outs)


def _in_proj(x2, b, s, l, g1, w_in, w_in_t, qg_col, kg, mqg, mk, mv, lru, moe=None):
    t = x2.shape[0]
    tm = SEQ_TILE
    assert s % tm == 0
    ns = s // tm
    m = mk.shape[0] // b
    gw = GROUP_WIDTH
    row = lambda i, *_: (i, 0)
    fm = lambda i, *_: (i // ns, i % ns, 0, 0)
    mem = pl.BlockSpec((m, gw), lambda i, *_: (i // ns, 0))
    in_specs = [_lspec(l, (1, D_MODEL)),
                _lspec(l, (D_MODEL, 2 * gw), (0, 0)), _lspec(l, (D_MODEL, gw), (0, 3)),
                _lspec(l, (D_MODEL, gw), (0, 5)), _lspec(l, (gw, D_MODEL), (2, 0)),
                _lspec(l, (gw, D_MODEL), (4, 0)), _lspec(l, (gw, 1)),
                _lspec(l, (1, LANES)), _lspec(l, (1, MEM_HEAD_DIM)), mem, mem]
    vec = _lspec(l, (1, gw))
    in_specs += [_lspec(l, (CONV_WIDTH, gw)), vec, _lspec(l, (gw, gw)), vec, _lspec(l, (gw, gw)), vec, vec]
    operands = (g1, w_in, w_in, w_in, w_in_t, w_in_t, qg_col, kg, mqg, mk, mv) + tuple(lru)
    assert len(operands) == N_PROJ_PARAMS
    out_specs = [pl.BlockSpec((tm, gw), row), pl.BlockSpec((tm, gw), row), pl.BlockSpec((tm, gw), row),
                 pl.BlockSpec((None, None, gw, tm), fm), pl.BlockSpec((None, None, gw, tm), fm)]
    out_shape = [jax.ShapeDtypeStruct((t, gw), BF16), jax.ShapeDtypeStruct((t, gw), BF16),
                 jax.ShapeDtypeStruct((t, gw), BF16), jax.ShapeDtypeStruct((b, ns, gw, tm), BF16),
                 jax.ShapeDtypeStruct((b, ns, gw, tm), BF16)]
    assert len(out_specs) == N_PROJ_OUTS
    scratch = [pltpu.VMEM((tm + SUBLANES, gw), F32), pltpu.VMEM((SUBLANES, gw), F32),
               pltpu.VMEM((tm, gw), F32), pltpu.VMEM((tm, gw), F32), pltpu.VMEM((tm, gw), F32)]
    xspec = pl.BlockSpec((tm, D_MODEL), row)
    if moe is None:
        return pl.pallas_call(
            functools.partial(_in_proj_kernel, ns=ns), grid=(t // tm,), in_specs=[xspec] + in_specs,
            out_specs=out_specs, out_shape=out_shape, scratch_shapes=scratch,
            compiler_params=_params("arbitrary"), name="in_proj",
        )(x2, *operands)
    inv, info, ys = moe
    return pl.pallas_call(
        functools.partial(_combine_in_proj_kernel, ns=ns),
        grid_spec=pltpu.PrefetchScalarGridSpec(
            num_scalar_prefetch=1, grid=(t // tm,),
            in_specs=[xspec, pl.BlockSpec((tm, LANES), row), pl.BlockSpec(memory_space=pl.ANY)] + in_specs,
            out_specs=[xspec] + out_specs, scratch_shapes=scratch + _combine_scratch(tm)),
        out_shape=[jax.ShapeDtypeStruct((t, D_MODEL), F32)] + out_shape,
        compiler_params=_params("arbitrary"),
        name="combine_in_proj",
    )(inv, x2, info, ys, *operands)


def _combine_scratch(tm):
    return [pltpu.VMEM((2, tm // MOE_TILE * SORTED_ROWS, D_MODEL), BF16), pltpu.SemaphoreType.DMA((2,))]


ATTN_HEADS_PER_STEP = 2
ATTN_MIN_SUM = 2.0 ** -40


def _score_bound(dq_gain, dk_gain):
    qscale = DIFF_QK_DIM ** -0.5 * math.log2(math.e)
    return (1.02 * qscale * DIFF_QK_DIM) * jnp.max(jnp.abs(dq_gain), axis=-1) * jnp.max(jnp.abs(dk_gain), axis=-1)


def _diff_attn_kernel(lam_ref, qt_ref, k_ref, vt_ref, hg_ref, o_ref,
                      qs_ref, sa_ref, sb_ref, m_ref, l_ref, acc_ref, *, t, layer, out_scale):
    qi = pl.program_id(2)
    hd = DIFF_HEAD_DIM
    heads = range(ATTN_HEADS_PER_STEP)
    lo = lax.broadcasted_iota(jnp.int32, (hd, t), 0) < DIFF_QK_DIM
    for h in heads:
        q = qt_ref[h * hd:(h + 1) * hd, :]
        zero = jnp.zeros_like(q)
        qs_ref[h, :, 0:t] = jnp.where(lo, q, zero)
        qs_ref[h, :, t:2 * t] = jnp.where(lo, zero, q)

    def causal(x, fill):
        kpos = lax.broadcasted_iota(jnp.int32, (t, 2 * t), 0)
        c = lax.broadcasted_iota(jnp.int32, (t, 2 * t), 1)
        return jnp.where(kpos <= jnp.where(c >= t, c - t, c), x, fill)

    def score(j, h):
        off = pl.multiple_of(j * t, t)
        return jnp.dot(k_ref[pl.ds(off, t), h * hd:(h + 1) * hd], qs_ref[h],
                       preferred_element_type=F32)

    bound = lam_ref[layer, 1]
    l_ref[...] = jnp.zeros(l_ref.shape, F32)
    acc_ref[...] = jnp.zeros(acc_ref.shape, F32)

    def fast_update(j, masked):
        for h in heads:
            p = jnp.exp2(score(j, h) - bound)
            if masked:
                p = causal(p, 0.0)
            l_ref[h] += jnp.sum(p, axis=0, keepdims=True)
            acc_ref[h] += jnp.dot(vt_ref[j, h * hd:(h + 1) * hd, :], p.astype(BF16),
                                  preferred_element_type=F32)

    def fast_pair(i, carry):
        fast_update(2 * i, False)
        fast_update(2 * i + 1, False)
        return carry

    lax.fori_loop(0, qi >> 1, fast_pair, 0)

    @pl.when((qi & 1) == 1)
    def _():
        fast_update(qi - 1, False)

    fast_update(qi, True)

    @pl.when(jnp.min(l_ref[...]) < ATTN_MIN_SUM)
    def _():
        m_ref[...] = jnp.full(m_ref.shape, NEG_BIG, F32)
        l_ref[...] = jnp.zeros(l_ref.shape, F32)
        acc_ref[...] = jnp.zeros(acc_ref.shape, F32)

        def scores(j, dst_ref):
            for h in heads:
                dst_ref[h] = score(j, h)

        def update(j, src_ref, masked):
            for h in heads:
                s = src_ref[h]
                if masked:
                    s = causal(s, NEG_BIG)
                m = m_ref[h]
                m_new = jnp.maximum(m, jnp.max(s, axis=0, keepdims=True))
                alpha = jnp.exp2(m - m_new)
                p = jnp.exp2(s - m_new)
                m_ref[h] = m_new
                l_ref[h] = alpha * l_ref[h] + jnp.sum(p, axis=0, keepdims=True)
                vt = vt_ref[j, h * hd:(h + 1) * hd, :]
                acc_ref[h] = alpha * acc_ref[h] + jnp.dot(vt, p.astype(BF16), preferred_element_type=F32)

        scores(0, sa_ref)

        def pair(i, carry):
            scores(2 * i + 1, sb_ref)
            update(2 * i, sa_ref, False)
            scores(2 * i + 2, sa_ref)
            update(2 * i + 1, sb_ref, False)
            return carry

        lax.fori_loop(0, qi >> 1, pair, 0)

        @pl.when((qi & 1) == 1)
        def _():
            scores(qi, sb_ref)
            update(qi - 1, sa_ref, False)
            update(qi, sb_ref, True)

        @pl.when((qi & 1) == 0)
        def _():
            update(qi, sa_ref, True)

    for h in heads:
        o = acc_ref[h] / l_ref[h]
        o = o[:, 0:t] - lam_ref[layer, 0] * o[:, t:2 * t]
        o = o * lax.rsqrt(jnp.mean(o * o, axis=0, keepdims=True) + EPS) * (hg_ref[...] * out_scale)
        o_ref[:, h * hd:(h + 1) * hd] = o.T.astype(BF16)


def _diff_attn(qt, k, vt, lam, hg_col, b, s, l, lam_init):
    t = SEQ_TILE
    nq = s // t
    hp = ATTN_HEADS_PER_STEP
    w = hp * DIFF_HEAD_DIM
    ng = DIFF_HEADS // hp
    kern = functools.partial(_diff_attn_kernel, t=t, layer=l, out_scale=1.0 - lam_init)
    return pl.pallas_call(
        kern,
        grid=(b, ng, nq),
        in_specs=[pl.BlockSpec(memory_space=pltpu.SMEM),
                  pl.BlockSpec((None, None, w, t), lambda bi, g, qi: (bi, qi, g, 0)),
                  pl.BlockSpec((s, w), lambda bi, g, qi: (bi, g)),
                  pl.BlockSpec((None, nq, w, t), lambda bi, g, qi: (bi, 0, g, 0)),
                  _lspec(l, (DIFF_HEAD_DIM, 1))],
        out_specs=pl.BlockSpec((t, w), lambda bi, g, qi: (bi * nq + qi, g)),
        out_shape=jax.ShapeDtypeStruct((b * s, GROUP_WIDTH), BF16),
        scratch_shapes=[pltpu.VMEM((hp, DIFF_HEAD_DIM, 2 * t), BF16),
                        pltpu.VMEM((hp, t, 2 * t), F32), pltpu.VMEM((hp, t, 2 * t), F32),
                        pltpu.VMEM((hp, 1, 2 * t), F32), pltpu.VMEM((hp, 1, 2 * t), F32),
                        pltpu.VMEM((hp, DIFF_HEAD_DIM, 2 * t), F32)],
        compiler_params=_params("arbitrary", "arbitrary", "arbitrary"),
        name="diff_attn",
    )(lam, qt, k, vt, hg_col)


def _mem_kv_kernel(mem_ref, mg_ref, w_ref, kg_ref, mk_ref, mv_ref):
    memn = _rms_rows(mem_ref[...], mg_ref[...]).astype(BF16)
    gw = GROUP_WIDTH
    k = jnp.dot(memn, w_ref[:, 0:gw], preferred_element_type=F32)
    for h in range(MEM_HEADS):
        sl = slice(h * MEM_HEAD_DIM, (h + 1) * MEM_HEAD_DIM)
        mk_ref[:, sl] = _rms_rows(k[:, sl], kg_ref[...]).astype(BF16)
    mv_ref[...] = jnp.dot(memn, w_ref[:, gw:2 * gw], preferred_element_type=F32).astype(BF16)


def _mem_kv(mem2, mg, l, w_kv, kg):
    r = mem2.shape[0]
    tm = _tile(r, 512)
    row = lambda i: (i, 0)
    gw = GROUP_WIDTH
    return pl.pallas_call(
        _mem_kv_kernel,
        grid=(r // tm,),
        in_specs=[pl.BlockSpec((tm, D_MODEL), row), pl.BlockSpec((1, D_MODEL), lambda i: (0, 0)),
                  _lspec(l, (D_MODEL, 2 * gw)), _lspec(l, (1, MEM_HEAD_DIM))],
        out_specs=[pl.BlockSpec((tm, gw), row), pl.BlockSpec((tm, gw), row)],
        out_shape=[jax.ShapeDtypeStruct((r, gw), BF16), jax.ShapeDtypeStruct((r, gw), BF16)],
        compiler_params=_params("arbitrary"),
        name="mem_kv",
    )(mem2, mg, w_kv, kg)


MOE_TILE = 256
GRANULE = 16
TILE_SLOTS = MOE_TILE // GRANULE + N_GROUPS
SORTED_ROWS = TILE_SLOTS * GRANULE
STEP_GRANULES = 32
STEP_ROWS = STEP_GRANULES * GRANULE
XS_WIDTH = D_MODEL + LANES
INFO_POS_LANE = 0


def _route_t(lt):
    row_i = lax.broadcasted_iota(jnp.int32, lt.shape, 0)
    row = row_i.astype(F32)
    e0 = ROUTER_EXPERT_LANE0
    lg = jnp.where(row_i < N_GROUPS, lt, NEG_BIG)
    mg = jnp.max(lg, axis=0, keepdims=True)
    g_gate = 1.0 / jnp.sum(jnp.exp(lg - mg), axis=0, keepdims=True)
    g_idx = jnp.min(jnp.where(lg == mg, row, float(LANES)), axis=0, keepdims=True)
    row_group = ((row_i - e0) >> 3).astype(F32)
    sel = (row_i >= e0) & (row_i < e0 + N_EXPERTS) & (row_group == g_idx)
    le = jnp.where(sel, lt, NEG_BIG)
    m1 = jnp.max(le, axis=0, keepdims=True)
    se = jnp.sum(jnp.where(sel, jnp.exp(le - m1), 0.0), axis=0, keepdims=True)
    i1 = jnp.min(jnp.where(sel & (le == m1), row, float(LANES)), axis=0, keepdims=True)
    le2 = jnp.where(row == i1, NEG_BIG, le)
    m2 = jnp.max(le2, axis=0, keepdims=True)
    i2 = jnp.min(jnp.where(sel & (le2 == m2) & (row != i1), row, float(LANES)), axis=0, keepdims=True)
    p1 = 1.0 / se
    p2 = jnp.exp(m2 - m1) / se
    tot = p1 + p2
    w = jnp.where(row == i1, p1 / tot, jnp.where(row == i2, p2 / tot, 0.0))
    return g_gate * w, g_idx


def _split3(c):
    hi = c.astype(BF16).astype(F32)
    r1 = c - hi
    mid = r1.astype(BF16).astype(F32)
    lo = (r1 - mid).astype(BF16).astype(F32)
    return hi, mid, lo


def _sort_logits(x1, g2, rt_hi, rt_lo, rb_col):
    xn = _rms_rows(x1, g2)
    hi = xn.astype(BF16)
    lo = (xn - hi.astype(F32)).astype(BF16)
    logits_t = (lax.dot_general(rt_hi, hi, _NT, preferred_element_type=F32)
                + lax.dot_general(rt_hi, lo, _NT, preferred_element_type=F32)
                + lax.dot_general(rt_lo, hi, _NT, preferred_element_type=F32) + rb_col)
    return hi, logits_t


def _sort_rank(logits_t, earlier):
    tl = logits_t.shape[1]
    comb_t, g_idx = _route_t(logits_t)
    grow = lax.broadcasted_iota(jnp.int32, (SUBLANES, tl), 0).astype(F32)
    gt = jnp.where(grow == g_idx, 1.0, 0.0)
    before = jnp.dot(gt.astype(BF16), earlier, preferred_element_type=F32)
    return comb_t, gt, before


def _sort_emit(hi, comb_t, gt, before):
    tl = hi.shape[0]
    rank = jnp.sum(gt * before, axis=0, keepdims=True)
    cnt = jnp.sum(gt, axis=1, keepdims=True)
    glen = jnp.floor((cnt + (GRANULE - 1)) * (1.0 / GRANULE))
    r8 = lax.broadcasted_iota(jnp.int32, (SUBLANES, 1), 0)
    start = jnp.zeros((SUBLANES, 1), F32)
    for g in range(1, N_GROUPS):
        start = jnp.where(r8 == g, jnp.sum(jnp.where(r8 < g, glen, 0.0), axis=0, keepdims=True), start)
    pos = jnp.sum(gt * (start * GRANULE), axis=0, keepdims=True) + rank

    rows = lax.broadcasted_iota(jnp.int32, (LANES, tl), 0)
    info = jnp.where(rows == INFO_POS_LANE, pos, comb_t).T
    lane = _lane_iota(info.shape)
    e0 = ROUTER_EXPERT_LANE0
    c_hi, c_mid, c_lo = _split3(jnp.where((lane >= e0) & (lane < e0 + N_EXPERTS), info, 0.0))
    aug = (c_hi + pltpu.roll(c_mid, N_EXPERTS, axis=1) + pltpu.roll(c_lo, 2 * N_EXPERTS, axis=1)).astype(BF16)
    perm = jnp.where(pos == lax.broadcasted_iota(jnp.int32, (SORTED_ROWS, tl), 0).astype(F32),
                     1.0, 0.0).astype(BF16)
    xs = jnp.dot(perm, hi, preferred_element_type=F32).astype(BF16)
    xs_aug = jnp.dot(perm, aug, preferred_element_type=F32).astype(BF16)
    return xs, xs_aug, info, glen


def _out_sort_kernel(x_ref, yl_ref, yd_ref, ym_ref, w_ref, g2_ref, rthi_ref, rtlo_ref, rb_ref, earlier_ref,
                     o_ref, xs_ref, info_ref, lens_ref, x1s_ref):
    @pl.when(pl.program_id(0) == 0)
    def _():
        x1s_ref[...] = jnp.zeros_like(x1s_ref)

    tm = x1s_ref.shape[0]
    subs = range(tm // MOE_TILE)

    def project(c, nchunks):
        r = slice(c * (tm // nchunks), (c + 1) * (tm // nchunks))
        y = jnp.concatenate([yl_ref[r, :], yd_ref[r, :], ym_ref[r, :]], axis=-1)
        x1 = x_ref[r, :] + jnp.dot(y, w_ref[...], preferred_element_type=F32)
        o_ref[r, :] = x1
        return r, x1

    fresh = [project(0, 2)]
    s1 = [_sort_logits(x1s_ref[sub * MOE_TILE:(sub + 1) * MOE_TILE, :], g2_ref[...], rthi_ref[...],
                       rtlo_ref[...], rb_ref[...]) for sub in subs]
    for r, x1 in fresh:
        x1s_ref[r, :] = x1
    fresh = [project(1, 2)]
    s2 = [_sort_rank(s1[sub][1], earlier_ref[...]) for sub in subs]
    for r, x1 in fresh:
        x1s_ref[r, :] = x1
    for sub in subs:
        xs, xs_aug, info, glen = _sort_emit(s1[sub][0], *s2[sub])
        r0 = sub * SORTED_ROWS
        xs_ref[r0:r0 + SORTED_ROWS, 0:D_MODEL] = xs
        xs_ref[r0:r0 + SORTED_ROWS, D_MODEL:XS_WIDTH] = xs_aug
        info_ref[sub * MOE_TILE:(sub + 1) * MOE_TILE, :] = info
        lens_ref[sub] = jnp.broadcast_to(glen, (SUBLANES, LANES))


def _out_sort(x2, y_lru, y_diff, y_mem, l, w_out, g2, rt_hi, rt_lo, rb_col, earlier):
    t = x2.shape[0]
    tm = SEQ_TILE
    assert t % tm == 0 and tm % MOE_TILE == 0
    sub = tm // MOE_TILE
    nt = t // MOE_TILE
    n = t // tm
    cur = lambda i: (jnp.minimum(i, n - 1), 0)
    prev = lambda i: (jnp.maximum(i - 1, 0), 0)
    gw = GROUP_WIDTH
    return pl.pallas_call(
        _out_sort_kernel,
        grid=(n + 1,),
        in_specs=[pl.BlockSpec((tm, D_MODEL), cur), pl.BlockSpec((tm, gw), cur), pl.BlockSpec((tm, gw), cur),
                  pl.BlockSpec((tm, gw), cur), _lspec(l, (3 * gw, D_MODEL)),
                  _lspec(l, (1, D_MODEL)), _lspec(l, (LANES, D_MODEL)),
                  _lspec(l, (LANES, D_MODEL)), _lspec(l, (LANES, 1)),
                  pl.BlockSpec((MOE_TILE, MOE_TILE), lambda i: (0, 0))],
        out_specs=[pl.BlockSpec((tm, D_MODEL), cur), pl.BlockSpec((sub * SORTED_ROWS, XS_WIDTH), prev),
                   pl.BlockSpec((tm, LANES), prev),
                   pl.BlockSpec((sub, SUBLANES, LANES), lambda i: (jnp.maximum(i - 1, 0), 0, 0))],
        out_shape=[jax.ShapeDtypeStruct((t, D_MODEL), F32),
                   jax.ShapeDtypeStruct((nt * SORTED_ROWS, XS_WIDTH), BF16),
                   jax.ShapeDtypeStruct((t, LANES), F32),
                   jax.ShapeDtypeStruct((nt, SUBLANES, LANES), F32)],
        scratch_shapes=[pltpu.VMEM((tm, D_MODEL), F32)],
        compiler_params=_params("arbitrary"),
        name="out_proj_sort",
    )(x2, y_lru, y_diff, y_mem, w_out, g2, rt_hi, rt_lo, rb_col, earlier)


def _moe_expert_kernel(sg_ref, sv_ref, gi_ref, xs_hbm, wg_ref, wu_ref, wd_ref, y_ref, xbuf_ref, sem_ref):
    s = pl.program_id(0)
    last = pl.num_programs(0) - 1
    slot = s & 1

    def start_fetch(step, dst_slot):
        _granule_copies(gi_ref, step, STEP_GRANULES, xs_hbm, xbuf_ref, dst_slot, sem_ref, True)

    def wait_fetch(step, dst_slot):
        _granule_copies(gi_ref, step, STEP_GRANULES, xs_hbm, xbuf_ref, dst_slot, sem_ref, False)

    @pl.when(s == 0)
    def _():
        start_fetch(0, 0)

    @pl.when((s == 0) | (sv_ref[jnp.maximum(s - 1, 0)] != 0))
    def _():
        wait_fetch(s, slot)

    @pl.when(sv_ref[s] == 0)
    def _():
        y_ref[...] = jnp.zeros_like(y_ref)

    @pl.when(sv_ref[s] != 0)
    def _():
        nxt = jnp.minimum(s + 1, last)
        start_fetch(nxt, 1 - slot)
        rows = xbuf_ref[slot]
        x = rows[:, 0:D_MODEL]
        aug = rows[:, D_MODEL:XS_WIDTH].astype(F32)
        lane = _lane_iota(aug.shape)
        e0 = ROUTER_EXPERT_LANE0
        comb = jnp.where((lane >= e0) & (lane < e0 + N_EXPERTS),
                         aug + pltpu.roll(aug, LANES - N_EXPERTS, axis=1)
                         + pltpu.roll(aug, LANES - 2 * N_EXPERTS, axis=1), 0.0)
        first = e0 + sg_ref[s] * EXPERTS_PER_GROUP
        hs = []
        for e in range(EXPERTS_PER_GROUP):
            cw = jnp.sum(jnp.where(lane == first + e, comb, 0.0), axis=-1, keepdims=True)
            hg = jnp.dot(x, wg_ref[e], preferred_element_type=F32)
            hu = jnp.dot(x, wu_ref[e], preferred_element_type=F32)
            hs.append((jax.nn.silu(hg) * hu * cw).astype(BF16))
        h = jnp.concatenate(hs, axis=-1)
        y_ref[...] = jnp.dot(h, wd_ref[...], preferred_element_type=F32).astype(BF16)

        @pl.when(s == last)
        def _():
            wait_fetch(nxt, 1 - slot)


def _moe_experts(step_group, step_valid, gran_idx, xs, l, wg, wu, wd):
    nstep = step_group.shape[0]
    epg = EXPERTS_PER_GROUP
    wmap4 = lambda s, sg, sv, gi: (l * N_GROUPS + sg[s], 0, 0, 0)
    in_specs = [pl.BlockSpec(memory_space=pl.ANY),
                pl.BlockSpec((None, epg, D_MODEL, D_EXPERT), wmap4),
                pl.BlockSpec((None, epg, D_MODEL, D_EXPERT), wmap4),
                pl.BlockSpec((None, epg * D_EXPERT, D_MODEL), lambda s, sg, sv, gi: (l * N_GROUPS + sg[s], 0, 0))]
    return pl.pallas_call(
        _moe_expert_kernel,
        grid_spec=pltpu.PrefetchScalarGridSpec(
            num_scalar_prefetch=3, grid=(nstep,), in_specs=in_specs,
            out_specs=pl.BlockSpec((STEP_ROWS, D_MODEL), lambda s, sg, sv, gi: (s, 0)),
            scratch_shapes=[pltpu.VMEM((2, STEP_ROWS, XS_WIDTH), BF16), pltpu.SemaphoreType.DMA((2,))]),
        out_shape=jax.ShapeDtypeStruct((nstep * STEP_ROWS, D_MODEL), BF16),
        compiler_params=_params("arbitrary"),
        name="moe_experts",
    )(step_group, step_valid, gran_idx, xs, wg, wu, wd)


def _moe_combine_kernel(inv_ref, x_ref, info_ref, ys_hbm, o_ref, ybuf_ref, sem_ref):
    o_ref[...] = _combined_tile(inv_ref, x_ref, info_ref, ys_hbm, ybuf_ref, sem_ref)


def _moe_combine(inv, x2, info, ys):
    t = x2.shape[0]
    tm = SEQ_TILE
    row = lambda i, inv: (i, 0)
    return pl.pallas_call(
        _moe_combine_kernel,
        grid_spec=pltpu.PrefetchScalarGridSpec(
            num_scalar_prefetch=1, grid=(t // tm,),
            in_specs=[pl.BlockSpec((tm, D_MODEL), row), pl.BlockSpec((tm, LANES), row),
                      pl.BlockSpec(memory_space=pl.ANY)],
            out_specs=pl.BlockSpec((tm, D_MODEL), row), scratch_shapes=_combine_scratch(tm)),
        out_shape=jax.ShapeDtypeStruct((t, D_MODEL), F32),
        compiler_params=_params("arbitrary"),
        name="moe_combine",
    )(inv, x2, info, ys)


def _moe_tables(lens, nt):
    i32 = jnp.int32
    ng = N_GROUPS
    garange = jnp.arange(ng, dtype=i32)
    cum = jnp.cumsum(lens, axis=1)
    start = cum - lens
    base = jnp.arange(nt, dtype=i32)[:, None] * TILE_SLOTS + start
    run_len = lens.T.reshape(-1)
    run_base = base.T.reshape(-1)
    run_end = jnp.cumsum(run_len)
    run_start = run_end - run_len
    n_g = jnp.sum(lens, axis=0)
    steps_g = (n_g + STEP_GRANULES - 1) // STEP_GRANULES
    step_end = jnp.cumsum(steps_g)
    step_off = step_end - steps_g
    gran_off = jnp.cumsum(n_g) - n_g
    nstep = (nt * (TILE_SLOTS - 1) + STEP_GRANULES - 1) // STEP_GRANULES + ng
    s = jnp.arange(nstep, dtype=i32)
    sg = jnp.minimum(jnp.sum((s[:, None] >= step_end[None, :]).astype(i32), axis=-1), ng - 1)
    goh = sg[:, None] == garange[None, :]
    pick = lambda v: jnp.sum(jnp.where(goh, v[None, :], 0), axis=-1)
    sv = s < step_end[-1]
    jl = (s - pick(step_off))[:, None] * STEP_GRANULES + jnp.arange(STEP_GRANULES, dtype=i32)[None, :]
    ok = sv[:, None] & (jl < pick(n_g)[:, None])
    j = jnp.where(ok, pick(gran_off)[:, None] + jl, 0)
    inrun = (j[..., None] >= run_start) & (j[..., None] < run_end)
    gran = j + jnp.sum(jnp.where(inrun, run_base - run_start, 0), axis=-1)
    gran = jnp.where(ok, gran, jnp.where(sv[:, None], gran[:, 0:1], 0)).astype(i32)
    q = jnp.arange(TILE_SLOTS, dtype=i32)[None, :]
    gq = jnp.minimum(jnp.sum((q[:, :, None] >= cum[:, None, :]).astype(i32), axis=-1), ng - 1)
    qoh = gq[..., None] == garange
    used = q < cum[:, -1:]
    per_g = run_start.reshape(ng, nt).T - start + (step_off * STEP_GRANULES - gran_off)[None, :]
    inv = jnp.where(used, q + jnp.sum(jnp.where(qoh, per_g[:, None, :], 0), axis=-1), 0).astype(i32)
    return sg, sv.astype(i32), gran.reshape(-1), inv.reshape(-1)


def _out_proj_moe(x2, y_lru, y_diff, y_mem, l, w_out, g2, rt_hi, rt_lo, rb_col, earlier, wg, wu, wd):
    nt = x2.shape[0] // MOE_TILE
    x1, xs, info, lens = _out_sort(x2, y_lru, y_diff, y_mem, l, w_out, g2, rt_hi, rt_lo, rb_col, earlier)
    sg, sv, gran, inv = _moe_tables(lens[:, 0:N_GROUPS, 0].astype(jnp.int32), nt)
    ys = _moe_experts(sg, sv, gran, xs, l, wg, wu, wd)
    return x1, (inv, info, ys)


def _block_diag(w):
    depth, h, n, _ = w.shape
    eye = jnp.eye(h, dtype=w.dtype)
    return (eye[None, :, None, :, None] * w[:, :, :, None, :]).reshape(depth, h * n, h * n)


def _router_tables(w_rg, b_rg, w_re, b_re):
    depth = w_rg.shape[0]
    e0 = ROUTER_EXPERT_LANE0
    pad = lambda rows, width: jnp.zeros((depth, rows, width), F32)
    w = jnp.concatenate([jnp.swapaxes(w_rg, 1, 2), pad(e0 - N_GROUPS, D_MODEL), jnp.swapaxes(w_re, 1, 2),
                         pad(LANES - e0 - N_EXPERTS, D_MODEL)], axis=1)
    bias = jnp.concatenate([b_rg[:, :, None], pad(e0 - N_GROUPS, 1), b_re[:, :, None],
                            pad(LANES - e0 - N_EXPERTS, 1)], axis=1)
    hi = w.astype(BF16)
    lo = (w - hi.astype(F32)).astype(BF16)
    return hi, lo, bias


def kernel(x, mem, norm1_g, w_in, conv_w, conv_b, rg_wa, rg_ba, rg_wx, rg_bx, rg_lambda, dq_norm_g, dk_norm_g, lambda_q1, lambda_k1, lambda_q2, lambda_k2, diff_head_norm_g, mem_norm_g, w_mem_kv, mq_norm_g, mk_norm_g, w_out, norm2_g, w_router_group, b_router_group, w_router_expert, b_router_expert, w_expert_gate, w_expert_up, w_expert_down):
    b, s, d = x.shape
    m = mem.shape[1]
    depth = w_in.shape[0]
    gw = GROUP_WIDTH
    epg = EXPERTS_PER_GROUP
    x2 = x.reshape(b * s, d)
    mem2 = mem.reshape(b * m, d)
    row = lambda v: v.reshape(depth, 1, -1).astype(F32)
    col = lambda v: v.reshape(depth, -1, 1).astype(F32)
    lam_inits = [0.8 - 0.6 * math.exp(-0.3 * l) for l in range(depth)]
    lam = (jnp.exp(jnp.sum(lambda_q1 * lambda_k1, axis=-1)) - jnp.exp(jnp.sum(lambda_q2 * lambda_k2, axis=-1))
           + jnp.asarray(lam_inits, F32))
    lam = jnp.stack([lam, _score_bound(dq_norm_g, dk_norm_g)], axis=1).astype(F32)
    w_in_bf = w_in.astype(BF16)
    w_in_t = jnp.swapaxes(w_in_bf, 1, 2)
    qg_col = col(jnp.tile(dq_norm_g, (1, gw // DIFF_QK_DIM)))
    kg = row(jnp.tile(dk_norm_g, (1, 2)))
    wa_bd = _block_diag(rg_wa).astype(BF16)
    wx_bd = _block_diag(rg_wx).astype(BF16)
    w_kv_bf = w_mem_kv.astype(BF16)
    w_out_bf = w_out.astype(BF16)
    rt_hi, rt_lo, rb_col = _router_tables(w_router_group, b_router_group, w_router_expert, b_router_expert)
    earlier = jnp.triu(jnp.ones((MOE_TILE, MOE_TILE), BF16), k=1)
    wg = w_expert_gate.astype(BF16).reshape(depth * N_GROUPS, epg, D_MODEL, D_EXPERT)
    wu = w_expert_up.astype(BF16).reshape(depth * N_GROUPS, epg, D_MODEL, D_EXPERT)
    wd = w_expert_down.astype(BF16).reshape(depth * N_GROUPS, epg * D_EXPERT, D_MODEL)
    g1, g2, mqg, mkg = row(norm1_g), row(norm2_g), row(mq_norm_g), row(mk_norm_g)
    conv_b3, ba, bx, lru_lam, hg_col = row(conv_b), row(rg_ba), row(rg_bx), row(rg_lambda), col(diff_head_norm_g)
    mem_g = mem_norm_g.reshape(1, -1).astype(F32)
    moe = None
    for l in range(depth):
        mk, mv = _mem_kv(mem2, mem_g, l, w_kv_bf, mkg)
        proj = _in_proj(x2, b, s, l, g1, w_in_bf, w_in_t, qg_col, kg, mqg, mk, mv,
                        (conv_w, conv_b3, wa_bd, ba, wx_bd, bx, lru_lam), moe)
        if moe is not None:
            x2, *proj = proj
        y_lru, k, y_mem, qt, vt = proj
        y_diff = _diff_attn(qt, k, vt, lam, hg_col, b, s, l, lam_inits[l])
        x2, moe = _out_proj_moe(x2, y_lru, y_diff, y_mem, l, w_out_bf, g2, rt_hi, rt_lo, rb_col, earlier,
                                wg, wu, wd)
    return _moe_combine(moe[0], x2, moe[1], moe[2]).reshape(b, s, d)
```

```python
import functools
import math

import jax
import jax.numpy as jnp
from jax import lax
from jax.experimental import pallas as pl
from jax.experimental.pallas import tpu as pltpu

F32 = jnp.float32
BF16 = jnp.bfloat16

D_MODEL = 1024
GROUP_WIDTH = D_MODEL // 2
LRU_HEADS = 8
CONV_WIDTH = 4
LRU_C = 8.0
DIFF_HEADS = 4
DIFF_HEAD_DIM = GROUP_WIDTH // DIFF_HEADS
DIFF_QK_DIM = DIFF_HEAD_DIM // 2
MEM_HEADS = 4
MEM_HEAD_DIM = GROUP_WIDTH // MEM_HEADS
N_GROUPS = 4
EXPERTS_PER_GROUP = 8
N_EXPERTS = N_GROUPS * EXPERTS_PER_GROUP
D_EXPERT = 256
D_IN_PROJ = 6 * GROUP_WIDTH
EPS = 1e-6

LANES = 128
SUBLANES = 8
VMEM_LIMIT = 56 * 1024 * 1024
NEG_BIG = -1e30
ROUTER_EXPERT_LANE0 = 32


def _tile(n, pref):
    t = min(n, pref)
    assert n % t == 0, (n, t)
    return t


def _params(*sem):
    return pltpu.CompilerParams(dimension_semantics=sem, vmem_limit_bytes=VMEM_LIMIT)


def _lspec(l, tail, blk=None):
    idx = (l,) + tuple(blk if blk is not None else (0,) * len(tail))
    return pl.BlockSpec((None,) + tuple(tail), lambda *_: idx)


def _rms_rows(x, g):
    return x * lax.rsqrt(jnp.mean(x * x, axis=-1, keepdims=True) + EPS) * g


def _lane_iota(shape):
    return lax.broadcasted_iota(jnp.int32, shape, len(shape) - 1)


def _half_head_norm(z, g):
    sq = z * z
    lo = _lane_iota(z.shape) < DIFF_QK_DIM
    s_all = jnp.sum(sq, axis=-1, keepdims=True)
    s_lo = jnp.sum(jnp.where(lo, sq, 0.0), axis=-1, keepdims=True)
    inv_lo = lax.rsqrt(s_lo * (1.0 / DIFF_QK_DIM) + EPS)
    inv_hi = lax.rsqrt((s_all - s_lo) * (1.0 / DIFF_QK_DIM) + EPS)
    return z * jnp.where(lo, inv_lo, inv_hi) * g


SEQ_TILE = 512
_NT = (((1,), (1,)), ((), ()))


N_PROJ_PARAMS = 18
N_PROJ_OUTS = 5


def _softplus(z):
    return jnp.maximum(z, 0.0) + jnp.log(1.0 + jnp.exp(-jnp.abs(z)))


def _in_proj_body(x, ns, g1_ref, wl_ref, wk_ref, wmq_ref, wqt_ref, wvt_ref, qg_ref, kg_ref, mqg_ref,
                  mk_ref, mv_ref, cw_ref, cb_ref, wa_ref, ba_ref, wx_ref, bx_ref, llam_ref,
                  ylru_ref, k_ref, ym_ref, qt_ref, vt_ref, ext_ref, hcar_ref, a_ref, b_ref, h_ref):
    gw = GROUP_WIDTH
    tm = x.shape[0]
    hist = SUBLANES
    first = lax.rem(pl.program_id(0), ns) == 0

    @pl.when(first)
    def _():
        ext_ref[0:hist, :] = jnp.zeros((hist, gw), F32)
        hcar_ref[...] = jnp.zeros_like(hcar_ref)

    @pl.when(jnp.logical_not(first))
    def _():
        ext_ref[0:hist, :] = ext_ref[tm:tm + hist, :]

    xn = _rms_rows(x, g1_ref[...]).astype(BF16)
    heads = range(MEM_HEADS)
    hsl = [slice(h * MEM_HEAD_DIM, (h + 1) * MEM_HEAD_DIM) for h in heads]
    mq = jnp.dot(xn, wmq_ref[...], preferred_element_type=F32)
    u = jnp.dot(xn, wl_ref[...], preferred_element_type=F32)
    ext_ref[hist:hist + tm, :] = u[:, 0:gw]
    xc = cb_ref[...] + cw_ref[CONV_WIDTH - 1:CONV_WIDTH, :] * u[:, 0:gw]
    for j in range(CONV_WIDTH - 1):
        off = hist - (CONV_WIDTH - 1) + j
        xc = xc + cw_ref[j:j + 1, :] * ext_ref[off:off + tm, :]
    xcb = xc.astype(BF16)
    sc = [lax.dot_general(_rms_rows(mq[:, hsl[h]], mqg_ref[...]).astype(BF16), mk_ref[:, hsl[h]], _NT,
                          preferred_element_type=F32) * MEM_HEAD_DIM ** -0.5 for h in heads]
    k = jnp.dot(xn, wk_ref[...], preferred_element_type=F32)
    r_pre = jnp.dot(xcb, wa_ref[...], preferred_element_type=F32)
    i_pre = jnp.dot(xcb, wx_ref[...], preferred_element_type=F32)
    p = [jnp.exp(sc[h] - jnp.max(sc[h], axis=-1, keepdims=True)) for h in heads]
    qt = lax.dot_general(wqt_ref[...], xn, _NT, preferred_element_type=F32)
    o = [jnp.dot(p[h].astype(BF16), mv_ref[:, hsl[h]], preferred_element_type=F32) for h in heads]
    vt_ref[...] = lax.dot_general(wvt_ref[...], xn, _NT, preferred_element_type=F32).astype(BF16)

    r = jax.nn.sigmoid(r_pre + ba_ref[...])
    gate_i = jax.nn.sigmoid(i_pre + bx_ref[...])
    a = jnp.exp((-LRU_C * r) * _softplus(-llam_ref[...]))
    om = 1.0 - a * a
    b = om * lax.rsqrt(jnp.maximum(om, 1e-30)) * (gate_i * xc)
    a = a.reshape(tm // SUBLANES, SUBLANES, gw)
    b = b.reshape(tm // SUBLANES, SUBLANES, gw)
    row = lax.broadcasted_iota(jnp.int32, a.shape, 1)
    d = 1
    while d < SUBLANES:
        keep = row >= d
        a_prev = pltpu.roll(a, d, axis=1)
        b_prev = pltpu.roll(b, d, axis=1)
        b = jnp.where(keep, a * b_prev + b, b)
        a = jnp.where(keep, a * a_prev, a)
        d *= 2
    a_ref[...] = a.reshape(tm, gw)
    b_ref[...] = b.reshape(tm, gw)
    gated = jax.nn.gelu(u[:, gw:2 * gw])

    for h in range(DIFF_HEADS):
        sl = slice(h * LANES, (h + 1) * LANES)
        k_ref[:, sl] = _half_head_norm(k[:, sl], kg_ref[...]).astype(BF16)
    for h in heads:
        ym_ref[:, hsl[h]] = (o[h] / jnp.sum(p[h], axis=-1, keepdims=True)).astype(BF16)
    q3 = qt.reshape(gw // DIFF_QK_DIM, DIFF_QK_DIM, tm)
    q3 = q3 * lax.rsqrt(jnp.mean(q3 * q3, axis=1, keepdims=True) + EPS)
    qscale = DIFF_QK_DIM ** -0.5 * math.log2(math.e)
    qt_ref[...] = (q3.reshape(gw, tm) * (qg_ref[...] * qscale)).astype(BF16)

    def block(i, hprev):
        off = pl.multiple_of(i * SUBLANES, SUBLANES)
        hb = a_ref[pl.ds(off, SUBLANES), :] * hprev + b_ref[pl.ds(off, SUBLANES), :]
        h_ref[pl.ds(off, SUBLANES), :] = hb
        return jnp.broadcast_to(hb[SUBLANES - 1:SUBLANES, :], (SUBLANES, gw))

    hcar_ref[...] = lax.fori_loop(0, tm // SUBLANES, block, hcar_ref[...], unroll=8)
    ylru_ref[...] = (h_ref[...] * gated).astype(BF16)


def _in_proj_kernel(x_ref, *refs, ns):
    _in_proj_body(x_ref[...], ns, *refs)


def _granule_copies(tab_ref, step, n, src_hbm, buf_ref, slot, sem_ref, start):
    for k in range(n):
        g = tab_ref[step * n + k]
        cp = pltpu.make_async_copy(
            src_hbm.at[pl.ds(pl.multiple_of(g * GRANULE, GRANULE), GRANULE), :],
            buf_ref.at[slot, pl.ds(k * GRANULE, GRANULE), :], sem_ref.at[slot])
        if start:
            cp.start()
        else:
            cp.wait()


def _unsort(info, ys):
    kpad = -SORTED_ROWS % LANES
    ys = jnp.concatenate([ys, jnp.zeros((kpad, D_MODEL), BF16)], axis=0)
    pos = info[:, INFO_POS_LANE:INFO_POS_LANE + 1]
    unperm = jnp.where(pos == _lane_iota((MOE_TILE, SORTED_ROWS + kpad)).astype(F32), 1.0, 0.0).astype(BF16)
    return jnp.dot(unperm, ys, preferred_element_type=F32)


def _combined_tile(inv_ref, x1_ref, info_ref, ys_hbm, ybuf_ref, sem_ref):
    s = pl.program_id(0)
    last = pl.num_programs(0) - 1
    slot = s & 1
    nsub = x1_ref.shape[0] // MOE_TILE
    n = nsub * TILE_SLOTS

    @pl.when(s == 0)
    def _():
        _granule_copies(inv_ref, 0, n, ys_hbm, ybuf_ref, 0, sem_ref, True)

    _granule_copies(inv_ref, s, n, ys_hbm, ybuf_ref, slot, sem_ref, False)
    nxt = jnp.minimum(s + 1, last)
    _granule_copies(inv_ref, nxt, n, ys_hbm, ybuf_ref, 1 - slot, sem_ref, True)
    x = jnp.concatenate(
        [x1_ref[u * MOE_TILE:(u + 1) * MOE_TILE, :]
         + _unsort(info_ref[u * MOE_TILE:(u + 1) * MOE_TILE, :],
                   ybuf_ref[slot, u * SORTED_ROWS:(u + 1) * SORTED_ROWS, :])
         for u in range(nsub)], axis=0)

    @pl.when(s == last)
    def _():
        _granule_copies(inv_ref, nxt, n, ys_hbm, ybuf_ref, 1 - slot, sem_ref, False)

    return x


def _combine_in_proj_kernel(inv_ref, x1_ref, info_ref, ys_hbm, *refs, ns):
    params, x_out_ref, outs = refs[:N_PROJ_PARAMS], refs[N_PROJ_PARAMS], refs[N_PROJ_PARAMS + 1:-2]
    x = _combined_tile(inv_ref, x1_ref, info_ref, ys_hbm, *refs[-2:])
    x_out_ref[...] = x
    _in_proj_body(x, ns, *params, *outs)


def _in_proj(x2, b, s, l, g1, w_in, w_in_t, qg_col, kg, mqg, mk, mv, lru, moe=None):
    t = x2.shape[0]
    tm = SEQ_TILE
    assert s % tm == 0
    ns = s // tm
    m = mk.shape[0] // b
    gw = GROUP_WIDTH
    row = lambda i, *_: (i, 0)
    fm = lambda i, *_: (i // ns, i % ns, 0, 0)
    mem = pl.BlockSpec((m, gw), lambda i, *_: (i // ns, 0))
    in_specs = [_lspec(l, (1, D_MODEL)),
                _lspec(l, (D_MODEL, 2 * gw), (0, 0)), _lspec(l, (D_MODEL, gw), (0, 3)),
                _lspec(l, (D_MODEL, gw), (0, 5)), _lspec(l, (gw, D_MODEL), (2, 0)),
                _lspec(l, (gw, D_MODEL), (4, 0)), _lspec(l, (gw, 1)),
                _lspec(l, (1, LANES)), _lspec(l, (1, MEM_HEAD_DIM)), mem, mem]
    vec = _lspec(l, (1, gw))
    in_specs += [_lspec(l, (CONV_WIDTH, gw)), vec, _lspec(l, (gw, gw)), vec, _lspec(l, (gw, gw)), vec, vec]
    operands = (g1, w_in, w_in, w_in, w_in_t, w_in_t, qg_col, kg, mqg, mk, mv) + tuple(lru)
    assert len(operands) == N_PROJ_PARAMS
    out_specs = [pl.BlockSpec((tm, gw), row), pl.BlockSpec((tm, gw), row), pl.BlockSpec((tm, gw), row),
                 pl.BlockSpec((None, None, gw, tm), fm), pl.BlockSpec((None, None, gw, tm), fm)]
    out_shape = [jax.ShapeDtypeStruct((t, gw), BF16), jax.ShapeDtypeStruct((t, gw), BF16),
                 jax.ShapeDtypeStruct((t, gw), BF16), jax.ShapeDtypeStruct((b, ns, gw, tm), BF16),
                 jax.ShapeDtypeStruct((b, ns, gw, tm), BF16)]
    assert len(out_specs) == N_PROJ_OUTS
    scratch = [pltpu.VMEM((tm + SUBLANES, gw), F32), pltpu.VMEM((SUBLANES, gw), F32),
               pltpu.VMEM((tm, gw), F32), pltpu.VMEM((tm, gw), F32), pltpu.VMEM((tm, gw), F32)]
    xspec = pl.BlockSpec((tm, D_MODEL), row)
    if moe is None:
        return pl.pallas_call(
            functools.partial(_in_proj_kernel, ns=ns), grid=(t // tm,), in_specs=[xspec] + in_specs,
            out_specs=out_specs, out_shape=out_shape, scratch_shapes=scratch,
            compiler_params=_params("arbitrary"), name="in_proj",
        )(x2, *operands)
    inv, info, ys = moe
    return pl.pallas_call(
        functools.partial(_combine_in_proj_kernel, ns=ns),
        grid_spec=pltpu.PrefetchScalarGridSpec(
            num_scalar_prefetch=1, grid=(t // tm,),
            in_specs=[xspec, pl.BlockSpec((tm, LANES), row), pl.BlockSpec(memory_space=pl.ANY)] + in_specs,
            out_specs=[xspec] + out_specs, scratch_shapes=scratch + _combine_scratch(tm)),
        out_shape=[jax.ShapeDtypeStruct((t, D_MODEL), F32)] + out_shape,
        compiler_params=_params("arbitrary"),
        name="combine_in_proj",
    )(inv, x2, info, ys, *operands)


def _combine_scratch(tm):
    return [pltpu.VMEM((2, tm // MOE_TILE * SORTED_ROWS, D_MODEL), BF16), pltpu.SemaphoreType.DMA((2,))]


ATTN_HEADS_PER_STEP = 4
ATTN_MIN_SUM = 2.0 ** -40


def _score_bound(dq_gain, dk_gain):
    qscale = DIFF_QK_DIM ** -0.5 * math.log2(math.e)
    return (1.02 * qscale * DIFF_QK_DIM) * jnp.max(jnp.abs(dq_gain), axis=-1) * jnp.max(jnp.abs(dk_gain), axis=-1)


def _diff_attn_kernel(lam_ref, qt_ref, k_ref, vt_ref, hg_ref, o_ref,
                      qs_ref, sa_ref, sb_ref, m_ref, l_ref, acc_ref, *, t, layer, out_scale):
    qi = pl.program_id(2)
    hd = DIFF_HEAD_DIM
    heads = range(ATTN_HEADS_PER_STEP)
    lo = lax.broadcasted_iota(jnp.int32, (hd, t), 0) < DIFF_QK_DIM
    for h in heads:
        q = qt_ref[h * hd:(h + 1) * hd, :]
        zero = jnp.zeros_like(q)
        qs_ref[h, :, 0:t] = jnp.where(lo, q, zero)
        qs_ref[h, :, t:2 * t] = jnp.where(lo, zero, q)

    def causal(x, fill):
        kpos = lax.broadcasted_iota(jnp.int32, (t, 2 * t), 0)
        c = lax.broadcasted_iota(jnp.int32, (t, 2 * t), 1)
        return jnp.where(kpos <= jnp.where(c >= t, c - t, c), x, fill)

    def score(j, h):
        off = pl.multiple_of(j * t, t)
        return jnp.dot(k_ref[pl.ds(off, t), h * hd:(h + 1) * hd], qs_ref[h],
                       preferred_element_type=F32)

    bound = lam_ref[layer, 1]
    l_ref[...] = jnp.zeros(l_ref.shape, F32)
    acc_ref[...] = jnp.zeros(acc_ref.shape, F32)

    def fast_update(j, masked):
        for h in heads:
            p = jnp.exp2(score(j, h) - bound)
            if masked:
                p = causal(p, 0.0)
            l_ref[h] += jnp.sum(p, axis=0, keepdims=True)
            acc_ref[h] += jnp.dot(vt_ref[j, h * hd:(h + 1) * hd, :], p.astype(BF16),
                                  preferred_element_type=F32)

    def fast_pair(i, carry):
        fast_update(2 * i, False)
        fast_update(2 * i + 1, False)
        return carry

    lax.fori_loop(0, qi >> 1, fast_pair, 0)

    @pl.when((qi & 1) == 1)
    def _():
        fast_update(qi - 1, False)

    fast_update(qi, True)

    @pl.when(jnp.min(l_ref[...]) < ATTN_MIN_SUM)
    def _():
        m_ref[...] = jnp.full(m_ref.shape, NEG_BIG, F32)
        l_ref[...] = jnp.zeros(l_ref.shape, F32)
        acc_ref[...] = jnp.zeros(acc_ref.shape, F32)

        def scores(j, dst_ref):
            for h in heads:
                dst_ref[h] = score(j, h)

        def update(j, src_ref, masked):
            for h in heads:
                s = src_ref[h]
                if masked:
                    s = causal(s, NEG_BIG)
                m = m_ref[h]
                m_new = jnp.maximum(m, jnp.max(s, axis=0, keepdims=True))
                alpha = jnp.exp2(m - m_new)
                p = jnp.exp2(s - m_new)
                m_ref[h] = m_new
                l_ref[h] = alpha * l_ref[h] + jnp.sum(p, axis=0, keepdims=True)
                vt = vt_ref[j, h * hd:(h + 1) * hd, :]
                acc_ref[h] = alpha * acc_ref[h] + jnp.dot(vt, p.astype(BF16), preferred_element_type=F32)

        scores(0, sa_ref)

        def pair(i, carry):
            scores(2 * i + 1, sb_ref)
            update(2 * i, sa_ref, False)
            scores(2 * i + 2, sa_ref)
            update(2 * i + 1, sb_ref, False)
            return carry

        lax.fori_loop(0, qi >> 1, pair, 0)

        @pl.when((qi & 1) == 1)
        def _():
            scores(qi, sb_ref)
            update(qi - 1, sa_ref, False)
            update(qi, sb_ref, True)

        @pl.when((qi & 1) == 0)
        def _():
            update(qi, sa_ref, True)

    for h in heads:
        o = acc_ref[h] / l_ref[h]
        o = o[:, 0:t] - lam_ref[layer, 0] * o[:, t:2 * t]
        o = o * lax.rsqrt(jnp.mean(o * o, axis=0, keepdims=True) + EPS) * (hg_ref[...] * out_scale)
        o_ref[:, h * hd:(h + 1) * hd] = o.T.astype(BF16)


def _diff_attn(qt, k, vt, lam, hg_col, b, s, l, lam_init):
    t = SEQ_TILE
    nq = s // t
    hp = ATTN_HEADS_PER_STEP
    w = hp * DIFF_HEAD_DIM
    ng = DIFF_HEADS // hp
    kern = functools.partial(_diff_attn_kernel, t=t, layer=l, out_scale=1.0 - lam_init)
    return pl.pallas_call(
        kern,
        grid=(b, ng, nq),
        in_specs=[pl.BlockSpec(memory_space=pltpu.SMEM),
                  pl.BlockSpec((None, None, w, t), lambda bi, g, qi: (bi, qi, g, 0)),
                  pl.BlockSpec((s, w), lambda bi, g, qi: (bi, g)),
                  pl.BlockSpec((None, nq, w, t), lambda bi, g, qi: (bi, 0, g, 0)),
                  _lspec(l, (DIFF_HEAD_DIM, 1))],
        out_specs=pl.BlockSpec((t, w), lambda bi, g, qi: (bi * nq + qi, g)),
        out_shape=jax.ShapeDtypeStruct((b * s, GROUP_WIDTH), BF16),
        scratch_shapes=[pltpu.VMEM((hp, DIFF_HEAD_DIM, 2 * t), BF16),
                        pltpu.VMEM((hp, t, 2 * t), F32), pltpu.VMEM((hp, t, 2 * t), F32),
                        pltpu.VMEM((hp, 1, 2 * t), F32), pltpu.VMEM((hp, 1, 2 * t), F32),
                        pltpu.VMEM((hp, DIFF_HEAD_DIM, 2 * t), F32)],
        compiler_params=_params("arbitrary", "arbitrary", "arbitrary"),
        name="diff_attn",
    )(lam, qt, k, vt, hg_col)


def _mem_kv_kernel(mem_ref, mg_ref, w_ref, kg_ref, mk_ref, mv_ref):
    memn = _rms_rows(mem_ref[...], mg_ref[...]).astype(BF16)
    gw = GROUP_WIDTH
    k = jnp.dot(memn, w_ref[:, 0:gw], preferred_element_type=F32)
    for h in range(MEM_HEADS):
        sl = slice(h * MEM_HEAD_DIM, (h + 1) * MEM_HEAD_DIM)
        mk_ref[:, sl] = _rms_rows(k[:, sl], kg_ref[...]).astype(BF16)
    mv_ref[...] = jnp.dot(memn, w_ref[:, gw:2 * gw], preferred_element_type=F32).astype(BF16)


def _mem_kv(mem2, mg, l, w_kv, kg):
    r = mem2.shape[0]
    tm = _tile(r, 512)
    row = lambda i: (i, 0)
    gw = GROUP_WIDTH
    return pl.pallas_call(
        _mem_kv_kernel,
        grid=(r // tm,),
        in_specs=[pl.BlockSpec((tm, D_MODEL), row), pl.BlockSpec((1, D_MODEL), lambda i: (0, 0)),
                  _lspec(l, (D_MODEL, 2 * gw)), _lspec(l, (1, MEM_HEAD_DIM))],
        out_specs=[pl.BlockSpec((tm, gw), row), pl.BlockSpec((tm, gw), row)],
        out_shape=[jax.ShapeDtypeStruct((r, gw), BF16), jax.ShapeDtypeStruct((r, gw), BF16)],
        compiler_params=_params("arbitrary"),
        name="mem_kv",
    )(mem2, mg, w_kv, kg)


MOE_TILE = 256
GRANULE = 16
TILE_SLOTS = MOE_TILE // GRANULE + N_GROUPS
SORTED_ROWS = TILE_SLOTS * GRANULE
STEP_GRANULES = 32
STEP_ROWS = STEP_GRANULES * GRANULE
XS_WIDTH = D_MODEL + LANES
INFO_POS_LANE = 0


def _route_t(lt):
    row_i = lax.broadcasted_iota(jnp.int32, lt.shape, 0)
    row = row_i.astype(F32)
    e0 = ROUTER_EXPERT_LANE0
    lg = jnp.where(row_i < N_GROUPS, lt, NEG_BIG)
    mg = jnp.max(lg, axis=0, keepdims=True)
    g_gate = 1.0 / jnp.sum(jnp.exp(lg - mg), axis=0, keepdims=True)
    g_idx = jnp.min(jnp.where(lg == mg, row, float(LANES)), axis=0, keepdims=True)
    row_group = ((row_i - e0) >> 3).astype(F32)
    sel = (row_i >= e0) & (row_i < e0 + N_EXPERTS) & (row_group == g_idx)
    le = jnp.where(sel, lt, NEG_BIG)
    m1 = jnp.max(le, axis=0, keepdims=True)
    se = jnp.sum(jnp.where(sel, jnp.exp(le - m1), 0.0), axis=0, keepdims=True)
    i1 = jnp.min(jnp.where(sel & (le == m1), row, float(LANES)), axis=0, keepdims=True)
    le2 = jnp.where(row == i1, NEG_BIG, le)
    m2 = jnp.max(le2, axis=0, keepdims=True)
    i2 = jnp.min(jnp.where(sel & (le2 == m2) & (row != i1), row, float(LANES)), axis=0, keepdims=True)
    p1 = 1.0 / se
    p2 = jnp.exp(m2 - m1) / se
    tot = p1 + p2
    w = jnp.where(row == i1, p1 / tot, jnp.where(row == i2, p2 / tot, 0.0))
    return g_gate * w, g_idx


def _split3(c):
    hi = c.astype(BF16).astype(F32)
    r1 = c - hi
    mid = r1.astype(BF16).astype(F32)
    lo = (r1 - mid).astype(BF16).astype(F32)
    return hi, mid, lo


def _sort_logits(x1, g2, rt_hi, rt_lo, rb_col):
    xn = _rms_rows(x1, g2)
    hi = xn.astype(BF16)
    lo = (xn - hi.astype(F32)).astype(BF16)
    logits_t = (lax.dot_general(rt_hi, hi, _NT, preferred_element_type=F32)
                + lax.dot_general(rt_hi, lo, _NT, preferred_element_type=F32)
                + lax.dot_general(rt_lo, hi, _NT, preferred_element_type=F32) + rb_col)
    return hi, logits_t


def _sort_rank(logits_t, earlier):
    tl = logits_t.shape[1]
    comb_t, g_idx = _route_t(logits_t)
    grow = lax.broadcasted_iota(jnp.int32, (SUBLANES, tl), 0).astype(F32)
    gt = jnp.where(grow == g_idx, 1.0, 0.0)
    before = jnp.dot(gt.astype(BF16), earlier, preferred_element_type=F32)
    return comb_t, gt, before


def _sort_emit(hi, comb_t, gt, before):
    tl = hi.shape[0]
    rank = jnp.sum(gt * before, axis=0, keepdims=True)
    cnt = jnp.sum(gt, axis=1, keepdims=True)
    glen = jnp.floor((cnt + (GRANULE - 1)) * (1.0 / GRANULE))
    r8 = lax.broadcasted_iota(jnp.int32, (SUBLANES, 1), 0)
    start = jnp.zeros((SUBLANES, 1), F32)
    for g in range(1, N_GROUPS):
        start = jnp.where(r8 == g, jnp.sum(jnp.where(r8 < g, glen, 0.0), axis=0, keepdims=True), start)
    pos = jnp.sum(gt * (start * GRANULE), axis=0, keepdims=True) + rank

    rows = lax.broadcasted_iota(jnp.int32, (LANES, tl), 0)
    info = jnp.where(rows == INFO_POS_LANE, pos, comb_t).T
    lane = _lane_iota(info.shape)
    e0 = ROUTER_EXPERT_LANE0
    c_hi, c_mid, c_lo = _split3(jnp.where((lane >= e0) & (lane < e0 + N_EXPERTS), info, 0.0))
    aug = (c_hi + pltpu.roll(c_mid, N_EXPERTS, axis=1) + pltpu.roll(c_lo, 2 * N_EXPERTS, axis=1)).astype(BF16)
    perm = jnp.where(pos == lax.broadcasted_iota(jnp.int32, (SORTED_ROWS, tl), 0).astype(F32),
                     1.0, 0.0).astype(BF16)
    xs = jnp.dot(perm, hi, preferred_element_type=F32).astype(BF16)
    xs_aug = jnp.dot(perm, aug, preferred_element_type=F32).astype(BF16)
    return xs, xs_aug, info, glen


def _out_sort_kernel(x_ref, yl_ref, yd_ref, ym_ref, w_ref, g2_ref, rthi_ref, rtlo_ref, rb_ref, earlier_ref,
                     o_ref, xs_ref, info_ref, lens_ref, x1s_ref):
    @pl.when(pl.program_id(0) == 0)
    def _():
        x1s_ref[...] = jnp.zeros_like(x1s_ref)

    tm = x1s_ref.shape[0]
    subs = range(tm // MOE_TILE)

    def project(c, nchunks):
        r = slice(c * (tm // nchunks), (c + 1) * (tm // nchunks))
        y = jnp.concatenate([yl_ref[r, :], yd_ref[r, :], ym_ref[r, :]], axis=-1)
        x1 = x_ref[r, :] + jnp.dot(y, w_ref[...], preferred_element_type=F32)
        o_ref[r, :] = x1
        return r, x1

    fresh = [project(0, 2)]
    s1 = [_sort_logits(x1s_ref[sub * MOE_TILE:(sub + 1) * MOE_TILE, :], g2_ref[...], rthi_ref[...],
                       rtlo_ref[...], rb_ref[...]) for sub in subs]
    for r, x1 in fresh:
        x1s_ref[r, :] = x1
    fresh = [project(1, 2)]
    s2 = [_sort_rank(s1[sub][1], earlier_ref[...]) for sub in subs]
    for r, x1 in fresh:
        x1s_ref[r, :] = x1
    for sub in subs:
        xs, xs_aug, info, glen = _sort_emit(s1[sub][0], *s2[sub])
        r0 = sub * SORTED_ROWS
        xs_ref[r0:r0 + SORTED_ROWS, 0:D_MODEL] = xs
        xs_ref[r0:r0 + SORTED_ROWS, D_MODEL:XS_WIDTH] = xs_aug
        info_ref[sub * MOE_TILE:(sub + 1) * MOE_TILE, :] = info
        lens_ref[sub] = jnp.broadcast_to(glen, (SUBLANES, LANES))


def _out_sort(x2, y_lru, y_diff, y_mem, l, w_out, g2, rt_hi, rt_lo, rb_col, earlier):
    t = x2.shape[0]
    tm = SEQ_TILE
    assert t % tm == 0 and tm % MOE_TILE == 0
    sub = tm // MOE_TILE
    nt = t // MOE_TILE
    n = t // tm
    cur = lambda i: (jnp.minimum(i, n - 1), 0)
    prev = lambda i: (jnp.maximum(i - 1, 0), 0)
    gw = GROUP_WIDTH
    return pl.pallas_call(
        _out_sort_kernel,
        grid=(n + 1,),
        in_specs=[pl.BlockSpec((tm, D_MODEL), cur), pl.BlockSpec((tm, gw), cur), pl.BlockSpec((tm, gw), cur),
                  pl.BlockSpec((tm, gw), cur), _lspec(l, (3 * gw, D_MODEL)),
                  _lspec(l, (1, D_MODEL)), _lspec(l, (LANES, D_MODEL)),
                  _lspec(l, (LANES, D_MODEL)), _lspec(l, (LANES, 1)),
                  pl.BlockSpec((MOE_TILE, MOE_TILE), lambda i: (0, 0))],
        out_specs=[pl.BlockSpec((tm, D_MODEL), cur), pl.BlockSpec((sub * SORTED_ROWS, XS_WIDTH), prev),
                   pl.BlockSpec((tm, LANES), prev),
                   pl.BlockSpec((sub, SUBLANES, LANES), lambda i: (jnp.maximum(i - 1, 0), 0, 0))],
        out_shape=[jax.ShapeDtypeStruct((t, D_MODEL), F32),
                   jax.ShapeDtypeStruct((nt * SORTED_ROWS, XS_WIDTH), BF16),
                   jax.ShapeDtypeStruct((t, LANES), F32),
                   jax.ShapeDtypeStruct((nt, SUBLANES, LANES), F32)],
        scratch_shapes=[pltpu.VMEM((tm, D_MODEL), F32)],
        compiler_params=_params("arbitrary"),
        name="out_proj_sort",
    )(x2, y_lru, y_diff, y_mem, w_out, g2, rt_hi, rt_lo, rb_col, earlier)


def _moe_expert_kernel(sg_ref, sv_ref, gi_ref, xs_hbm, wg_ref, wu_ref, wd_ref, y_ref, xbuf_ref, sem_ref):
    s = pl.program_id(0)
    last = pl.num_programs(0) - 1
    slot = s & 1

    def start_fetch(step, dst_slot):
        _granule_copies(gi_ref, step, STEP_GRANULES, xs_hbm, xbuf_ref, dst_slot, sem_ref, True)

    def wait_fetch(step, dst_slot):
        _granule_copies(gi_ref, step, STEP_GRANULES, xs_hbm, xbuf_ref, dst_slot, sem_ref, False)

    @pl.when(s == 0)
    def _():
        start_fetch(0, 0)

    @pl.when((s == 0) | (sv_ref[jnp.maximum(s - 1, 0)] != 0))
    def _():
        wait_fetch(s, slot)

    @pl.when(sv_ref[s] == 0)
    def _():
        y_ref[...] = jnp.zeros_like(y_ref)

    @pl.when(sv_ref[s] != 0)
    def _():
        nxt = jnp.minimum(s + 1, last)
        start_fetch(nxt, 1 - slot)
        rows = xbuf_ref[slot]
        x = rows[:, 0:D_MODEL]
        aug = rows[:, D_MODEL:XS_WIDTH].astype(F32)
        lane = _lane_iota(aug.shape)
        e0 = ROUTER_EXPERT_LANE0
        comb = jnp.where((lane >= e0) & (lane < e0 + N_EXPERTS),
                         aug + pltpu.roll(aug, LANES - N_EXPERTS, axis=1)
                         + pltpu.roll(aug, LANES - 2 * N_EXPERTS, axis=1), 0.0)
        first = e0 + sg_ref[s] * EXPERTS_PER_GROUP
        hs = []
        for e in range(EXPERTS_PER_GROUP):
            cw = jnp.sum(jnp.where(lane == first + e, comb, 0.0), axis=-1, keepdims=True)
            hg = jnp.dot(x, wg_ref[e], preferred_element_type=F32)
            hu = jnp.dot(x, wu_ref[e], preferred_element_type=F32)
            hs.append((jax.nn.silu(hg) * hu * cw).astype(BF16))
        h = jnp.concatenate(hs, axis=-1)
        y_ref[...] = jnp.dot(h, wd_ref[...], preferred_element_type=F32).astype(BF16)

        @pl.when(s == last)
        def _():
            wait_fetch(nxt, 1 - slot)


def _moe_experts(step_group, step_valid, gran_idx, xs, l, wg, wu, wd):
    nstep = step_group.shape[0]
    epg = EXPERTS_PER_GROUP
    wmap4 = lambda s, sg, sv, gi: (l * N_GROUPS + sg[s], 0, 0, 0)
    in_specs = [pl.BlockSpec(memory_space=pl.ANY),
                pl.BlockSpec((None, epg, D_MODEL, D_EXPERT), wmap4),
                pl.BlockSpec((None, epg, D_MODEL, D_EXPERT), wmap4),
                pl.BlockSpec((None, epg * D_EXPERT, D_MODEL), lambda s, sg, sv, gi: (l * N_GROUPS + sg[s], 0, 0))]
    return pl.pallas_call(
        _moe_expert_kernel,
        grid_spec=pltpu.PrefetchScalarGridSpec(
            num_scalar_prefetch=3, grid=(nstep,), in_specs=in_specs,
            out_specs=pl.BlockSpec((STEP_ROWS, D_MODEL), lambda s, sg, sv, gi: (s, 0)),
            scratch_shapes=[pltpu.VMEM((2, STEP_ROWS, XS_WIDTH), BF16), pltpu.SemaphoreType.DMA((2,))]),
        out_shape=jax.ShapeDtypeStruct((nstep * STEP_ROWS, D_MODEL), BF16),
        compiler_params=_params("arbitrary"),
        name="moe_experts",
    )(step_group, step_valid, gran_idx, xs, wg, wu, wd)


def _moe_combine_kernel(inv_ref, x_ref, info_ref, ys_hbm, o_ref, ybuf_ref, sem_ref):
    o_ref[...] = _combined_tile(inv_ref, x_ref, info_ref, ys_hbm, ybuf_ref, sem_ref)


def _moe_combine(inv, x2, info, ys):
    t = x2.shape[0]
    tm = SEQ_TILE
    row = lambda i, inv: (i, 0)
    return pl.pallas_call(
        _moe_combine_kernel,
        grid_spec=pltpu.PrefetchScalarGridSpec(
            num_scalar_prefetch=1, grid=(t // tm,),
            in_specs=[pl.BlockSpec((tm, D_MODEL), row), pl.BlockSpec((tm, LANES), row),
                      pl.BlockSpec(memory_space=pl.ANY)],
            out_specs=pl.BlockSpec((tm, D_MODEL), row), scratch_shapes=_combine_scratch(tm)),
        out_shape=jax.ShapeDtypeStruct((t, D_MODEL), F32),
        compiler_params=_params("arbitrary"),
        name="moe_combine",
    )(inv, x2, info, ys)


def _moe_tables(lens, nt):
    i32 = jnp.int32
    ng = N_GROUPS
    garange = jnp.arange(ng, dtype=i32)
    cum = jnp.cumsum(lens, axis=1)
    start = cum - lens
    base = jnp.arange(nt, dtype=i32)[:, None] * TILE_SLOTS + start
    run_len = lens.T.reshape(-1)
    run_base = base.T.reshape(-1)
    run_end = jnp.cumsum(run_len)
    run_start = run_end - run_len
    n_g = jnp.sum(lens, axis=0)
    steps_g = (n_g + STEP_GRANULES - 1) // STEP_GRANULES
    step_end = jnp.cumsum(steps_g)
    step_off = step_end - steps_g
    gran_off = jnp.cumsum(n_g) - n_g
    nstep = (nt * (TILE_SLOTS - 1) + STEP_GRANULES - 1) // STEP_GRANULES + ng
    s = jnp.arange(nstep, dtype=i32)
    sg = jnp.minimum(jnp.sum((s[:, None] >= step_end[None, :]).astype(i32), axis=-1), ng - 1)
    goh = sg[:, None] == garange[None, :]
    pick = lambda v: jnp.sum(jnp.where(goh, v[None, :], 0), axis=-1)
    sv = s < step_end[-1]
    jl = (s - pick(step_off))[:, None] * STEP_GRANULES + jnp.arange(STEP_GRANULES, dtype=i32)[None, :]
    ok = sv[:, None] & (jl < pick(n_g)[:, None])
    j = jnp.where(ok, pick(gran_off)[:, None] + jl, 0)
    inrun = (j[..., None] >= run_start) & (j[..., None] < run_end)
    gran = j + jnp.sum(jnp.where(inrun, run_base - run_start, 0), axis=-1)
    gran = jnp.where(ok, gran, jnp.where(sv[:, None], gran[:, 0:1], 0)).astype(i32)
    q = jnp.arange(TILE_SLOTS, dtype=i32)[None, :]
    gq = jnp.minimum(jnp.sum((q[:, :, None] >= cum[:, None, :]).astype(i32), axis=-1), ng - 1)
    qoh = gq[..., None] == garange
    used = q < cum[:, -1:]
    per_g = run_start.reshape(ng, nt).T - start + (step_off * STEP_GRANULES - gran_off)[None, :]
    inv = jnp.where(used, q + jnp.sum(jnp.where(qoh, per_g[:, None, :], 0), axis=-1), 0).astype(i32)
    return sg, sv.astype(i32), gran.reshape(-1), inv.reshape(-1)


def _out_proj_moe(x2, y_lru, y_diff, y_mem, l, w_out, g2, rt_hi, rt_lo, rb_col, earlier, wg, wu, wd):
    nt = x2.shape[0] // MOE_TILE
    x1, xs, info, lens = _out_sort(x2, y_lru, y_diff, y_mem, l, w_out, g2, rt_hi, rt_lo, rb_col, earlier)
    sg, sv, gran, inv = _moe_tables(lens[:, 0:N_GROUPS, 0].astype(jnp.int32), nt)
    ys = _moe_experts(sg, sv, gran, xs, l, wg, wu, wd)
    return x1, (inv, info, ys)


def _block_diag(w):
    depth, h, n, _ = w.shape
    eye = jnp.eye(h, dtype=w.dtype)
    return (eye[None, :, None, :, None] * w[:, :, :, None, :]).reshape(depth, h * n, h * n)


def _router_tables(w_rg, b_rg, w_re, b_re):
    depth = w_rg.shape[0]
    e0 = ROUTER_EXPERT_LANE0
    pad = lambda rows, width: jnp.zeros((depth, rows, width), F32)
    w = jnp.concatenate([jnp.swapaxes(w_rg, 1, 2), pad(e0 - N_GROUPS, D_MODEL), jnp.swapaxes(w_re, 1, 2),
                         pad(LANES - e0 - N_EXPERTS, D_MODEL)], axis=1)
    bias = jnp.concatenate([b_rg[:, :, None], pad(e0 - N_GROUPS, 1), b_re[:, :, None],
                            pad(LANES - e0 - N_EXPERTS, 1)], axis=1)
    hi = w.astype(BF16)
    lo = (w - hi.astype(F32)).astype(BF16)
    return hi, lo, bias


def kernel(x, mem, norm1_g, w_in, conv_w, conv_b, rg_wa, rg_ba, rg_wx, rg_bx, rg_lambda, dq_norm_g, dk_norm_g, lambda_q1, lambda_k1, lambda_q2, lambda_k2, diff_head_norm_g, mem_norm_g, w_mem_kv, mq_norm_g, mk_norm_g, w_out, norm2_g, w_router_group, b_router_group, w_router_expert, b_router_expert, w_expert_gate, w_expert_up, w_expert_down):
    b, s, d = x.shape
    m = mem.shape[1]
    depth = w_in.shape[0]
    gw = GROUP_WIDTH
    epg = EXPERTS_PER_GROUP
    x2 = x.reshape(b * s, d)
    mem2 = mem.reshape(b * m, d)
    row = lambda v: v.reshape(depth, 1, -1).astype(F32)
    col = lambda v: v.reshape(depth, -1, 1).astype(F32)
    lam_inits = [0.8 - 0.6 * math.exp(-0.3 * l) for l in range(depth)]
    lam = (jnp.exp(jnp.sum(lambda_q1 * lambda_k1, axis=-1)) - jnp.exp(jnp.sum(lambda_q2 * lambda_k2, axis=-1))
           + jnp.asarray(lam_inits, F32))
    lam = jnp.stack([lam, _score_bound(dq_norm_g, dk_norm_g)], axis=1).astype(F32)
    w_in_bf = w_in.astype(BF16)
    w_in_t = jnp.swapaxes(w_in_bf, 1, 2)
    qg_col = col(jnp.tile(dq_norm_g, (1, gw // DIFF_QK_DIM)))
    kg = row(jnp.tile(dk_norm_g, (1, 2)))
    wa_bd = _block_diag(rg_wa).astype(BF16)
    wx_bd = _block_diag(rg_wx).astype(BF16)
    w_kv_bf = w_mem_kv.astype(BF16)
    w_out_bf = w_out.astype(BF16)
    rt_hi, rt_lo, rb_col = _router_tables(w_router_group, b_router_group, w_router_expert, b_router_expert)
    earlier = jnp.triu(jnp.ones((MOE_TILE, MOE_TILE), BF16), k=1)
    wg = w_expert_gate.astype(BF16).reshape(depth * N_GROUPS, epg, D_MODEL, D_EXPERT)
    wu = w_expert_up.astype(BF16).reshape(depth * N_GROUPS, epg, D_MODEL, D_EXPERT)
    wd = w_expert_down.astype(BF16).reshape(depth * N_GROUPS, epg * D_EXPERT, D_MODEL)
    g1, g2, mqg, mkg = row(norm1_g), row(norm2_g), row(mq_norm_g), row(mk_norm_g)
    conv_b3, ba, bx, lru_lam, hg_col = row(conv_b), row(rg_ba), row(rg_bx), row(rg_lambda), col(diff_head_norm_g)
    mem_g = mem_norm_g.reshape(1, -1).astype(F32)
    moe = None
    for l in range(depth):
        mk, mv = _mem_kv(mem2, mem_g, l, w_kv_bf, mkg)
        proj = _in_proj(x2, b, s, l, g1, w_in_bf, w_in_t, qg_col, kg, mqg, mk, mv,
                        (conv_w, conv_b3, wa_bd, ba, wx_bd, bx, lru_lam), moe)
        if moe is not None:
            x2, *proj = proj
        y_lru, k, y_mem, qt, vt = proj
        y_diff = _diff_attn(qt, k, vt, lam, hg_col, b, s, l, lam_inits[l])
        x2, moe = _out_proj_moe(x2, y_lru, y_diff, y_mem, l, w_out_bf, g2, rt_hi, rt_lo, rb_col, earlier,
                                wg, wu, wd)
    return _moe_combine(moe[0], x2, moe[1], moe[2]).reshape(b, s, d)
```

```python
import functools
import math

import jax
import jax.numpy as jnp
from jax import lax
from jax.experimental import pallas as pl
from jax.experimental.pallas import tpu as pltpu

F32 = jnp.float32
BF16 = jnp.bfloat16

D_MODEL = 1024
GROUP_WIDTH = D_MODEL // 2
CONV_WIDTH = 4
LRU_C = 8.0
DIFF_HEADS = 4
DIFF_HEAD_DIM = GROUP_WIDTH // DIFF_HEADS
DIFF_QK_DIM = DIFF_HEAD_DIM // 2
MEM_HEADS = 4
MEM_HEAD_DIM = GROUP_WIDTH // MEM_HEADS
N_GROUPS = 4
EXPERTS_PER_GROUP = 8
N_EXPERTS = N_GROUPS * EXPERTS_PER_GROUP
D_EXPERT = 256
EPS = 1e-6

LANES = 128
SUBLANES = 8
VMEM_LIMIT = 56 * 1024 * 1024
NEG_BIG = -1e30
ROUTER_EXPERT_LANE0 = 32


def _tile(n, pref):
    t = min(n, pref)
    assert n % t == 0, (n, t)
    return t


def _params(*sem):
    return pltpu.CompilerParams(dimension_semantics=sem, vmem_limit_bytes=VMEM_LIMIT)


def _lspec(l, tail, blk=None):
    idx = (l,) + tuple(blk if blk is not None else (0,) * len(tail))
    return pl.BlockSpec((None,) + tuple(tail), lambda *_: idx)


def _rms_rows(x, g):
    return x * lax.rsqrt(jnp.mean(x * x, axis=-1, keepdims=True) + EPS) * g


def _lane_iota(shape):
    return lax.broadcasted_iota(jnp.int32, shape, len(shape) - 1)


def _half_head_norm(z, g):
    sq = z * z
    lo = _lane_iota(z.shape) < DIFF_QK_DIM
    s_all = jnp.sum(sq, axis=-1, keepdims=True)
    s_lo = jnp.sum(jnp.where(lo, sq, 0.0), axis=-1, keepdims=True)
    inv_lo = lax.rsqrt(s_lo * (1.0 / DIFF_QK_DIM) + EPS)
    inv_hi = lax.rsqrt((s_all - s_lo) * (1.0 / DIFF_QK_DIM) + EPS)
    return z * jnp.where(lo, inv_lo, inv_hi) * g


SEQ_TILE = 512
_NT = (((1,), (1,)), ((), ()))


N_PROJ_PARAMS = 18
N_PROJ_OUTS = 5


def _softplus(z):
    return jnp.maximum(z, 0.0) + jnp.log(1.0 + jnp.exp(-jnp.abs(z)))


def _in_proj_body(x, ns, g1_ref, wl_ref, wk_ref, wmq_ref, wqt_ref, wvt_ref, qg_ref, kg_ref, mqg_ref,
                  mk_ref, mv_ref, cw_ref, cb_ref, wa_ref, ba_ref, wx_ref, bx_ref, llam_ref,
                  ylru_ref, k_ref, ym_ref, qt_ref, vt_ref, ext_ref, hcar_ref, a_ref, b_ref, h_ref):
    gw = GROUP_WIDTH
    tm = x.shape[0]
    hist = SUBLANES
    first = lax.rem(pl.program_id(0), ns) == 0

    @pl.when(first)
    def _():
        ext_ref[0:hist, :] = jnp.zeros((hist, gw), F32)
        hcar_ref[...] = jnp.zeros_like(hcar_ref)

    @pl.when(jnp.logical_not(first))
    def _():
        ext_ref[0:hist, :] = ext_ref[tm:tm + hist, :]

    xn = _rms_rows(x, g1_ref[...]).astype(BF16)
    heads = range(MEM_HEADS)
    hsl = [slice(h * MEM_HEAD_DIM, (h + 1) * MEM_HEAD_DIM) for h in heads]
    mq = jnp.dot(xn, wmq_ref[...], preferred_element_type=F32)
    u = jnp.dot(xn, wl_ref[...], preferred_element_type=F32)
    ext_ref[hist:hist + tm, :] = u[:, 0:gw]
    xc = cb_ref[...] + cw_ref[CONV_WIDTH - 1:CONV_WIDTH, :] * u[:, 0:gw]
    for j in range(CONV_WIDTH - 1):
        off = hist - (CONV_WIDTH - 1) + j
        xc = xc + cw_ref[j:j + 1, :] * ext_ref[off:off + tm, :]
    xcb = xc.astype(BF16)
    sc = [lax.dot_general(_rms_rows(mq[:, hsl[h]], mqg_ref[...]).astype(BF16), mk_ref[:, hsl[h]], _NT,
                          preferred_element_type=F32) * MEM_HEAD_DIM ** -0.5 for h in heads]
    k = jnp.dot(xn, wk_ref[...], preferred_element_type=F32)
    r_pre = jnp.dot(xcb, wa_ref[...], preferred_element_type=F32)
    i_pre = jnp.dot(xcb, wx_ref[...], preferred_element_type=F32)
    p = [jnp.exp(sc[h] - jnp.max(sc[h], axis=-1, keepdims=True)) for h in heads]
    qt = lax.dot_general(wqt_ref[...], xn, _NT, preferred_element_type=F32)
    o = [jnp.dot(p[h].astype(BF16), mv_ref[:, hsl[h]], preferred_element_type=F32) for h in heads]
    vt_ref[...] = lax.dot_general(wvt_ref[...], xn, _NT, preferred_element_type=F32).astype(BF16)

    r = jax.nn.sigmoid(r_pre + ba_ref[...])
    gate_i = jax.nn.sigmoid(i_pre + bx_ref[...])
    a = jnp.exp((-LRU_C * r) * _softplus(-llam_ref[...]))
    om = 1.0 - a * a
    b = om * lax.rsqrt(jnp.maximum(om, 1e-30)) * (gate_i * xc)
    a = a.reshape(tm // SUBLANES, SUBLANES, gw)
    b = b.reshape(tm // SUBLANES, SUBLANES, gw)
    row = lax.broadcasted_iota(jnp.int32, a.shape, 1)
    d = 1
    while d < SUBLANES:
        keep = row >= d
        a_prev = pltpu.roll(a, d, axis=1)
        b_prev = pltpu.roll(b, d, axis=1)
        b = jnp.where(keep, a * b_prev + b, b)
        a = jnp.where(keep, a * a_prev, a)
        d *= 2
    a_ref[...] = a.reshape(tm, gw)
    b_ref[...] = b.reshape(tm, gw)
    gated = jax.nn.gelu(u[:, gw:2 * gw])

    for h in range(DIFF_HEADS):
        sl = slice(h * LANES, (h + 1) * LANES)
        k_ref[:, sl] = _half_head_norm(k[:, sl], kg_ref[...]).astype(BF16)
    for h in heads:
        ym_ref[:, hsl[h]] = (o[h] / jnp.sum(p[h], axis=-1, keepdims=True)).astype(BF16)
    q3 = qt.reshape(gw // DIFF_QK_DIM, DIFF_QK_DIM, tm)
    q3 = q3 * lax.rsqrt(jnp.mean(q3 * q3, axis=1, keepdims=True) + EPS)
    qscale = DIFF_QK_DIM ** -0.5 * math.log2(math.e)
    qt_ref[...] = (q3.reshape(gw, tm) * (qg_ref[...] * qscale)).astype(BF16)

    def block(i, hprev):
        off = pl.multiple_of(i * SUBLANES, SUBLANES)
        hb = a_ref[pl.ds(off, SUBLANES), :] * hprev + b_ref[pl.ds(off, SUBLANES), :]
        h_ref[pl.ds(off, SUBLANES), :] = hb
        return jnp.broadcast_to(hb[SUBLANES - 1:SUBLANES, :], (SUBLANES, gw))

    hcar_ref[...] = lax.fori_loop(0, tm // SUBLANES, block, hcar_ref[...], unroll=8)
    ylru_ref[...] = (h_ref[...] * gated).astype(BF16)


def _in_proj_kernel(x_ref, *refs, ns):
    _in_proj_body(x_ref[...], ns, *refs)


def _granule_copies(tab_ref, step, n, src_hbm, buf_ref, slot, sem_ref, start):
    for k in range(n):
        g = tab_ref[step * n + k]
        cp = pltpu.make_async_copy(
            src_hbm.at[pl.ds(pl.multiple_of(g * GRANULE, GRANULE), GRANULE), :],
            buf_ref.at[slot, pl.ds(k * GRANULE, GRANULE), :], sem_ref.at[slot])
        if start:
            cp.start()
        else:
            cp.wait()


def _unsort(info, ys):
    kpad = -SORTED_ROWS % LANES
    ys = jnp.concatenate([ys, jnp.zeros((kpad, D_MODEL), BF16)], axis=0)
    pos = info[:, INFO_POS_LANE:INFO_POS_LANE + 1]
    unperm = jnp.where(pos == _lane_iota((MOE_TILE, SORTED_ROWS + kpad)).astype(F32), 1.0, 0.0).astype(BF16)
    return jnp.dot(unperm, ys, preferred_element_type=F32)


def _combined_tile(inv_ref, x1_ref, info_ref, ys_hbm, ybuf_ref, sem_ref):
    s = pl.program_id(0)
    last = pl.num_programs(0) - 1
    slot = s & 1
    nsub = x1_ref.shape[0] // MOE_TILE
    n = nsub * TILE_SLOTS

    @pl.when(s == 0)
    def _():
        _granule_copies(inv_ref, 0, n, ys_hbm, ybuf_ref, 0, sem_ref, True)

    _granule_copies(inv_ref, s, n, ys_hbm, ybuf_ref, slot, sem_ref, False)
    nxt = jnp.minimum(s + 1, last)
    x = jnp.concatenate(
        [x1_ref[u * MOE_TILE:(u + 1) * MOE_TILE, :]
         + _unsort(info_ref[u * MOE_TILE:(u + 1) * MOE_TILE, :],
                   ybuf_ref[slot, u * SORTED_ROWS:(u + 1) * SORTED_ROWS, :])
         for u in range(nsub)], axis=0)
    _granule_copies(inv_ref, nxt, n, ys_hbm, ybuf_ref, 1 - slot, sem_ref, True)

    @pl.when(s == last)
    def _():
        _granule_copies(inv_ref, nxt, n, ys_hbm, ybuf_ref, 1 - slot, sem_ref, False)

    return x


def _combine_in_proj_kernel(inv_ref, x1_ref, info_ref, ys_hbm, *refs, ns):
    params, x_out_ref, outs = refs[:N_PROJ_PARAMS], refs[N_PROJ_PARAMS], refs[N_PROJ_PARAMS + 1:-2]
    x = _combined_tile(inv_ref, x1_ref, info_ref, ys_hbm, *refs[-2:])
    x_out_ref[...] = x
    _in_proj_body(x, ns, *params, *outs)


def _in_proj(x2, b, s, l, g1, w_in, w_in_t, qg_col, kg, mqg, mk, mv, lru, moe=None):
    t = x2.shape[0]
    tm = SEQ_TILE
    assert s % tm == 0
    ns = s // tm
    m = mk.shape[1] // b
    gw = GROUP_WIDTH
    row = lambda i, *_: (i, 0)
    fm = lambda i, *_: (i // ns, i % ns, 0, 0)
    mem = pl.BlockSpec((None, m, gw), lambda i, *_: (l, i // ns, 0))
    in_specs = [_lspec(l, (1, D_MODEL)),
                _lspec(l, (D_MODEL, 2 * gw), (0, 0)), _lspec(l, (D_MODEL, gw), (0, 3)),
                _lspec(l, (D_MODEL, gw), (0, 5)), _lspec(l, (gw, D_MODEL), (0, 0)),
                _lspec(l, (gw, D_MODEL), (1, 0)), _lspec(l, (gw, 1)),
                _lspec(l, (1, LANES)), _lspec(l, (1, MEM_HEAD_DIM)), mem, mem]
    vec = _lspec(l, (1, gw))
    in_specs += [_lspec(l, (CONV_WIDTH, gw)), vec, _lspec(l, (gw, gw)), vec, _lspec(l, (gw, gw)), vec, vec]
    operands = (g1, w_in, w_in, w_in, w_in_t, w_in_t, qg_col, kg, mqg, mk, mv) + tuple(lru)
    assert len(operands) == N_PROJ_PARAMS
    out_specs = [pl.BlockSpec((tm, gw), row), pl.BlockSpec((tm, gw), row), pl.BlockSpec((tm, gw), row),
                 pl.BlockSpec((None, None, gw, tm), fm), pl.BlockSpec((None, None, gw, tm), fm)]
    out_shape = [jax.ShapeDtypeStruct((t, gw), BF16), jax.ShapeDtypeStruct((t, gw), BF16),
                 jax.ShapeDtypeStruct((t, gw), BF16), jax.ShapeDtypeStruct((b, ns, gw, tm), BF16),
                 jax.ShapeDtypeStruct((b, ns, gw, tm), BF16)]
    assert len(out_specs) == N_PROJ_OUTS
    scratch = [pltpu.VMEM((tm + SUBLANES, gw), F32), pltpu.VMEM((SUBLANES, gw), F32),
               pltpu.VMEM((tm, gw), F32), pltpu.VMEM((tm, gw), F32), pltpu.VMEM((tm, gw), F32)]
    xspec = pl.BlockSpec((tm, D_MODEL), row)
    if moe is None:
        return pl.pallas_call(
            functools.partial(_in_proj_kernel, ns=ns), grid=(t // tm,), in_specs=[xspec] + in_specs,
            out_specs=out_specs, out_shape=out_shape, scratch_shapes=scratch,
            compiler_params=_params("arbitrary"), name="in_proj",
        )(x2, *operands)
    inv, info, ys = moe
    return pl.pallas_call(
        functools.partial(_combine_in_proj_kernel, ns=ns),
        grid_spec=pltpu.PrefetchScalarGridSpec(
            num_scalar_prefetch=1, grid=(t // tm,),
            in_specs=[xspec, pl.BlockSpec((tm, LANES), row), pl.BlockSpec(memory_space=pl.ANY)] + in_specs,
            out_specs=[xspec] + out_specs, scratch_shapes=scratch + _combine_scratch(tm)),
        out_shape=[jax.ShapeDtypeStruct((t, D_MODEL), F32)] + out_shape,
        compiler_params=_params("arbitrary"),
        name="combine_in_proj",
    )(inv, x2, info, ys, *operands)


def _combine_scratch(tm):
    return [pltpu.VMEM((2, tm // MOE_TILE * SORTED_ROWS, D_MODEL), BF16), pltpu.SemaphoreType.DMA((2,))]


ATTN_HEADS_PER_STEP = 4
ATTN_MIN_SUM = 2.0 ** -40


def _score_bound(dq_gain, dk_gain):
    qscale = DIFF_QK_DIM ** -0.5 * math.log2(math.e)
    return (1.02 * qscale * DIFF_QK_DIM) * jnp.max(jnp.abs(dq_gain), axis=-1) * jnp.max(jnp.abs(dk_gain), axis=-1)


def _diff_attn_kernel(lam_ref, qt_ref, k_ref, vt_ref, hg_ref, o_ref,
                      qs_ref, sa_ref, sb_ref, m_ref, l_ref, acc_ref, *, t, layer, out_scale):
    qi = pl.program_id(2)
    hd = DIFF_HEAD_DIM
    heads = range(ATTN_HEADS_PER_STEP)
    lo = lax.broadcasted_iota(jnp.int32, (hd, t), 0) < DIFF_QK_DIM
    for h in heads:
        q = qt_ref[h * hd:(h + 1) * hd, :]
        zero = jnp.zeros_like(q)
        qs_ref[h, :, 0:t] = jnp.where(lo, q, zero)
        qs_ref[h, :, t:2 * t] = jnp.where(lo, zero, q)

    def causal(x, fill):
        kpos = lax.broadcasted_iota(jnp.int32, (t, 2 * t), 0)
        c = lax.broadcasted_iota(jnp.int32, (t, 2 * t), 1)
        return jnp.where(kpos <= jnp.where(c >= t, c - t, c), x, fill)

    def score(j, h):
        off = pl.multiple_of(j * t, t)
        return jnp.dot(k_ref[pl.ds(off, t), h * hd:(h + 1) * hd], qs_ref[h],
                       preferred_element_type=F32)

    bound = lam_ref[layer, 1]
    l_ref[...] = jnp.zeros(l_ref.shape, F32)
    acc_ref[...] = jnp.zeros(acc_ref.shape, F32)

    def fast_update(j, masked):
        for h in heads:
            p = jnp.exp2(score(j, h) - bound)
            if masked:
                p = causal(p, 0.0)
            l_ref[h] += jnp.sum(p, axis=0, keepdims=True)
            acc_ref[h] += jnp.dot(vt_ref[j, h * hd:(h + 1) * hd, :], p.astype(BF16),
                                  preferred_element_type=F32)

    def fast_pair(i, carry):
        fast_update(2 * i, False)
        fast_update(2 * i + 1, False)
        return carry

    lax.fori_loop(0, qi >> 1, fast_pair, 0)

    @pl.when((qi & 1) == 1)
    def _():
        fast_update(qi - 1, False)

    fast_update(qi, True)

    @pl.when(jnp.min(l_ref[...]) < ATTN_MIN_SUM)
    def _():
        m_ref[...] = jnp.full(m_ref.shape, NEG_BIG, F32)
        l_ref[...] = jnp.zeros(l_ref.shape, F32)
        acc_ref[...] = jnp.zeros(acc_ref.shape, F32)

        def scores(j, dst_ref):
            for h in heads:
                dst_ref[h] = score(j, h)

        def update(j, src_ref, masked):
            for h in heads:
                s = src_ref[h]
                if masked:
                    s = causal(s, NEG_BIG)
                m = m_ref[h]
                m_new = jnp.maximum(m, jnp.max(s, axis=0, keepdims=True))
                alpha = jnp.exp2(m - m_new)
                p = jnp.exp2(s - m_new)
                m_ref[h] = m_new
                l_ref[h] = alpha * l_ref[h] + jnp.sum(p, axis=0, keepdims=True)
                vt = vt_ref[j, h * hd:(h + 1) * hd, :]
                acc_ref[h] = alpha * acc_ref[h] + jnp.dot(vt, p.astype(BF16), preferred_element_type=F32)

        scores(0, sa_ref)

        def pair(i, carry):
            scores(2 * i + 1, sb_ref)
            update(2 * i, sa_ref, False)
            scores(2 * i + 2, sa_ref)
            update(2 * i + 1, sb_ref, False)
            return carry

        lax.fori_loop(0, qi >> 1, pair, 0)

        @pl.when((qi & 1) == 1)
        def _():
            scores(qi, sb_ref)
            update(qi - 1, sa_ref, False)
            update(qi, sb_ref, True)

        @pl.when((qi & 1) == 0)
        def _():
            update(qi, sa_ref, True)

    for h in heads:
        o = acc_ref[h] / l_ref[h]
        o = o[:, 0:t] - lam_ref[layer, 0] * o[:, t:2 * t]
        o = o * lax.rsqrt(jnp.mean(o * o, axis=0, keepdims=True) + EPS) * (hg_ref[...] * out_scale)
        o_ref[:, h * hd:(h + 1) * hd] = o.T.astype(BF16)


def _diff_attn(qt, k, vt, lam, hg_col, b, s, l, lam_init):
    t = SEQ_TILE
    nq = s // t
    hp = ATTN_HEADS_PER_STEP
    w = hp * DIFF_HEAD_DIM
    ng = DIFF_HEADS // hp
    kern = functools.partial(_diff_attn_kernel, t=t, layer=l, out_scale=1.0 - lam_init)
    return pl.pallas_call(
        kern,
        grid=(b, ng, nq),
        in_specs=[pl.BlockSpec(memory_space=pltpu.SMEM),
                  pl.BlockSpec((None, None, w, t), lambda bi, g, qi: (bi, qi, g, 0)),
                  pl.BlockSpec((s, w), lambda bi, g, qi: (bi, g)),
                  pl.BlockSpec((None, nq, w, t), lambda bi, g, qi: (bi, 0, g, 0)),
                  _lspec(l, (DIFF_HEAD_DIM, 1))],
        out_specs=pl.BlockSpec((t, w), lambda bi, g, qi: (bi * nq + qi, g)),
        out_shape=jax.ShapeDtypeStruct((b * s, GROUP_WIDTH), BF16),
        scratch_shapes=[pltpu.VMEM((hp, DIFF_HEAD_DIM, 2 * t), BF16),
                        pltpu.VMEM((hp, t, 2 * t), F32), pltpu.VMEM((hp, t, 2 * t), F32),
                        pltpu.VMEM((hp, 1, 2 * t), F32), pltpu.VMEM((hp, 1, 2 * t), F32),
                        pltpu.VMEM((hp, DIFF_HEAD_DIM, 2 * t), F32)],
        compiler_params=_params("arbitrary", "arbitrary", "arbitrary"),
        name="diff_attn",
    )(lam, qt, k, vt, hg_col)


def _mem_kv_kernel(mem_ref, mg_ref, w_ref, kg_ref, mk_ref, mv_ref):
    memn = _rms_rows(mem_ref[...], mg_ref[...]).astype(BF16)
    gw = GROUP_WIDTH
    k = jnp.dot(memn, w_ref[:, 0:gw], preferred_element_type=F32)
    for h in range(MEM_HEADS):
        sl = slice(h * MEM_HEAD_DIM, (h + 1) * MEM_HEAD_DIM)
        mk_ref[:, sl] = _rms_rows(k[:, sl], kg_ref[...]).astype(BF16)
    mv_ref[...] = jnp.dot(memn, w_ref[:, gw:2 * gw], preferred_element_type=F32).astype(BF16)


def _mem_kv(mem2, mg, w_kv, kg):
    r = mem2.shape[0]
    depth = w_kv.shape[0]
    tm = _tile(r, 512)
    gw = GROUP_WIDTH
    out = pl.BlockSpec((None, tm, gw), lambda l, i: (l, i, 0))
    return pl.pallas_call(
        _mem_kv_kernel,
        grid=(depth, r // tm),
        in_specs=[pl.BlockSpec((tm, D_MODEL), lambda l, i: (i, 0)), pl.BlockSpec((1, D_MODEL), lambda l, i: (0, 0)),
                  pl.BlockSpec((None, D_MODEL, 2 * gw), lambda l, i: (l, 0, 0)),
                  pl.BlockSpec((None, 1, MEM_HEAD_DIM), lambda l, i: (l, 0, 0))],
        out_specs=[out, out],
        out_shape=[jax.ShapeDtypeStruct((depth, r, gw), BF16), jax.ShapeDtypeStruct((depth, r, gw), BF16)],
        compiler_params=_params("arbitrary", "arbitrary"),
        name="mem_kv",
    )(mem2, mg, w_kv, kg)


MOE_TILE = 256
GRANULE = 16
TILE_SLOTS = MOE_TILE // GRANULE + N_GROUPS
SORTED_ROWS = TILE_SLOTS * GRANULE
STEP_GRANULES = 32
STEP_ROWS = STEP_GRANULES * GRANULE
XS_WIDTH = D_MODEL + LANES
INFO_POS_LANE = 0


def _route_t(lt):
    row_i = lax.broadcasted_iota(jnp.int32, lt.shape, 0)
    row = row_i.astype(F32)
    e0 = ROUTER_EXPERT_LANE0
    lg = jnp.where(row_i < N_GROUPS, lt, NEG_BIG)
    mg = jnp.max(lg, axis=0, keepdims=True)
    g_gate = 1.0 / jnp.sum(jnp.exp(lg - mg), axis=0, keepdims=True)
    g_idx = jnp.min(jnp.where(lg == mg, row, float(LANES)), axis=0, keepdims=True)
    row_group = ((row_i - e0) >> 3).astype(F32)
    sel = (row_i >= e0) & (row_i < e0 + N_EXPERTS) & (row_group == g_idx)
    le = jnp.where(sel, lt, NEG_BIG)
    m1 = jnp.max(le, axis=0, keepdims=True)
    se = jnp.sum(jnp.where(sel, jnp.exp(le - m1), 0.0), axis=0, keepdims=True)
    i1 = jnp.min(jnp.where(sel & (le == m1), row, float(LANES)), axis=0, keepdims=True)
    le2 = jnp.where(row == i1, NEG_BIG, le)
    m2 = jnp.max(le2, axis=0, keepdims=True)
    i2 = jnp.min(jnp.where(sel & (le2 == m2) & (row != i1), row, float(LANES)), axis=0, keepdims=True)
    p1 = 1.0 / se
    p2 = jnp.exp(m2 - m1) / se
    tot = p1 + p2
    w = jnp.where(row == i1, p1 / tot, jnp.where(row == i2, p2 / tot, 0.0))
    return g_gate * w, g_idx


def _split3(c):
    hi = c.astype(BF16).astype(F32)
    r1 = c - hi
    mid = r1.astype(BF16).astype(F32)
    lo = (r1 - mid).astype(BF16).astype(F32)
    return hi, mid, lo


def _sort_logits(x1, g2, rt_hi, rt_lo, rb_col):
    xn = _rms_rows(x1, g2)
    hi = xn.astype(BF16)
    lo = (xn - hi.astype(F32)).astype(BF16)
    logits_t = (lax.dot_general(rt_hi, hi, _NT, preferred_element_type=F32)
                + lax.dot_general(rt_hi, lo, _NT, preferred_element_type=F32)
                + lax.dot_general(rt_lo, hi, _NT, preferred_element_type=F32) + rb_col)
    return hi, logits_t


def _sort_rank(logits_t, earlier):
    tl = logits_t.shape[1]
    comb_t, g_idx = _route_t(logits_t)
    grow = lax.broadcasted_iota(jnp.int32, (SUBLANES, tl), 0).astype(F32)
    gt = jnp.where(grow == g_idx, 1.0, 0.0)
    before = jnp.dot(gt.astype(BF16), earlier, preferred_element_type=F32)
    return comb_t, gt, before


def _sort_emit(hi, comb_t, gt, before):
    tl = hi.shape[0]
    rank = jnp.sum(gt * before, axis=0, keepdims=True)
    cnt = jnp.sum(gt, axis=1, keepdims=True)
    glen = jnp.floor((cnt + (GRANULE - 1)) * (1.0 / GRANULE))
    r8 = lax.broadcasted_iota(jnp.int32, (SUBLANES, 1), 0)
    start = jnp.zeros((SUBLANES, 1), F32)
    for g in range(1, N_GROUPS):
        start = jnp.where(r8 == g, jnp.sum(jnp.where(r8 < g, glen, 0.0), axis=0, keepdims=True), start)
    pos = jnp.sum(gt * (start * GRANULE), axis=0, keepdims=True) + rank

    rows = lax.broadcasted_iota(jnp.int32, (LANES, tl), 0)
    info = jnp.where(rows == INFO_POS_LANE, pos, comb_t).T
    lane = _lane_iota(info.shape)
    e0 = ROUTER_EXPERT_LANE0
    c_hi, c_mid, c_lo = _split3(jnp.where((lane >= e0) & (lane < e0 + N_EXPERTS), info, 0.0))
    aug = (c_hi + pltpu.roll(c_mid, N_EXPERTS, axis=1) + pltpu.roll(c_lo, 2 * N_EXPERTS, axis=1)).astype(BF16)
    perm = jnp.where(pos == lax.broadcasted_iota(jnp.int32, (SORTED_ROWS, tl), 0).astype(F32),
                     1.0, 0.0).astype(BF16)
    xs = jnp.dot(perm, hi, preferred_element_type=F32).astype(BF16)
    xs_aug = jnp.dot(perm, aug, preferred_element_type=F32).astype(BF16)
    return xs, xs_aug, info, glen


def _out_sort_kernel(x_ref, yl_ref, yd_ref, ym_ref, w_ref, g2_ref, rthi_ref, rtlo_ref, rb_ref, earlier_ref,
                     o_ref, xs_ref, info_ref, lens_ref, x1s_ref):
    @pl.when(pl.program_id(0) == 0)
    def _():
        x1s_ref[...] = jnp.zeros_like(x1s_ref)

    tm = x1s_ref.shape[0]
    subs = range(tm // MOE_TILE)

    def project(c, nchunks):
        r = slice(c * (tm // nchunks), (c + 1) * (tm // nchunks))
        y = jnp.concatenate([yl_ref[r, :], yd_ref[r, :], ym_ref[r, :]], axis=-1)
        x1 = x_ref[r, :] + jnp.dot(y, w_ref[...], preferred_element_type=F32)
        o_ref[r, :] = x1
        return r, x1

    fresh = [project(0, 2)]
    s1 = [_sort_logits(x1s_ref[sub * MOE_TILE:(sub + 1) * MOE_TILE, :], g2_ref[...], rthi_ref[...],
                       rtlo_ref[...], rb_ref[...]) for sub in subs]
    for r, x1 in fresh:
        x1s_ref[r, :] = x1
    fresh = [project(1, 2)]
    s2 = [_sort_rank(s1[sub][1], earlier_ref[...]) for sub in subs]
    for r, x1 in fresh:
        x1s_ref[r, :] = x1
    for sub in subs:
        xs, xs_aug, info, glen = _sort_emit(s1[sub][0], *s2[sub])
        r0 = sub * SORTED_ROWS
        xs_ref[r0:r0 + SORTED_ROWS, 0:D_MODEL] = xs
        xs_ref[r0:r0 + SORTED_ROWS, D_MODEL:XS_WIDTH] = xs_aug
        info_ref[sub * MOE_TILE:(sub + 1) * MOE_TILE, :] = info
        lens_ref[sub] = jnp.broadcast_to(glen, (SUBLANES, LANES))


def _out_sort(x2, y_lru, y_diff, y_mem, l, w_out, g2, rt_hi, rt_lo, rb_col, earlier):
    t = x2.shape[0]
    tm = SEQ_TILE
    assert t % tm == 0 and tm % MOE_TILE == 0
    sub = tm // MOE_TILE
    nt = t // MOE_TILE
    n = t // tm
    cur = lambda i: (jnp.minimum(i, n - 1), 0)
    prev = lambda i: (jnp.maximum(i - 1, 0), 0)
    gw = GROUP_WIDTH
    return pl.pallas_call(
        _out_sort_kernel,
        grid=(n + 1,),
        in_specs=[pl.BlockSpec((tm, D_MODEL), cur), pl.BlockSpec((tm, gw), cur), pl.BlockSpec((tm, gw), cur),
                  pl.BlockSpec((tm, gw), cur), _lspec(l, (3 * gw, D_MODEL)),
                  _lspec(l, (1, D_MODEL)), _lspec(l, (LANES, D_MODEL)),
                  _lspec(l, (LANES, D_MODEL)), _lspec(l, (LANES, 1)),
                  pl.BlockSpec((MOE_TILE, MOE_TILE), lambda i: (0, 0))],
        out_specs=[pl.BlockSpec((tm, D_MODEL), cur), pl.BlockSpec((sub * SORTED_ROWS, XS_WIDTH), prev),
                   pl.BlockSpec((tm, LANES), prev),
                   pl.BlockSpec((sub, SUBLANES, LANES), lambda i: (jnp.maximum(i - 1, 0), 0, 0))],
        out_shape=[jax.ShapeDtypeStruct((t, D_MODEL), F32),
                   jax.ShapeDtypeStruct((nt * SORTED_ROWS, XS_WIDTH), BF16),
                   jax.ShapeDtypeStruct((t, LANES), F32),
                   jax.ShapeDtypeStruct((nt, SUBLANES, LANES), F32)],
        scratch_shapes=[pltpu.VMEM((tm, D_MODEL), F32)],
        compiler_params=_params("arbitrary"),
        name="out_proj_sort",
    )(x2, y_lru, y_diff, y_mem, w_out, g2, rt_hi, rt_lo, rb_col, earlier)


def _moe_expert_kernel(sg_ref, sv_ref, gi_ref, xs_hbm, wg_ref, wu_ref, wd_ref, y_ref, xbuf_ref, sem_ref):
    s = pl.program_id(0)
    last = pl.num_programs(0) - 1
    slot = s & 1

    def start_fetch(step, dst_slot):
        _granule_copies(gi_ref, step, STEP_GRANULES, xs_hbm, xbuf_ref, dst_slot, sem_ref, True)

    def wait_fetch(step, dst_slot):
        _granule_copies(gi_ref, step, STEP_GRANULES, xs_hbm, xbuf_ref, dst_slot, sem_ref, False)

    @pl.when(s == 0)
    def _():
        start_fetch(0, 0)

    @pl.when((s == 0) | (sv_ref[jnp.maximum(s - 1, 0)] != 0))
    def _():
        wait_fetch(s, slot)

    @pl.when(sv_ref[s] == 0)
    def _():
        y_ref[...] = jnp.zeros_like(y_ref)

    @pl.when(sv_ref[s] != 0)
    def _():
        nxt = jnp.minimum(s + 1, last)
        rows = xbuf_ref[slot]
        x = rows[:, 0:D_MODEL]
        aug = rows[:, D_MODEL:XS_WIDTH].astype(F32)
        lane = _lane_iota(aug.shape)
        e0 = ROUTER_EXPERT_LANE0
        comb = jnp.where((lane >= e0) & (lane < e0 + N_EXPERTS),
                         aug + pltpu.roll(aug, LANES - N_EXPERTS, axis=1)
                         + pltpu.roll(aug, LANES - 2 * N_EXPERTS, axis=1), 0.0)
        first = e0 + sg_ref[s] * EXPERTS_PER_GROUP
        hs = []
        for e in range(EXPERTS_PER_GROUP):
            if e == 1:
                start_fetch(nxt, 1 - slot)
            cw = jnp.sum(jnp.where(lane == first + e, comb, 0.0), axis=-1, keepdims=True)
            hg = jnp.dot(x, wg_ref[e], preferred_element_type=F32)
            hu = jnp.dot(x, wu_ref[e], preferred_element_type=F32)
            hs.append((jax.nn.silu(hg) * hu * cw).astype(BF16))
        h = jnp.concatenate(hs, axis=-1)
        y_ref[...] = jnp.dot(h, wd_ref[...], preferred_element_type=F32).astype(BF16)

        @pl.when(s == last)
        def _():
            wait_fetch(nxt, 1 - slot)


def _moe_experts(step_group, step_valid, gran_idx, xs, l, wg, wu, wd):
    nstep = step_group.shape[0]
    epg = EXPERTS_PER_GROUP
    wmap4 = lambda s, sg, sv, gi: (l * N_GROUPS + sg[s], 0, 0, 0)
    in_specs = [pl.BlockSpec(memory_space=pl.ANY),
                pl.BlockSpec((None, epg, D_MODEL, D_EXPERT), wmap4),
                pl.BlockSpec((None, epg, D_MODEL, D_EXPERT), wmap4),
                pl.BlockSpec((None, epg * D_EXPERT, D_MODEL), lambda s, sg, sv, gi: (l * N_GROUPS + sg[s], 0, 0))]
    return pl.pallas_call(
        _moe_expert_kernel,
        grid_spec=pltpu.PrefetchScalarGridSpec(
            num_scalar_prefetch=3, grid=(nstep,), in_specs=in_specs,
            out_specs=pl.BlockSpec((STEP_ROWS, D_MODEL), lambda s, sg, sv, gi: (s, 0)),
            scratch_shapes=[pltpu.VMEM((2, STEP_ROWS, XS_WIDTH), BF16), pltpu.SemaphoreType.DMA((2,))]),
        out_shape=jax.ShapeDtypeStruct((nstep * STEP_ROWS, D_MODEL), BF16),
        compiler_params=_params("arbitrary"),
        name="moe_experts",
    )(step_group, step_valid, gran_idx, xs, wg, wu, wd)


def _moe_combine_kernel(inv_ref, x_ref, info_ref, ys_hbm, o_ref, ybuf_ref, sem_ref):
    o_ref[...] = _combined_tile(inv_ref, x_ref, info_ref, ys_hbm, ybuf_ref, sem_ref)


def _moe_combine(inv, x2, info, ys):
    t = x2.shape[0]
    tm = SEQ_TILE
    row = lambda i, inv: (i, 0)
    return pl.pallas_call(
        _moe_combine_kernel,
        grid_spec=pltpu.PrefetchScalarGridSpec(
            num_scalar_prefetch=1, grid=(t // tm,),
            in_specs=[pl.BlockSpec((tm, D_MODEL), row), pl.BlockSpec((tm, LANES), row),
                      pl.BlockSpec(memory_space=pl.ANY)],
            out_specs=pl.BlockSpec((tm, D_MODEL), row), scratch_shapes=_combine_scratch(tm)),
        out_shape=jax.ShapeDtypeStruct((t, D_MODEL), F32),
        compiler_params=_params("arbitrary"),
        name="moe_combine",
    )(inv, x2, info, ys)


def _moe_tables(lens, nt):
    i32 = jnp.int32
    ng = N_GROUPS
    garange = jnp.arange(ng, dtype=i32)
    cum = jnp.cumsum(lens, axis=1)
    start = cum - lens
    base = jnp.arange(nt, dtype=i32)[:, None] * TILE_SLOTS + start
    run_len = lens.T.reshape(-1)
    run_base = base.T.reshape(-1)
    run_end = jnp.cumsum(run_len)
    run_start = run_end - run_len
    n_g = jnp.sum(lens, axis=0)
    steps_g = (n_g + STEP_GRANULES - 1) // STEP_GRANULES
    step_end = jnp.cumsum(steps_g)
    step_off = step_end - steps_g
    gran_off = jnp.cumsum(n_g) - n_g
    nstep = (nt * (TILE_SLOTS - 1) + STEP_GRANULES - 1) // STEP_GRANULES + ng
    s = jnp.arange(nstep, dtype=i32)
    sg = jnp.minimum(jnp.sum((s[:, None] >= step_end[None, :]).astype(i32), axis=-1), ng - 1)
    goh = sg[:, None] == garange[None, :]
    pick = lambda v: jnp.sum(jnp.where(goh, v[None, :], 0), axis=-1)
    sv = s < step_end[-1]
    jl = (s - pick(step_off))[:, None] * STEP_GRANULES + jnp.arange(STEP_GRANULES, dtype=i32)[None, :]
    ok = sv[:, None] & (jl < pick(n_g)[:, None])
    j = jnp.where(ok, pick(gran_off)[:, None] + jl, 0)
    inrun = (j[..., None] >= run_start) & (j[..., None] < run_end)
    gran = j + jnp.sum(jnp.where(inrun, run_base - run_start, 0), axis=-1)
    gran = jnp.where(ok, gran, jnp.where(sv[:, None], gran[:, 0:1], 0)).astype(i32)
    q = jnp.arange(TILE_SLOTS, dtype=i32)[None, :]
    gq = jnp.minimum(jnp.sum((q[:, :, None] >= cum[:, None, :]).astype(i32), axis=-1), ng - 1)
    qoh = gq[..., None] == garange
    used = q < cum[:, -1:]
    per_g = run_start.reshape(ng, nt).T - start + (step_off * STEP_GRANULES - gran_off)[None, :]
    inv = jnp.where(used, q + jnp.sum(jnp.where(qoh, per_g[:, None, :], 0), axis=-1), 0).astype(i32)
    return sg, sv.astype(i32), gran.reshape(-1), inv.reshape(-1)


def _out_proj_moe(x2, y_lru, y_diff, y_mem, l, w_out, g2, rt_hi, rt_lo, rb_col, earlier, wg, wu, wd):
    nt = x2.shape[0] // MOE_TILE
    x1, xs, info, lens = _out_sort(x2, y_lru, y_diff, y_mem, l, w_out, g2, rt_hi, rt_lo, rb_col, earlier)
    sg, sv, gran, inv = _moe_tables(lens[:, 0:N_GROUPS, 0].astype(jnp.int32), nt)
    ys = _moe_experts(sg, sv, gran, xs, l, wg, wu, wd)
    return x1, (inv, info, ys)


def _block_diag(w):
    depth, h, n, _ = w.shape
    eye = jnp.eye(h, dtype=w.dtype)
    return (eye[None, :, None, :, None] * w[:, :, :, None, :]).reshape(depth, h * n, h * n)


def _router_tables(w_rg, b_rg, w_re, b_re):
    depth = w_rg.shape[0]
    e0 = ROUTER_EXPERT_LANE0
    pad = lambda rows, width: jnp.zeros((depth, rows, width), F32)
    w = jnp.concatenate([jnp.swapaxes(w_rg, 1, 2), pad(e0 - N_GROUPS, D_MODEL), jnp.swapaxes(w_re, 1, 2),
                         pad(LANES - e0 - N_EXPERTS, D_MODEL)], axis=1)
    bias = jnp.concatenate([b_rg[:, :, None], pad(e0 - N_GROUPS, 1), b_re[:, :, None],
                            pad(LANES - e0 - N_EXPERTS, 1)], axis=1)
    hi = w.astype(BF16)
    lo = (w - hi.astype(F32)).astype(BF16)
    return hi, lo, bias


def kernel(x, mem, norm1_g, w_in, conv_w, conv_b, rg_wa, rg_ba, rg_wx, rg_bx, rg_lambda, dq_norm_g, dk_norm_g, lambda_q1, lambda_k1, lambda_q2, lambda_k2, diff_head_norm_g, mem_norm_g, w_mem_kv, mq_norm_g, mk_norm_g, w_out, norm2_g, w_router_group, b_router_group, w_router_expert, b_router_expert, w_expert_gate, w_expert_up, w_expert_down):
    b, s, d = x.shape
    m = mem.shape[1]
    depth = w_in.shape[0]
    gw = GROUP_WIDTH
    epg = EXPERTS_PER_GROUP
    x2 = x.reshape(b * s, d)
    mem2 = mem.reshape(b * m, d)
    row = lambda v: v.reshape(depth, 1, -1).astype(F32)
    col = lambda v: v.reshape(depth, -1, 1).astype(F32)
    lam_inits = [0.8 - 0.6 * math.exp(-0.3 * l) for l in range(depth)]
    lam = (jnp.exp(jnp.sum(lambda_q1 * lambda_k1, axis=-1)) - jnp.exp(jnp.sum(lambda_q2 * lambda_k2, axis=-1))
           + jnp.asarray(lam_inits, F32))
    lam = jnp.stack([lam, _score_bound(dq_norm_g, dk_norm_g)], axis=1).astype(F32)
    w_in_bf = w_in.astype(BF16)
    w_qv_t = jnp.swapaxes(jnp.concatenate([w_in_bf[:, :, 2 * gw:3 * gw], w_in_bf[:, :, 4 * gw:5 * gw]], axis=2), 1, 2)
    qg_col = col(jnp.tile(dq_norm_g, (1, gw // DIFF_QK_DIM)))
    kg = row(jnp.tile(dk_norm_g, (1, 2)))
    wa_bd = _block_diag(rg_wa).astype(BF16)
    wx_bd = _block_diag(rg_wx).astype(BF16)
    w_kv_bf = w_mem_kv.astype(BF16)
    w_out_bf = w_out.astype(BF16)
    rt_hi, rt_lo, rb_col = _router_tables(w_router_group, b_router_group, w_router_expert, b_router_expert)
    earlier = jnp.triu(jnp.ones((MOE_TILE, MOE_TILE), BF16), k=1)
    wg = w_expert_gate.astype(BF16).reshape(depth * N_GROUPS, epg, D_MODEL, D_EXPERT)
    wu = w_expert_up.astype(BF16).reshape(depth * N_GROUPS, epg, D_MODEL, D_EXPERT)
    wd = w_expert_down.astype(BF16).reshape(depth * N_GROUPS, epg * D_EXPERT, D_MODEL)
    g1, g2, mqg, mkg = row(norm1_g), row(norm2_g), row(mq_norm_g), row(mk_norm_g)
    conv_b3, ba, bx, lru_lam, hg_col = row(conv_b), row(rg_ba), row(rg_bx), row(rg_lambda), col(diff_head_norm_g)
    mem_g = mem_norm_g.reshape(1, -1).astype(F32)
    mk, mv = _mem_kv(mem2, mem_g, w_kv_bf, mkg)
    moe = None
    for l in range(depth):
        proj = _in_proj(x2, b, s, l, g1, w_in_bf, w_qv_t, qg_col, kg, mqg, mk, mv,
                        (conv_w, conv_b3, wa_bd, ba, wx_bd, bx, lru_lam), moe)
        if moe is not None:
            x2, *proj = proj
        y_lru, k, y_mem, qt, vt = proj
        y_diff = _diff_attn(qt, k, vt, lam, hg_col, b, s, l, lam_inits[l])
        x2, moe = _out_proj_moe(x2, y_lru, y_diff, y_mem, l, w_out_bf, g2, rt_hi, rt_lo, rb_col, earlier,
                                wg, wu, wd)
    return _moe_combine(moe[0], x2, moe[1], moe[2]).reshape(b, s, d)
```

```python
import functools
import math

import jax
import jax.numpy as jnp
from jax import lax
from jax.experimental import pallas as pl
from jax.experimental.pallas import tpu as pltpu

F32 = jnp.float32
BF16 = jnp.bfloat16

D_MODEL = 1024
GROUP_WIDTH = D_MODEL // 2
CONV_WIDTH = 4
LRU_C = 8.0
DIFF_HEADS = 4
DIFF_HEAD_DIM = GROUP_WIDTH // DIFF_HEADS
DIFF_QK_DIM = DIFF_HEAD_DIM // 2
MEM_HEADS = 4
MEM_HEAD_DIM = GROUP_WIDTH // MEM_HEADS
N_GROUPS = 4
EXPERTS_PER_GROUP = 8
N_EXPERTS = N_GROUPS * EXPERTS_PER_GROUP
D_EXPERT = 256
EPS = 1e-6

LANES = 128
SUBLANES = 8
VMEM_LIMIT = 56 * 1024 * 1024
NEG_BIG = -1e30
ROUTER_EXPERT_LANE0 = 32


def _tile(n, pref):
    t = min(n, pref)
    assert n % t == 0, (n, t)
    return t


def _params(*sem):
    return pltpu.CompilerParams(dimension_semantics=sem, vmem_limit_bytes=VMEM_LIMIT)


def _lspec(l, tail, blk=None):
    idx = (l,) + tuple(blk if blk is not None else (0,) * len(tail))
    return pl.BlockSpec((None,) + tuple(tail), lambda *_: idx)


def _rms_rows(x, g):
    return x * lax.rsqrt(jnp.mean(x * x, axis=-1, keepdims=True) + EPS) * g


def _lane_iota(shape):
    return lax.broadcasted_iota(jnp.int32, shape, len(shape) - 1)


def _half_head_norm(z, g):
    sq = z * z
    lo = _lane_iota(z.shape) < DIFF_QK_DIM
    s_all = jnp.sum(sq, axis=-1, keepdims=True)
    s_lo = jnp.sum(jnp.where(lo, sq, 0.0), axis=-1, keepdims=True)
    inv_lo = lax.rsqrt(s_lo * (1.0 / DIFF_QK_DIM) + EPS)
    inv_hi = lax.rsqrt((s_all - s_lo) * (1.0 / DIFF_QK_DIM) + EPS)
    return z * jnp.where(lo, inv_lo, inv_hi) * g


SEQ_TILE = 512
_NT = (((1,), (1,)), ((), ()))


N_PROJ_PARAMS = 18
N_PROJ_OUTS = 5


def _softplus(z):
    return jnp.maximum(z, 0.0) + jnp.log(1.0 + jnp.exp(-jnp.abs(z)))


def _in_proj_body(x, ns, g1_ref, wl_ref, wk_ref, wmq_ref, wqt_ref, wvt_ref, qg_ref, kg_ref, mqg_ref,
                  mk_ref, mv_ref, cw_ref, cb_ref, wa_ref, ba_ref, wx_ref, bx_ref, llam_ref,
                  ylru_ref, k_ref, ym_ref, qt_ref, vt_ref, ext_ref, hcar_ref, a_ref, b_ref, h_ref):
    gw = GROUP_WIDTH
    tm = x.shape[0]
    hist = SUBLANES
    first = lax.rem(pl.program_id(0), ns) == 0

    @pl.when(first)
    def _():
        ext_ref[0:hist, :] = jnp.zeros((hist, gw), F32)
        hcar_ref[...] = jnp.zeros_like(hcar_ref)

    @pl.when(jnp.logical_not(first))
    def _():
        ext_ref[0:hist, :] = ext_ref[tm:tm + hist, :]

    xn = _rms_rows(x, g1_ref[...]).astype(BF16)
    heads = range(MEM_HEADS)
    hsl = [slice(h * MEM_HEAD_DIM, (h + 1) * MEM_HEAD_DIM) for h in heads]
    mq = jnp.dot(xn, wmq_ref[...], preferred_element_type=F32)
    u = jnp.dot(xn, wl_ref[...], preferred_element_type=F32)
    ext_ref[hist:hist + tm, :] = u[:, 0:gw]
    xc = cb_ref[...] + cw_ref[CONV_WIDTH - 1:CONV_WIDTH, :] * u[:, 0:gw]
    for j in range(CONV_WIDTH - 1):
        off = hist - (CONV_WIDTH - 1) + j
        xc = xc + cw_ref[j:j + 1, :] * ext_ref[off:off + tm, :]
    xcb = xc.astype(BF16)
    sc = [lax.dot_general(_rms_rows(mq[:, hsl[h]], mqg_ref[...]).astype(BF16), mk_ref[:, hsl[h]], _NT,
                          preferred_element_type=F32) * MEM_HEAD_DIM ** -0.5 for h in heads]
    k = jnp.dot(xn, wk_ref[...], preferred_element_type=F32)
    r_pre = jnp.dot(xcb, wa_ref[...], preferred_element_type=F32)
    i_pre = jnp.dot(xcb, wx_ref[...], preferred_element_type=F32)
    p = [jnp.exp(sc[h] - jnp.max(sc[h], axis=-1, keepdims=True)) for h in heads]
    qt = lax.dot_general(wqt_ref[...], xn, _NT, preferred_element_type=F32)
    o = [jnp.dot(p[h].astype(BF16), mv_ref[:, hsl[h]], preferred_element_type=F32) for h in heads]
    vt_ref[...] = lax.dot_general(wvt_ref[...], xn, _NT, preferred_element_type=F32).astype(BF16)

    r = jax.nn.sigmoid(r_pre + ba_ref[...])
    gate_i = jax.nn.sigmoid(i_pre + bx_ref[...])
    a = jnp.exp((-LRU_C * r) * _softplus(-llam_ref[...]))
    om = 1.0 - a * a
    b = om * lax.rsqrt(jnp.maximum(om, 1e-30)) * (gate_i * xc)
    a = a.reshape(tm // SUBLANES, SUBLANES, gw)
    b = b.reshape(tm // SUBLANES, SUBLANES, gw)
    row = lax.broadcasted_iota(jnp.int32, a.shape, 1)
    d = 1
    while d < SUBLANES:
        keep = row >= d
        a_prev = pltpu.roll(a, d, axis=1)
        b_prev = pltpu.roll(b, d, axis=1)
        b = jnp.where(keep, a * b_prev + b, b)
        a = jnp.where(keep, a * a_prev, a)
        d *= 2
    a_ref[...] = a.reshape(tm, gw)
    b_ref[...] = b.reshape(tm, gw)
    gated = jax.nn.gelu(u[:, gw:2 * gw])

    for h in range(DIFF_HEADS):
        sl = slice(h * LANES, (h + 1) * LANES)
        k_ref[:, sl] = _half_head_norm(k[:, sl], kg_ref[...]).astype(BF16)
    for h in heads:
        ym_ref[:, hsl[h]] = (o[h] / jnp.sum(p[h], axis=-1, keepdims=True)).astype(BF16)
    q3 = qt.reshape(gw // DIFF_QK_DIM, DIFF_QK_DIM, tm)
    q3 = q3 * lax.rsqrt(jnp.mean(q3 * q3, axis=1, keepdims=True) + EPS)
    qscale = DIFF_QK_DIM ** -0.5 * math.log2(math.e)
    qt_ref[...] = (q3.reshape(gw, tm) * (qg_ref[...] * qscale)).astype(BF16)

    def block(i, hprev):
        off = pl.multiple_of(i * SUBLANES, SUBLANES)
        hb = a_ref[pl.ds(off, SUBLANES), :] * hprev + b_ref[pl.ds(off, SUBLANES), :]
        h_ref[pl.ds(off, SUBLANES), :] = hb
        return jnp.broadcast_to(hb[SUBLANES - 1:SUBLANES, :], (SUBLANES, gw))

    hcar_ref[...] = lax.fori_loop(0, tm // SUBLANES, block, hcar_ref[...], unroll=8)
    ylru_ref[...] = (h_ref[...] * gated).astype(BF16)


def _in_proj_kernel(x_ref, *refs, ns):
    _in_proj_body(x_ref[...], ns, *refs)


def _granule_copies(tab_ref, step, n, src_hbm, buf_ref, slot, sem_ref, start):
    for k in range(n):
        g = tab_ref[step * n + k]
        cp = pltpu.make_async_copy(
            src_hbm.at[pl.ds(pl.multiple_of(g * GRANULE, GRANULE), GRANULE), :],
            buf_ref.at[slot, pl.ds(k * GRANULE, GRANULE), :], sem_ref.at[slot])
        if start:
            cp.start()
        else:
            cp.wait()


def _unsort(info, ys):
    kpad = -SORTED_ROWS % LANES
    ys = jnp.concatenate([ys, jnp.zeros((kpad, D_MODEL), BF16)], axis=0)
    pos = info[:, INFO_POS_LANE:INFO_POS_LANE + 1]
    unperm = jnp.where(pos == _lane_iota((MOE_TILE, SORTED_ROWS + kpad)).astype(F32), 1.0, 0.0).astype(BF16)
    return jnp.dot(unperm, ys, preferred_element_type=F32)


def _combined_tile(inv_ref, x1_ref, info_ref, ys_hbm, ybuf_ref, sem_ref):
    s = pl.program_id(0)
    last = pl.num_programs(0) - 1
    slot = s & 1
    nsub = x1_ref.shape[0] // MOE_TILE
    n = nsub * TILE_SLOTS

    @pl.when(s == 0)
    def _():
        _granule_copies(inv_ref, 0, n, ys_hbm, ybuf_ref, 0, sem_ref, True)

    _granule_copies(inv_ref, s, n, ys_hbm, ybuf_ref, slot, sem_ref, False)
    nxt = jnp.minimum(s + 1, last)
    _granule_copies(inv_ref, nxt, n, ys_hbm, ybuf_ref, 1 - slot, sem_ref, True)
    x = jnp.concatenate(
        [x1_ref[u * MOE_TILE:(u + 1) * MOE_TILE, :]
         + _unsort(info_ref[u * MOE_TILE:(u + 1) * MOE_TILE, :],
                   ybuf_ref[slot, u * SORTED_ROWS:(u + 1) * SORTED_ROWS, :])
         for u in range(nsub)], axis=0)

    @pl.when(s == last)
    def _():
        _granule_copies(inv_ref, nxt, n, ys_hbm, ybuf_ref, 1 - slot, sem_ref, False)

    return x


def _combine_in_proj_kernel(inv_ref, x1_ref, info_ref, ys_hbm, *refs, ns):
    params, x_out_ref, outs = refs[:N_PROJ_PARAMS], refs[N_PROJ_PARAMS], refs[N_PROJ_PARAMS + 1:-2]
    x = _combined_tile(inv_ref, x1_ref, info_ref, ys_hbm, *refs[-2:])
    x_out_ref[...] = x
    _in_proj_body(x, ns, *params, *outs)


def _in_proj(x2, b, s, l, g1, w_in, w_in_t, qg_col, kg, mqg, mk, mv, lru, moe=None):
    t = x2.shape[0]
    tm = SEQ_TILE
    assert s % tm == 0
    ns = s // tm
    m = mk.shape[1] // b
    gw = GROUP_WIDTH
    row = lambda i, *_: (i, 0)
    fm = lambda i, *_: (i // ns, i % ns, 0, 0)
    mem = pl.BlockSpec((None, m, gw), lambda i, *_: (l, i // ns, 0))
    in_specs = [_lspec(l, (1, D_MODEL)),
                _lspec(l, (D_MODEL, 2 * gw), (0, 0)), _lspec(l, (D_MODEL, gw), (0, 3)),
                _lspec(l, (D_MODEL, gw), (0, 5)), _lspec(l, (gw, D_MODEL), (0, 0)),
                _lspec(l, (gw, D_MODEL), (1, 0)), _lspec(l, (gw, 1)),
                _lspec(l, (1, LANES)), _lspec(l, (1, MEM_HEAD_DIM)), mem, mem]
    vec = _lspec(l, (1, gw))
    in_specs += [_lspec(l, (CONV_WIDTH, gw)), vec, _lspec(l, (gw, gw)), vec, _lspec(l, (gw, gw)), vec, vec]
    operands = (g1, w_in, w_in, w_in, w_in_t, w_in_t, qg_col, kg, mqg, mk, mv) + tuple(lru)
    assert len(operands) == N_PROJ_PARAMS
    out_specs = [pl.BlockSpec((tm, gw), row), pl.BlockSpec((tm, gw), row), pl.BlockSpec((tm, gw), row),
                 pl.BlockSpec((None, None, gw, tm), fm), pl.BlockSpec((None, None, gw, tm), fm)]
    out_shape = [jax.ShapeDtypeStruct((t, gw), BF16), jax.ShapeDtypeStruct((t, gw), BF16),
                 jax.ShapeDtypeStruct((t, gw), BF16), jax.ShapeDtypeStruct((b, ns, gw, tm), BF16),
                 jax.ShapeDtypeStruct((b, ns, gw, tm), BF16)]
    assert len(out_specs) == N_PROJ_OUTS
    scratch = [pltpu.VMEM((tm + SUBLANES, gw), F32), pltpu.VMEM((SUBLANES, gw), F32),
               pltpu.VMEM((tm, gw), F32), pltpu.VMEM((tm, gw), F32), pltpu.VMEM((tm, gw), F32)]
    xspec = pl.BlockSpec((tm, D_MODEL), row)
    if moe is None:
        return pl.pallas_call(
            functools.partial(_in_proj_kernel, ns=ns), grid=(t // tm,), in_specs=[xspec] + in_specs,
            out_specs=out_specs, out_shape=out_shape, scratch_shapes=scratch,
            compiler_params=_params("arbitrary"), name="in_proj",
        )(x2, *operands)
    inv, info, ys = moe
    return pl.pallas_call(
        functools.partial(_combine_in_proj_kernel, ns=ns),
        grid_spec=pltpu.PrefetchScalarGridSpec(
            num_scalar_prefetch=1, grid=(t // tm,),
            in_specs=[xspec, pl.BlockSpec((tm, LANES), row), pl.BlockSpec(memory_space=pl.ANY)] + in_specs,
            out_specs=[xspec] + out_specs, scratch_shapes=scratch + _combine_scratch(tm)),
        out_shape=[jax.ShapeDtypeStruct((t, D_MODEL), F32)] + out_shape,
        compiler_params=_params("arbitrary"),
        name="combine_in_proj",
    )(inv, x2, info, ys, *operands)


def _combine_scratch(tm):
    return [pltpu.VMEM((2, tm // MOE_TILE * SORTED_ROWS, D_MODEL), BF16), pltpu.SemaphoreType.DMA((2,))]


ATTN_HEADS_PER_STEP = 4
ATTN_MIN_SUM = 2.0 ** -40


def _score_bound(dq_gain, dk_gain):
    qscale = DIFF_QK_DIM ** -0.5 * math.log2(math.e)
    return (1.02 * qscale * DIFF_QK_DIM) * jnp.max(jnp.abs(dq_gain), axis=-1) * jnp.max(jnp.abs(dk_gain), axis=-1)


def _diff_attn_kernel(lam_ref, qt_ref, k_ref, vt_ref, hg_ref, o_ref,
                      qs_ref, sa_ref, sb_ref, m_ref, l_ref, acc_ref, *, t, layer, out_scale):
    qi = pl.program_id(2)
    hd = DIFF_HEAD_DIM
    heads = range(ATTN_HEADS_PER_STEP)
    lo = lax.broadcasted_iota(jnp.int32, (hd, t), 0) < DIFF_QK_DIM
    for h in heads:
        q = qt_ref[h * hd:(h + 1) * hd, :]
        zero = jnp.zeros_like(q)
        qs_ref[h, :, 0:t] = jnp.where(lo, q, zero)
        qs_ref[h, :, t:2 * t] = jnp.where(lo, zero, q)

    def causal(x, fill):
        kpos = lax.broadcasted_iota(jnp.int32, (t, 2 * t), 0)
        c = lax.broadcasted_iota(jnp.int32, (t, 2 * t), 1)
        return jnp.where(kpos <= jnp.where(c >= t, c - t, c), x, fill)

    def score(j, h):
        off = pl.multiple_of(j * t, t)
        return jnp.dot(k_ref[pl.ds(off, t), h * hd:(h + 1) * hd], qs_ref[h],
                       preferred_element_type=F32)

    bound = lam_ref[layer, 1]
    l_ref[...] = jnp.zeros(l_ref.shape, F32)
    acc_ref[...] = jnp.zeros(acc_ref.shape, F32)

    def fast_update(j, masked):
        for h in heads:
            p = jnp.exp2(score(j, h) - bound)
            if masked:
                p = causal(p, 0.0)
            l_ref[h] += jnp.sum(p, axis=0, keepdims=True)
            acc_ref[h] += jnp.dot(vt_ref[j, h * hd:(h + 1) * hd, :], p.astype(BF16),
                                  preferred_element_type=F32)

    def fast_pair(i, carry):
        fast_update(2 * i, False)
        fast_update(2 * i + 1, False)
        return carry

    lax.fori_loop(0, qi >> 1, fast_pair, 0)

    @pl.when((qi & 1) == 1)
    def _():
        fast_update(qi - 1, False)

    fast_update(qi, True)

    @pl.when(jnp.min(l_ref[...]) < ATTN_MIN_SUM)
    def _():
        m_ref[...] = jnp.full(m_ref.shape, NEG_BIG, F32)
        l_ref[...] = jnp.zeros(l_ref.shape, F32)
        acc_ref[...] = jnp.zeros(acc_ref.shape, F32)

        def scores(j, dst_ref):
            for h in heads:
                dst_ref[h] = score(j, h)

        def update(j, src_ref, masked):
            for h in heads:
                s = src_ref[h]
                if masked:
                    s = causal(s, NEG_BIG)
                m = m_ref[h]
                m_new = jnp.maximum(m, jnp.max(s, axis=0, keepdims=True))
                alpha = jnp.exp2(m - m_new)
                p = jnp.exp2(s - m_new)
                m_ref[h] = m_new
                l_ref[h] = alpha * l_ref[h] + jnp.sum(p, axis=0, keepdims=True)
                vt = vt_ref[j, h * hd:(h + 1) * hd, :]
                acc_ref[h] = alpha * acc_ref[h] + jnp.dot(vt, p.astype(BF16), preferred_element_type=F32)

        scores(0, sa_ref)

        def pair(i, carry):
            scores(2 * i + 1, sb_ref)
            update(2 * i, sa_ref, False)
            scores(2 * i + 2, sa_ref)
            update(2 * i + 1, sb_ref, False)
            return carry

        lax.fori_loop(0, qi >> 1, pair, 0)

        @pl.when((qi & 1) == 1)
        def _():
            scores(qi, sb_ref)
            update(qi - 1, sa_ref, False)
            update(qi, sb_ref, True)

        @pl.when((qi & 1) == 0)
        def _():
            update(qi, sa_ref, True)

    for h in heads:
        o = acc_ref[h] / l_ref[h]
        o = o[:, 0:t] - lam_ref[layer, 0] * o[:, t:2 * t]
        o = o * lax.rsqrt(jnp.mean(o * o, axis=0, keepdims=True) + EPS) * (hg_ref[...] * out_scale)
        o_ref[:, h * hd:(h + 1) * hd] = o.T.astype(BF16)


def _diff_attn(qt, k, vt, lam, hg_col, b, s, l, lam_init):
    t = SEQ_TILE
    nq = s // t
    hp = ATTN_HEADS_PER_STEP
    w = hp * DIFF_HEAD_DIM
    ng = DIFF_HEADS // hp
    kern = functools.partial(_diff_attn_kernel, t=t, layer=l, out_scale=1.0 - lam_init)
    return pl.pallas_call(
        kern,
        grid=(b, ng, nq),
        in_specs=[pl.BlockSpec(memory_space=pltpu.SMEM),
                  pl.BlockSpec((None, None, w, t), lambda bi, g, qi: (bi, qi, g, 0)),
                  pl.BlockSpec((s, w), lambda bi, g, qi: (bi, g)),
                  pl.BlockSpec((None, nq, w, t), lambda bi, g, qi: (bi, 0, g, 0)),
                  _lspec(l, (DIFF_HEAD_DIM, 1))],
        out_specs=pl.BlockSpec((t, w), lambda bi, g, qi: (bi * nq + qi, g)),
        out_shape=jax.ShapeDtypeStruct((b * s, GROUP_WIDTH), BF16),
        scratch_shapes=[pltpu.VMEM((hp, DIFF_HEAD_DIM, 2 * t), BF16),
                        pltpu.VMEM((hp, t, 2 * t), F32), pltpu.VMEM((hp, t, 2 * t), F32),
                        pltpu.VMEM((hp, 1, 2 * t), F32), pltpu.VMEM((hp, 1, 2 * t), F32),
                        pltpu.VMEM((hp, DIFF_HEAD_DIM, 2 * t), F32)],
        compiler_params=_params("arbitrary", "arbitrary", "arbitrary"),
        name="diff_attn",
    )(lam, qt, k, vt, hg_col)


def _mem_kv_kernel(mem_ref, mg_ref, w_ref, kg_ref, mk_ref, mv_ref):
    memn = _rms_rows(mem_ref[...], mg_ref[...]).astype(BF16)
    gw = GROUP_WIDTH
    k = jnp.dot(memn, w_ref[:, 0:gw], preferred_element_type=F32)
    for h in range(MEM_HEADS):
        sl = slice(h * MEM_HEAD_DIM, (h + 1) * MEM_HEAD_DIM)
        mk_ref[:, sl] = _rms_rows(k[:, sl], kg_ref[...]).astype(BF16)
    mv_ref[...] = jnp.dot(memn, w_ref[:, gw:2 * gw], preferred_element_type=F32).astype(BF16)


def _mem_kv(mem2, mg, w_kv, kg):
    r = mem2.shape[0]
    depth = w_kv.shape[0]
    tm = _tile(r, 512)
    gw = GROUP_WIDTH
    out = pl.BlockSpec((None, tm, gw), lambda l, i: (l, i, 0))
    return pl.pallas_call(
        _mem_kv_kernel,
        grid=(depth, r // tm),
        in_specs=[pl.BlockSpec((tm, D_MODEL), lambda l, i: (i, 0)), pl.BlockSpec((1, D_MODEL), lambda l, i: (0, 0)),
                  pl.BlockSpec((None, D_MODEL, 2 * gw), lambda l, i: (l, 0, 0)),
                  pl.BlockSpec((None, 1, MEM_HEAD_DIM), lambda l, i: (l, 0, 0))],
        out_specs=[out, out],
        out_shape=[jax.ShapeDtypeStruct((depth, r, gw), BF16), jax.ShapeDtypeStruct((depth, r, gw), BF16)],
        compiler_params=_params("arbitrary", "arbitrary"),
        name="mem_kv",
    )(mem2, mg, w_kv, kg)


MOE_TILE = 256
OUT_TILE = 1024
GRANULE = 16
TILE_SLOTS = MOE_TILE // GRANULE + N_GROUPS
SORTED_ROWS = TILE_SLOTS * GRANULE
STEP_GRANULES = 32
STEP_ROWS = STEP_GRANULES * GRANULE
XS_WIDTH = D_MODEL + LANES
INFO_POS_LANE = 0


def _route_t(lt):
    row_i = lax.broadcasted_iota(jnp.int32, lt.shape, 0)
    row = row_i.astype(F32)
    e0 = ROUTER_EXPERT_LANE0
    lg = jnp.where(row_i < N_GROUPS, lt, NEG_BIG)
    mg = jnp.max(lg, axis=0, keepdims=True)
    g_gate = 1.0 / jnp.sum(jnp.exp(lg - mg), axis=0, keepdims=True)
    g_idx = jnp.min(jnp.where(lg == mg, row, float(LANES)), axis=0, keepdims=True)
    row_group = ((row_i - e0) >> 3).astype(F32)
    sel = (row_i >= e0) & (row_i < e0 + N_EXPERTS) & (row_group == g_idx)
    le = jnp.where(sel, lt, NEG_BIG)
    m1 = jnp.max(le, axis=0, keepdims=True)
    se = jnp.sum(jnp.where(sel, jnp.exp(le - m1), 0.0), axis=0, keepdims=True)
    i1 = jnp.min(jnp.where(sel & (le == m1), row, float(LANES)), axis=0, keepdims=True)
    le2 = jnp.where(row == i1, NEG_BIG, le)
    m2 = jnp.max(le2, axis=0, keepdims=True)
    i2 = jnp.min(jnp.where(sel & (le2 == m2) & (row != i1), row, float(LANES)), axis=0, keepdims=True)
    p1 = 1.0 / se
    p2 = jnp.exp(m2 - m1) / se
    tot = p1 + p2
    w = jnp.where(row == i1, p1 / tot, jnp.where(row == i2, p2 / tot, 0.0))
    return g_gate * w, g_idx


def _split3(c):
    hi = c.astype(BF16).astype(F32)
    r1 = c - hi
    mid = r1.astype(BF16).astype(F32)
    lo = (r1 - mid).astype(BF16).astype(F32)
    return hi, mid, lo


def _sort_logits(x1, g2, rt_hi, rt_lo, rb_col):
    xn = _rms_rows(x1, g2)
    hi = xn.astype(BF16)
    lo = (xn - hi.astype(F32)).astype(BF16)
    logits_t = (lax.dot_general(rt_hi, hi, _NT, preferred_element_type=F32)
                + lax.dot_general(rt_hi, lo, _NT, preferred_element_type=F32)
                + lax.dot_general(rt_lo, hi, _NT, preferred_element_type=F32) + rb_col)
    return hi, logits_t


def _sort_rank(logits_t, earlier):
    tl = logits_t.shape[1]
    comb_t, g_idx = _route_t(logits_t)
    grow = lax.broadcasted_iota(jnp.int32, (SUBLANES, tl), 0).astype(F32)
    gt = jnp.where(grow == g_idx, 1.0, 0.0)
    before = jnp.dot(gt.astype(BF16), earlier, preferred_element_type=F32)
    return comb_t, gt, before


def _sort_emit(hi, comb_t, gt, before):
    tl = hi.shape[0]
    rank = jnp.sum(gt * before, axis=0, keepdims=True)
    cnt = jnp.sum(gt, axis=1, keepdims=True)
    glen = jnp.floor((cnt + (GRANULE - 1)) * (1.0 / GRANULE))
    r8 = lax.broadcasted_iota(jnp.int32, (SUBLANES, 1), 0)
    start = jnp.zeros((SUBLANES, 1), F32)
    for g in range(1, N_GROUPS):
        start = jnp.where(r8 == g, jnp.sum(jnp.where(r8 < g, glen, 0.0), axis=0, keepdims=True), start)
    pos = jnp.sum(gt * (start * GRANULE), axis=0, keepdims=True) + rank

    rows = lax.broadcasted_iota(jnp.int32, (LANES, tl), 0)
    info = jnp.where(rows == INFO_POS_LANE, pos, comb_t).T
    lane = _lane_iota(info.shape)
    e0 = ROUTER_EXPERT_LANE0
    c_hi, c_mid, c_lo = _split3(jnp.where((lane >= e0) & (lane < e0 + N_EXPERTS), info, 0.0))
    aug = (c_hi + pltpu.roll(c_mid, N_EXPERTS, axis=1) + pltpu.roll(c_lo, 2 * N_EXPERTS, axis=1)).astype(BF16)
    perm = jnp.where(pos == lax.broadcasted_iota(jnp.int32, (SORTED_ROWS, tl), 0).astype(F32),
                     1.0, 0.0).astype(BF16)
    xs = jnp.dot(perm, hi, preferred_element_type=F32).astype(BF16)
    xs_aug = jnp.dot(perm, aug, preferred_element_type=F32).astype(BF16)
    return xs, xs_aug, info, glen


def _out_sort_kernel(x_ref, yl_ref, yd_ref, ym_ref, w_ref, g2_ref, rthi_ref, rtlo_ref, rb_ref, earlier_ref,
                     o_ref, xs_ref, info_ref, lens_ref, x1s_ref):
    @pl.when(pl.program_id(0) == 0)
    def _():
        x1s_ref[...] = jnp.zeros_like(x1s_ref)

    tm = x1s_ref.shape[0]
    subs = range(tm // MOE_TILE)

    def project(c, nchunks):
        r = slice(c * (tm // nchunks), (c + 1) * (tm // nchunks))
        y = jnp.concatenate([yl_ref[r, :], yd_ref[r, :], ym_ref[r, :]], axis=-1)
        x1 = x_ref[r, :] + jnp.dot(y, w_ref[...], preferred_element_type=F32)
        o_ref[r, :] = x1
        return r, x1

    nch = len(subs)
    first_half = range(0, nch // 2)
    second_half = range(nch // 2, nch)
    fresh = [project(c, nch) for c in first_half]
    s1 = [_sort_logits(x1s_ref[sub * MOE_TILE:(sub + 1) * MOE_TILE, :], g2_ref[...], rthi_ref[...],
                       rtlo_ref[...], rb_ref[...]) for sub in subs]
    for r, x1 in fresh:
        x1s_ref[r, :] = x1
    fresh = [project(c, nch) for c in second_half]
    s2 = [_sort_rank(s1[sub][1], earlier_ref[...]) for sub in subs]
    for r, x1 in fresh:
        x1s_ref[r, :] = x1
    for sub in subs:
        xs, xs_aug, info, glen = _sort_emit(s1[sub][0], *s2[sub])
        r0 = sub * SORTED_ROWS
        xs_ref[r0:r0 + SORTED_ROWS, 0:D_MODEL] = xs
        xs_ref[r0:r0 + SORTED_ROWS, D_MODEL:XS_WIDTH] = xs_aug
        info_ref[sub * MOE_TILE:(sub + 1) * MOE_TILE, :] = info
        lens_ref[sub] = jnp.broadcast_to(glen, (SUBLANES, LANES))


def _out_sort(x2, y_lru, y_diff, y_mem, l, w_out, g2, rt_hi, rt_lo, rb_col, earlier):
    t = x2.shape[0]
    tm = OUT_TILE if t % OUT_TILE == 0 else SEQ_TILE
    assert t % tm == 0 and tm % MOE_TILE == 0
    sub = tm // MOE_TILE
    nt = t // MOE_TILE
    n = t // tm
    cur = lambda i: (jnp.minimum(i, n - 1), 0)
    prev = lambda i: (jnp.maximum(i - 1, 0), 0)
    gw = GROUP_WIDTH
    return pl.pallas_call(
        _out_sort_kernel,
        grid=(n + 1,),
        in_specs=[pl.BlockSpec((tm, D_MODEL), cur), pl.BlockSpec((tm, gw), cur), pl.BlockSpec((tm, gw), cur),
                  pl.BlockSpec((tm, gw), cur), _lspec(l, (3 * gw, D_MODEL)),
                  _lspec(l, (1, D_MODEL)), _lspec(l, (LANES, D_MODEL)),
                  _lspec(l, (LANES, D_MODEL)), _lspec(l, (LANES, 1)),
                  pl.BlockSpec((MOE_TILE, MOE_TILE), lambda i: (0, 0))],
        out_specs=[pl.BlockSpec((tm, D_MODEL), cur), pl.BlockSpec((sub * SORTED_ROWS, XS_WIDTH), prev),
                   pl.BlockSpec((tm, LANES), prev),
                   pl.BlockSpec((sub, SUBLANES, LANES), lambda i: (jnp.maximum(i - 1, 0), 0, 0))],
        out_shape=[jax.ShapeDtypeStruct((t, D_MODEL), F32),
                   jax.ShapeDtypeStruct((nt * SORTED_ROWS, XS_WIDTH), BF16),
                   jax.ShapeDtypeStruct((t, LANES), F32),
                   jax.ShapeDtypeStruct((nt, SUBLANES, LANES), F32)],
        scratch_shapes=[pltpu.VMEM((tm, D_MODEL), F32)],
        compiler_params=_params("arbitrary"),
        name="out_proj_sort",
    )(x2, y_lru, y_diff, y_mem, w_out, g2, rt_hi, rt_lo, rb_col, earlier)


def _moe_expert_kernel(sg_ref, sv_ref, gi_ref, xs_hbm, wg_ref, wu_ref, wd_ref, y_ref, xbuf_ref, sem_ref):
    s = pl.program_id(0)
    last = pl.num_programs(0) - 1
    slot = s & 1

    def start_fetch(step, dst_slot):
        _granule_copies(gi_ref, step, STEP_GRANULES, xs_hbm, xbuf_ref, dst_slot, sem_ref, True)

    def wait_fetch(step, dst_slot):
        _granule_copies(gi_ref, step, STEP_GRANULES, xs_hbm, xbuf_ref, dst_slot, sem_ref, False)

    @pl.when(s == 0)
    def _():
        start_fetch(0, 0)

    @pl.when((s == 0) | (sv_ref[jnp.maximum(s - 1, 0)] != 0))
    def _():
        wait_fetch(s, slot)

    @pl.when(sv_ref[s] == 0)
    def _():
        y_ref[...] = jnp.zeros_like(y_ref)

    @pl.when(sv_ref[s] != 0)
    def _():
        nxt = jnp.minimum(s + 1, last)
        start_fetch(nxt, 1 - slot)
        rows = xbuf_ref[slot]
        x = rows[:, 0:D_MODEL]
        aug = rows[:, D_MODEL:XS_WIDTH].astype(F32)
        lane = _lane_iota(aug.shape)
        e0 = ROUTER_EXPERT_LANE0
        comb = jnp.where((lane >= e0) & (lane < e0 + N_EXPERTS),
                         aug + pltpu.roll(aug, LANES - N_EXPERTS, axis=1)
                         + pltpu.roll(aug, LANES - 2 * N_EXPERTS, axis=1), 0.0)
        first = e0 + sg_ref[s] * EXPERTS_PER_GROUP
        hs = []
        for e in range(EXPERTS_PER_GROUP):
            cw = jnp.sum(jnp.where(lane == first + e, comb, 0.0), axis=-1, keepdims=True)
            hg = jnp.dot(x, wg_ref[e], preferred_element_type=F32)
            hu = jnp.dot(x, wu_ref[e], preferred_element_type=F32)
            hs.append((jax.nn.silu(hg) * hu * cw).astype(BF16))
        h = jnp.concatenate(hs, axis=-1)
        y_ref[...] = jnp.dot(h, wd_ref[...], preferred_element_type=F32).astype(BF16)

        @pl.when(s == last)
        def _():
            wait_fetch(nxt, 1 - slot)


def _moe_experts(step_group, step_valid, gran_idx, xs, l, wg, wu, wd):
    nstep = step_group.shape[0]
    epg = EXPERTS_PER_GROUP
    wmap4 = lambda s, sg, sv, gi: (l * N_GROUPS + sg[s], 0, 0, 0)
    in_specs = [pl.BlockSpec(memory_space=pl.ANY),
                pl.BlockSpec((None, epg, D_MODEL, D_EXPERT), wmap4),
                pl.BlockSpec((None, epg, D_MODEL, D_EXPERT), wmap4),
                pl.BlockSpec((None, epg * D_EXPERT, D_MODEL), lambda s, sg, sv, gi: (l * N_GROUPS + sg[s], 0, 0))]
    return pl.pallas_call(
        _moe_expert_kernel,
        grid_spec=pltpu.PrefetchScalarGridSpec(
            num_scalar_prefetch=3, grid=(nstep,), in_specs=in_specs,
            out_specs=pl.BlockSpec((STEP_ROWS, D_MODEL), lambda s, sg, sv, gi: (s, 0)),
            scratch_shapes=[pltpu.VMEM((2, STEP_ROWS, XS_WIDTH), BF16), pltpu.SemaphoreType.DMA((2,))]),
        out_shape=jax.ShapeDtypeStruct((nstep * STEP_ROWS, D_MODEL), BF16),
        compiler_params=_params("arbitrary"),
        name="moe_experts",
    )(step_group, step_valid, gran_idx, xs, wg, wu, wd)


def _moe_combine_kernel(inv_ref, x_ref, info_ref, ys_hbm, o_ref, ybuf_ref, sem_ref):
    o_ref[...] = _combined_tile(inv_ref, x_ref, info_ref, ys_hbm, ybuf_ref, sem_ref)


def _moe_combine(inv, x2, info, ys):
    t = x2.shape[0]
    tm = SEQ_TILE
    row = lambda i, inv: (i, 0)
    return pl.pallas_call(
        _moe_combine_kernel,
        grid_spec=pltpu.PrefetchScalarGridSpec(
            num_scalar_prefetch=1, grid=(t // tm,),
            in_specs=[pl.BlockSpec((tm, D_MODEL), row), pl.BlockSpec((tm, LANES), row),
                      pl.BlockSpec(memory_space=pl.ANY)],
            out_specs=pl.BlockSpec((tm, D_MODEL), row), scratch_shapes=_combine_scratch(tm)),
        out_shape=jax.ShapeDtypeStruct((t, D_MODEL), F32),
        compiler_params=_params("arbitrary"),
        name="moe_combine",
    )(inv, x2, info, ys)


def _moe_tables(lens, nt):
    i32 = jnp.int32
    ng = N_GROUPS
    garange = jnp.arange(ng, dtype=i32)
    cum = jnp.cumsum(lens, axis=1)
    start = cum - lens
    base = jnp.arange(nt, dtype=i32)[:, None] * TILE_SLOTS + start
    run_len = lens.T.reshape(-1)
    run_base = base.T.reshape(-1)
    run_end = jnp.cumsum(run_len)
    run_start = run_end - run_len
    n_g = jnp.sum(lens, axis=0)
    steps_g = (n_g + STEP_GRANULES - 1) // STEP_GRANULES
    step_end = jnp.cumsum(steps_g)
    step_off = step_end - steps_g
    gran_off = jnp.cumsum(n_g) - n_g
    nstep = (nt * (TILE_SLOTS - 1) + STEP_GRANULES - 1) // STEP_GRANULES + ng
    s = jnp.arange(nstep, dtype=i32)
    sg = jnp.minimum(jnp.sum((s[:, None] >= step_end[None, :]).astype(i32), axis=-1), ng - 1)
    goh = sg[:, None] == garange[None, :]
    pick = lambda v: jnp.sum(jnp.where(goh, v[None, :], 0), axis=-1)
    sv = s < step_end[-1]
    jl = (s - pick(step_off))[:, None] * STEP_GRANULES + jnp.arange(STEP_GRANULES, dtype=i32)[None, :]
    ok = sv[:, None] & (jl < pick(n_g)[:, None])
    j = jnp.where(ok, pick(gran_off)[:, None] + jl, 0)
    inrun = (j[..., None] >= run_start) & (j[..., None] < run_end)
    gran = j + jnp.sum(jnp.where(inrun, run_base - run_start, 0), axis=-1)
    gran = jnp.where(ok, gran, jnp.where(sv[:, None], gran[:, 0:1], 0)).astype(i32)
    q = jnp.arange(TILE_SLOTS, dtype=i32)[None, :]
    gq = jnp.minimum(jnp.sum((q[:, :, None] >= cum[:, None, :]).astype(i32), axis=-1), ng - 1)
    qoh = gq[..., None] == garange
    used = q < cum[:, -1:]
    per_g = run_start.reshape(ng, nt).T - start + (step_off * STEP_GRANULES - gran_off)[None, :]
    inv = jnp.where(used, q + jnp.sum(jnp.where(qoh, per_g[:, None, :], 0), axis=-1), 0).astype(i32)
    return sg, sv.astype(i32), gran.reshape(-1), inv.reshape(-1)


def _out_proj_moe(x2, y_lru, y_diff, y_mem, l, w_out, g2, rt_hi, rt_lo, rb_col, earlier, wg, wu, wd):
    nt = x2.shape[0] // MOE_TILE
    x1, xs, info, lens = _out_sort(x2, y_lru, y_diff, y_mem, l, w_out, g2, rt_hi, rt_lo, rb_col, earlier)
    sg, sv, gran, inv = _moe_tables(lens[:, 0:N_GROUPS, 0].astype(jnp.int32), nt)
    ys = _moe_experts(sg, sv, gran, xs, l, wg, wu, wd)
    return x1, (inv, info, ys)


def _block_diag(w):
    depth, h, n, _ = w.shape
    eye = jnp.eye(h, dtype=w.dtype)
    return (eye[None, :, None, :, None] * w[:, :, :, None, :]).reshape(depth, h * n, h * n)


def _router_tables(w_rg, b_rg, w_re, b_re):
    depth = w_rg.shape[0]
    e0 = ROUTER_EXPERT_LANE0
    pad = lambda rows, width: jnp.zeros((depth, rows, width), F32)
    w = jnp.concatenate([jnp.swapaxes(w_rg, 1, 2), pad(e0 - N_GROUPS, D_MODEL), jnp.swapaxes(w_re, 1, 2),
                         pad(LANES - e0 - N_EXPERTS, D_MODEL)], axis=1)
    bias = jnp.concatenate([b_rg[:, :, None], pad(e0 - N_GROUPS, 1), b_re[:, :, None],
                            pad(LANES - e0 - N_EXPERTS, 1)], axis=1)
    hi = w.astype(BF16)
    lo = (w - hi.astype(F32)).astype(BF16)
    return hi, lo, bias


def kernel(x, mem, norm1_g, w_in, conv_w, conv_b, rg_wa, rg_ba, rg_wx, rg_bx, rg_lambda, dq_norm_g, dk_norm_g, lambda_q1, lambda_k1, lambda_q2, lambda_k2, diff_head_norm_g, mem_norm_g, w_mem_kv, mq_norm_g, mk_norm_g, w_out, norm2_g, w_router_group, b_router_group, w_router_expert, b_router_expert, w_expert_gate, w_expert_up, w_expert_down):
    b, s, d = x.shape
    m = mem.shape[1]
    depth = w_in.shape[0]
    gw = GROUP_WIDTH
    epg = EXPERTS_PER_GROUP
    x2 = x.reshape(b * s, d)
    mem2 = mem.reshape(b * m, d)
    row = lambda v: v.reshape(depth, 1, -1).astype(F32)
    col = lambda v: v.reshape(depth, -1, 1).astype(F32)
    lam_inits = [0.8 - 0.6 * math.exp(-0.3 * l) for l in range(depth)]
    lam = (jnp.exp(jnp.sum(lambda_q1 * lambda_k1, axis=-1)) - jnp.exp(jnp.sum(lambda_q2 * lambda_k2, axis=-1))
           + jnp.asarray(lam_inits, F32))
    lam = jnp.stack([lam, _score_bound(dq_norm_g, dk_norm_g)], axis=1).astype(F32)
    w_in_bf = w_in.astype(BF16)
    w_qv_t = jnp.swapaxes(jnp.concatenate([w_in_bf[:, :, 2 * gw:3 * gw], w_in_bf[:, :, 4 * gw:5 * gw]], axis=2), 1, 2)
    qg_col = col(jnp.tile(dq_norm_g, (1, gw // DIFF_QK_DIM)))
    kg = row(jnp.tile(dk_norm_g, (1, 2)))
    wa_bd = _block_diag(rg_wa).astype(BF16)
    wx_bd = _block_diag(rg_wx).astype(BF16)
    w_kv_bf = w_mem_kv.astype(BF16)
    w_out_bf = w_out.astype(BF16)
    rt_hi, rt_lo, rb_col = _router_tables(w_router_group, b_router_group, w_router_expert, b_router_expert)
    earlier = jnp.triu(jnp.ones((MOE_TILE, MOE_TILE), BF16), k=1)
    wg = w_expert_gate.astype(BF16).reshape(depth * N_GROUPS, epg, D_MODEL, D_EXPERT)
    wu = w_expert_up.astype(BF16).reshape(depth * N_GROUPS, epg, D_MODEL, D_EXPERT)
    wd = w_expert_down.astype(BF16).reshape(depth * N_GROUPS, epg * D_EXPERT, D_MODEL)
    g1, g2, mqg, mkg = row(norm1_g), row(norm2_g), row(mq_norm_g), row(mk_norm_g)
    conv_b3, ba, bx, lru_lam, hg_col = row(conv_b), row(rg_ba), row(rg_bx), row(rg_lambda), col(diff_head_norm_g)
    mem_g = mem_norm_g.reshape(1, -1).astype(F32)
    mk, mv = _mem_kv(mem2, mem_g, w_kv_bf, mkg)
    moe = None
    for l in range(depth):
        proj = _in_proj(x2, b, s, l, g1, w_in_bf, w_qv_t, qg_col, kg, mqg, mk, mv,
                        (conv_w, conv_b3, wa_bd, ba, wx_bd, bx, lru_lam), moe)
        if moe is not None:
            x2, *proj = proj
        y_lru, k, y_mem, qt, vt = proj
        y_diff = _diff_attn(qt, k, vt, lam, hg_col, b, s, l, lam_inits[l])
        x2, moe = _out_proj_moe(x2, y_lru, y_diff, y_mem, l, w_out_bf, g2, rt_hi, rt_lo, rb_col, earlier,
                                wg, wu, wd)
    return _moe_combine(moe[0], x2, moe[1], moe[2]).reshape(b, s, d)
```

```python
import functools
import math

import jax
import jax.numpy as jnp
from jax import lax
from jax.experimental import pallas as pl
from jax.experimental.pallas import tpu as pltpu

F32 = jnp.float32
BF16 = jnp.bfloat16

D_MODEL = 1024
GROUP_WIDTH = D_MODEL // 2
CONV_WIDTH = 4
LRU_C = 8.0
DIFF_HEADS = 4
DIFF_HEAD_DIM = GROUP_WIDTH // DIFF_HEADS
DIFF_QK_DIM = DIFF_HEAD_DIM // 2
MEM_HEADS = 4
MEM_HEAD_DIM = GROUP_WIDTH // MEM_HEADS
N_GROUPS = 4
EXPERTS_PER_GROUP = 8
N_EXPERTS = N_GROUPS * EXPERTS_PER_GROUP
D_EXPERT = 256
EPS = 1e-6

LANES = 128
SUBLANES = 8
VMEM_LIMIT = 56 * 1024 * 1024
NEG_BIG = -1e30
ROUTER_EXPERT_LANE0 = 32


def _tile(n, pref):
    t = min(n, pref)
    assert n % t == 0, (n, t)
    return t


def _params(*sem):
    return pltpu.CompilerParams(dimension_semantics=sem, vmem_limit_bytes=VMEM_LIMIT)


def _lspec(l, tail, blk=None):
    idx = (l,) + tuple(blk if blk is not None else (0,) * len(tail))
    return pl.BlockSpec((None,) + tuple(tail), lambda *_: idx)


def _rms_rows(x, g):
    return x * lax.rsqrt(jnp.mean(x * x, axis=-1, keepdims=True) + EPS) * g


def _lane_iota(shape):
    return lax.broadcasted_iota(jnp.int32, shape, len(shape) - 1)


def _half_head_norm(z, g):
    sq = z * z
    lo = _lane_iota(z.shape) < DIFF_QK_DIM
    s_all = jnp.sum(sq, axis=-1, keepdims=True)
    s_lo = jnp.sum(jnp.where(lo, sq, 0.0), axis=-1, keepdims=True)
    inv_lo = lax.rsqrt(s_lo * (1.0 / DIFF_QK_DIM) + EPS)
    inv_hi = lax.rsqrt((s_all - s_lo) * (1.0 / DIFF_QK_DIM) + EPS)
    return z * jnp.where(lo, inv_lo, inv_hi) * g


SEQ_TILE = 512
_NT = (((1,), (1,)), ((), ()))


N_PROJ_PARAMS = 18
N_PROJ_OUTS = 5


def _softplus(z):
    return jnp.maximum(z, 0.0) + jnp.log(1.0 + jnp.exp(-jnp.abs(z)))


def _in_proj_body(x, ns, g1_ref, wl_ref, wk_ref, wmq_ref, wqt_ref, wvt_ref, qg_ref, kg_ref, mqg_ref,
                  mk_ref, mv_ref, cw_ref, cb_ref, wa_ref, ba_ref, wx_ref, bx_ref, llam_ref,
                  ylru_ref, k_ref, ym_ref, qt_ref, vt_ref, ext_ref, hcar_ref, a_ref, b_ref, h_ref):
    gw = GROUP_WIDTH
    tm = x.shape[0]
    hist = SUBLANES
    first = lax.rem(pl.program_id(0), ns) == 0

    @pl.when(first)
    def _():
        ext_ref[0:hist, :] = jnp.zeros((hist, gw), F32)
        hcar_ref[...] = jnp.zeros_like(hcar_ref)

    @pl.when(jnp.logical_not(first))
    def _():
        ext_ref[0:hist, :] = ext_ref[tm:tm + hist, :]

    xn = _rms_rows(x, g1_ref[...]).astype(BF16)
    heads = range(MEM_HEADS)
    hsl = [slice(h * MEM_HEAD_DIM, (h + 1) * MEM_HEAD_DIM) for h in heads]
    mq = jnp.dot(xn, wmq_ref[...], preferred_element_type=F32)
    u = jnp.dot(xn, wl_ref[...], preferred_element_type=F32)
    ext_ref[hist:hist + tm, :] = u[:, 0:gw]
    xc = cb_ref[...] + cw_ref[CONV_WIDTH - 1:CONV_WIDTH, :] * u[:, 0:gw]
    for j in range(CONV_WIDTH - 1):
        off = hist - (CONV_WIDTH - 1) + j
        xc = xc + cw_ref[j:j + 1, :] * ext_ref[off:off + tm, :]
    xcb = xc.astype(BF16)
    sc = [lax.dot_general(_rms_rows(mq[:, hsl[h]], mqg_ref[...]).astype(BF16), mk_ref[:, hsl[h]], _NT,
                          preferred_element_type=F32) * MEM_HEAD_DIM ** -0.5 for h in heads]
    k = jnp.dot(xn, wk_ref[...], preferred_element_type=F32)
    r_pre = jnp.dot(xcb, wa_ref[...], preferred_element_type=F32)
    i_pre = jnp.dot(xcb, wx_ref[...], preferred_element_type=F32)
    p = [jnp.exp(sc[h] - jnp.max(sc[h], axis=-1, keepdims=True)) for h in heads]
    qt = lax.dot_general(wqt_ref[...], xn, _NT, preferred_element_type=F32)
    o = [jnp.dot(p[h].astype(BF16), mv_ref[:, hsl[h]], preferred_element_type=F32) for h in heads]
    vt_ref[...] = lax.dot_general(wvt_ref[...], xn, _NT, preferred_element_type=F32).astype(BF16)

    r = jax.nn.sigmoid(r_pre + ba_ref[...])
    gate_i = jax.nn.sigmoid(i_pre + bx_ref[...])
    a = jnp.exp((-LRU_C * r) * _softplus(-llam_ref[...]))
    om = 1.0 - a * a
    b = om * lax.rsqrt(jnp.maximum(om, 1e-30)) * (gate_i * xc)
    a = a.reshape(tm // SUBLANES, SUBLANES, gw)
    b = b.reshape(tm // SUBLANES, SUBLANES, gw)
    row = lax.broadcasted_iota(jnp.int32, a.shape, 1)
    d = 1
    while d < SUBLANES:
        keep = row >= d
        a_prev = pltpu.roll(a, d, axis=1)
        b_prev = pltpu.roll(b, d, axis=1)
        b = jnp.where(keep, a * b_prev + b, b)
        a = jnp.where(keep, a * a_prev, a)
        d *= 2
    a_ref[...] = a.reshape(tm, gw)
    b_ref[...] = b.reshape(tm, gw)
    gated = jax.nn.gelu(u[:, gw:2 * gw])

    for h in range(DIFF_HEADS):
        sl = slice(h * LANES, (h + 1) * LANES)
        k_ref[:, sl] = _half_head_norm(k[:, sl], kg_ref[...]).astype(BF16)
    for h in heads:
        ym_ref[:, hsl[h]] = (o[h] / jnp.sum(p[h], axis=-1, keepdims=True)).astype(BF16)
    q3 = qt.reshape(gw // DIFF_QK_DIM, DIFF_QK_DIM, tm)
    q3 = q3 * lax.rsqrt(jnp.mean(q3 * q3, axis=1, keepdims=True) + EPS)
    qscale = DIFF_QK_DIM ** -0.5 * math.log2(math.e)
    qt_ref[...] = (q3.reshape(gw, tm) * (qg_ref[...] * qscale)).astype(BF16)

    def block(i, hprev):
        off = pl.multiple_of(i * SUBLANES, SUBLANES)
        hb = a_ref[pl.ds(off, SUBLANES), :] * hprev + b_ref[pl.ds(off, SUBLANES), :]
        h_ref[pl.ds(off, SUBLANES), :] = hb
        return jnp.broadcast_to(hb[SUBLANES - 1:SUBLANES, :], (SUBLANES, gw))

    hcar_ref[...] = lax.fori_loop(0, tm // SUBLANES, block, hcar_ref[...], unroll=8)
    ylru_ref[...] = (h_ref[...] * gated).astype(BF16)


def _in_proj_kernel(x_ref, *refs, ns):
    _in_proj_body(x_ref[...], ns, *refs)


def _granule_copies(tab_ref, step, n, src_hbm, buf_ref, slot, sem_ref, start):
    for k in range(n):
        g = tab_ref[step * n + k]
        cp = pltpu.make_async_copy(
            src_hbm.at[pl.ds(pl.multiple_of(g * GRANULE, GRANULE), GRANULE), :],
            buf_ref.at[slot, pl.ds(k * GRANULE, GRANULE), :], sem_ref.at[slot])
        if start:
            cp.start()
        else:
            cp.wait()


def _unsort(info, ys):
    kpad = -SORTED_ROWS % LANES
    ys = jnp.concatenate([ys, jnp.zeros((kpad, D_MODEL), BF16)], axis=0)
    pos = info[:, INFO_POS_LANE:INFO_POS_LANE + 1]
    unperm = jnp.where(pos == _lane_iota((MOE_TILE, SORTED_ROWS + kpad)).astype(F32), 1.0, 0.0).astype(BF16)
    return jnp.dot(unperm, ys, preferred_element_type=F32)


def _combined_tile(inv_ref, x1_ref, info_ref, ys_hbm, ybuf_ref, sem_ref):
    s = pl.program_id(0)
    last = pl.num_programs(0) - 1
    slot = s & 1
    nsub = x1_ref.shape[0] // MOE_TILE
    n = nsub * TILE_SLOTS

    @pl.when(s == 0)
    def _():
        _granule_copies(inv_ref, 0, n, ys_hbm, ybuf_ref, 0, sem_ref, True)

    _granule_copies(inv_ref, s, n, ys_hbm, ybuf_ref, slot, sem_ref, False)
    nxt = jnp.minimum(s + 1, last)
    _granule_copies(inv_ref, nxt, n, ys_hbm, ybuf_ref, 1 - slot, sem_ref, True)
    x = jnp.concatenate(
        [x1_ref[u * MOE_TILE:(u + 1) * MOE_TILE, :]
         + _unsort(info_ref[u * MOE_TILE:(u + 1) * MOE_TILE, :],
                   ybuf_ref[slot, u * SORTED_ROWS:(u + 1) * SORTED_ROWS, :])
         for u in range(nsub)], axis=0)

    @pl.when(s == last)
    def _():
        _granule_copies(inv_ref, nxt, n, ys_hbm, ybuf_ref, 1 - slot, sem_ref, False)

    return x


def _combine_in_proj_kernel(inv_ref, x1_ref, info_ref, ys_hbm, *refs, ns):
    params, x_out_ref, outs = refs[:N_PROJ_PARAMS], refs[N_PROJ_PARAMS], refs[N_PROJ_PARAMS + 1:-2]
    x = _combined_tile(inv_ref, x1_ref, info_ref, ys_hbm, *refs[-2:])
    x_out_ref[...] = x
    _in_proj_body(x, ns, *params, *outs)


def _in_proj(x2, b, s, l, g1, w_in, w_in_t, qg_col, kg, mqg, mk, mv, lru, moe=None):
    t = x2.shape[0]
    tm = SEQ_TILE
    assert s % tm == 0
    ns = s // tm
    m = mk.shape[1] // b
    gw = GROUP_WIDTH
    row = lambda i, *_: (i, 0)
    fm = lambda i, *_: (i // ns, i % ns, 0, 0)
    mem = pl.BlockSpec((None, m, gw), lambda i, *_: (l, i // ns, 0))
    in_specs = [_lspec(l, (1, D_MODEL)),
                _lspec(l, (D_MODEL, 2 * gw), (0, 0)), _lspec(l, (D_MODEL, gw), (0, 3)),
                _lspec(l, (D_MODEL, gw), (0, 5)), _lspec(l, (gw, D_MODEL), (0, 0)),
                _lspec(l, (gw, D_MODEL), (1, 0)), _lspec(l, (gw, 1)),
                _lspec(l, (1, LANES)), _lspec(l, (1, MEM_HEAD_DIM)), mem, mem]
    vec = _lspec(l, (1, gw))
    in_specs += [_lspec(l, (CONV_WIDTH, gw)), vec, _lspec(l, (gw, gw)), vec, _lspec(l, (gw, gw)), vec, vec]
    operands = (g1, w_in, w_in, w_in, w_in_t, w_in_t, qg_col, kg, mqg, mk, mv) + tuple(lru)
    assert len(operands) == N_PROJ_PARAMS
    out_specs = [pl.BlockSpec((tm, gw), row), pl.BlockSpec((tm, gw), row), pl.BlockSpec((tm, gw), row),
                 pl.BlockSpec((None, None, gw, tm), fm), pl.BlockSpec((None, None, gw, tm), fm)]
    out_shape = [jax.ShapeDtypeStruct((t, gw), BF16), jax.ShapeDtypeStruct((t, gw), BF16),
                 jax.ShapeDtypeStruct((t, gw), BF16), jax.ShapeDtypeStruct((b, ns, gw, tm), BF16),
                 jax.ShapeDtypeStruct((b, ns, gw, tm), BF16)]
    assert len(out_specs) == N_PROJ_OUTS
    scratch = [pltpu.VMEM((tm + SUBLANES, gw), F32), pltpu.VMEM((SUBLANES, gw), F32),
               pltpu.VMEM((tm, gw), F32), pltpu.VMEM((tm, gw), F32), pltpu.VMEM((tm, gw), F32)]
    xspec = pl.BlockSpec((tm, D_MODEL), row)
    if moe is None:
        return pl.pallas_call(
            functools.partial(_in_proj_kernel, ns=ns), grid=(t // tm,), in_specs=[xspec] + in_specs,
            out_specs=out_specs, out_shape=out_shape, scratch_shapes=scratch,
            compiler_params=_params("arbitrary"), name="in_proj",
        )(x2, *operands)
    inv, info, ys = moe
    return pl.pallas_call(
        functools.partial(_combine_in_proj_kernel, ns=ns),
        grid_spec=pltpu.PrefetchScalarGridSpec(
            num_scalar_prefetch=1, grid=(t // tm,),
            in_specs=[xspec, pl.BlockSpec((tm, LANES), row), pl.BlockSpec(memory_space=pl.ANY)] + in_specs,
            out_specs=[xspec] + out_specs, scratch_shapes=scratch + _combine_scratch(tm)),
        out_shape=[jax.ShapeDtypeStruct((t, D_MODEL), F32)] + out_shape,
        compiler_params=_params("arbitrary"),
        name="combine_in_proj",
    )(inv, x2, info, ys, *operands)


def _combine_scratch(tm):
    return [pltpu.VMEM((2, tm // MOE_TILE * SORTED_ROWS, D_MODEL), BF16), pltpu.SemaphoreType.DMA((2,))]


ATTN_HEADS_PER_STEP = 4
ATTN_MIN_SUM = 2.0 ** -40


def _score_bound(dq_gain, dk_gain):
    qscale = DIFF_QK_DIM ** -0.5 * math.log2(math.e)
    return (1.02 * qscale * DIFF_QK_DIM) * jnp.max(jnp.abs(dq_gain), axis=-1) * jnp.max(jnp.abs(dk_gain), axis=-1)


def _diff_attn_kernel(lam_ref, qt_ref, k_ref, vt_ref, hg_ref, o_ref,
                      qs_ref, sa_ref, sb_ref, m_ref, l_ref, acc_ref, *, t, layer, out_scale):
    qi = pl.program_id(2)
    hd = DIFF_HEAD_DIM
    heads = range(ATTN_HEADS_PER_STEP)
    lo = lax.broadcasted_iota(jnp.int32, (hd, t), 0) < DIFF_QK_DIM
    for h in heads:
        q = qt_ref[h * hd:(h + 1) * hd, :]
        zero = jnp.zeros_like(q)
        qs_ref[h, :, 0:t] = jnp.where(lo, q, zero)
        qs_ref[h, :, t:2 * t] = jnp.where(lo, zero, q)

    def causal(x, fill):
        kpos = lax.broadcasted_iota(jnp.int32, (t, 2 * t), 0)
        c = lax.broadcasted_iota(jnp.int32, (t, 2 * t), 1)
        return jnp.where(kpos <= jnp.where(c >= t, c - t, c), x, fill)

    def score(j, h):
        off = pl.multiple_of(j * t, t)
        return jnp.dot(k_ref[pl.ds(off, t), h * hd:(h + 1) * hd], qs_ref[h],
                       preferred_element_type=F32)

    bound = lam_ref[layer, 1]
    l_ref[...] = jnp.zeros(l_ref.shape, F32)
    acc_ref[...] = jnp.zeros(acc_ref.shape, F32)

    def fast_update(j, masked):
        for h in heads:
            p = jnp.exp2(score(j, h) - bound)
            if masked:
                p = causal(p, 0.0)
            l_ref[h] += jnp.sum(p, axis=0, keepdims=True)
            acc_ref[h] += jnp.dot(vt_ref[j, h * hd:(h + 1) * hd, :], p.astype(BF16),
                                  preferred_element_type=F32)

    def fast_pair(i, carry):
        fast_update(2 * i, False)
        fast_update(2 * i + 1, False)
        return carry

    lax.fori_loop(0, qi >> 1, fast_pair, 0)

    @pl.when((qi & 1) == 1)
    def _():
        fast_update(qi - 1, False)

    fast_update(qi, True)

    @pl.when(jnp.min(l_ref[...]) < ATTN_MIN_SUM)
    def _():
        m_ref[...] = jnp.full(m_ref.shape, NEG_BIG, F32)
        l_ref[...] = jnp.zeros(l_ref.shape, F32)
        acc_ref[...] = jnp.zeros(acc_ref.shape, F32)

        def scores(j, dst_ref):
            for h in heads:
                dst_ref[h] = score(j, h)

        def update(j, src_ref, masked):
            for h in heads:
                s = src_ref[h]
                if masked:
                    s = causal(s, NEG_BIG)
                m = m_ref[h]
                m_new = jnp.maximum(m, jnp.max(s, axis=0, keepdims=True))
                alpha = jnp.exp2(m - m_new)
                p = jnp.exp2(s - m_new)
                m_ref[h] = m_new
                l_ref[h] = alpha * l_ref[h] + jnp.sum(p, axis=0, keepdims=True)
                vt = vt_ref[j, h * hd:(h + 1) * hd, :]
                acc_ref[h] = alpha * acc_ref[h] + jnp.dot(vt, p.astype(BF16), preferred_element_type=F32)

        scores(0, sa_ref)

        def pair(i, carry):
            scores(2 * i + 1, sb_ref)
            update(2 * i, sa_ref, False)
            scores(2 * i + 2, sa_ref)
            update(2 * i + 1, sb_ref, False)
            return carry

        lax.fori_loop(0, qi >> 1, pair, 0)

        @pl.when((qi & 1) == 1)
        def _():
            scores(qi, sb_ref)
            update(qi - 1, sa_ref, False)
            update(qi, sb_ref, True)

        @pl.when((qi & 1) == 0)
        def _():
            update(qi, sa_ref, True)

    for h in heads:
        o = acc_ref[h] / l_ref[h]
        o = o[:, 0:t] - lam_ref[layer, 0] * o[:, t:2 * t]
        o = o * lax.rsqrt(jnp.mean(o * o, axis=0, keepdims=True) + EPS) * (hg_ref[...] * out_scale)
        o_ref[:, h * hd:(h + 1) * hd] = o.T.astype(BF16)


def _diff_attn(qt, k, vt, lam, hg_col, b, s, l, lam_init):
    t = SEQ_TILE
    nq = s // t
    hp = ATTN_HEADS_PER_STEP
    w = hp * DIFF_HEAD_DIM
    ng = DIFF_HEADS // hp
    kern = functools.partial(_diff_attn_kernel, t=t, layer=l, out_scale=1.0 - lam_init)
    return pl.pallas_call(
        kern,
        grid=(b, ng, nq),
        in_specs=[pl.BlockSpec(memory_space=pltpu.SMEM),
                  pl.BlockSpec((None, None, w, t), lambda bi, g, qi: (bi, qi, g, 0)),
                  pl.BlockSpec((s, w), lambda bi, g, qi: (bi, g)),
                  pl.BlockSpec((None, nq, w, t), lambda bi, g, qi: (bi, 0, g, 0)),
                  _lspec(l, (DIFF_HEAD_DIM, 1))],
        out_specs=pl.BlockSpec((t, w), lambda bi, g, qi: (bi * nq + qi, g)),
        out_shape=jax.ShapeDtypeStruct((b * s, GROUP_WIDTH), BF16),
        scratch_shapes=[pltpu.VMEM((hp, DIFF_HEAD_DIM, 2 * t), BF16),
                        pltpu.VMEM((hp, t, 2 * t), F32), pltpu.VMEM((hp, t, 2 * t), F32),
                        pltpu.VMEM((hp, 1, 2 * t), F32), pltpu.VMEM((hp, 1, 2 * t), F32),
                        pltpu.VMEM((hp, DIFF_HEAD_DIM, 2 * t), F32)],
        compiler_params=_params("arbitrary", "arbitrary", "arbitrary"),
        name="diff_attn",
    )(lam, qt, k, vt, hg_col)


def _mem_kv_kernel(mem_ref, mg_ref, w_ref, kg_ref, mk_ref, mv_ref):
    memn = _rms_rows(mem_ref[...], mg_ref[...]).astype(BF16)
    gw = GROUP_WIDTH
    k = jnp.dot(memn, w_ref[:, 0:gw], preferred_element_type=F32)
    for h in range(MEM_HEADS):
        sl = slice(h * MEM_HEAD_DIM, (h + 1) * MEM_HEAD_DIM)
        mk_ref[:, sl] = _rms_rows(k[:, sl], kg_ref[...]).astype(BF16)
    mv_ref[...] = jnp.dot(memn, w_ref[:, gw:2 * gw], preferred_element_type=F32).astype(BF16)


def _mem_kv(mem2, mg, w_kv, kg):
    r = mem2.shape[0]
    depth = w_kv.shape[0]
    tm = _tile(r, 512)
    gw = GROUP_WIDTH
    out = pl.BlockSpec((None, tm, gw), lambda l, i: (l, i, 0))
    return pl.pallas_call(
        _mem_kv_kernel,
        grid=(depth, r // tm),
        in_specs=[pl.BlockSpec((tm, D_MODEL), lambda l, i: (i, 0)), pl.BlockSpec((1, D_MODEL), lambda l, i: (0, 0)),
                  pl.BlockSpec((None, D_MODEL, 2 * gw), lambda l, i: (l, 0, 0)),
                  pl.BlockSpec((None, 1, MEM_HEAD_DIM), lambda l, i: (l, 0, 0))],
        out_specs=[out, out],
        out_shape=[jax.ShapeDtypeStruct((depth, r, gw), BF16), jax.ShapeDtypeStruct((depth, r, gw), BF16)],
        compiler_params=_params("arbitrary", "arbitrary"),
        name="mem_kv",
    )(mem2, mg, w_kv, kg)


MOE_TILE = 256
OUT_TILE = 1024
GRANULE = 16
TILE_SLOTS = MOE_TILE // GRANULE + N_GROUPS
SORTED_ROWS = TILE_SLOTS * GRANULE
STEP_GRANULES = 32
STEP_ROWS = STEP_GRANULES * GRANULE
EXPERT_BUFFERS = 3
XS_WIDTH = D_MODEL + LANES
INFO_POS_LANE = 0


def _route_t(lt):
    row_i = lax.broadcasted_iota(jnp.int32, lt.shape, 0)
    row = row_i.astype(F32)
    e0 = ROUTER_EXPERT_LANE0
    lg = jnp.where(row_i < N_GROUPS, lt, NEG_BIG)
    mg = jnp.max(lg, axis=0, keepdims=True)
    g_gate = 1.0 / jnp.sum(jnp.exp(lg - mg), axis=0, keepdims=True)
    g_idx = jnp.min(jnp.where(lg == mg, row, float(LANES)), axis=0, keepdims=True)
    row_group = ((row_i - e0) >> 3).astype(F32)
    sel = (row_i >= e0) & (row_i < e0 + N_EXPERTS) & (row_group == g_idx)
    le = jnp.where(sel, lt, NEG_BIG)
    m1 = jnp.max(le, axis=0, keepdims=True)
    se = jnp.sum(jnp.where(sel, jnp.exp(le - m1), 0.0), axis=0, keepdims=True)
    i1 = jnp.min(jnp.where(sel & (le == m1), row, float(LANES)), axis=0, keepdims=True)
    le2 = jnp.where(row == i1, NEG_BIG, le)
    m2 = jnp.max(le2, axis=0, keepdims=True)
    i2 = jnp.min(jnp.where(sel & (le2 == m2) & (row != i1), row, float(LANES)), axis=0, keepdims=True)
    p1 = 1.0 / se
    p2 = jnp.exp(m2 - m1) / se
    tot = p1 + p2
    w = jnp.where(row == i1, p1 / tot, jnp.where(row == i2, p2 / tot, 0.0))
    return g_gate * w, g_idx


def _split3(c):
    hi = c.astype(BF16).astype(F32)
    r1 = c - hi
    mid = r1.astype(BF16).astype(F32)
    lo = (r1 - mid).astype(BF16).astype(F32)
    return hi, mid, lo


def _sort_logits(x1, g2, rt_hi, rt_lo, rb_col):
    xn = _rms_rows(x1, g2)
    hi = xn.astype(BF16)
    lo = (xn - hi.astype(F32)).astype(BF16)
    logits_t = (lax.dot_general(rt_hi, hi, _NT, preferred_element_type=F32)
                + lax.dot_general(rt_hi, lo, _NT, preferred_element_type=F32)
                + lax.dot_general(rt_lo, hi, _NT, preferred_element_type=F32) + rb_col)
    return hi, logits_t


def _sort_rank(logits_t, earlier):
    tl = logits_t.shape[1]
    comb_t, g_idx = _route_t(logits_t)
    grow = lax.broadcasted_iota(jnp.int32, (SUBLANES, tl), 0).astype(F32)
    gt = jnp.where(grow == g_idx, 1.0, 0.0)
    before = jnp.dot(gt.astype(BF16), earlier, preferred_element_type=F32)
    return comb_t, gt, before


def _sort_emit(hi, comb_t, gt, before):
    tl = hi.shape[0]
    rank = jnp.sum(gt * before, axis=0, keepdims=True)
    cnt = jnp.sum(gt, axis=1, keepdims=True)
    glen = jnp.floor((cnt + (GRANULE - 1)) * (1.0 / GRANULE))
    r8 = lax.broadcasted_iota(jnp.int32, (SUBLANES, 1), 0)
    start = jnp.zeros((SUBLANES, 1), F32)
    for g in range(1, N_GROUPS):
        start = jnp.where(r8 == g, jnp.sum(jnp.where(r8 < g, glen, 0.0), axis=0, keepdims=True), start)
    pos = jnp.sum(gt * (start * GRANULE), axis=0, keepdims=True) + rank

    rows = lax.broadcasted_iota(jnp.int32, (LANES, tl), 0)
    info = jnp.where(rows == INFO_POS_LANE, pos, comb_t).T
    lane = _lane_iota(info.shape)
    e0 = ROUTER_EXPERT_LANE0
    c_hi, c_mid, c_lo = _split3(jnp.where((lane >= e0) & (lane < e0 + N_EXPERTS), info, 0.0))
    aug = (c_hi + pltpu.roll(c_mid, N_EXPERTS, axis=1) + pltpu.roll(c_lo, 2 * N_EXPERTS, axis=1)).astype(BF16)
    perm = jnp.where(pos == lax.broadcasted_iota(jnp.int32, (SORTED_ROWS, tl), 0).astype(F32),
                     1.0, 0.0).astype(BF16)
    xs = jnp.dot(perm, hi, preferred_element_type=F32).astype(BF16)
    xs_aug = jnp.dot(perm, aug, preferred_element_type=F32).astype(BF16)
    return xs, xs_aug, info, glen


def _out_sort_kernel(x_ref, yl_ref, yd_ref, ym_ref, w_ref, g2_ref, rthi_ref, rtlo_ref, rb_ref, earlier_ref,
                     o_ref, xs_ref, info_ref, lens_ref, x1s_ref):
    @pl.when(pl.program_id(0) == 0)
    def _():
        x1s_ref[...] = jnp.zeros_like(x1s_ref)

    tm = x1s_ref.shape[0]
    subs = range(tm // MOE_TILE)

    def project(c, nchunks):
        r = slice(c * (tm // nchunks), (c + 1) * (tm // nchunks))
        y = jnp.concatenate([yl_ref[r, :], yd_ref[r, :], ym_ref[r, :]], axis=-1)
        x1 = x_ref[r, :] + jnp.dot(y, w_ref[...], preferred_element_type=F32)
        o_ref[r, :] = x1
        return r, x1

    nch = len(subs)
    first_half = range(0, nch // 2)
    second_half = range(nch // 2, nch)
    fresh = [project(c, nch) for c in first_half]
    s1 = [_sort_logits(x1s_ref[sub * MOE_TILE:(sub + 1) * MOE_TILE, :], g2_ref[...], rthi_ref[...],
                       rtlo_ref[...], rb_ref[...]) for sub in subs]
    for r, x1 in fresh:
        x1s_ref[r, :] = x1
    fresh = [project(c, nch) for c in second_half]
    s2 = [_sort_rank(s1[sub][1], earlier_ref[...]) for sub in subs]
    for r, x1 in fresh:
        x1s_ref[r, :] = x1
    for sub in subs:
        xs, xs_aug, info, glen = _sort_emit(s1[sub][0], *s2[sub])
        r0 = sub * SORTED_ROWS
        xs_ref[r0:r0 + SORTED_ROWS, 0:D_MODEL] = xs
        xs_ref[r0:r0 + SORTED_ROWS, D_MODEL:XS_WIDTH] = xs_aug
        info_ref[sub * MOE_TILE:(sub + 1) * MOE_TILE, :] = info
        lens_ref[sub] = jnp.broadcast_to(glen, (SUBLANES, LANES))


def _out_sort(x2, y_lru, y_diff, y_mem, l, w_out, g2, rt_hi, rt_lo, rb_col, earlier):
    t = x2.shape[0]
    tm = OUT_TILE if t % OUT_TILE == 0 else SEQ_TILE
    assert t % tm == 0 and tm % MOE_TILE == 0
    sub = tm // MOE_TILE
    nt = t // MOE_TILE
    n = t // tm
    cur = lambda i: (jnp.minimum(i, n - 1), 0)
    prev = lambda i: (jnp.maximum(i - 1, 0), 0)
    gw = GROUP_WIDTH
    return pl.pallas_call(
        _out_sort_kernel,
        grid=(n + 1,),
        in_specs=[pl.BlockSpec((tm, D_MODEL), cur), pl.BlockSpec((tm, gw), cur), pl.BlockSpec((tm, gw), cur),
                  pl.BlockSpec((tm, gw), cur), _lspec(l, (3 * gw, D_MODEL)),
                  _lspec(l, (1, D_MODEL)), _lspec(l, (LANES, D_MODEL)),
                  _lspec(l, (LANES, D_MODEL)), _lspec(l, (LANES, 1)),
                  pl.BlockSpec((MOE_TILE, MOE_TILE), lambda i: (0, 0))],
        out_specs=[pl.BlockSpec((tm, D_MODEL), cur), pl.BlockSpec((sub * SORTED_ROWS, XS_WIDTH), prev),
                   pl.BlockSpec((tm, LANES), prev),
                   pl.BlockSpec((sub, SUBLANES, LANES), lambda i: (jnp.maximum(i - 1, 0), 0, 0))],
        out_shape=[jax.ShapeDtypeStruct((t, D_MODEL), F32),
                   jax.ShapeDtypeStruct((nt * SORTED_ROWS, XS_WIDTH), BF16),
                   jax.ShapeDtypeStruct((t, LANES), F32),
                   jax.ShapeDtypeStruct((nt, SUBLANES, LANES), F32)],
        scratch_shapes=[pltpu.VMEM((tm, D_MODEL), F32)],
        compiler_params=_params("arbitrary"),
        name="out_proj_sort",
    )(x2, y_lru, y_diff, y_mem, w_out, g2, rt_hi, rt_lo, rb_col, earlier)


def _moe_expert_kernel(sg_ref, sv_ref, gi_ref, xs_hbm, wg_ref, wu_ref, wd_ref, y_ref, xbuf_ref, sem_ref):
    s = pl.program_id(0)
    last = pl.num_programs(0) - 1
    slot = lax.rem(s, EXPERT_BUFFERS)

    def start_fetch(step, dst_slot):
        _granule_copies(gi_ref, step, STEP_GRANULES, xs_hbm, xbuf_ref, dst_slot, sem_ref, True)

    def wait_fetch(step, dst_slot):
        _granule_copies(gi_ref, step, STEP_GRANULES, xs_hbm, xbuf_ref, dst_slot, sem_ref, False)

    @pl.when(s == 0)
    def _():
        start_fetch(0, 0)
        start_fetch(jnp.minimum(1, last), 1)

    @pl.when((s < 2) | (sv_ref[jnp.maximum(s - 2, 0)] != 0))
    def _():
        wait_fetch(s, slot)

    @pl.when(sv_ref[s] == 0)
    def _():
        y_ref[...] = jnp.zeros_like(y_ref)

    @pl.when(sv_ref[s] != 0)
    def _():
        ahead = s + 2
        nxt = jnp.minimum(ahead, last)
        nxt_slot = jnp.where(ahead <= last, lax.rem(ahead, EXPERT_BUFFERS), EXPERT_BUFFERS)
        rows = xbuf_ref[slot]
        x = rows[:, 0:D_MODEL]
        aug = rows[:, D_MODEL:XS_WIDTH].astype(F32)
        lane = _lane_iota(aug.shape)
        e0 = ROUTER_EXPERT_LANE0
        comb = jnp.where((lane >= e0) & (lane < e0 + N_EXPERTS),
                         aug + pltpu.roll(aug, LANES - N_EXPERTS, axis=1)
                         + pltpu.roll(aug, LANES - 2 * N_EXPERTS, axis=1), 0.0)
        first = e0 + sg_ref[s] * EXPERTS_PER_GROUP
        hs = []
        for e in range(EXPERTS_PER_GROUP):
            if e == 1:
                start_fetch(nxt, nxt_slot)
            cw = jnp.sum(jnp.where(lane == first + e, comb, 0.0), axis=-1, keepdims=True)
            hg = jnp.dot(x, wg_ref[e], preferred_element_type=F32)
            hu = jnp.dot(x, wu_ref[e], preferred_element_type=F32)
            hs.append((jax.nn.silu(hg) * hu * cw).astype(BF16))
        h = jnp.concatenate(hs, axis=-1)
        y_ref[...] = jnp.dot(h, wd_ref[...], preferred_element_type=F32).astype(BF16)

        @pl.when(ahead > last)
        def _():
            wait_fetch(nxt, nxt_slot)


def _moe_experts(step_group, step_valid, gran_idx, xs, l, wg, wu, wd):
    nstep = step_group.shape[0]
    epg = EXPERTS_PER_GROUP
    wmap4 = lambda s, sg, sv, gi: (l * N_GROUPS + sg[s], 0, 0, 0)
    in_specs = [pl.BlockSpec(memory_space=pl.ANY),
                pl.BlockSpec((None, epg, D_MODEL, D_EXPERT), wmap4),
                pl.BlockSpec((None, epg, D_MODEL, D_EXPERT), wmap4),
                pl.BlockSpec((None, epg * D_EXPERT, D_MODEL), lambda s, sg, sv, gi: (l * N_GROUPS + sg[s], 0, 0))]
    return pl.pallas_call(
        _moe_expert_kernel,
        grid_spec=pltpu.PrefetchScalarGridSpec(
            num_scalar_prefetch=3, grid=(nstep,), in_specs=in_specs,
            out_specs=pl.BlockSpec((STEP_ROWS, D_MODEL), lambda s, sg, sv, gi: (s, 0)),
            scratch_shapes=[pltpu.VMEM((EXPERT_BUFFERS + 1, STEP_ROWS, XS_WIDTH), BF16),
                            pltpu.SemaphoreType.DMA((EXPERT_BUFFERS + 1,))]),
        out_shape=jax.ShapeDtypeStruct((nstep * STEP_ROWS, D_MODEL), BF16),
        compiler_params=_params("arbitrary"),
        name="moe_experts",
    )(step_group, step_valid, gran_idx, xs, wg, wu, wd)


def _moe_combine_kernel(inv_ref, x_ref, info_ref, ys_hbm, o_ref, ybuf_ref, sem_ref):
    o_ref[...] = _combined_tile(inv_ref, x_ref, info_ref, ys_hbm, ybuf_ref, sem_ref)


def _moe_combine(inv, x2, info, ys):
    t = x2.shape[0]
    tm = SEQ_TILE
    row = lambda i, inv: (i, 0)
    return pl.pallas_call(
        _moe_combine_kernel,
        grid_spec=pltpu.PrefetchScalarGridSpec(
            num_scalar_prefetch=1, grid=(t // tm,),
            in_specs=[pl.BlockSpec((tm, D_MODEL), row), pl.BlockSpec((tm, LANES), row),
                      pl.BlockSpec(memory_space=pl.ANY)],
            out_specs=pl.BlockSpec((tm, D_MODEL), row), scratch_shapes=_combine_scratch(tm)),
        out_shape=jax.ShapeDtypeStruct((t, D_MODEL), F32),
        compiler_params=_params("arbitrary"),
        name="moe_combine",
    )(inv, x2, info, ys)


def _moe_tables(lens, nt):
    i32 = jnp.int32
    ng = N_GROUPS
    garange = jnp.arange(ng, dtype=i32)
    cum = jnp.cumsum(lens, axis=1)
    start = cum - lens
    base = jnp.arange(nt, dtype=i32)[:, None] * TILE_SLOTS + start
    run_len = lens.T.reshape(-1)
    run_base = base.T.reshape(-1)
    run_end = jnp.cumsum(run_len)
    run_start = run_end - run_len
    n_g = jnp.sum(lens, axis=0)
    steps_g = (n_g + STEP_GRANULES - 1) // STEP_GRANULES
    step_end = jnp.cumsum(steps_g)
    step_off = step_end - steps_g
    gran_off = jnp.cumsum(n_g) - n_g
    nstep = (nt * (TILE_SLOTS - 1) + STEP_GRANULES - 1) // STEP_GRANULES + ng
    s = jnp.arange(nstep, dtype=i32)
    sg = jnp.minimum(jnp.sum((s[:, None] >= step_end[None, :]).astype(i32), axis=-1), ng - 1)
    goh = sg[:, None] == garange[None, :]
    pick = lambda v: jnp.sum(jnp.where(goh, v[None, :], 0), axis=-1)
    sv = s < step_end[-1]
    jl = (s - pick(step_off))[:, None] * STEP_GRANULES + jnp.arange(STEP_GRANULES, dtype=i32)[None, :]
    ok = sv[:, None] & (jl < pick(n_g)[:, None])
    j = jnp.where(ok, pick(gran_off)[:, None] + jl, 0)
    inrun = (j[..., None] >= run_start) & (j[..., None] < run_end)
    gran = j + jnp.sum(jnp.where(inrun, run_base - run_start, 0), axis=-1)
    gran = jnp.where(ok, gran, jnp.where(sv[:, None], gran[:, 0:1], 0)).astype(i32)
    q = jnp.arange(TILE_SLOTS, dtype=i32)[None, :]
    gq = jnp.minimum(jnp.sum((q[:, :, None] >= cum[:, None, :]).astype(i32), axis=-1), ng - 1)
    qoh = gq[..., None] == garange
    used = q < cum[:, -1:]
    per_g = run_start.reshape(ng, nt).T - start + (step_off * STEP_GRANULES - gran_off)[None, :]
    inv = jnp.where(used, q + jnp.sum(jnp.where(qoh, per_g[:, None, :], 0), axis=-1), 0).astype(i32)
    return sg, sv.astype(i32), gran.reshape(-1), inv.reshape(-1)


def _out_proj_moe(x2, y_lru, y_diff, y_mem, l, w_out, g2, rt_hi, rt_lo, rb_col, earlier, wg, wu, wd):
    nt = x2.shape[0] // MOE_TILE
    x1, xs, info, lens = _out_sort(x2, y_lru, y_diff, y_mem, l, w_out, g2, rt_hi, rt_lo, rb_col, earlier)
    sg, sv, gran, inv = _moe_tables(lens[:, 0:N_GROUPS, 0].astype(jnp.int32), nt)
    ys = _moe_experts(sg, sv, gran, xs, l, wg, wu, wd)
    return x1, (inv, info, ys)


def _block_diag(w):
    depth, h, n, _ = w.shape
    eye = jnp.eye(h, dtype=w.dtype)
    return (eye[None, :, None, :, None] * w[:, :, :, None, :]).reshape(depth, h * n, h * n)


def _router_tables(w_rg, b_rg, w_re, b_re):
    depth = w_rg.shape[0]
    e0 = ROUTER_EXPERT_LANE0
    pad = lambda rows, width: jnp.zeros((depth, rows, width), F32)
    w = jnp.concatenate([jnp.swapaxes(w_rg, 1, 2), pad(e0 - N_GROUPS, D_MODEL), jnp.swapaxes(w_re, 1, 2),
                         pad(LANES - e0 - N_EXPERTS, D_MODEL)], axis=1)
    bias = jnp.concatenate([b_rg[:, :, None], pad(e0 - N_GROUPS, 1), b_re[:, :, None],
                            pad(LANES - e0 - N_EXPERTS, 1)], axis=1)
    hi = w.astype(BF16)
    lo = (w - hi.astype(F32)).astype(BF16)
    return hi, lo, bias


def kernel(x, mem, norm1_g, w_in, conv_w, conv_b, rg_wa, rg_ba, rg_wx, rg_bx, rg_lambda, dq_norm_g, dk_norm_g, lambda_q1, lambda_k1, lambda_q2, lambda_k2, diff_head_norm_g, mem_norm_g, w_mem_kv, mq_norm_g, mk_norm_g, w_out, norm2_g, w_router_group, b_router_group, w_router_expert, b_router_expert, w_expert_gate, w_expert_up, w_expert_down):
    b, s, d = x.shape
    m = mem.shape[1]
    depth = w_in.shape[0]
    gw = GROUP_WIDTH
    epg = EXPERTS_PER_GROUP
    x2 = x.reshape(b * s, d)
    mem2 = mem.reshape(b * m, d)
    row = lambda v: v.reshape(depth, 1, -1).astype(F32)
    col = lambda v: v.reshape(depth, -1, 1).astype(F32)
    lam_inits = [0.8 - 0.6 * math.exp(-0.3 * l) for l in range(depth)]
    lam = (jnp.exp(jnp.sum(lambda_q1 * lambda_k1, axis=-1)) - jnp.exp(jnp.sum(lambda_q2 * lambda_k2, axis=-1))
           + jnp.asarray(lam_inits, F32))
    lam = jnp.stack([lam, _score_bound(dq_norm_g, dk_norm_g)], axis=1).astype(F32)
    w_in_bf = w_in.astype(BF16)
    w_qv_t = jnp.swapaxes(jnp.concatenate([w_in_bf[:, :, 2 * gw:3 * gw], w_in_bf[:, :, 4 * gw:5 * gw]], axis=2), 1, 2)
    qg_col = col(jnp.tile(dq_norm_g, (1, gw // DIFF_QK_DIM)))
    kg = row(jnp.tile(dk_norm_g, (1, 2)))
    wa_bd = _block_diag(rg_wa).astype(BF16)
    wx_bd = _block_diag(rg_wx).astype(BF16)
    w_kv_bf = w_mem_kv.astype(BF16)
    w_out_bf = w_out.astype(BF16)
    rt_hi, rt_lo, rb_col = _router_tables(w_router_group, b_router_group, w_router_expert, b_router_expert)
    earlier = jnp.triu(jnp.ones((MOE_TILE, MOE_TILE), BF16), k=1)
    wg = w_expert_gate.astype(BF16).reshape(depth * N_GROUPS, epg, D_MODEL, D_EXPERT)
    wu = w_expert_up.astype(BF16).reshape(depth * N_GROUPS, epg, D_MODEL, D_EXPERT)
    wd = w_expert_down.astype(BF16).reshape(depth * N_GROUPS, epg * D_EXPERT, D_MODEL)
    g1, g2, mqg, mkg = row(norm1_g), row(norm2_g), row(mq_norm_g), row(mk_norm_g)
    conv_b3, ba, bx, lru_lam, hg_col = row(conv_b), row(rg_ba), row(rg_bx), row(rg_lambda), col(diff_head_norm_g)
    mem_g = mem_norm_g.reshape(1, -1).astype(F32)
    mk, mv = _mem_kv(mem2, mem_g, w_kv_bf, mkg)
    moe = None
    for l in range(depth):
        proj = _in_proj(x2, b, s, l, g1, w_in_bf, w_qv_t, qg_col, kg, mqg, mk, mv,
                        (conv_w, conv_b3, wa_bd, ba, wx_bd, bx, lru_lam), moe)
        if moe is not None:
            x2, *proj = proj
        y_lru, k, y_mem, qt, vt = proj
        y_diff = _diff_attn(qt, k, vt, lam, hg_col, b, s, l, lam_inits[l])
        x2, moe = _out_proj_moe(x2, y_lru, y_diff, y_mem, l, w_out_bf, g2, rt_hi, rt_lo, rb_col, earlier,
                                wg, wu, wd)
    return _moe_combine(moe[0], x2, moe[1], moe[2]).reshape(b, s, d)
```

```python
import functools
import math

import jax
import jax.numpy as jnp
from jax import lax
from jax.experimental import pallas as pl
from jax.experimental.pallas import tpu as pltpu

F32 = jnp.float32
BF16 = jnp.bfloat16

D_MODEL = 1024
GROUP_WIDTH = D_MODEL // 2
CONV_WIDTH = 4
LRU_C = 8.0
DIFF_HEADS = 4
DIFF_HEAD_DIM = GROUP_WIDTH // DIFF_HEADS
DIFF_QK_DIM = DIFF_HEAD_DIM // 2
MEM_HEADS = 4
MEM_HEAD_DIM = GROUP_WIDTH // MEM_HEADS
N_GROUPS = 4
EXPERTS_PER_GROUP = 8
N_EXPERTS = N_GROUPS * EXPERTS_PER_GROUP
D_EXPERT = 256
EPS = 1e-6

LANES = 128
SUBLANES = 8
VMEM_LIMIT = 56 * 1024 * 1024
NEG_BIG = -1e30
ROUTER_EXPERT_LANE0 = 32


def _tile(n, pref):
    t = min(n, pref)
    assert n % t == 0, (n, t)
    return t


def _params(*sem):
    return pltpu.CompilerParams(dimension_semantics=sem, vmem_limit_bytes=VMEM_LIMIT)


def _lspec(l, tail, blk=None):
    idx = (l,) + tuple(blk if blk is not None else (0,) * len(tail))
    return pl.BlockSpec((None,) + tuple(tail), lambda *_: idx)


def _rms_rows(x, g):
    return x * lax.rsqrt(jnp.mean(x * x, axis=-1, keepdims=True) + EPS) * g


def _lane_iota(shape):
    return lax.broadcasted_iota(jnp.int32, shape, len(shape) - 1)


def _half_head_norm(z, g):
    sq = z * z
    lo = _lane_iota(z.shape) < DIFF_QK_DIM
    s_all = jnp.sum(sq, axis=-1, keepdims=True)
    s_lo = jnp.sum(jnp.where(lo, sq, 0.0), axis=-1, keepdims=True)
    inv_lo = lax.rsqrt(s_lo * (1.0 / DIFF_QK_DIM) + EPS)
    inv_hi = lax.rsqrt((s_all - s_lo) * (1.0 / DIFF_QK_DIM) + EPS)
    return z * jnp.where(lo, inv_lo, inv_hi) * g


SEQ_TILE = 512
_NT = (((1,), (1,)), ((), ()))


N_PROJ_PARAMS = 18
N_PROJ_OUTS = 5


def _softplus(z):
    return jnp.maximum(z, 0.0) + jnp.log(1.0 + jnp.exp(-jnp.abs(z)))


def _in_proj_body(x, ns, g1_ref, wl_ref, wk_ref, wmq_ref, wqt_ref, wvt_ref, qg_ref, kg_ref, mqg_ref,
                  mk_ref, mv_ref, cw_ref, cb_ref, wa_ref, ba_ref, wx_ref, bx_ref, llam_ref,
                  ylru_ref, k_ref, ym_ref, qt_ref, vt_ref, ext_ref, hcar_ref, a_ref, b_ref, h_ref):
    gw = GROUP_WIDTH
    tm = x.shape[0]
    hist = SUBLANES
    first = lax.rem(pl.program_id(0), ns) == 0

    @pl.when(first)
    def _():
        ext_ref[0:hist, :] = jnp.zeros((hist, gw), F32)
        hcar_ref[...] = jnp.zeros_like(hcar_ref)

    @pl.when(jnp.logical_not(first))
    def _():
        ext_ref[0:hist, :] = ext_ref[tm:tm + hist, :]

    xn = _rms_rows(x, g1_ref[...]).astype(BF16)
    heads = range(MEM_HEADS)
    hsl = [slice(h * MEM_HEAD_DIM, (h + 1) * MEM_HEAD_DIM) for h in heads]
    mq = jnp.dot(xn, wmq_ref[...], preferred_element_type=F32)
    u = jnp.dot(xn, wl_ref[...], preferred_element_type=F32)
    ext_ref[hist:hist + tm, :] = u[:, 0:gw]
    xc = cb_ref[...] + cw_ref[CONV_WIDTH - 1:CONV_WIDTH, :] * u[:, 0:gw]
    for j in range(CONV_WIDTH - 1):
        off = hist - (CONV_WIDTH - 1) + j
        xc = xc + cw_ref[j:j + 1, :] * ext_ref[off:off + tm, :]
    xcb = xc.astype(BF16)
    sc = [lax.dot_general(_rms_rows(mq[:, hsl[h]], mqg_ref[...]).astype(BF16), mk_ref[:, hsl[h]], _NT,
                          preferred_element_type=F32) * MEM_HEAD_DIM ** -0.5 for h in heads]
    k = jnp.dot(xn, wk_ref[...], preferred_element_type=F32)
    r_pre = jnp.dot(xcb, wa_ref[...], preferred_element_type=F32)
    i_pre = jnp.dot(xcb, wx_ref[...], preferred_element_type=F32)
    p = [jnp.exp(sc[h] - jnp.max(sc[h], axis=-1, keepdims=True)) for h in heads]
    qt = lax.dot_general(wqt_ref[...], xn, _NT, preferred_element_type=F32)
    o = [jnp.dot(p[h].astype(BF16), mv_ref[:, hsl[h]], preferred_element_type=F32) for h in heads]
    vt_ref[...] = lax.dot_general(wvt_ref[...], xn, _NT, preferred_element_type=F32).astype(BF16)

    r = jax.nn.sigmoid(r_pre + ba_ref[...])
    gate_i = jax.nn.sigmoid(i_pre + bx_ref[...])
    a = jnp.exp((-LRU_C * r) * _softplus(-llam_ref[...]))
    om = 1.0 - a * a
    b = om * lax.rsqrt(jnp.maximum(om, 1e-30)) * (gate_i * xc)
    a = a.reshape(tm // SUBLANES, SUBLANES, gw)
    b = b.reshape(tm // SUBLANES, SUBLANES, gw)
    row = lax.broadcasted_iota(jnp.int32, a.shape, 1)
    d = 1
    while d < SUBLANES:
        keep = row >= d
        a_prev = pltpu.roll(a, d, axis=1)
        b_prev = pltpu.roll(b, d, axis=1)
        b = jnp.where(keep, a * b_prev + b, b)
        a = jnp.where(keep, a * a_prev, a)
        d *= 2
    a_ref[...] = a.reshape(tm, gw)
    b_ref[...] = b.reshape(tm, gw)
    gated = jax.nn.gelu(u[:, gw:2 * gw])

    for h in range(DIFF_HEADS):
        sl = slice(h * LANES, (h + 1) * LANES)
        k_ref[:, sl] = _half_head_norm(k[:, sl], kg_ref[...]).astype(BF16)
    for h in heads:
        ym_ref[:, hsl[h]] = (o[h] / jnp.sum(p[h], axis=-1, keepdims=True)).astype(BF16)
    q3 = qt.reshape(gw // DIFF_QK_DIM, DIFF_QK_DIM, tm)
    q3 = q3 * lax.rsqrt(jnp.mean(q3 * q3, axis=1, keepdims=True) + EPS)
    qscale = DIFF_QK_DIM ** -0.5 * math.log2(math.e)
    qt_ref[...] = (q3.reshape(gw, tm) * (qg_ref[...] * qscale)).astype(BF16)

    def block(i, hprev):
        off = pl.multiple_of(i * SUBLANES, SUBLANES)
        hb = a_ref[pl.ds(off, SUBLANES), :] * hprev + b_ref[pl.ds(off, SUBLANES), :]
        h_ref[pl.ds(off, SUBLANES), :] = hb
        return jnp.broadcast_to(hb[SUBLANES - 1:SUBLANES, :], (SUBLANES, gw))

    hcar_ref[...] = lax.fori_loop(0, tm // SUBLANES, block, hcar_ref[...], unroll=8)
    ylru_ref[...] = (h_ref[...] * gated).astype(BF16)


def _in_proj_kernel(x_ref, *refs, ns):
    _in_proj_body(x_ref[...], ns, *refs)


def _granule_copies(tab_ref, step, n, src_hbm, buf_ref, slot, sem_ref, start):
    for k in range(n):
        g = tab_ref[step * n + k]
        cp = pltpu.make_async_copy(
            src_hbm.at[pl.ds(pl.multiple_of(g * GRANULE, GRANULE), GRANULE), :],
            buf_ref.at[slot, pl.ds(k * GRANULE, GRANULE), :], sem_ref.at[slot])
        if start:
            cp.start()
        else:
            cp.wait()


def _unsort(info, ys):
    kpad = -SORTED_ROWS % LANES
    ys = jnp.concatenate([ys, jnp.zeros((kpad, D_MODEL), BF16)], axis=0)
    pos = info[:, INFO_POS_LANE:INFO_POS_LANE + 1]
    unperm = jnp.where(pos == _lane_iota((MOE_TILE, SORTED_ROWS + kpad)).astype(F32), 1.0, 0.0).astype(BF16)
    return jnp.dot(unperm, ys, preferred_element_type=F32)


def _combined_tile(inv_ref, x1_ref, info_ref, ys_hbm, ybuf_ref, sem_ref):
    s = pl.program_id(0)
    last = pl.num_programs(0) - 1
    slot = s & 1
    nsub = x1_ref.shape[0] // MOE_TILE
    n = nsub * TILE_SLOTS

    @pl.when(s == 0)
    def _():
        _granule_copies(inv_ref, 0, n, ys_hbm, ybuf_ref, 0, sem_ref, True)

    _granule_copies(inv_ref, s, n, ys_hbm, ybuf_ref, slot, sem_ref, False)
    nxt = jnp.minimum(s + 1, last)
    _granule_copies(inv_ref, nxt, n, ys_hbm, ybuf_ref, 1 - slot, sem_ref, True)
    x = jnp.concatenate(
        [x1_ref[u * MOE_TILE:(u + 1) * MOE_TILE, :]
         + _unsort(info_ref[u * MOE_TILE:(u + 1) * MOE_TILE, :],
                   ybuf_ref[slot, u * SORTED_ROWS:(u + 1) * SORTED_ROWS, :])
         for u in range(nsub)], axis=0)

    @pl.when(s == last)
    def _():
        _granule_copies(inv_ref, nxt, n, ys_hbm, ybuf_ref, 1 - slot, sem_ref, False)

    return x


def _combine_in_proj_kernel(inv_ref, x1_ref, info_ref, ys_hbm, *refs, ns):
    params, x_out_ref, outs = refs[:N_PROJ_PARAMS], refs[N_PROJ_PARAMS], refs[N_PROJ_PARAMS + 1:-2]
    x = _combined_tile(inv_ref, x1_ref, info_ref, ys_hbm, *refs[-2:])
    x_out_ref[...] = x
    _in_proj_body(x, ns, *params, *outs)


def _in_proj(x2, b, s, l, g1, w_in, w_in_t, qg_col, kg, mqg, mk, mv, lru, moe=None):
    t = x2.shape[0]
    tm = SEQ_TILE
    assert s % tm == 0
    ns = s // tm
    m = mk.shape[1] // b
    gw = GROUP_WIDTH
    row = lambda i, *_: (i, 0)
    fm = lambda i, *_: (i // ns, i % ns, 0, 0)
    mem = pl.BlockSpec((None, m, gw), lambda i, *_: (l, i // ns, 0))
    in_specs = [_lspec(l, (1, D_MODEL)),
                _lspec(l, (D_MODEL, 2 * gw), (0, 0)), _lspec(l, (D_MODEL, gw), (0, 3)),
                _lspec(l, (D_MODEL, gw), (0, 5)), _lspec(l, (gw, D_MODEL), (0, 0)),
                _lspec(l, (gw, D_MODEL), (1, 0)), _lspec(l, (gw, 1)),
                _lspec(l, (1, LANES)), _lspec(l, (1, MEM_HEAD_DIM)), mem, mem]
    vec = _lspec(l, (1, gw))
    in_specs += [_lspec(l, (CONV_WIDTH, gw)), vec, _lspec(l, (gw, gw)), vec, _lspec(l, (gw, gw)), vec, vec]
    operands = (g1, w_in, w_in, w_in, w_in_t, w_in_t, qg_col, kg, mqg, mk, mv) + tuple(lru)
    assert len(operands) == N_PROJ_PARAMS
    out_specs = [pl.BlockSpec((tm, gw), row), pl.BlockSpec((tm, gw), row), pl.BlockSpec((tm, gw), row),
                 pl.BlockSpec((None, None, gw, tm), fm), pl.BlockSpec((None, None, gw, tm), fm)]
    out_shape = [jax.ShapeDtypeStruct((t, gw), BF16), jax.ShapeDtypeStruct((t, gw), BF16),
                 jax.ShapeDtypeStruct((t, gw), BF16), jax.ShapeDtypeStruct((b, ns, gw, tm), BF16),
                 jax.ShapeDtypeStruct((b, ns, gw, tm), BF16)]
    assert len(out_specs) == N_PROJ_OUTS
    scratch = [pltpu.VMEM((tm + SUBLANES, gw), F32), pltpu.VMEM((SUBLANES, gw), F32),
               pltpu.VMEM((tm, gw), F32), pltpu.VMEM((tm, gw), F32), pltpu.VMEM((tm, gw), F32)]
    xspec = pl.BlockSpec((tm, D_MODEL), row)
    if moe is None:
        return pl.pallas_call(
            functools.partial(_in_proj_kernel, ns=ns), grid=(t // tm,), in_specs=[xspec] + in_specs,
            out_specs=out_specs, out_shape=out_shape, scratch_shapes=scratch,
            compiler_params=_params("arbitrary"), name="in_proj",
        )(x2, *operands)
    inv, info, ys = moe
    return pl.pallas_call(
        functools.partial(_combine_in_proj_kernel, ns=ns),
        grid_spec=pltpu.PrefetchScalarGridSpec(
            num_scalar_prefetch=1, grid=(t // tm,),
            in_specs=[xspec, pl.BlockSpec((tm, LANES), row), pl.BlockSpec(memory_space=pl.ANY)] + in_specs,
            out_specs=[xspec] + out_specs, scratch_shapes=scratch + _combine_scratch(tm)),
        out_shape=[jax.ShapeDtypeStruct((t, D_MODEL), F32)] + out_shape,
        compiler_params=_params("arbitrary"),
        name="combine_in_proj",
    )(inv, x2, info, ys, *operands)


def _combine_scratch(tm):
    return [pltpu.VMEM((2, tm // MOE_TILE * SORTED_ROWS, D_MODEL), BF16), pltpu.SemaphoreType.DMA((2,))]


ATTN_HEADS_PER_STEP = 4
ATTN_MIN_SUM = 2.0 ** -40


def _score_bound(dq_gain, dk_gain):
    qscale = DIFF_QK_DIM ** -0.5 * math.log2(math.e)
    return (1.02 * qscale * DIFF_QK_DIM) * jnp.max(jnp.abs(dq_gain), axis=-1) * jnp.max(jnp.abs(dk_gain), axis=-1)


def _diff_attn_kernel(lam_ref, qt_ref, k_ref, vt_ref, hg_ref, o_ref,
                      qs_ref, sa_ref, sb_ref, m_ref, l_ref, acc_ref, *, t, layer, out_scale):
    qi = pl.program_id(2)
    hd = DIFF_HEAD_DIM
    heads = range(ATTN_HEADS_PER_STEP)
    lo = lax.broadcasted_iota(jnp.int32, (hd, t), 0) < DIFF_QK_DIM
    for h in heads:
        q = qt_ref[h * hd:(h + 1) * hd, :]
        zero = jnp.zeros_like(q)
        qs_ref[h, :, 0:t] = jnp.where(lo, q, zero)
        qs_ref[h, :, t:2 * t] = jnp.where(lo, zero, q)

    def causal(x, fill):
        kpos = lax.broadcasted_iota(jnp.int32, (t, 2 * t), 0)
        c = lax.broadcasted_iota(jnp.int32, (t, 2 * t), 1)
        return jnp.where(kpos <= jnp.where(c >= t, c - t, c), x, fill)

    def score(j, h):
        off = pl.multiple_of(j * t, t)
        return jnp.dot(k_ref[pl.ds(off, t), h * hd:(h + 1) * hd], qs_ref[h],
                       preferred_element_type=F32)

    bound = lam_ref[layer, 1]
    l_ref[...] = jnp.zeros(l_ref.shape, F32)
    acc_ref[...] = jnp.zeros(acc_ref.shape, F32)

    def fast_update(j, masked):
        for h in heads:
            p = jnp.exp2(score(j, h) - bound)
            if masked:
                p = causal(p, 0.0)
            l_ref[h] += jnp.sum(p, axis=0, keepdims=True)
            acc_ref[h] += jnp.dot(vt_ref[j, h * hd:(h + 1) * hd, :], p.astype(BF16),
                                  preferred_element_type=F32)

    def fast_pair(i, carry):
        fast_update(2 * i, False)
        fast_update(2 * i + 1, False)
        return carry

    lax.fori_loop(0, qi >> 1, fast_pair, 0)

    @pl.when((qi & 1) == 1)
    def _():
        fast_update(qi - 1, False)

    fast_update(qi, True)

    @pl.when(jnp.min(l_ref[...]) < ATTN_MIN_SUM)
    def _():
        m_ref[...] = jnp.full(m_ref.shape, NEG_BIG, F32)
        l_ref[...] = jnp.zeros(l_ref.shape, F32)
        acc_ref[...] = jnp.zeros(acc_ref.shape, F32)

        def scores(j, dst_ref):
            for h in heads:
                dst_ref[h] = score(j, h)

        def update(j, src_ref, masked):
            for h in heads:
                s = src_ref[h]
                if masked:
                    s = causal(s, NEG_BIG)
                m = m_ref[h]
                m_new = jnp.maximum(m, jnp.max(s, axis=0, keepdims=True))
                alpha = jnp.exp2(m - m_new)
                p = jnp.exp2(s - m_new)
                m_ref[h] = m_new
                l_ref[h] = alpha * l_ref[h] + jnp.sum(p, axis=0, keepdims=True)
                vt = vt_ref[j, h * hd:(h + 1) * hd, :]
                acc_ref[h] = alpha * acc_ref[h] + jnp.dot(vt, p.astype(BF16), preferred_element_type=F32)

        scores(0, sa_ref)

        def pair(i, carry):
            scores(2 * i + 1, sb_ref)
            update(2 * i, sa_ref, False)
            scores(2 * i + 2, sa_ref)
            update(2 * i + 1, sb_ref, False)
            return carry

        lax.fori_loop(0, qi >> 1, pair, 0)

        @pl.when((qi & 1) == 1)
        def _():
            scores(qi, sb_ref)
            update(qi - 1, sa_ref, False)
            update(qi, sb_ref, True)

        @pl.when((qi & 1) == 0)
        def _():
            update(qi, sa_ref, True)

    for h in heads:
        o = acc_ref[h] / l_ref[h]
        o = o[:, 0:t] - lam_ref[layer, 0] * o[:, t:2 * t]
        o = o * lax.rsqrt(jnp.mean(o * o, axis=0, keepdims=True) + EPS) * (hg_ref[...] * out_scale)
        o_ref[:, h * hd:(h + 1) * hd] = o.T.astype(BF16)


def _diff_attn(qt, k, vt, lam, hg_col, b, s, l, lam_init):
    t = SEQ_TILE
    nq = s // t
    hp = ATTN_HEADS_PER_STEP
    w = hp * DIFF_HEAD_DIM
    ng = DIFF_HEADS // hp
    kern = functools.partial(_diff_attn_kernel, t=t, layer=l, out_scale=1.0 - lam_init)
    return pl.pallas_call(
        kern,
        grid=(b, ng, nq),
        in_specs=[pl.BlockSpec(memory_space=pltpu.SMEM),
                  pl.BlockSpec((None, None, w, t), lambda bi, g, qi: (bi, qi, g, 0)),
                  pl.BlockSpec((s, w), lambda bi, g, qi: (bi, g)),
                  pl.BlockSpec((None, nq, w, t), lambda bi, g, qi: (bi, 0, g, 0)),
                  _lspec(l, (DIFF_HEAD_DIM, 1))],
        out_specs=pl.BlockSpec((t, w), lambda bi, g, qi: (bi * nq + qi, g)),
        out_shape=jax.ShapeDtypeStruct((b * s, GROUP_WIDTH), BF16),
        scratch_shapes=[pltpu.VMEM((hp, DIFF_HEAD_DIM, 2 * t), BF16),
                        pltpu.VMEM((hp, t, 2 * t), F32), pltpu.VMEM((hp, t, 2 * t), F32),
                        pltpu.VMEM((hp, 1, 2 * t), F32), pltpu.VMEM((hp, 1, 2 * t), F32),
                        pltpu.VMEM((hp, DIFF_HEAD_DIM, 2 * t), F32)],
        compiler_params=_params("arbitrary", "arbitrary", "arbitrary"),
        name="diff_attn",
    )(lam, qt, k, vt, hg_col)


def _mem_kv_kernel(mem_ref, mg_ref, w_ref, kg_ref, mk_ref, mv_ref):
    memn = _rms_rows(mem_ref[...], mg_ref[...]).astype(BF16)
    gw = GROUP_WIDTH
    k = jnp.dot(memn, w_ref[:, 0:gw], preferred_element_type=F32)
    for h in range(MEM_HEADS):
        sl = slice(h * MEM_HEAD_DIM, (h + 1) * MEM_HEAD_DIM)
        mk_ref[:, sl] = _rms_rows(k[:, sl], kg_ref[...]).astype(BF16)
    mv_ref[...] = jnp.dot(memn, w_ref[:, gw:2 * gw], preferred_element_type=F32).astype(BF16)


def _mem_kv(mem2, mg, w_kv, kg):
    r = mem2.shape[0]
    depth = w_kv.shape[0]
    tm = _tile(r, 512)
    gw = GROUP_WIDTH
    out = pl.BlockSpec((None, tm, gw), lambda l, i: (l, i, 0))
    return pl.pallas_call(
        _mem_kv_kernel,
        grid=(depth, r // tm),
        in_specs=[pl.BlockSpec((tm, D_MODEL), lambda l, i: (i, 0)), pl.BlockSpec((1, D_MODEL), lambda l, i: (0, 0)),
                  pl.BlockSpec((None, D_MODEL, 2 * gw), lambda l, i: (l, 0, 0)),
                  pl.BlockSpec((None, 1, MEM_HEAD_DIM), lambda l, i: (l, 0, 0))],
        out_specs=[out, out],
        out_shape=[jax.ShapeDtypeStruct((depth, r, gw), BF16), jax.ShapeDtypeStruct((depth, r, gw), BF16)],
        compiler_params=_params("arbitrary", "arbitrary"),
        name="mem_kv",
    )(mem2, mg, w_kv, kg)


MOE_TILE = 256
OUT_TILE = 1024
GRANULE = 16
TILE_SLOTS = MOE_TILE // GRANULE + N_GROUPS
SORTED_ROWS = TILE_SLOTS * GRANULE
STEP_GRANULES = 32
STEP_ROWS = STEP_GRANULES * GRANULE
XS_WIDTH = D_MODEL + LANES
INFO_POS_LANE = 0


def _route_t(lt):
    row_i = lax.broadcasted_iota(jnp.int32, lt.shape, 0)
    row = row_i.astype(F32)
    e0 = ROUTER_EXPERT_LANE0
    lg = jnp.where(row_i < N_GROUPS, lt, NEG_BIG)
    mg = jnp.max(lg, axis=0, keepdims=True)
    g_gate = 1.0 / jnp.sum(jnp.exp(lg - mg), axis=0, keepdims=True)
    g_idx = jnp.min(jnp.where(lg == mg, row, float(LANES)), axis=0, keepdims=True)
    row_group = ((row_i - e0) >> 3).astype(F32)
    sel = (row_i >= e0) & (row_i < e0 + N_EXPERTS) & (row_group == g_idx)
    le = jnp.where(sel, lt, NEG_BIG)
    m1 = jnp.max(le, axis=0, keepdims=True)
    se = jnp.sum(jnp.where(sel, jnp.exp(le - m1), 0.0), axis=0, keepdims=True)
    i1 = jnp.min(jnp.where(sel & (le == m1), row, float(LANES)), axis=0, keepdims=True)
    le2 = jnp.where(row == i1, NEG_BIG, le)
    m2 = jnp.max(le2, axis=0, keepdims=True)
    i2 = jnp.min(jnp.where(sel & (le2 == m2) & (row != i1), row, float(LANES)), axis=0, keepdims=True)
    p1 = 1.0 / se
    p2 = jnp.exp(m2 - m1) / se
    tot = p1 + p2
    w = jnp.where(row == i1, p1 / tot, jnp.where(row == i2, p2 / tot, 0.0))
    return g_gate * w, g_idx


def _split3(c):
    hi = c.astype(BF16).astype(F32)
    r1 = c - hi
    mid = r1.astype(BF16).astype(F32)
    lo = (r1 - mid).astype(BF16).astype(F32)
    return hi, mid, lo


def _sort_logits(x1, g2, rt_hi, rt_lo, rb_col):
    xn = _rms_rows(x1, g2)
    hi = xn.astype(BF16)
    lo = (xn - hi.astype(F32)).astype(BF16)
    both = lax.dot_general(jnp.concatenate([rt_hi, rt_lo], axis=0), hi, _NT, preferred_element_type=F32)
    logits_t = (both[0:LANES] + lax.dot_general(rt_hi, lo, _NT, preferred_element_type=F32)
                + both[LANES:2 * LANES] + rb_col)
    return hi, logits_t


def _sort_rank(logits_t, earlier):
    tl = logits_t.shape[1]
    comb_t, g_idx = _route_t(logits_t)
    grow = lax.broadcasted_iota(jnp.int32, (SUBLANES, tl), 0).astype(F32)
    gt = jnp.where(grow == g_idx, 1.0, 0.0)
    before = jnp.dot(gt.astype(BF16), earlier, preferred_element_type=F32)
    return comb_t, gt, before


def _sort_emit(hi, comb_t, gt, before):
    tl = hi.shape[0]
    rank = jnp.sum(gt * before, axis=0, keepdims=True)
    cnt = jnp.sum(gt, axis=1, keepdims=True)
    glen = jnp.floor((cnt + (GRANULE - 1)) * (1.0 / GRANULE))
    r8 = lax.broadcasted_iota(jnp.int32, (SUBLANES, 1), 0)
    start = jnp.zeros((SUBLANES, 1), F32)
    for g in range(1, N_GROUPS):
        start = jnp.where(r8 == g, jnp.sum(jnp.where(r8 < g, glen, 0.0), axis=0, keepdims=True), start)
    pos = jnp.sum(gt * (start * GRANULE), axis=0, keepdims=True) + rank

    rows = lax.broadcasted_iota(jnp.int32, (LANES, tl), 0)
    info = jnp.where(rows == INFO_POS_LANE, pos, comb_t).T
    lane = _lane_iota(info.shape)
    e0 = ROUTER_EXPERT_LANE0
    c_hi, c_mid, c_lo = _split3(jnp.where((lane >= e0) & (lane < e0 + N_EXPERTS), info, 0.0))
    aug = (c_hi + pltpu.roll(c_mid, N_EXPERTS, axis=1) + pltpu.roll(c_lo, 2 * N_EXPERTS, axis=1)).astype(BF16)
    perm = jnp.where(pos == lax.broadcasted_iota(jnp.int32, (SORTED_ROWS, tl), 0).astype(F32),
                     1.0, 0.0).astype(BF16)
    xs = jnp.dot(perm, hi, preferred_element_type=F32).astype(BF16)
    xs_aug = jnp.dot(perm, aug, preferred_element_type=F32).astype(BF16)
    return xs, xs_aug, info, glen


def _out_sort_kernel(x_ref, yl_ref, yd_ref, ym_ref, w_ref, g2_ref, rthi_ref, rtlo_ref, rb_ref, earlier_ref,
                     o_ref, xs_ref, info_ref, lens_ref, x1s_ref):
    @pl.when(pl.program_id(0) == 0)
    def _():
        x1s_ref[...] = jnp.zeros_like(x1s_ref)

    tm = x1s_ref.shape[0]
    subs = range(tm // MOE_TILE)

    def project(c, nchunks):
        r = slice(c * (tm // nchunks), (c + 1) * (tm // nchunks))
        y = jnp.concatenate([yl_ref[r, :], yd_ref[r, :], ym_ref[r, :]], axis=-1)
        x1 = x_ref[r, :] + jnp.dot(y, w_ref[...], preferred_element_type=F32)
        o_ref[r, :] = x1
        return r, x1

    nch = len(subs)
    first_half = range(0, nch // 2)
    second_half = range(nch // 2, nch)
    fresh = [project(c, nch) for c in first_half]
    s1 = [_sort_logits(x1s_ref[sub * MOE_TILE:(sub + 1) * MOE_TILE, :], g2_ref[...], rthi_ref[...],
                       rtlo_ref[...], rb_ref[...]) for sub in subs]
    for r, x1 in fresh:
        x1s_ref[r, :] = x1
    fresh = [project(c, nch) for c in second_half]
    s2 = [_sort_rank(s1[sub][1], earlier_ref[...]) for sub in subs]
    for r, x1 in fresh:
        x1s_ref[r, :] = x1
    for sub in subs:
        xs, xs_aug, info, glen = _sort_emit(s1[sub][0], *s2[sub])
        r0 = sub * SORTED_ROWS
        xs_ref[r0:r0 + SORTED_ROWS, 0:D_MODEL] = xs
        xs_ref[r0:r0 + SORTED_ROWS, D_MODEL:XS_WIDTH] = xs_aug
        info_ref[sub * MOE_TILE:(sub + 1) * MOE_TILE, :] = info
        lens_ref[sub] = jnp.broadcast_to(glen, (SUBLANES, LANES))


def _out_sort(x2, y_lru, y_diff, y_mem, l, w_out, g2, rt_hi, rt_lo, rb_col, earlier):
    t = x2.shape[0]
    tm = OUT_TILE if t % OUT_TILE == 0 else SEQ_TILE
    assert t % tm == 0 and tm % MOE_TILE == 0
    sub = tm // MOE_TILE
    nt = t // MOE_TILE
    n = t // tm
    cur = lambda i: (jnp.minimum(i, n - 1), 0)
    prev = lambda i: (jnp.maximum(i - 1, 0), 0)
    gw = GROUP_WIDTH
    return pl.pallas_call(
        _out_sort_kernel,
        grid=(n + 1,),
        in_specs=[pl.BlockSpec((tm, D_MODEL), cur), pl.BlockSpec((tm, gw), cur), pl.BlockSpec((tm, gw), cur),
                  pl.BlockSpec((tm, gw), cur), _lspec(l, (3 * gw, D_MODEL)),
                  _lspec(l, (1, D_MODEL)), _lspec(l, (LANES, D_MODEL)),
                  _lspec(l, (LANES, D_MODEL)), _lspec(l, (LANES, 1)),
                  pl.BlockSpec((MOE_TILE, MOE_TILE), lambda i: (0, 0))],
        out_specs=[pl.BlockSpec((tm, D_MODEL), cur), pl.BlockSpec((sub * SORTED_ROWS, XS_WIDTH), prev),
                   pl.BlockSpec((tm, LANES), prev),
                   pl.BlockSpec((sub, SUBLANES, LANES), lambda i: (jnp.maximum(i - 1, 0), 0, 0))],
        out_shape=[jax.ShapeDtypeStruct((t, D_MODEL), F32),
                   jax.ShapeDtypeStruct((nt * SORTED_ROWS, XS_WIDTH), BF16),
                   jax.ShapeDtypeStruct((t, LANES), F32),
                   jax.ShapeDtypeStruct((nt, SUBLANES, LANES), F32)],
        scratch_shapes=[pltpu.VMEM((tm, D_MODEL), F32)],
        compiler_params=_params("arbitrary"),
        name="out_proj_sort",
    )(x2, y_lru, y_diff, y_mem, w_out, g2, rt_hi, rt_lo, rb_col, earlier)


def _moe_expert_kernel(sg_ref, sv_ref, gi_ref, xs_hbm, wg_ref, wu_ref, wd_ref, y_ref, xbuf_ref, sem_ref):
    s = pl.program_id(0)
    last = pl.num_programs(0) - 1
    slot = s & 1

    def start_fetch(step, dst_slot):
        _granule_copies(gi_ref, step, STEP_GRANULES, xs_hbm, xbuf_ref, dst_slot, sem_ref, True)

    def wait_fetch(step, dst_slot):
        _granule_copies(gi_ref, step, STEP_GRANULES, xs_hbm, xbuf_ref, dst_slot, sem_ref, False)

    @pl.when(s == 0)
    def _():
        start_fetch(0, 0)

    @pl.when((s == 0) | (sv_ref[jnp.maximum(s - 1, 0)] != 0))
    def _():
        wait_fetch(s, slot)

    @pl.when(sv_ref[s] == 0)
    def _():
        y_ref[...] = jnp.zeros_like(y_ref)

    @pl.when(sv_ref[s] != 0)
    def _():
        nxt = jnp.minimum(s + 1, last)
        start_fetch(nxt, 1 - slot)
        rows = xbuf_ref[slot]
        x = rows[:, 0:D_MODEL]
        aug = rows[:, D_MODEL:XS_WIDTH].astype(F32)
        lane = _lane_iota(aug.shape)
        e0 = ROUTER_EXPERT_LANE0
        comb = jnp.where((lane >= e0) & (lane < e0 + N_EXPERTS),
                         aug + pltpu.roll(aug, LANES - N_EXPERTS, axis=1)
                         + pltpu.roll(aug, LANES - 2 * N_EXPERTS, axis=1), 0.0)
        first = e0 + sg_ref[s] * EXPERTS_PER_GROUP
        hs = []
        for e in range(EXPERTS_PER_GROUP):
            cw = jnp.sum(jnp.where(lane == first + e, comb, 0.0), axis=-1, keepdims=True)
            hg = jnp.dot(x, wg_ref[e], preferred_element_type=F32)
            hu = jnp.dot(x, wu_ref[e], preferred_element_type=F32)
            hs.append((jax.nn.silu(hg) * hu * cw).astype(BF16))
        h = jnp.concatenate(hs, axis=-1)
        y_ref[...] = jnp.dot(h, wd_ref[...], preferred_element_type=F32).astype(BF16)

        @pl.when(s == last)
        def _():
            wait_fetch(nxt, 1 - slot)


def _moe_experts(step_group, step_valid, gran_idx, xs, l, wg, wu, wd):
    nstep = step_group.shape[0]
    epg = EXPERTS_PER_GROUP
    wmap4 = lambda s, sg, sv, gi: (l * N_GROUPS + sg[s], 0, 0, 0)
    in_specs = [pl.BlockSpec(memory_space=pl.ANY),
                pl.BlockSpec((None, epg, D_MODEL, D_EXPERT), wmap4),
                pl.BlockSpec((None, epg, D_MODEL, D_EXPERT), wmap4),
                pl.BlockSpec((None, epg * D_EXPERT, D_MODEL), lambda s, sg, sv, gi: (l * N_GROUPS + sg[s], 0, 0))]
    return pl.pallas_call(
        _moe_expert_kernel,
        grid_spec=pltpu.PrefetchScalarGridSpec(
            num_scalar_prefetch=3, grid=(nstep,), in_specs=in_specs,
            out_specs=pl.BlockSpec((STEP_ROWS, D_MODEL), lambda s, sg, sv, gi: (s, 0)),
            scratch_shapes=[pltpu.VMEM((2, STEP_ROWS, XS_WIDTH), BF16), pltpu.SemaphoreType.DMA((2,))]),
        out_shape=jax.ShapeDtypeStruct((nstep * STEP_ROWS, D_MODEL), BF16),
        compiler_params=_params("arbitrary"),
        name="moe_experts",
    )(step_group, step_valid, gran_idx, xs, wg, wu, wd)


def _moe_combine_kernel(inv_ref, x_ref, info_ref, ys_hbm, o_ref, ybuf_ref, sem_ref):
    o_ref[...] = _combined_tile(inv_ref, x_ref, info_ref, ys_hbm, ybuf_ref, sem_ref)


def _moe_combine(inv, x2, info, ys):
    t = x2.shape[0]
    tm = SEQ_TILE
    row = lambda i, inv: (i, 0)
    return pl.pallas_call(
        _moe_combine_kernel,
        grid_spec=pltpu.PrefetchScalarGridSpec(
            num_scalar_prefetch=1, grid=(t // tm,),
            in_specs=[pl.BlockSpec((tm, D_MODEL), row), pl.BlockSpec((tm, LANES), row),
                      pl.BlockSpec(memory_space=pl.ANY)],
            out_specs=pl.BlockSpec((tm, D_MODEL), row), scratch_shapes=_combine_scratch(tm)),
        out_shape=jax.ShapeDtypeStruct((t, D_MODEL), F32),
        compiler_params=_params("arbitrary"),
        name="moe_combine",
    )(inv, x2, info, ys)


def _moe_tables(lens, nt):
    i32 = jnp.int32
    ng = N_GROUPS
    garange = jnp.arange(ng, dtype=i32)
    cum = jnp.cumsum(lens, axis=1)
    start = cum - lens
    base = jnp.arange(nt, dtype=i32)[:, None] * TILE_SLOTS + start
    run_len = lens.T.reshape(-1)
    run_base = base.T.reshape(-1)
    run_end = jnp.cumsum(run_len)
    run_start = run_end - run_len
    n_g = jnp.sum(lens, axis=0)
    steps_g = (n_g + STEP_GRANULES - 1) // STEP_GRANULES
    step_end = jnp.cumsum(steps_g)
    step_off = step_end - steps_g
    gran_off = jnp.cumsum(n_g) - n_g
    nstep = (nt * (TILE_SLOTS - 1) + STEP_GRANULES - 1) // STEP_GRANULES + ng
    s = jnp.arange(nstep, dtype=i32)
    sg = jnp.minimum(jnp.sum((s[:, None] >= step_end[None, :]).astype(i32), axis=-1), ng - 1)
    goh = sg[:, None] == garange[None, :]
    pick = lambda v: jnp.sum(jnp.where(goh, v[None, :], 0), axis=-1)
    sv = s < step_end[-1]
    jl = (s - pick(step_off))[:, None] * STEP_GRANULES + jnp.arange(STEP_GRANULES, dtype=i32)[None, :]
    ok = sv[:, None] & (jl < pick(n_g)[:, None])
    j = jnp.where(ok, pick(gran_off)[:, None] + jl, 0)
    inrun = (j[..., None] >= run_start) & (j[..., None] < run_end)
    gran = j + jnp.sum(jnp.where(inrun, run_base - run_start, 0), axis=-1)
    gran = jnp.where(ok, gran, jnp.where(sv[:, None], gran[:, 0:1], 0)).astype(i32)
    q = jnp.arange(TILE_SLOTS, dtype=i32)[None, :]
    gq = jnp.minimum(jnp.sum((q[:, :, None] >= cum[:, None, :]).astype(i32), axis=-1), ng - 1)
    qoh = gq[..., None] == garange
    used = q < cum[:, -1:]
    per_g = run_start.reshape(ng, nt).T - start + (step_off * STEP_GRANULES - gran_off)[None, :]
    inv = jnp.where(used, q + jnp.sum(jnp.where(qoh, per_g[:, None, :], 0), axis=-1), 0).astype(i32)
    return sg, sv.astype(i32), gran.reshape(-1), inv.reshape(-1)


def _out_proj_moe(x2, y_lru, y_diff, y_mem, l, w_out, g2, rt_hi, rt_lo, rb_col, earlier, wg, wu, wd):
    nt = x2.shape[0] // MOE_TILE
    x1, xs, info, lens = _out_sort(x2, y_lru, y_diff, y_mem, l, w_out, g2, rt_hi, rt_lo, rb_col, earlier)
    sg, sv, gran, inv = _moe_tables(lens[:, 0:N_GROUPS, 0].astype(jnp.int32), nt)
    ys = _moe_experts(sg, sv, gran, xs, l, wg, wu, wd)
    return x1, (inv, info, ys)


def _block_diag(w):
    depth, h, n, _ = w.shape
    eye = jnp.eye(h, dtype=w.dtype)
    return (eye[None, :, None, :, None] * w[:, :, :, None, :]).reshape(depth, h * n, h * n)


def _router_tables(w_rg, b_rg, w_re, b_re):
    depth = w_rg.shape[0]
    e0 = ROUTER_EXPERT_LANE0
    pad = lambda rows, width: jnp.zeros((depth, rows, width), F32)
    w = jnp.concatenate([jnp.swapaxes(w_rg, 1, 2), pad(e0 - N_GROUPS, D_MODEL), jnp.swapaxes(w_re, 1, 2),
                         pad(LANES - e0 - N_EXPERTS, D_MODEL)], axis=1)
    bias = jnp.concatenate([b_rg[:, :, None], pad(e0 - N_GROUPS, 1), b_re[:, :, None],
                            pad(LANES - e0 - N_EXPERTS, 1)], axis=1)
    hi = w.astype(BF16)
    lo = (w - hi.astype(F32)).astype(BF16)
    return hi, lo, bias


def kernel(x, mem, norm1_g, w_in, conv_w, conv_b, rg_wa, rg_ba, rg_wx, rg_bx, rg_lambda, dq_norm_g, dk_norm_g, lambda_q1, lambda_k1, lambda_q2, lambda_k2, diff_head_norm_g, mem_norm_g, w_mem_kv, mq_norm_g, mk_norm_g, w_out, norm2_g, w_router_group, b_router_group, w_router_expert, b_router_expert, w_expert_gate, w_expert_up, w_expert_down):
    b, s, d = x.shape
    m = mem.shape[1]
    depth = w_in.shape[0]
    gw = GROUP_WIDTH
    epg = EXPERTS_PER_GROUP
    x2 = x.reshape(b * s, d)
    mem2 = mem.reshape(b * m, d)
    row = lambda v: v.reshape(depth, 1, -1).astype(F32)
    col = lambda v: v.reshape(depth, -1, 1).astype(F32)
    lam_inits = [0.8 - 0.6 * math.exp(-0.3 * l) for l in range(depth)]
    lam = (jnp.exp(jnp.sum(lambda_q1 * lambda_k1, axis=-1)) - jnp.exp(jnp.sum(lambda_q2 * lambda_k2, axis=-1))
           + jnp.asarray(lam_inits, F32))
    lam = jnp.stack([lam, _score_bound(dq_norm_g, dk_norm_g)], axis=1).astype(F32)
    w_in_bf = w_in.astype(BF16)
    w_qv_t = jnp.swapaxes(jnp.concatenate([w_in_bf[:, :, 2 * gw:3 * gw], w_in_bf[:, :, 4 * gw:5 * gw]], axis=2), 1, 2)
    qg_col = col(jnp.tile(dq_norm_g, (1, gw // DIFF_QK_DIM)))
    kg = row(jnp.tile(dk_norm_g, (1, 2)))
    wa_bd = _block_diag(rg_wa).astype(BF16)
    wx_bd = _block_diag(rg_wx).astype(BF16)
    w_kv_bf = w_mem_kv.astype(BF16)
    w_out_bf = w_out.astype(BF16)
    rt_hi, rt_lo, rb_col = _router_tables(w_router_group, b_router_group, w_router_expert, b_router_expert)
    earlier = jnp.triu(jnp.ones((MOE_TILE, MOE_TILE), BF16), k=1)
    wg = w_expert_gate.astype(BF16).reshape(depth * N_GROUPS, epg, D_MODEL, D_EXPERT)
    wu = w_expert_up.astype(BF16).reshape(depth * N_GROUPS, epg, D_MODEL, D_EXPERT)
    wd = w_expert_down.astype(BF16).reshape(depth * N_GROUPS, epg * D_EXPERT, D_MODEL)
    g1, g2, mqg, mkg = row(norm1_g), row(norm2_g), row(mq_norm_g), row(mk_norm_g)
    conv_b3, ba, bx, lru_lam, hg_col = row(conv_b), row(rg_ba), row(rg_bx), row(rg_lambda), col(diff_head_norm_g)
    mem_g = mem_norm_g.reshape(1, -1).astype(F32)
    mk, mv = _mem_kv(mem2, mem_g, w_kv_bf, mkg)
    moe = None
    for l in range(depth):
        proj = _in_proj(x2, b, s, l, g1, w_in_bf, w_qv_t, qg_col, kg, mqg, mk, mv,
                        (conv_w, conv_b3, wa_bd, ba, wx_bd, bx, lru_lam), moe)
        if moe is not None:
            x2, *proj = proj
        y_lru, k, y_mem, qt, vt = proj
        y_diff = _diff_attn(qt, k, vt, lam, hg_col, b, s, l, lam_inits[l])
        x2, moe = _out_proj_moe(x2, y_lru, y_diff, y_mem, l, w_out_bf, g2, rt_hi, rt_lo, rb_col, earlier,
                                wg, wu, wd)
    return _moe_combine(moe[0], x2, moe[1], moe[2]).reshape(b, s, d)
```

```python
import functools
import math

import jax
import jax.numpy as jnp
from jax import lax
from jax.experimental import pallas as pl
from jax.experimental.pallas import tpu as pltpu

F32 = jnp.float32
BF16 = jnp.bfloat16

D_MODEL = 1024
GROUP_WIDTH = D_MODEL // 2
CONV_WIDTH = 4
LRU_C = 8.0
DIFF_HEADS = 4
DIFF_HEAD_DIM = GROUP_WIDTH // DIFF_HEADS
DIFF_QK_DIM = DIFF_HEAD_DIM // 2
MEM_HEADS = 4
MEM_HEAD_DIM = GROUP_WIDTH // MEM_HEADS
N_GROUPS = 4
EXPERTS_PER_GROUP = 8
N_EXPERTS = N_GROUPS * EXPERTS_PER_GROUP
D_EXPERT = 256
EPS = 1e-6

LANES = 128
SUBLANES = 8
VMEM_LIMIT = 56 * 1024 * 1024
NEG_BIG = -1e30
ROUTER_EXPERT_LANE0 = 32
ROUTER_ROWS = 64


def _tile(n, pref):
    t = min(n, pref)
    assert n % t == 0, (n, t)
    return t


def _params(*sem):
    return pltpu.CompilerParams(dimension_semantics=sem, vmem_limit_bytes=VMEM_LIMIT)


def _lspec(l, tail, blk=None):
    idx = (l,) + tuple(blk if blk is not None else (0,) * len(tail))
    return pl.BlockSpec((None,) + tuple(tail), lambda *_: idx)


def _rms_rows(x, g):
    return x * lax.rsqrt(jnp.mean(x * x, axis=-1, keepdims=True) + EPS) * g


def _lane_iota(shape):
    return lax.broadcasted_iota(jnp.int32, shape, len(shape) - 1)


def _half_head_norm(z, g):
    sq = z * z
    lo = _lane_iota(z.shape) < DIFF_QK_DIM
    s_all = jnp.sum(sq, axis=-1, keepdims=True)
    s_lo = jnp.sum(jnp.where(lo, sq, 0.0), axis=-1, keepdims=True)
    inv_lo = lax.rsqrt(s_lo * (1.0 / DIFF_QK_DIM) + EPS)
    inv_hi = lax.rsqrt((s_all - s_lo) * (1.0 / DIFF_QK_DIM) + EPS)
    return z * jnp.where(lo, inv_lo, inv_hi) * g


SEQ_TILE = 512
_NT = (((1,), (1,)), ((), ()))


N_PROJ_PARAMS = 18
N_PROJ_OUTS = 5


def _softplus(z):
    return jnp.maximum(z, 0.0) + jnp.log(1.0 + jnp.exp(-jnp.abs(z)))


def _in_proj_body(x, ns, g1_ref, wl_ref, wk_ref, wmq_ref, wqt_ref, wvt_ref, qg_ref, kg_ref, mqg_ref,
                  mk_ref, mv_ref, cw_ref, cb_ref, wa_ref, ba_ref, wx_ref, bx_ref, llam_ref,
                  ylru_ref, k_ref, ym_ref, qt_ref, vt_ref, ext_ref, hcar_ref, a_ref, b_ref, h_ref):
    gw = GROUP_WIDTH
    tm = x.shape[0]
    hist = SUBLANES
    first = lax.rem(pl.program_id(0), ns) == 0

    @pl.when(first)
    def _():
        ext_ref[0:hist, :] = jnp.zeros((hist, gw), F32)
        hcar_ref[...] = jnp.zeros_like(hcar_ref)

    @pl.when(jnp.logical_not(first))
    def _():
        ext_ref[0:hist, :] = ext_ref[tm:tm + hist, :]

    xn = _rms_rows(x, g1_ref[...]).astype(BF16)
    heads = range(MEM_HEADS)
    hsl = [slice(h * MEM_HEAD_DIM, (h + 1) * MEM_HEAD_DIM) for h in heads]
    mq = jnp.dot(xn, wmq_ref[...], preferred_element_type=F32)
    u = jnp.dot(xn, wl_ref[...], preferred_element_type=F32)
    ext_ref[hist:hist + tm, :] = u[:, 0:gw]
    xc = cb_ref[...] + cw_ref[CONV_WIDTH - 1:CONV_WIDTH, :] * u[:, 0:gw]
    for j in range(CONV_WIDTH - 1):
        off = hist - (CONV_WIDTH - 1) + j
        xc = xc + cw_ref[j:j + 1, :] * ext_ref[off:off + tm, :]
    xcb = xc.astype(BF16)
    sc = [lax.dot_general(_rms_rows(mq[:, hsl[h]], mqg_ref[...]).astype(BF16), mk_ref[:, hsl[h]], _NT,
                          preferred_element_type=F32) * MEM_HEAD_DIM ** -0.5 for h in heads]
    k = jnp.dot(xn, wk_ref[...], preferred_element_type=F32)
    r_pre = jnp.dot(xcb, wa_ref[...], preferred_element_type=F32)
    i_pre = jnp.dot(xcb, wx_ref[...], preferred_element_type=F32)
    p = [jnp.exp(sc[h] - jnp.max(sc[h], axis=-1, keepdims=True)) for h in heads]
    qt = lax.dot_general(wqt_ref[...], xn, _NT, preferred_element_type=F32)
    o = [jnp.dot(p[h].astype(BF16), mv_ref[:, hsl[h]], preferred_element_type=F32) for h in heads]
    vt_ref[...] = lax.dot_general(wvt_ref[...], xn, _NT, preferred_element_type=F32).astype(BF16)

    r = jax.nn.sigmoid(r_pre + ba_ref[...])
    gate_i = jax.nn.sigmoid(i_pre + bx_ref[...])
    a = jnp.exp((-LRU_C * r) * _softplus(-llam_ref[...]))
    om = 1.0 - a * a
    b = om * lax.rsqrt(jnp.maximum(om, 1e-30)) * (gate_i * xc)
    a = a.reshape(tm // SUBLANES, SUBLANES, gw)
    b = b.reshape(tm // SUBLANES, SUBLANES, gw)
    row = lax.broadcasted_iota(jnp.int32, a.shape, 1)
    d = 1
    while d < SUBLANES:
        keep = row >= d
        a_prev = pltpu.roll(a, d, axis=1)
        b_prev = pltpu.roll(b, d, axis=1)
        b = jnp.where(keep, a * b_prev + b, b)
        a = jnp.where(keep, a * a_prev, a)
        d *= 2
    a_ref[...] = a.reshape(tm, gw)
    b_ref[...] = b.reshape(tm, gw)
    gated = jax.nn.gelu(u[:, gw:2 * gw])

    for h in range(DIFF_HEADS):
        sl = slice(h * LANES, (h + 1) * LANES)
        k_ref[:, sl] = _half_head_norm(k[:, sl], kg_ref[...]).astype(BF16)
    for h in heads:
        ym_ref[:, hsl[h]] = (o[h] / jnp.sum(p[h], axis=-1, keepdims=True)).astype(BF16)
    q3 = qt.reshape(gw // DIFF_QK_DIM, DIFF_QK_DIM, tm)
    q3 = q3 * lax.rsqrt(jnp.mean(q3 * q3, axis=1, keepdims=True) + EPS)
    qscale = DIFF_QK_DIM ** -0.5 * math.log2(math.e)
    qt_ref[...] = (q3.reshape(gw, tm) * (qg_ref[...] * qscale)).astype(BF16)

    def block(i, hprev):
        off = pl.multiple_of(i * SUBLANES, SUBLANES)
        hb = a_ref[pl.ds(off, SUBLANES), :] * hprev + b_ref[pl.ds(off, SUBLANES), :]
        h_ref[pl.ds(off, SUBLANES), :] = hb
        return jnp.broadcast_to(hb[SUBLANES - 1:SUBLANES, :], (SUBLANES, gw))

    hcar_ref[...] = lax.fori_loop(0, tm // SUBLANES, block, hcar_ref[...], unroll=8)
    ylru_ref[...] = (h_ref[...] * gated).astype(BF16)


def _in_proj_kernel(x_ref, *refs, ns):
    _in_proj_body(x_ref[...], ns, *refs)


def _granule_copies(tab_ref, step, n, src_hbm, buf_ref, slot, sem_ref, start):
    for k in range(n):
        g = tab_ref[step * n + k]
        cp = pltpu.make_async_copy(
            src_hbm.at[pl.ds(pl.multiple_of(g * GRANULE, GRANULE), GRANULE), :],
            buf_ref.at[slot, pl.ds(k * GRANULE, GRANULE), :], sem_ref.at[slot])
        if start:
            cp.start()
        else:
            cp.wait()


def _unsort(info, ys):
    kpad = -SORTED_ROWS % LANES
    ys = jnp.concatenate([ys, jnp.zeros((kpad, D_MODEL), BF16)], axis=0)
    pos = info[:, INFO_POS_LANE:INFO_POS_LANE + 1]
    unperm = jnp.where(pos == _lane_iota((MOE_TILE, SORTED_ROWS + kpad)).astype(F32), 1.0, 0.0).astype(BF16)
    return jnp.dot(unperm, ys, preferred_element_type=F32)


def _combined_tile(inv_ref, x1_ref, info_ref, ys_hbm, ybuf_ref, sem_ref):
    s = pl.program_id(0)
    last = pl.num_programs(0) - 1
    slot = s & 1
    nsub = x1_ref.shape[0] // MOE_TILE
    n = nsub * TILE_SLOTS

    @pl.when(s == 0)
    def _():
        _granule_copies(inv_ref, 0, n, ys_hbm, ybuf_ref, 0, sem_ref, True)

    _granule_copies(inv_ref, s, n, ys_hbm, ybuf_ref, slot, sem_ref, False)
    nxt = jnp.minimum(s + 1, last)
    _granule_copies(inv_ref, nxt, n, ys_hbm, ybuf_ref, 1 - slot, sem_ref, True)
    x = jnp.concatenate(
        [x1_ref[u * MOE_TILE:(u + 1) * MOE_TILE, :]
         + _unsort(info_ref[u * MOE_TILE:(u + 1) * MOE_TILE, :],
                   ybuf_ref[slot, u * SORTED_ROWS:(u + 1) * SORTED_ROWS, :])
         for u in range(nsub)], axis=0)

    @pl.when(s == last)
    def _():
        _granule_copies(inv_ref, nxt, n, ys_hbm, ybuf_ref, 1 - slot, sem_ref, False)

    return x


def _combine_in_proj_kernel(inv_ref, x1_ref, info_ref, ys_hbm, *refs, ns):
    params, x_out_ref, outs = refs[:N_PROJ_PARAMS], refs[N_PROJ_PARAMS], refs[N_PROJ_PARAMS + 1:-2]
    x = _combined_tile(inv_ref, x1_ref, info_ref, ys_hbm, *refs[-2:])
    x_out_ref[...] = x
    _in_proj_body(x, ns, *params, *outs)


def _in_proj(x2, b, s, l, g1, w_in, w_in_t, qg_col, kg, mqg, mk, mv, lru, moe=None):
    t = x2.shape[0]
    tm = SEQ_TILE
    assert s % tm == 0
    ns = s // tm
    m = mk.shape[1] // b
    gw = GROUP_WIDTH
    row = lambda i, *_: (i, 0)
    fm = lambda i, *_: (i // ns, i % ns, 0, 0)
    mem = pl.BlockSpec((None, m, gw), lambda i, *_: (l, i // ns, 0))
    in_specs = [_lspec(l, (1, D_MODEL)),
                _lspec(l, (D_MODEL, 2 * gw), (0, 0)), _lspec(l, (D_MODEL, gw), (0, 3)),
                _lspec(l, (D_MODEL, gw), (0, 5)), _lspec(l, (gw, D_MODEL), (0, 0)),
                _lspec(l, (gw, D_MODEL), (1, 0)), _lspec(l, (gw, 1)),
                _lspec(l, (1, LANES)), _lspec(l, (1, MEM_HEAD_DIM)), mem, mem]
    vec = _lspec(l, (1, gw))
    in_specs += [_lspec(l, (CONV_WIDTH, gw)), vec, _lspec(l, (gw, gw)), vec, _lspec(l, (gw, gw)), vec, vec]
    operands = (g1, w_in, w_in, w_in, w_in_t, w_in_t, qg_col, kg, mqg, mk, mv) + tuple(lru)
    assert len(operands) == N_PROJ_PARAMS
    out_specs = [pl.BlockSpec((tm, gw), row), pl.BlockSpec((tm, gw), row), pl.BlockSpec((tm, gw), row),
                 pl.BlockSpec((None, None, gw, tm), fm), pl.BlockSpec((None, None, gw, tm), fm)]
    out_shape = [jax.ShapeDtypeStruct((t, gw), BF16), jax.ShapeDtypeStruct((t, gw), BF16),
                 jax.ShapeDtypeStruct((t, gw), BF16), jax.ShapeDtypeStruct((b, ns, gw, tm), BF16),
                 jax.ShapeDtypeStruct((b, ns, gw, tm), BF16)]
    assert len(out_specs) == N_PROJ_OUTS
    scratch = [pltpu.VMEM((tm + SUBLANES, gw), F32), pltpu.VMEM((SUBLANES, gw), F32),
               pltpu.VMEM((tm, gw), F32), pltpu.VMEM((tm, gw), F32), pltpu.VMEM((tm, gw), F32)]
    xspec = pl.BlockSpec((tm, D_MODEL), row)
    if moe is None:
        return pl.pallas_call(
            functools.partial(_in_proj_kernel, ns=ns), grid=(t // tm,), in_specs=[xspec] + in_specs,
            out_specs=out_specs, out_shape=out_shape, scratch_shapes=scratch,
            compiler_params=_params("arbitrary"), name="in_proj",
        )(x2, *operands)
    inv, info, ys = moe
    return pl.pallas_call(
        functools.partial(_combine_in_proj_kernel, ns=ns),
        grid_spec=pltpu.PrefetchScalarGridSpec(
            num_scalar_prefetch=1, grid=(t // tm,),
            in_specs=[xspec, pl.BlockSpec((tm, LANES), row), pl.BlockSpec(memory_space=pl.ANY)] + in_specs,
            out_specs=[xspec] + out_specs, scratch_shapes=scratch + _combine_scratch(tm)),
        out_shape=[jax.ShapeDtypeStruct((t, D_MODEL), F32)] + out_shape,
        compiler_params=_params("arbitrary"),
        name="combine_in_proj",
    )(inv, x2, info, ys, *operands)


def _combine_scratch(tm):
    return [pltpu.VMEM((2, tm // MOE_TILE * SORTED_ROWS, D_MODEL), BF16), pltpu.SemaphoreType.DMA((2,))]


ATTN_HEADS_PER_STEP = 4
ATTN_MIN_SUM = 2.0 ** -40


def _score_bound(dq_gain, dk_gain):
    qscale = DIFF_QK_DIM ** -0.5 * math.log2(math.e)
    return (1.02 * qscale * DIFF_QK_DIM) * jnp.max(jnp.abs(dq_gain), axis=-1) * jnp.max(jnp.abs(dk_gain), axis=-1)


def _diff_attn_kernel(lam_ref, qt_ref, k_ref, vt_ref, hg_ref, o_ref,
                      qs_ref, sa_ref, sb_ref, m_ref, l_ref, acc_ref, *, t, layer, out_scale):
    qi = pl.program_id(2)
    hd = DIFF_HEAD_DIM
    heads = range(ATTN_HEADS_PER_STEP)
    lo = lax.broadcasted_iota(jnp.int32, (hd, t), 0) < DIFF_QK_DIM
    for h in heads:
        q = qt_ref[h * hd:(h + 1) * hd, :]
        zero = jnp.zeros_like(q)
        qs_ref[h, :, 0:t] = jnp.where(lo, q, zero)
        qs_ref[h, :, t:2 * t] = jnp.where(lo, zero, q)

    def causal(x, fill):
        kpos = lax.broadcasted_iota(jnp.int32, (t, 2 * t), 0)
        c = lax.broadcasted_iota(jnp.int32, (t, 2 * t), 1)
        return jnp.where(kpos <= jnp.where(c >= t, c - t, c), x, fill)

    def score(j, h):
        off = pl.multiple_of(j * t, t)
        return jnp.dot(k_ref[pl.ds(off, t), h * hd:(h + 1) * hd], qs_ref[h],
                       preferred_element_type=F32)

    bound = lam_ref[layer, 1]
    l_ref[...] = jnp.zeros(l_ref.shape, F32)
    acc_ref[...] = jnp.zeros(acc_ref.shape, F32)

    def fast_update(j, masked):
        for h in heads:
            p = jnp.exp2(score(j, h) - bound)
            if masked:
                p = causal(p, 0.0)
            l_ref[h] += jnp.sum(p, axis=0, keepdims=True)
            acc_ref[h] += jnp.dot(vt_ref[j, h * hd:(h + 1) * hd, :], p.astype(BF16),
                                  preferred_element_type=F32)

    def fast_pair(i, carry):
        fast_update(2 * i, False)
        fast_update(2 * i + 1, False)
        return carry

    lax.fori_loop(0, qi >> 1, fast_pair, 0)

    @pl.when((qi & 1) == 1)
    def _():
        fast_update(qi - 1, False)

    fast_update(qi, True)

    @pl.when(jnp.min(l_ref[...]) < ATTN_MIN_SUM)
    def _():
        m_ref[...] = jnp.full(m_ref.shape, NEG_BIG, F32)
        l_ref[...] = jnp.zeros(l_ref.shape, F32)
        acc_ref[...] = jnp.zeros(acc_ref.shape, F32)

        def scores(j, dst_ref):
            for h in heads:
                dst_ref[h] = score(j, h)

        def update(j, src_ref, masked):
            for h in heads:
                s = src_ref[h]
                if masked:
                    s = causal(s, NEG_BIG)
                m = m_ref[h]
                m_new = jnp.maximum(m, jnp.max(s, axis=0, keepdims=True))
                alpha = jnp.exp2(m - m_new)
                p = jnp.exp2(s - m_new)
                m_ref[h] = m_new
                l_ref[h] = alpha * l_ref[h] + jnp.sum(p, axis=0, keepdims=True)
                vt = vt_ref[j, h * hd:(h + 1) * hd, :]
                acc_ref[h] = alpha * acc_ref[h] + jnp.dot(vt, p.astype(BF16), preferred_element_type=F32)

        scores(0, sa_ref)

        def pair(i, carry):
            scores(2 * i + 1, sb_ref)
            update(2 * i, sa_ref, False)
            scores(2 * i + 2, sa_ref)
            update(2 * i + 1, sb_ref, False)
            return carry

        lax.fori_loop(0, qi >> 1, pair, 0)

        @pl.when((qi & 1) == 1)
        def _():
            scores(qi, sb_ref)
            update(qi - 1, sa_ref, False)
            update(qi, sb_ref, True)

        @pl.when((qi & 1) == 0)
        def _():
            update(qi, sa_ref, True)

    for h in heads:
        o = acc_ref[h] / l_ref[h]
        o = o[:, 0:t] - lam_ref[layer, 0] * o[:, t:2 * t]
        o = o * lax.rsqrt(jnp.mean(o * o, axis=0, keepdims=True) + EPS) * (hg_ref[...] * out_scale)
        o_ref[:, h * hd:(h + 1) * hd] = o.T.astype(BF16)


def _diff_attn(qt, k, vt, lam, hg_col, b, s, l, lam_init):
    t = SEQ_TILE
    nq = s // t
    hp = ATTN_HEADS_PER_STEP
    w = hp * DIFF_HEAD_DIM
    ng = DIFF_HEADS // hp
    kern = functools.partial(_diff_attn_kernel, t=t, layer=l, out_scale=1.0 - lam_init)
    return pl.pallas_call(
        kern,
        grid=(b, ng, nq),
        in_specs=[pl.BlockSpec(memory_space=pltpu.SMEM),
                  pl.BlockSpec((None, None, w, t), lambda bi, g, qi: (bi, qi, g, 0)),
                  pl.BlockSpec((s, w), lambda bi, g, qi: (bi, g)),
                  pl.BlockSpec((None, nq, w, t), lambda bi, g, qi: (bi, 0, g, 0)),
                  _lspec(l, (DIFF_HEAD_DIM, 1))],
        out_specs=pl.BlockSpec((t, w), lambda bi, g, qi: (bi * nq + qi, g)),
        out_shape=jax.ShapeDtypeStruct((b * s, GROUP_WIDTH), BF16),
        scratch_shapes=[pltpu.VMEM((hp, DIFF_HEAD_DIM, 2 * t), BF16),
                        pltpu.VMEM((hp, t, 2 * t), F32), pltpu.VMEM((hp, t, 2 * t), F32),
                        pltpu.VMEM((hp, 1, 2 * t), F32), pltpu.VMEM((hp, 1, 2 * t), F32),
                        pltpu.VMEM((hp, DIFF_HEAD_DIM, 2 * t), F32)],
        compiler_params=_params("arbitrary", "arbitrary", "arbitrary"),
        name="diff_attn",
    )(lam, qt, k, vt, hg_col)


def _mem_kv_kernel(mem_ref, mg_ref, w_ref, kg_ref, mk_ref, mv_ref):
    memn = _rms_rows(mem_ref[...], mg_ref[...]).astype(BF16)
    gw = GROUP_WIDTH
    k = jnp.dot(memn, w_ref[:, 0:gw], preferred_element_type=F32)
    for h in range(MEM_HEADS):
        sl = slice(h * MEM_HEAD_DIM, (h + 1) * MEM_HEAD_DIM)
        mk_ref[:, sl] = _rms_rows(k[:, sl], kg_ref[...]).astype(BF16)
    mv_ref[...] = jnp.dot(memn, w_ref[:, gw:2 * gw], preferred_element_type=F32).astype(BF16)


def _mem_kv(mem2, mg, w_kv, kg):
    r = mem2.shape[0]
    depth = w_kv.shape[0]
    tm = _tile(r, 512)
    gw = GROUP_WIDTH
    out = pl.BlockSpec((None, tm, gw), lambda l, i: (l, i, 0))
    return pl.pallas_call(
        _mem_kv_kernel,
        grid=(depth, r // tm),
        in_specs=[pl.BlockSpec((tm, D_MODEL), lambda l, i: (i, 0)), pl.BlockSpec((1, D_MODEL), lambda l, i: (0, 0)),
                  pl.BlockSpec((None, D_MODEL, 2 * gw), lambda l, i: (l, 0, 0)),
                  pl.BlockSpec((None, 1, MEM_HEAD_DIM), lambda l, i: (l, 0, 0))],
        out_specs=[out, out],
        out_shape=[jax.ShapeDtypeStruct((depth, r, gw), BF16), jax.ShapeDtypeStruct((depth, r, gw), BF16)],
        compiler_params=_params("arbitrary", "arbitrary"),
        name="mem_kv",
    )(mem2, mg, w_kv, kg)


MOE_TILE = 256
OUT_TILE = 1024
GRANULE = 16
TILE_SLOTS = MOE_TILE // GRANULE + N_GROUPS
SORTED_ROWS = TILE_SLOTS * GRANULE
STEP_GRANULES = 32
STEP_ROWS = STEP_GRANULES * GRANULE
XS_WIDTH = D_MODEL + LANES
INFO_POS_LANE = 0


def _route_t(lt):
    row_i = lax.broadcasted_iota(jnp.int32, lt.shape, 0)
    row = row_i.astype(F32)
    e0 = ROUTER_EXPERT_LANE0
    lg = jnp.where(row_i < N_GROUPS, lt, NEG_BIG)
    mg = jnp.max(lg, axis=0, keepdims=True)
    g_gate = 1.0 / jnp.sum(jnp.exp(lg - mg), axis=0, keepdims=True)
    g_idx = jnp.min(jnp.where(lg == mg, row, float(LANES)), axis=0, keepdims=True)
    row_group = ((row_i - e0) >> 3).astype(F32)
    sel = (row_i >= e0) & (row_i < e0 + N_EXPERTS) & (row_group == g_idx)
    le = jnp.where(sel, lt, NEG_BIG)
    m1 = jnp.max(le, axis=0, keepdims=True)
    se = jnp.sum(jnp.where(sel, jnp.exp(le - m1), 0.0), axis=0, keepdims=True)
    i1 = jnp.min(jnp.where(sel & (le == m1), row, float(LANES)), axis=0, keepdims=True)
    le2 = jnp.where(row == i1, NEG_BIG, le)
    m2 = jnp.max(le2, axis=0, keepdims=True)
    i2 = jnp.min(jnp.where(sel & (le2 == m2) & (row != i1), row, float(LANES)), axis=0, keepdims=True)
    p1 = 1.0 / se
    p2 = jnp.exp(m2 - m1) / se
    tot = p1 + p2
    w = jnp.where(row == i1, p1 / tot, jnp.where(row == i2, p2 / tot, 0.0))
    return g_gate * w, g_idx


def _split3(c):
    hi = c.astype(BF16).astype(F32)
    r1 = c - hi
    mid = r1.astype(BF16).astype(F32)
    lo = (r1 - mid).astype(BF16).astype(F32)
    return hi, mid, lo


def _sort_logits(x1, g2, rt_hi, rt_lo, rb_col):
    xn = _rms_rows(x1, g2)
    hi = xn.astype(BF16)
    lo = (xn - hi.astype(F32)).astype(BF16)
    nr = ROUTER_ROWS
    rt_hi = rt_hi[0:nr]
    both = lax.dot_general(jnp.concatenate([rt_hi, rt_lo[0:nr]], axis=0), hi, _NT, preferred_element_type=F32)
    logits_t = (both[0:nr] + lax.dot_general(rt_hi, lo, _NT, preferred_element_type=F32)
                + both[nr:2 * nr] + rb_col[0:nr])
    return hi, logits_t


def _sort_rank(logits_t, earlier):
    tl = logits_t.shape[1]
    comb_t, g_idx = _route_t(logits_t)
    grow = lax.broadcasted_iota(jnp.int32, (SUBLANES, tl), 0).astype(F32)
    gt = jnp.where(grow == g_idx, 1.0, 0.0)
    before = jnp.dot(gt.astype(BF16), earlier, preferred_element_type=F32)
    return comb_t, gt, before


def _sort_emit(hi, comb_t, gt, before):
    tl = hi.shape[0]
    rank = jnp.sum(gt * before, axis=0, keepdims=True)
    cnt = jnp.sum(gt, axis=1, keepdims=True)
    glen = jnp.floor((cnt + (GRANULE - 1)) * (1.0 / GRANULE))
    r8 = lax.broadcasted_iota(jnp.int32, (SUBLANES, 1), 0)
    start = jnp.zeros((SUBLANES, 1), F32)
    for g in range(1, N_GROUPS):
        start = jnp.where(r8 == g, jnp.sum(jnp.where(r8 < g, glen, 0.0), axis=0, keepdims=True), start)
    pos = jnp.sum(gt * (start * GRANULE), axis=0, keepdims=True) + rank

    rows = lax.broadcasted_iota(jnp.int32, (LANES, tl), 0)
    comb_t = jnp.concatenate([comb_t, jnp.zeros((LANES - comb_t.shape[0], tl), F32)], axis=0)
    info = jnp.where(rows == INFO_POS_LANE, pos, comb_t).T
    lane = _lane_iota(info.shape)
    e0 = ROUTER_EXPERT_LANE0
    c_hi, c_mid, c_lo = _split3(jnp.where((lane >= e0) & (lane < e0 + N_EXPERTS), info, 0.0))
    aug = (c_hi + pltpu.roll(c_mid, N_EXPERTS, axis=1) + pltpu.roll(c_lo, 2 * N_EXPERTS, axis=1)).astype(BF16)
    perm = jnp.where(pos == lax.broadcasted_iota(jnp.int32, (SORTED_ROWS, tl), 0).astype(F32),
                     1.0, 0.0).astype(BF16)
    xs = jnp.dot(perm, hi, preferred_element_type=F32).astype(BF16)
    xs_aug = jnp.dot(perm, aug, preferred_element_type=F32).astype(BF16)
    return xs, xs_aug, info, glen


def _out_sort_kernel(x_ref, yl_ref, yd_ref, ym_ref, w_ref, g2_ref, rthi_ref, rtlo_ref, rb_ref, earlier_ref,
                     o_ref, xs_ref, info_ref, lens_ref, x1s_ref):
    @pl.when(pl.program_id(0) == 0)
    def _():
        x1s_ref[...] = jnp.zeros_like(x1s_ref)

    tm = x1s_ref.shape[0]
    subs = range(tm // MOE_TILE)

    def project(c, nchunks):
        r = slice(c * (tm // nchunks), (c + 1) * (tm // nchunks))
        y = jnp.concatenate([yl_ref[r, :], yd_ref[r, :], ym_ref[r, :]], axis=-1)
        x1 = x_ref[r, :] + jnp.dot(y, w_ref[...], preferred_element_type=F32)
        o_ref[r, :] = x1
        return r, x1

    nch = len(subs)
    first_half = range(0, nch // 2)
    second_half = range(nch // 2, nch)
    fresh = [project(c, nch) for c in first_half]
    s1 = [_sort_logits(x1s_ref[sub * MOE_TILE:(sub + 1) * MOE_TILE, :], g2_ref[...], rthi_ref[...],
                       rtlo_ref[...], rb_ref[...]) for sub in subs]
    for r, x1 in fresh:
        x1s_ref[r, :] = x1
    fresh = [project(c, nch) for c in second_half]
    s2 = [_sort_rank(s1[sub][1], earlier_ref[...]) for sub in subs]
    for r, x1 in fresh:
        x1s_ref[r, :] = x1
    for sub in subs:
        xs, xs_aug, info, glen = _sort_emit(s1[sub][0], *s2[sub])
        r0 = sub * SORTED_ROWS
        xs_ref[r0:r0 + SORTED_ROWS, 0:D_MODEL] = xs
        xs_ref[r0:r0 + SORTED_ROWS, D_MODEL:XS_WIDTH] = xs_aug
        info_ref[sub * MOE_TILE:(sub + 1) * MOE_TILE, :] = info
        lens_ref[sub] = jnp.broadcast_to(glen, (SUBLANES, LANES))


def _out_sort(x2, y_lru, y_diff, y_mem, l, w_out, g2, rt_hi, rt_lo, rb_col, earlier):
    t = x2.shape[0]
    tm = OUT_TILE if t % OUT_TILE == 0 else SEQ_TILE
    assert t % tm == 0 and tm % MOE_TILE == 0
    sub = tm // MOE_TILE
    nt = t // MOE_TILE
    n = t // tm
    cur = lambda i: (jnp.minimum(i, n - 1), 0)
    prev = lambda i: (jnp.maximum(i - 1, 0), 0)
    gw = GROUP_WIDTH
    return pl.pallas_call(
        _out_sort_kernel,
        grid=(n + 1,),
        in_specs=[pl.BlockSpec((tm, D_MODEL), cur), pl.BlockSpec((tm, gw), cur), pl.BlockSpec((tm, gw), cur),
                  pl.BlockSpec((tm, gw), cur), _lspec(l, (3 * gw, D_MODEL)),
                  _lspec(l, (1, D_MODEL)), _lspec(l, (LANES, D_MODEL)),
                  _lspec(l, (LANES, D_MODEL)), _lspec(l, (LANES, 1)),
                  pl.BlockSpec((MOE_TILE, MOE_TILE), lambda i: (0, 0))],
        out_specs=[pl.BlockSpec((tm, D_MODEL), cur), pl.BlockSpec((sub * SORTED_ROWS, XS_WIDTH), prev),
                   pl.BlockSpec((tm, LANES), prev),
                   pl.BlockSpec((sub, SUBLANES, LANES), lambda i: (jnp.maximum(i - 1, 0), 0, 0))],
        out_shape=[jax.ShapeDtypeStruct((t, D_MODEL), F32),
                   jax.ShapeDtypeStruct((nt * SORTED_ROWS, XS_WIDTH), BF16),
                   jax.ShapeDtypeStruct((t, LANES), F32),
                   jax.ShapeDtypeStruct((nt, SUBLANES, LANES), F32)],
        scratch_shapes=[pltpu.VMEM((tm, D_MODEL), F32)],
        compiler_params=_params("arbitrary"),
        name="out_proj_sort",
    )(x2, y_lru, y_diff, y_mem, w_out, g2, rt_hi, rt_lo, rb_col, earlier)


def _moe_expert_kernel(sg_ref, sv_ref, gi_ref, xs_hbm, wg_ref, wu_ref, wd_ref, y_ref, xbuf_ref, sem_ref):
    s = pl.program_id(0)
    last = pl.num_programs(0) - 1
    slot = s & 1

    def start_fetch(step, dst_slot):
        _granule_copies(gi_ref, step, STEP_GRANULES, xs_hbm, xbuf_ref, dst_slot, sem_ref, True)

    def wait_fetch(step, dst_slot):
        _granule_copies(gi_ref, step, STEP_GRANULES, xs_hbm, xbuf_ref, dst_slot, sem_ref, False)

    @pl.when(s == 0)
    def _():
        start_fetch(0, 0)

    @pl.when((s == 0) | (sv_ref[jnp.maximum(s - 1, 0)] != 0))
    def _():
        wait_fetch(s, slot)

    @pl.when(sv_ref[s] == 0)
    def _():
        y_ref[...] = jnp.zeros_like(y_ref)

    @pl.when(sv_ref[s] != 0)
    def _():
        nxt = jnp.minimum(s + 1, last)
        start_fetch(nxt, 1 - slot)
        rows = xbuf_ref[slot]
        x = rows[:, 0:D_MODEL]
        aug = rows[:, D_MODEL:XS_WIDTH].astype(F32)
        lane = _lane_iota(aug.shape)
        e0 = ROUTER_EXPERT_LANE0
        comb = jnp.where((lane >= e0) & (lane < e0 + N_EXPERTS),
                         aug + pltpu.roll(aug, LANES - N_EXPERTS, axis=1)
                         + pltpu.roll(aug, LANES - 2 * N_EXPERTS, axis=1), 0.0)
        first = e0 + sg_ref[s] * EXPERTS_PER_GROUP
        hs = []
        for e in range(EXPERTS_PER_GROUP):
            cw = jnp.sum(jnp.where(lane == first + e, comb, 0.0), axis=-1, keepdims=True)
            hg = jnp.dot(x, wg_ref[e], preferred_element_type=F32)
            hu = jnp.dot(x, wu_ref[e], preferred_element_type=F32)
            hs.append((jax.nn.silu(hg) * hu * cw).astype(BF16))
        h = jnp.concatenate(hs, axis=-1)
        y_ref[...] = jnp.dot(h, wd_ref[...], preferred_element_type=F32).astype(BF16)

        @pl.when(s == last)
        def _():
            wait_fetch(nxt, 1 - slot)


def _moe_experts(step_group, step_valid, gran_idx, xs, l, wg, wu, wd):
    nstep = step_group.shape[0]
    epg = EXPERTS_PER_GROUP
    wmap4 = lambda s, sg, sv, gi: (l * N_GROUPS + sg[s], 0, 0, 0)
    in_specs = [pl.BlockSpec(memory_space=pl.ANY),
                pl.BlockSpec((None, epg, D_MODEL, D_EXPERT), wmap4),
                pl.BlockSpec((None, epg, D_MODEL, D_EXPERT), wmap4),
                pl.BlockSpec((None, epg * D_EXPERT, D_MODEL), lambda s, sg, sv, gi: (l * N_GROUPS + sg[s], 0, 0))]
    return pl.pallas_call(
        _moe_expert_kernel,
        grid_spec=pltpu.PrefetchScalarGridSpec(
            num_scalar_prefetch=3, grid=(nstep,), in_specs=in_specs,
            out_specs=pl.BlockSpec((STEP_ROWS, D_MODEL), lambda s, sg, sv, gi: (s, 0)),
            scratch_shapes=[pltpu.VMEM((2, STEP_ROWS, XS_WIDTH), BF16), pltpu.SemaphoreType.DMA((2,))]),
        out_shape=jax.ShapeDtypeStruct((nstep * STEP_ROWS, D_MODEL), BF16),
        compiler_params=_params("arbitrary"),
        name="moe_experts",
    )(step_group, step_valid, gran_idx, xs, wg, wu, wd)


def _moe_combine_kernel(inv_ref, x_ref, info_ref, ys_hbm, o_ref, ybuf_ref, sem_ref):
    o_ref[...] = _combined_tile(inv_ref, x_ref, info_ref, ys_hbm, ybuf_ref, sem_ref)


def _moe_combine(inv, x2, info, ys):
    t = x2.shape[0]
    tm = SEQ_TILE
    row = lambda i, inv: (i, 0)
    return pl.pallas_call(
        _moe_combine_kernel,
        grid_spec=pltpu.PrefetchScalarGridSpec(
            num_scalar_prefetch=1, grid=(t // tm,),
            in_specs=[pl.BlockSpec((tm, D_MODEL), row), pl.BlockSpec((tm, LANES), row),
                      pl.BlockSpec(memory_space=pl.ANY)],
            out_specs=pl.BlockSpec((tm, D_MODEL), row), scratch_shapes=_combine_scratch(tm)),
        out_shape=jax.ShapeDtypeStruct((t, D_MODEL), F32),
        compiler_params=_params("arbitrary"),
        name="moe_combine",
    )(inv, x2, info, ys)


def _moe_tables(lens, nt):
    i32 = jnp.int32
    ng = N_GROUPS
    garange = jnp.arange(ng, dtype=i32)
    cum = jnp.cumsum(lens, axis=1)
    start = cum - lens
    base = jnp.arange(nt, dtype=i32)[:, None] * TILE_SLOTS + start
    run_len = lens.T.reshape(-1)
    run_base = base.T.reshape(-1)
    run_end = jnp.cumsum(run_len)
    run_start = run_end - run_len
    n_g = jnp.sum(lens, axis=0)
    steps_g = (n_g + STEP_GRANULES - 1) // STEP_GRANULES
    step_end = jnp.cumsum(steps_g)
    step_off = step_end - steps_g
    gran_off = jnp.cumsum(n_g) - n_g
    nstep = (nt * (TILE_SLOTS - 1) + STEP_GRANULES - 1) // STEP_GRANULES + ng
    s = jnp.arange(nstep, dtype=i32)
    sg = jnp.minimum(jnp.sum((s[:, None] >= step_end[None, :]).astype(i32), axis=-1), ng - 1)
    goh = sg[:, None] == garange[None, :]
    pick = lambda v: jnp.sum(jnp.where(goh, v[None, :], 0), axis=-1)
    sv = s < step_end[-1]
    jl = (s - pick(step_off))[:, None] * STEP_GRANULES + jnp.arange(STEP_GRANULES, dtype=i32)[None, :]
    ok = sv[:, None] & (jl < pick(n_g)[:, None])
    j = jnp.where(ok, pick(gran_off)[:, None] + jl, 0)
    inrun = (j[..., None] >= run_start) & (j[..., None] < run_end)
    gran = j + jnp.sum(jnp.where(inrun, run_base - run_start, 0), axis=-1)
    gran = jnp.where(ok, gran, jnp.where(sv[:, None], gran[:, 0:1], 0)).astype(i32)
    q = jnp.arange(TILE_SLOTS, dtype=i32)[None, :]
    gq = jnp.minimum(jnp.sum((q[:, :, None] >= cum[:, None, :]).astype(i32), axis=-1), ng - 1)
    qoh = gq[..., None] == garange
    used = q < cum[:, -1:]
    per_g = run_start.reshape(ng, nt).T - start + (step_off * STEP_GRANULES - gran_off)[None, :]
    inv = jnp.where(used, q + jnp.sum(jnp.where(qoh, per_g[:, None, :], 0), axis=-1), 0).astype(i32)
    return sg, sv.astype(i32), gran.reshape(-1), inv.reshape(-1)


def _out_proj_moe(x2, y_lru, y_diff, y_mem, l, w_out, g2, rt_hi, rt_lo, rb_col, earlier, wg, wu, wd):
    nt = x2.shape[0] // MOE_TILE
    x1, xs, info, lens = _out_sort(x2, y_lru, y_diff, y_mem, l, w_out, g2, rt_hi, rt_lo, rb_col, earlier)
    sg, sv, gran, inv = _moe_tables(lens[:, 0:N_GROUPS, 0].astype(jnp.int32), nt)
    ys = _moe_experts(sg, sv, gran, xs, l, wg, wu, wd)
    return x1, (inv, info, ys)


def _block_diag(w):
    depth, h, n, _ = w.shape
    eye = jnp.eye(h, dtype=w.dtype)
    return (eye[None, :, None, :, None] * w[:, :, :, None, :]).reshape(depth, h * n, h * n)


def _router_tables(w_rg, b_rg, w_re, b_re):
    depth = w_rg.shape[0]
    e0 = ROUTER_EXPERT_LANE0
    pad = lambda rows, width: jnp.zeros((depth, rows, width), F32)
    w = jnp.concatenate([jnp.swapaxes(w_rg, 1, 2), pad(e0 - N_GROUPS, D_MODEL), jnp.swapaxes(w_re, 1, 2),
                         pad(LANES - e0 - N_EXPERTS, D_MODEL)], axis=1)
    bias = jnp.concatenate([b_rg[:, :, None], pad(e0 - N_GROUPS, 1), b_re[:, :, None],
                            pad(LANES - e0 - N_EXPERTS, 1)], axis=1)
    hi = w.astype(BF16)
    lo = (w - hi.astype(F32)).astype(BF16)
    return hi, lo, bias


def kernel(x, mem, norm1_g, w_in, conv_w, conv_b, rg_wa, rg_ba, rg_wx, rg_bx, rg_lambda, dq_norm_g, dk_norm_g, lambda_q1, lambda_k1, lambda_q2, lambda_k2, diff_head_norm_g, mem_norm_g, w_mem_kv, mq_norm_g, mk_norm_g, w_out, norm2_g, w_router_group, b_router_group, w_router_expert, b_router_expert, w_expert_gate, w_expert_up, w_expert_down):
    b, s, d = x.shape
    m = mem.shape[1]
    depth = w_in.shape[0]
    gw = GROUP_WIDTH
    epg = EXPERTS_PER_GROUP
    x2 = x.reshape(b * s, d)
    mem2 = mem.reshape(b * m, d)
    row = lambda v: v.reshape(depth, 1, -1).astype(F32)
    col = lambda v: v.reshape(depth, -1, 1).astype(F32)
    lam_inits = [0.8 - 0.6 * math.exp(-0.3 * l) for l in range(depth)]
    lam = (jnp.exp(jnp.sum(lambda_q1 * lambda_k1, axis=-1)) - jnp.exp(jnp.sum(lambda_q2 * lambda_k2, axis=-1))
           + jnp.asarray(lam_inits, F32))
    lam = jnp.stack([lam, _score_bound(dq_norm_g, dk_norm_g)], axis=1).astype(F32)
    w_in_bf = w_in.astype(BF16)
    w_qv_t = jnp.swapaxes(jnp.concatenate([w_in_bf[:, :, 2 * gw:3 * gw], w_in_bf[:, :, 4 * gw:5 * gw]], axis=2), 1, 2)
    qg_col = col(jnp.tile(dq_norm_g, (1, gw // DIFF_QK_DIM)))
    kg = row(jnp.tile(dk_norm_g, (1, 2)))
    wa_bd = _block_diag(rg_wa).astype(BF16)
    wx_bd = _block_diag(rg_wx).astype(BF16)
    w_kv_bf = w_mem_kv.astype(BF16)
    w_out_bf = w_out.astype(BF16)
    rt_hi, rt_lo, rb_col = _router_tables(w_router_group, b_router_group, w_router_expert, b_router_expert)
    earlier = jnp.triu(jnp.ones((MOE_TILE, MOE_TILE), BF16), k=1)
    wg = w_expert_gate.astype(BF16).reshape(depth * N_GROUPS, epg, D_MODEL, D_EXPERT)
    wu = w_expert_up.astype(BF16).reshape(depth * N_GROUPS, epg, D_MODEL, D_EXPERT)
    wd = w_expert_down.astype(BF16).reshape(depth * N_GROUPS, epg * D_EXPERT, D_MODEL)
    g1, g2, mqg, mkg = row(norm1_g), row(norm2_g), row(mq_norm_g), row(mk_norm_g)
    conv_b3, ba, bx, lru_lam, hg_col = row(conv_b), row(rg_ba), row(rg_bx), row(rg_lambda), col(diff_head_norm_g)
    mem_g = mem_norm_g.reshape(1, -1).astype(F32)
    mk, mv = _mem_kv(mem2, mem_g, w_kv_bf, mkg)
    moe = None
    for l in range(depth):
        proj = _in_proj(x2, b, s, l, g1, w_in_bf, w_qv_t, qg_col, kg, mqg, mk, mv,
                        (conv_w, conv_b3, wa_bd, ba, wx_bd, bx, lru_lam), moe)
        if moe is not None:
            x2, *proj = proj
        y_lru, k, y_mem, qt, vt = proj
        y_diff = _diff_attn(qt, k, vt, lam, hg_col, b, s, l, lam_inits[l])
        x2, moe = _out_proj_moe(x2, y_lru, y_diff, y_mem, l, w_out_bf, g2, rt_hi, rt_lo, rb_col, earlier,
                                wg, wu, wd)
    return _moe_combine(moe[0], x2, moe[1], moe[2]).reshape(b, s, d)
```
